```python
import math
import jax
import jax.numpy as jnp
from jax import lax
import numpy as np

D_MODEL = 1024
BATCH = 8
SEQ = 2048
DEPTH = 1
DEC_BATCH = 128
DEC_SEQ = 1
PAST_LEN = 2048
PAGE_SIZE = 128

HEAD_DIM = 64
NSA_HEADS = 8
NSA_KV_HEADS = 2
NSA_GROUP = NSA_HEADS // NSA_KV_HEADS
CMP_LEN = 32
CMP_STRIDE = 16
SEL_BLOCK = 64
SEL_TOPN = 16
WINDOW = 512
DIFF_HEADS = 4
DIFF_VDIM = 2 * HEAD_DIM
N_EXPERTS = 32
TOP_K = 4
D_FF = D_MODEL
SWIGLU_ALPHA = 1.702
SWIGLU_LIMIT = 7.0
MOE_BLOCK = 128
Q_BLOCK = 128
LN_EPS = 1e-5
RMS_EPS = 1e-5
ATTN_SCALE = HEAD_DIM ** -0.5
DEEPNORM_ALPHA = (2 * DEPTH) ** 0.25
DEEPNORM_BETA = (8 * DEPTH) ** -0.25
SPLITS = (NSA_HEADS * HEAD_DIM,) + (NSA_KV_HEADS * HEAD_DIM,) * 6 + (3 * NSA_HEADS,) + (DIFF_HEADS * 2 * HEAD_DIM,) * 2 + (DIFF_HEADS * DIFF_VDIM,)
V_COLS = (False, False, True, False, True, False, True, False, False, False, True)
SPLIT_POINTS = tuple(int(v) for v in np.cumsum(SPLITS)[:-1])
D_IN = sum(SPLITS)
MIX_WIDTH = NSA_HEADS * HEAD_DIM + DIFF_HEADS * DIFF_VDIM

kernel_name = "hymba_nsa_diffattn_moe_deepnorm_step"


def lambda_init(layer):
    return 0.8 - 0.6 * math.exp(-0.3 * layer)


def alibi_slopes(n):
    return jnp.asarray(np.array([2.0 ** (-8.0 * (i + 1) / n) for i in range(n)], np.float32))


def layer_norm(x, g, b):
    xf = x.astype(jnp.float32)
    mu = jnp.mean(xf, -1, keepdims=True)
    var = jnp.mean(jnp.square(xf - mu), -1, keepdims=True)
    return ((xf - mu) * lax.rsqrt(var + LN_EPS) * g + b).astype(x.dtype)


def masked_softmax(s, mask):
    s = jnp.where(mask, s.astype(jnp.float32), -jnp.inf)
    m = jnp.max(s, axis=-1, keepdims=True)
    m = jnp.where(jnp.isfinite(m), m, 0.0)
    e = jnp.exp(s - m)
    d = jnp.sum(e, axis=-1, keepdims=True)
    return e / jnp.where(d > 0, d, 1.0)


def gqa_attend(qg, k, v, dist, mask, slopes_gr):
    s = jnp.einsum('...qgrd,...kgd->...qgrk', qg, k).astype(jnp.float32) * ATTN_SCALE
    s = s - slopes_gr[:, :, None] * dist[..., :, None, None, :]
    p = masked_softmax(s, mask[..., :, None, None, :])
    return jnp.einsum('...qgrk,...kgd->...qgrd', p.astype(v.dtype), v), p


def compress(k, pe, w1, w2):
    B, T = k.shape[:2]
    n16 = T // CMP_STRIDE
    chunks = k[:, :n16 * CMP_STRIDE].reshape(B, n16, CMP_STRIDE, *k.shape[2:])
    blocks = jnp.concatenate([chunks[:, :-1], chunks[:, 1:]], axis=2)
    blocks = blocks + pe[None, None, :, None, :]
    h = jax.nn.gelu(jnp.einsum('bnlgd,lde->bnge', blocks, w1))
    return jnp.einsum('bnge,ef->bngf', h, w2)


def cmp_to_sel(n_cmp, n_sel):
    c0 = np.arange(n_cmp)[:, None] * CMP_STRIDE
    s0 = np.arange(n_sel)[None, :] * SEL_BLOCK
    ov = np.clip(np.minimum(c0 + CMP_LEN, s0 + SEL_BLOCK) - np.maximum(c0, s0), 0, None)
    return jnp.asarray((ov / CMP_LEN).astype(np.float32))


def select_attend(qg, ks, vs, sel_idx, q_pos, slopes_gr):
    B, Tq = qg.shape[:2]
    tk = ks.shape[1]
    n_sel = -(-tk // SEL_BLOCK)
    pad = ((0, 0), (0, n_sel * SEL_BLOCK - tk), (0, 0), (0, 0))
    ks = jnp.pad(ks, pad)
    vs = jnp.pad(vs, pad)
    qb = Q_BLOCK if Tq % Q_BLOCK == 0 else Tq
    nb = Tq // qb
    topn = sel_idx.shape[-1]
    g_idx = jnp.arange(NSA_KV_HEADS)[None, :, None]

    def one(args):
        b, q_blk, idx_blk, pos_blk = args
        tok = (idx_blk[..., None] * SEL_BLOCK + jnp.arange(SEL_BLOCK)).reshape(qb, NSA_KV_HEADS, topn * SEL_BLOCK)
        k_g = ks[b][tok, g_idx]
        v_g = vs[b][tok, g_idx]
        dist = pos_blk[:, None, None] - tok
        s = jnp.einsum('qgrd,qgld->qgrl', q_blk, k_g).astype(jnp.float32) * ATTN_SCALE
        s = s - slopes_gr[None, :, :, None] * dist[:, :, None, :].astype(jnp.float32)
        p = masked_softmax(s, (dist >= 0)[:, :, None, :])
        return jnp.einsum('qgrl,qgld->qgrd', p.astype(v_g.dtype), v_g)

    xs = (jnp.repeat(jnp.arange(B), nb),
          qg.reshape(B * nb, qb, NSA_KV_HEADS, NSA_GROUP, HEAD_DIM),
          sel_idx.reshape(B * nb, qb, NSA_KV_HEADS, topn),
          jnp.tile(q_pos.reshape(nb, qb), (B, 1)))
    return lax.map(one, xs).reshape(qg.shape)


def nsa_cmp_sel(qg, kc, vc, ks, vs, q_pos, cmp_pe, cmp_w1, cmp_w2, slopes_gr):
    k_cmp = compress(kc, cmp_pe[0], cmp_w1[0], cmp_w2[0])
    v_cmp = compress(vc, cmp_pe[1], cmp_w1[1], cmp_w2[1])
    n_cmp = k_cmp.shape[1]
    cmp_end = jnp.arange(n_cmp) * CMP_STRIDE + (CMP_LEN - 1)
    dist = q_pos[:, None] - cmp_end[None, :]
    o_cmp, p_cmp = gqa_attend(qg, k_cmp, v_cmp, dist.astype(jnp.float32), dist >= 0, slopes_gr)
    n_sel = -(-kc.shape[1] // SEL_BLOCK)
    imp = jnp.einsum('bqgrn,ns->bqgs', p_cmp, cmp_to_sel(n_cmp, n_sel))
    blk = jnp.arange(n_sel)[None, :]
    cur = (q_pos // SEL_BLOCK)[:, None]
    valid = blk * SEL_BLOCK <= q_pos[:, None]
    forced = (blk == 0) | (blk == cur) | (blk == cur - 1)
    imp = jnp.where(forced[None, :, None, :], jnp.inf, jnp.where(valid[None, :, None, :], imp, -jnp.inf))
    _, sel_idx = lax.top_k(imp, min(SEL_TOPN, n_sel))
    o_sel = select_attend(qg, ks, vs, sel_idx, q_pos, slopes_gr)
    return o_cmp, o_sel


def window_prompt(qg, kw, vw, slopes_gr):
    B, T = qg.shape[:2]
    nb = T // Q_BLOCK
    span = WINDOW + Q_BLOCK
    pad = ((0, 0), (WINDOW, 0), (0, 0), (0, 0))
    idx = np.arange(nb)[:, None] * Q_BLOCK + np.arange(span)[None, :]
    kb = jnp.pad(kw, pad)[:, idx]
    vb = jnp.pad(vw, pad)[:, idx]
    k_pos = idx - WINDOW
    q_pos = np.arange(T).reshape(nb, Q_BLOCK)
    dist = q_pos[:, :, None] - k_pos[:, None, :]
    mask = (dist >= 0) & (dist <= WINDOW) & (k_pos[:, None, :] >= 0)
    o, _ = gqa_attend(qg.reshape(B, nb, Q_BLOCK, *qg.shape[2:]), kb, vb,
                      jnp.asarray(dist, jnp.float32), jnp.asarray(mask), slopes_gr)
    return o.reshape(qg.shape)


def diff_attend(q, k, v, dist, mask, slopes, lam):
    s = jnp.einsum('...qhcd,...khcd->...qhck', q, k).astype(jnp.float32) * ATTN_SCALE
    s = s - slopes[:, None, None] * dist[..., :, None, None, :]
    p = masked_softmax(s, mask[..., :, None, None, :])
    a = p[..., 0, :] - lam * p[..., 1, :]
    return jnp.einsum('...qhk,...khe->...qhe', a.astype(v.dtype), v)


def diff_prompt(q, k, v, slopes, lam):
    B, T = q.shape[:2]
    nb = T // Q_BLOCK
    qs = jnp.moveaxis(q.reshape(B, nb, Q_BLOCK, *q.shape[2:]), 1, 0)
    pos = jnp.arange(T).reshape(nb, Q_BLOCK)
    k_pos = jnp.arange(T)

    def blk(args):
        q_blk, p = args
        dist = p[:, None] - k_pos[None, :]
        return diff_attend(q_blk, k, v, dist.astype(jnp.float32), dist >= 0, slopes, lam)

    o = lax.map(blk, (qs, pos))
    return jnp.moveaxis(o, 0, 1).reshape(B, T, DIFF_HEADS, DIFF_VDIM)


def project(x, w_in):
    B, T = x.shape[:2]
    h = jnp.einsum('btd,de->bte', x, w_in)
    nq, kc, vc, ks, vs, kw, vw, gt, dq, dk, dv = jnp.split(h, SPLIT_POINTS, axis=-1)
    kv = lambda a: a.reshape(B, T, NSA_KV_HEADS, HEAD_DIM)
    gates = jax.nn.sigmoid(gt.astype(jnp.float32)).reshape(B, T, 3, NSA_KV_HEADS, NSA_GROUP, 1)
    return (nq.reshape(B, T, NSA_KV_HEADS, NSA_GROUP, HEAD_DIM), kv(kc), kv(vc), kv(ks), kv(vs), kv(kw), kv(vw), gates,
            dq.reshape(B, T, DIFF_HEADS, 2, HEAD_DIM), dk.reshape(B, T, DIFF_HEADS, 2, HEAD_DIM),
            dv.reshape(B, T, DIFF_HEADS, DIFF_VDIM))


def merge_out(gates, o_cmp, o_sel, o_win, o_diff, lam_init, subln_g, w_out, dtype):
    B, T = o_cmp.shape[:2]
    o_nsa = (gates[:, :, 0] * o_cmp + gates[:, :, 1] * o_sel + gates[:, :, 2] * o_win).astype(dtype)
    od = o_diff.astype(jnp.float32)
    od = od * lax.rsqrt(jnp.mean(jnp.square(od), -1, keepdims=True) + RMS_EPS) * subln_g * (1.0 - lam_init)
    o = jnp.concatenate([o_nsa.reshape(B, T, -1), od.astype(dtype).reshape(B, T, -1)], axis=-1)
    return jnp.einsum('btm,md->btd', o, w_out)


def gather_pages(pool, page_table):
    g = pool[page_table]
    return g.reshape(page_table.shape[0], -1, *pool.shape[2:])


def mixer_prompt(x, w_in, w_out, cmp_pe, cmp_w1, cmp_w2, lam, lam_init, subln_g):
    B, T, _ = x.shape
    nq, kc, vc, ks, vs, kw, vw, gates, dq, dk, dv = project(x, w_in)
    s_nsa = alibi_slopes(NSA_HEADS).reshape(NSA_KV_HEADS, NSA_GROUP)
    q_pos = jnp.arange(T)
    o_cmp, o_sel = nsa_cmp_sel(nq, kc, vc, ks, vs, q_pos, cmp_pe, cmp_w1, cmp_w2, s_nsa)
    o_win = window_prompt(nq, kw, vw, s_nsa)
    o_diff = diff_prompt(dq, dk, dv, alibi_slopes(DIFF_HEADS), lam)
    y = merge_out(gates, o_cmp, o_sel, o_win, o_diff, lam_init, subln_g, w_out, x.dtype)
    nsa_rows = jnp.stack([kc, vc, ks, vs], axis=2)
    diff_rows = jnp.stack([dk.reshape(B, T, DIFF_HEADS, DIFF_VDIM), dv], axis=2)
    win_rows = jnp.stack([kw, vw], axis=2)[:, T - min(WINDOW, T):]
    return y, nsa_rows, diff_rows, win_rows


def mixer_sample(x, pool_nsa, pool_diff, win_buf, page_table, w_in, w_out, cmp_pe, cmp_w1, cmp_w2, lam, lam_init, subln_g):
    B, T, _ = x.shape
    nq, kc, vc, ks, vs, kw, vw, gates, dq, dk, dv = project(x, w_in)
    s_nsa = alibi_slopes(NSA_HEADS).reshape(NSA_KV_HEADS, NSA_GROUP)
    q_pos = PAST_LEN + jnp.arange(T)
    nsa_rows = jnp.stack([kc, vc, ks, vs], axis=2)
    diff_rows = jnp.stack([dk.reshape(B, T, DIFF_HEADS, DIFF_VDIM), dv], axis=2)
    win_rows = jnp.stack([kw, vw], axis=2)
    nsa_full = jnp.concatenate([gather_pages(pool_nsa, page_table), nsa_rows], axis=1)
    diff_full = jnp.concatenate([gather_pages(pool_diff, page_table), diff_rows], axis=1)
    tk = nsa_full.shape[1]
    o_cmp, o_sel = nsa_cmp_sel(nq, nsa_full[:, :, 0], nsa_full[:, :, 1], nsa_full[:, :, 2], nsa_full[:, :, 3],
                               q_pos, cmp_pe, cmp_w1, cmp_w2, s_nsa)
    win_full = jnp.concatenate([win_buf, win_rows], axis=1)
    wbuf = win_buf.shape[1]
    w_pos = PAST_LEN - wbuf + jnp.arange(wbuf + T)
    wd = q_pos[:, None] - w_pos[None, :]
    o_win, _ = gqa_attend(nq, win_full[:, :, 0], win_full[:, :, 1], wd.astype(jnp.float32),
                          (wd >= 0) & (wd <= WINDOW), s_nsa)
    dd = q_pos[:, None] - jnp.arange(tk)[None, :]
    o_diff = diff_attend(dq, diff_full[:, :, 0].reshape(B, tk, DIFF_HEADS, 2, HEAD_DIM), diff_full[:, :, 1],
                         dd.astype(jnp.float32), dd >= 0, alibi_slopes(DIFF_HEADS), lam)
    y = merge_out(gates, o_cmp, o_sel, o_win, o_diff, lam_init, subln_g, w_out, x.dtype)
    new_win = win_full[:, win_full.shape[1] - min(WINDOW, PAST_LEN + T):]
    return y, nsa_rows, diff_rows, new_win


def moe(x, w_router, b_router, w_gate_up, b_gate_up, w_down, b_down):
    shp = x.shape
    xt = x.reshape(-1, D_MODEL)
    n = xt.shape[0]
    logits = (xt @ w_router + b_router).astype(jnp.float32)
    top_v, top_e = lax.top_k(logits, TOP_K)
    gate = jax.nn.softmax(top_v, axis=-1)
    e_flat = top_e.reshape(-1)
    tok_flat = jnp.repeat(jnp.arange(n), TOP_K)
    g_flat = gate.reshape(-1)
    order = jnp.argsort(e_flat)
    e_sorted = e_flat[order]
    counts = jnp.bincount(e_flat, length=N_EXPERTS)
    start = jnp.cumsum(counts) - counts
    padded = (counts + MOE_BLOCK - 1) // MOE_BLOCK * MOE_BLOCK
    pend = jnp.cumsum(padded)
    pstart = pend - padded
    slot = pstart[e_sorted] + (jnp.arange(n * TOP_K) - start[e_sorted])
    n_blocks = -(-(n * TOP_K) // MOE_BLOCK) + N_EXPERTS
    n_slots = n_blocks * MOE_BLOCK
    slot_tok = jnp.zeros((n_slots,), jnp.int32).at[slot].set(tok_flat[order])
    slot_gate = jnp.zeros((n_slots,), jnp.float32).at[slot].set(g_flat[order])
    blk_e = jnp.minimum(jnp.searchsorted(pend, jnp.arange(n_blocks) * MOE_BLOCK, side='right'), N_EXPERTS - 1)

    def run(args):
        e, toks, g = args
        h = xt[toks] @ w_gate_up[e] + b_gate_up[e]
        hg = jnp.minimum(h[:, 0::2], SWIGLU_LIMIT)
        hl = jnp.clip(h[:, 1::2], -SWIGLU_LIMIT, SWIGLU_LIMIT)
        a = (hl + 1.0) * hg * jax.nn.sigmoid(SWIGLU_ALPHA * hg)
        return (a @ w_down[e] + b_down[e]) * g[:, None].astype(x.dtype)

    out = lax.map(run, (blk_e, slot_tok.reshape(n_blocks, MOE_BLOCK), slot_gate.reshape(n_blocks, MOE_BLOCK)))
    y = jax.ops.segment_sum(out.reshape(n_slots, D_MODEL), slot_tok, num_segments=n)
    return y.reshape(shp)


def setup_inputs(seed: int = 0) -> dict:
    key = jax.random.key(seed)
    ks = jax.random.split(key, 24)
    f32 = jnp.float32
    nrm = lambda k, shape, s: jax.random.normal(k, shape, f32) * s
    n_pages = PAST_LEN // PAGE_SIZE
    n_pool = (DEC_BATCH * n_pages * 5 + 3) // 4
    wbuf = min(WINDOW, PAST_LEN)
    col_scale = jnp.asarray(np.concatenate(
        [np.full((n,), DEEPNORM_BETA if is_v else 1.0, np.float32) for n, is_v in zip(SPLITS, V_COLS)]))
    page_table = jax.random.permutation(ks[5], n_pool)[:DEC_BATCH * n_pages].reshape(DEC_BATCH, n_pages).astype(jnp.int32)
    return {
        "x_prompt": nrm(ks[0], (BATCH, SEQ, D_MODEL), 1.0),
        "x_sample": nrm(ks[1], (DEC_BATCH, DEC_SEQ, D_MODEL), 1.0),
        "cache_nsa_kv": nrm(ks[2], (DEPTH, n_pool, PAGE_SIZE, 4, NSA_KV_HEADS, HEAD_DIM), 1.0),
        "cache_diff_kv": nrm(ks[3], (DEPTH, n_pool, PAGE_SIZE, 2, DIFF_HEADS, DIFF_VDIM), 1.0),
        "state_nsa_win": nrm(ks[4], (DEPTH, DEC_BATCH, wbuf, 2, NSA_KV_HEADS, HEAD_DIM), 1.0),
        "page_table": page_table,
        "w_in": nrm(ks[6], (DEPTH, D_MODEL, D_IN), D_MODEL ** -0.5) * col_scale,
        "w_out": nrm(ks[7], (DEPTH, MIX_WIDTH, D_MODEL), MIX_WIDTH ** -0.5 * DEEPNORM_BETA),
        "cmp_pe": nrm(ks[8], (DEPTH, 2, CMP_LEN, HEAD_DIM), 0.02),
        "cmp_w1": nrm(ks[9], (DEPTH, 2, CMP_LEN, HEAD_DIM, HEAD_DIM), (CMP_LEN * HEAD_DIM) ** -0.5),
        "cmp_w2": nrm(ks[10], (DEPTH, 2, HEAD_DIM, HEAD_DIM), HEAD_DIM ** -0.5),
        "diff_lambda": nrm(ks[11], (DEPTH, 4, HEAD_DIM), 0.1),
        "diff_subln_g": 1.0 + nrm(ks[12], (DEPTH, DIFF_VDIM), 0.02),
        "ln1_g": 1.0 + nrm(ks[13], (DEPTH, D_MODEL), 0.02),
        "ln1_b": nrm(ks[14], (DEPTH, D_MODEL), 0.02),
        "ln2_g": 1.0 + nrm(ks[15], (DEPTH, D_MODEL), 0.02),
        "ln2_b": nrm(ks[16], (DEPTH, D_MODEL), 0.02),
        "w_router": nrm(ks[17], (DEPTH, D_MODEL, N_EXPERTS), D_MODEL ** -0.5),
        "b_router": nrm(ks[18], (DEPTH, N_EXPERTS), 0.01),
        "w_gate_up": nrm(ks[19], (DEPTH, N_EXPERTS, D_MODEL, 2 * D_FF), D_MODEL ** -0.5),
        "b_gate_up": nrm(ks[20], (DEPTH, N_EXPERTS, 2 * D_FF), 0.02),
        "w_down": nrm(ks[21], (DEPTH, N_EXPERTS, D_FF, D_MODEL), D_FF ** -0.5 * DEEPNORM_BETA),
        "b_down": nrm(ks[22], (DEPTH, N_EXPERTS, D_MODEL), 0.02),
    }


def reference(x_prompt, x_sample, cache_nsa_kv, cache_diff_kv, state_nsa_win, page_table, w_in, w_out, cmp_pe, cmp_w1,
              cmp_w2, diff_lambda, diff_subln_g, ln1_g, ln1_b, ln2_g, ln2_b, w_router, b_router, w_gate_up, b_gate_up,
              w_down, b_down):
    xp = x_prompt
    xs = x_sample
    nsa_p, diff_p, win_p, nsa_s, diff_s, win_s = [], [], [], [], [], []
    for l in range(DEPTH):
        lam0 = lambda_init(l)
        lv = diff_lambda[l].astype(jnp.float32)
        lam = jnp.exp(jnp.sum(lv[0] * lv[1])) - jnp.exp(jnp.sum(lv[2] * lv[3])) + lam0
        m, r_nsa, r_diff, r_win = mixer_prompt(xp, w_in[l], w_out[l], cmp_pe[l], cmp_w1[l], cmp_w2[l], lam, lam0,
                                               diff_subln_g[l])
        xp = layer_norm(DEEPNORM_ALPHA * xp + m, ln1_g[l], ln1_b[l])
        f = moe(xp, w_router[l], b_router[l], w_gate_up[l], b_gate_up[l], w_down[l], b_down[l])
        xp = layer_norm(DEEPNORM_ALPHA * xp + f, ln2_g[l], ln2_b[l])
        nsa_p.append(r_nsa)
        diff_p.append(r_diff)
        win_p.append(r_win)
        m, r_nsa, r_diff, r_win = mixer_sample(xs, cache_nsa_kv[l], cache_diff_kv[l], state_nsa_win[l], page_table,
                                               w_in[l], w_out[l], cmp_pe[l], cmp_w1[l], cmp_w2[l], lam, lam0,
                                               diff_subln_g[l])
        xs = layer_norm(DEEPNORM_ALPHA * xs + m, ln1_g[l], ln1_b[l])
        f = moe(xs, w_router[l], b_router[l], w_gate_up[l], b_gate_up[l], w_down[l], b_down[l])
        xs = layer_norm(DEEPNORM_ALPHA * xs + f, ln2_g[l], ln2_b[l])
        nsa_s.append(r_nsa)
        diff_s.append(r_diff)
        win_s.append(r_win)
    return (xp, xs, jnp.stack(nsa_p), jnp.stack(diff_p), jnp.stack(win_p), jnp.stack(nsa_s), jnp.stack(diff_s),
            jnp.stack(win_s))
```

```python
import functools
import math

import numpy as np
import jax
import jax.numpy as jnp
from jax import lax
from jax.experimental import pallas as pl
from jax.experimental.pallas import tpu as pltpu

F32 = jnp.float32
BF16 = jnp.bfloat16
HIGHEST = lax.Precision.HIGHEST

D_MODEL = 1024
HEAD_DIM = 64
NSA_HEADS = 8
NSA_KV_HEADS = 2
NSA_GROUP = NSA_HEADS // NSA_KV_HEADS
CMP_LEN = 32
CMP_STRIDE = 16
SEL_BLOCK = 64
SEL_TOPN = 16
WINDOW = 512
DIFF_HEADS = 4
DIFF_VDIM = 2 * HEAD_DIM
N_EXPERTS = 32
TOP_K = 4
SWIGLU_ALPHA = 1.702
SWIGLU_LIMIT = 7.0
LN_EPS = 1e-5
RMS_EPS = 1e-5
ATTN_SCALE = HEAD_DIM ** -0.5
DEPTH = 1
DEEPNORM_ALPHA = (2 * DEPTH) ** 0.25
PAGE_SIZE = 128

NSA_SLOPES = tuple(2.0 ** (-8.0 * (i + 1) / NSA_HEADS) for i in range(NSA_HEADS))
DIFF_SLOPES = tuple(2.0 ** (-8.0 * (i + 1) / DIFF_HEADS) for i in range(DIFF_HEADS))

VMEM_LIMIT_BYTES = 56 * 1024 * 1024
LANES = 128

_GATE_PAD = LANES - 3 * NSA_HEADS
_PROJ_GROUPS = ((0, 512), (512, 512), (1024, 256), (1280, 128), (1408, 512), (1920, 1024))
_PROJ_COLS = 2944


def _cparams(*sem):
    return pltpu.CompilerParams(dimension_semantics=sem, vmem_limit_bytes=VMEM_LIMIT_BYTES)


def _softmax_parts(s, mask):
    s = jnp.where(mask, s, -jnp.inf)
    m = jnp.max(s, axis=-1, keepdims=True)
    m = jnp.where(jnp.isfinite(m), m, 0.0)
    e = jnp.exp(s - m)
    d = jnp.sum(e, axis=-1, keepdims=True)
    return e, jnp.where(d > 0, d, 1.0)


def _dot_nt(a, b):
    return lax.dot_general(a, b, (((1,), (1,)), ((), ())), preferred_element_type=F32)


def _dot(a, b):
    return jnp.dot(a, b, preferred_element_type=F32)


def _pad_head(qh, g):
    z = jnp.zeros_like(qh)
    return jnp.concatenate([qh, z] if g == 0 else [z, qh], axis=1)


def _select_blocks(imp, blk, qpos, n_blk_lanes):
    cur = jnp.right_shift(qpos, int(math.log2(SEL_BLOCK)))
    valid = blk * SEL_BLOCK <= qpos
    forced = (blk == 0) | (blk == cur) | (blk == cur - 1)
    val = jnp.where(forced, jnp.inf, jnp.where(valid, imp, -jnp.inf))
    rank = jnp.zeros(val.shape, F32)
    for i in range(n_blk_lanes):
        ci = val[:, i:i + 1]
        beats = (ci > val) | ((ci == val) & (blk > i))
        rank = rank + jnp.where(beats, 1.0, 0.0)
    return jnp.where(rank < SEL_TOPN, 1.0, 0.0)


def _in_proj_kernel(x_ref, w_ref, q_ref, nsa_ref, win_ref, gt_ref, dq_ref, diff_ref):
    xb = x_ref[...].astype(BF16)
    outs = (q_ref, nsa_ref, win_ref, gt_ref, dq_ref, diff_ref)
    for (st, wd), o in zip(_PROJ_GROUPS, outs):
        r = _dot(xb, w_ref[:, st:st + wd])
        if o is gt_ref:
            r = jax.nn.sigmoid(r)
        o[...] = r


def _in_proj(x2d, w_pad):
    n = x2d.shape[0]
    tm = min(256, n)
    out_shape = tuple(jax.ShapeDtypeStruct((n, wd), F32) for _, wd in _PROJ_GROUPS)
    return pl.pallas_call(
        _in_proj_kernel,
        grid=(n // tm,),
        in_specs=[pl.BlockSpec((tm, D_MODEL), lambda i: (i, 0)),
                  pl.BlockSpec((D_MODEL, _PROJ_COLS), lambda i: (0, 0))],
        out_specs=tuple(pl.BlockSpec((tm, wd), lambda i: (i, 0)) for _, wd in _PROJ_GROUPS),
        out_shape=out_shape,
        compiler_params=_cparams("parallel"),
        name="in_proj",
    )(x2d, w_pad)


def _compress_kernel(blk_ref, pe_ref, w1_ref, w2_ref, o_ref):
    nb = blk_ref.shape[3]
    x = blk_ref[0, 0].reshape(2 * nb, CMP_LEN * HEAD_DIM) + pe_ref[0]
    h = jax.nn.gelu(_dot(x.astype(BF16), w1_ref[0]))
    o = _dot(h.astype(BF16), w2_ref[0])
    o_ref[0, 0, :, 0:HEAD_DIM] = o[0:nb]
    o_ref[0, 0, :, HEAD_DIM:2 * HEAD_DIM] = o[nb:2 * nb]


def _compress_prompt(nsa_rows, pe_flat, w1_flat, w2_b):
    B, T, _ = nsa_rows.shape
    n16 = T // CMP_STRIDE
    kv = nsa_rows[:, :, :2 * NSA_KV_HEADS * HEAD_DIM].reshape(B, n16, CMP_STRIDE, 2, NSA_KV_HEADS, HEAD_DIM)
    kv = kv.transpose(0, 3, 4, 1, 2, 5).reshape(B, 2, NSA_KV_HEADS, n16, CMP_STRIDE * HEAD_DIM)
    blocks = jnp.concatenate([kv[:, :, :, :-1], kv[:, :, :, 1:]], axis=-1)
    blocks = jnp.pad(blocks, ((0, 0), (0, 0), (0, 0), (0, 1), (0, 0)))
    kdim = CMP_LEN * HEAD_DIM
    return pl.pallas_call(
        _compress_kernel,
        grid=(B, 2),
        in_specs=[pl.BlockSpec((1, 1, NSA_KV_HEADS, n16, kdim), lambda b, k: (b, k, 0, 0, 0)),
                  pl.BlockSpec((1, 1, kdim), lambda b, k: (k, 0, 0)),
                  pl.BlockSpec((1, kdim, HEAD_DIM), lambda b, k: (k, 0, 0)),
                  pl.BlockSpec((1, HEAD_DIM, HEAD_DIM), lambda b, k: (k, 0, 0))],
        out_specs=pl.BlockSpec((1, 1, n16, 2 * HEAD_DIM), lambda b, k: (b, k, 0, 0)),
        out_shape=jax.ShapeDtypeStruct((B, 2, n16, 2 * HEAD_DIM), F32),
        compiler_params=_cparams("parallel", "parallel"),
        name="compress_prompt",
    )(blocks, pe_flat, w1_flat, w2_b)


def _cmp_attn_kernel(q_ref, kc_ref, vc_ref, m_ref, o_ref, sel_ref, *, tq, n_sel):
    i = pl.program_id(1)
    ncmp = kc_ref.shape[2]
    qpos = i * tq + lax.broadcasted_iota(jnp.int32, (tq, 1), 0)
    cmp_end = lax.broadcasted_iota(jnp.int32, (1, ncmp), 1) * CMP_STRIDE + (CMP_LEN - 1)
    dist = qpos - cmp_end
    mask = dist >= 0
    distf = dist.astype(F32)
    kcb = kc_ref[0, 0].astype(BF16)
    vcb = vc_ref[0, 0].astype(BF16)
    q = q_ref[0]
    psum = [None, None]
    for h in range(NSA_HEADS):
        g = h // NSA_GROUP
        qp = _pad_head(q[:, h * HEAD_DIM:(h + 1) * HEAD_DIM], g).astype(BF16)
        s = _dot_nt(qp, kcb) * ATTN_SCALE - NSA_SLOPES[h] * distf
        e, d = _softmax_parts(s, mask)
        p = e / d
        oh = _dot(p.astype(BF16), vcb)
        o_ref[0, :, h * HEAD_DIM:(h + 1) * HEAD_DIM] = oh[:, g * HEAD_DIM:(g + 1) * HEAD_DIM]
        psum[g] = p if psum[g] is None else psum[g] + p
    blk = lax.broadcasted_iota(jnp.int32, (tq, n_sel), 1)
    sels = []
    for g in range(NSA_KV_HEADS):
        imp = jnp.dot(psum[g], m_ref[...], precision=HIGHEST, preferred_element_type=F32)
        sels.append(_select_blocks(imp, blk, qpos, n_sel))
    sels.append(jnp.zeros((tq, LANES - NSA_KV_HEADS * n_sel), F32))
    sel_ref[0] = jnp.concatenate(sels, axis=1)


def _cmp_attn_prompt(q_nsa, kvc, m_mat, tq=256):
    B, T, _ = q_nsa.shape
    ncmp = kvc.shape[2]
    n_sel = m_mat.shape[1]
    return pl.pallas_call(
        functools.partial(_cmp_attn_kernel, tq=tq, n_sel=n_sel),
        grid=(B, T // tq),
        in_specs=[pl.BlockSpec((1, tq, 512), lambda b, i: (b, i, 0)),
                  pl.BlockSpec((1, 1, ncmp, LANES), lambda b, i: (b, 0, 0, 0)),
                  pl.BlockSpec((1, 1, ncmp, LANES), lambda b, i: (b, 1, 0, 0)),
                  pl.BlockSpec((ncmp, n_sel), lambda b, i: (0, 0))],
        out_specs=(pl.BlockSpec((1, tq, 512), lambda b, i: (b, i, 0)),
                   pl.BlockSpec((1, tq, LANES), lambda b, i: (b, i, 0))),
        out_shape=(jax.ShapeDtypeStruct((B, T, 512), F32), jax.ShapeDtypeStruct((B, T, LANES), F32)),
        compiler_params=_cparams("parallel", "parallel"),
        name="cmp_attn_prompt",
    )(q_nsa, kvc, kvc, m_mat)


def _sel_win_kernel(q_ref, ks_ref, vs_ref, kw_ref, vw_ref, sel_ref, e_ref, ocmp_ref, gt_ref, o_ref, *, tq, span):
    i = pl.program_id(1)
    T = ks_ref.shape[1]
    qpos = i * tq + lax.broadcasted_iota(jnp.int32, (tq, 1), 0)
    q = q_ref[0]
    gt = gt_ref[0]
    ocmp = ocmp_ref[0]
    kpos = lax.broadcasted_iota(jnp.int32, (1, T), 1)
    dist_s = qpos - kpos
    causal = dist_s >= 0
    dist_sf = dist_s.astype(F32)
    ksb = ks_ref[0].astype(BF16)
    vsb = vs_ref[0].astype(BF16)
    selb = sel_ref[0].astype(BF16)
    start = pl.multiple_of(jnp.maximum(i * tq - WINDOW, 0), tq)
    wpos = start + lax.broadcasted_iota(jnp.int32, (1, span), 1)
    dist_w = qpos - wpos
    mask_w = (dist_w >= 0) & (dist_w <= WINDOW)
    dist_wf = dist_w.astype(F32)
    kwb = kw_ref[0, pl.ds(start, span), :].astype(BF16)
    vwb = vw_ref[0, pl.ds(start, span), :].astype(BF16)
    for g in range(NSA_KV_HEADS):
        mask_s = (_dot(selb, e_ref[g]) > 0.5) & causal
        for r in range(NSA_GROUP):
            h = g * NSA_GROUP + r
            lo, hi = h * HEAD_DIM, (h + 1) * HEAD_DIM
            qp = _pad_head(q[:, lo:hi], g).astype(BF16)
            s = _dot_nt(qp, ksb) * ATTN_SCALE - NSA_SLOPES[h] * dist_sf
            e, d = _softmax_parts(s, mask_s)
            o_sel = _dot(e.astype(BF16), vsb)[:, g * HEAD_DIM:(g + 1) * HEAD_DIM] / d
            s = _dot_nt(qp, kwb) * ATTN_SCALE - NSA_SLOPES[h] * dist_wf
            e, d = _softmax_parts(s, mask_w)
            o_win = _dot(e.astype(BF16), vwb)[:, g * HEAD_DIM:(g + 1) * HEAD_DIM] / d
            o_ref[0, :, lo:hi] = (gt[:, h:h + 1] * ocmp[:, lo:hi]
                                  + gt[:, NSA_HEADS + h:NSA_HEADS + h + 1] * o_sel
                                  + gt[:, 2 * NSA_HEADS + h:2 * NSA_HEADS + h + 1] * o_win)


def _sel_win_prompt(q_nsa, nsa_rows, win_rows, sel, e_mat, o_cmp, gates, tq=128):
    B, T, _ = q_nsa.shape
    span = WINDOW + tq
    return pl.pallas_call(
        functools.partial(_sel_win_kernel, tq=tq, span=span),
        grid=(B, T // tq),
        in_specs=[pl.BlockSpec((1, tq, 512), lambda b, i: (b, i, 0)),
                  pl.BlockSpec((1, T, LANES), lambda b, i: (b, 0, 2)),
                  pl.BlockSpec((1, T, LANES), lambda b, i: (b, 0, 3)),
                  pl.BlockSpec((1, T, LANES), lambda b, i: (b, 0, 0)),
                  pl.BlockSpec((1, T, LANES), lambda b, i: (b, 0, 1)),
                  pl.BlockSpec((1, tq, LANES), lambda b, i: (b, i, 0)),
                  pl.BlockSpec((NSA_KV_HEADS, LANES, T), lambda b, i: (0, 0, 0)),
                  pl.BlockSpec((1, tq, 512), lambda b, i: (b, i, 0)),
                  pl.BlockSpec((1, tq, LANES), lambda b, i: (b, i, 0))],
        out_specs=pl.BlockSpec((1, tq, 512), lambda b, i: (b, i, 0)),
        out_shape=jax.ShapeDtypeStruct((B, T, 512), F32),
        compiler_params=_cparams("parallel", "parallel"),
        name="sel_win_prompt",
    )(q_nsa, nsa_rows, nsa_rows, win_rows, win_rows, sel, e_mat, o_cmp, gates)


def _diff_kernel(lam_ref, sl_ref, q_ref, k_ref, v_ref, g_ref, o_ref, *, tq, lam_init):
    i = pl.program_id(2)
    T = k_ref.shape[1]
    lam = lam_ref[0:1, 0:1]
    qpos = i * tq + lax.broadcasted_iota(jnp.int32, (tq, 1), 0)
    kpos = lax.broadcasted_iota(jnp.int32, (1, T), 1)
    dist = qpos - kpos
    mask = dist >= 0
    bias = sl_ref[0, :, 0:1] * dist.astype(F32)
    q = q_ref[0]
    kb = k_ref[0].astype(BF16)
    vb = v_ref[0].astype(BF16)
    outs = []
    for c in range(2):
        qp = _pad_head(q[:, c * HEAD_DIM:(c + 1) * HEAD_DIM], c).astype(BF16)
        s = _dot_nt(qp, kb) * ATTN_SCALE - bias
        e, d = _softmax_parts(s, mask)
        outs.append(_dot(e.astype(BF16), vb) / d)
    od = outs[0] - lam * outs[1]
    od = od * lax.rsqrt(jnp.mean(jnp.square(od), axis=-1, keepdims=True) + RMS_EPS) * g_ref[...] * (1.0 - lam_init)
    o_ref[0] = od


def _diff_prompt(lam_tile, dq, diff_rows, subln_g, lam_init, tq=256):
    B, T, _ = dq.shape
    slopes = jnp.asarray(np.tile(np.asarray(DIFF_SLOPES, np.float32)[:, None, None], (1, 1, LANES)))
    return pl.pallas_call(
        functools.partial(_diff_kernel, tq=tq, lam_init=lam_init),
        grid=(B, DIFF_HEADS, T // tq),
        in_specs=[pl.BlockSpec((8, LANES), lambda b, h, i: (0, 0)),
                  pl.BlockSpec((1, 1, LANES), lambda b, h, i: (h, 0, 0)),
                  pl.BlockSpec((1, tq, LANES), lambda b, h, i: (b, i, h)),
                  pl.BlockSpec((1, T, LANES), lambda b, h, i: (b, 0, h)),
                  pl.BlockSpec((1, T, LANES), lambda b, h, i: (b, 0, DIFF_HEADS + h)),
                  pl.BlockSpec((1, DIFF_VDIM), lambda b, h, i: (0, 0))],
        out_specs=pl.BlockSpec((1, tq, LANES), lambda b, h, i: (b, i, h)),
        out_shape=jax.ShapeDtypeStruct((B, T, DIFF_HEADS * DIFF_VDIM), F32),
        compiler_params=_cparams("parallel", "parallel", "parallel"),
        name="diff_prompt",
    )(lam_tile, slopes, dq, diff_rows, diff_rows, subln_g)


def _layer_norm(z, g, b):
    mu = jnp.mean(z, axis=-1, keepdims=True)
    zc = z - mu
    var = jnp.mean(jnp.square(zc), axis=-1, keepdims=True)
    return zc * lax.rsqrt(var + LN_EPS) * g + b


def _out_proj_kernel(on_ref, od_ref, x_ref, wa_ref, wb_ref, g_ref, b_ref, wr_ref, br_ref,
                     x1_ref, x1b_ref, sel_ref, gate_ref):
    y = _dot(on_ref[...].astype(BF16), wa_ref[...]) + _dot(od_ref[...].astype(BF16), wb_ref[...])
    x1 = _layer_norm(DEEPNORM_ALPHA * x_ref[...] + y, g_ref[...], b_ref[...])
    x1_ref[...] = x1
    x1b_ref[...] = x1.astype(BF16)
    logits = lax.dot_general(wr_ref[...], x1, (((1,), (1,)), ((), ())), precision=HIGHEST,
                             preferred_element_type=F32) + br_ref[...]
    eidx = lax.broadcasted_iota(jnp.int32, logits.shape, 0).astype(F32)
    sel = jnp.zeros(logits.shape, F32)
    vmax = None
    for k in range(TOP_K):
        cur = jnp.where(sel > 0.5, -jnp.inf, logits)
        m = jnp.max(cur, axis=0, keepdims=True)
        first = jnp.min(jnp.where(cur == m, eidx, float(N_EXPERTS)), axis=0, keepdims=True)
        sel = jnp.where(eidx == first, 1.0, sel)
        if k == 0:
            vmax = m
    ex = jnp.where(sel > 0.5, jnp.exp(logits - vmax), 0.0)
    sel_ref[...] = sel
    gate_ref[...] = ex / jnp.sum(ex, axis=0, keepdims=True)


def _out_proj_ln_router(o_nsa, o_diff, x2d, wa, wb, ln_g, ln_b, wr_t, br):
    n = x2d.shape[0]
    tm = min(256, n)
    row = lambda i: (i, 0)
    fix = lambda i: (0, 0)
    col = lambda i: (0, i)
    return pl.pallas_call(
        _out_proj_kernel,
        grid=(n // tm,),
        in_specs=[pl.BlockSpec((tm, 512), row), pl.BlockSpec((tm, 512), row), pl.BlockSpec((tm, D_MODEL), row),
                  pl.BlockSpec((512, D_MODEL), fix), pl.BlockSpec((512, D_MODEL), fix),
                  pl.BlockSpec((1, D_MODEL), fix), pl.BlockSpec((1, D_MODEL), fix),
                  pl.BlockSpec((N_EXPERTS, D_MODEL), fix), pl.BlockSpec((N_EXPERTS, 1), fix)],
        out_specs=(pl.BlockSpec((tm, D_MODEL), row), pl.BlockSpec((tm, D_MODEL), row),
                   pl.BlockSpec((N_EXPERTS, tm), col), pl.BlockSpec((N_EXPERTS, tm), col)),
        out_shape=(jax.ShapeDtypeStruct((n, D_MODEL), F32), jax.ShapeDtypeStruct((n, D_MODEL), BF16),
                   jax.ShapeDtypeStruct((N_EXPERTS, n), F32), jax.ShapeDtypeStruct((N_EXPERTS, n), F32)),
        compiler_params=_cparams("parallel"),
        name="out_proj_ln_router",
    )(o_nsa, o_diff, x2d, wa, wb, ln_g, ln_b, wr_t, br)


def _moe_pos_kernel(sel_ref, u_ref, pos_ref, cnt_ref):
    sel = sel_ref[...]
    rank = _dot(sel.astype(BF16), u_ref[...])
    pos_ref[...] = jnp.where(sel > 0.5, rank, -1.0)
    cnt_ref[0] = jnp.sum(sel, axis=1, keepdims=True)


def _moe_positions(sel_t, s_tile):
    n = sel_t.shape[1]
    ns = n // s_tile
    upper = jnp.asarray(np.triu(np.ones((s_tile, s_tile), np.float32), 1), BF16)
    return pl.pallas_call(
        _moe_pos_kernel,
        grid=(ns,),
        in_specs=[pl.BlockSpec((N_EXPERTS, s_tile), lambda s: (0, s)),
                  pl.BlockSpec((s_tile, s_tile), lambda s: (0, 0))],
        out_specs=(pl.BlockSpec((N_EXPERTS, s_tile), lambda s: (0, s)),
                   pl.BlockSpec((1, N_EXPERTS, 1), lambda s: (s, 0, 0))),
        out_shape=(jax.ShapeDtypeStruct((N_EXPERTS, n), F32), jax.ShapeDtypeStruct((ns, N_EXPERTS, 1), F32)),
        compiler_params=_cparams("parallel"),
        name="moe_positions",
    )(sel_t, upper)


def _moe_kernel(nch_ref, x_ref, pos_ref, gate_ref, wg_ref, wl_ref, bg_ref, bl_ref, wd_ref, bd_ref, y_ref, *, chunk):
    s = pl.program_id(0)
    e = pl.program_id(1)

    @pl.when(e == 0)
    def _():
        y_ref[...] = jnp.zeros_like(y_ref)

    pos = pos_ref[0, 0].astype(jnp.int32)
    gate = gate_ref[0, 0]
    slot0 = lax.broadcasted_iota(jnp.int32, (chunk, 1), 0)

    def body(j, carry):
        hit = pos == slot0 + j * chunk
        onehot = jnp.where(hit, 1.0, 0.0).astype(BF16)
        xs = _dot(onehot, x_ref[...]).astype(BF16)
        hg = jnp.minimum(_dot(xs, wg_ref[0]) + bg_ref[0], SWIGLU_LIMIT)
        hl = jnp.clip(_dot(xs, wl_ref[0]) + bl_ref[0], -SWIGLU_LIMIT, SWIGLU_LIMIT)
        a = (hl + 1.0) * hg * jax.nn.sigmoid(SWIGLU_ALPHA * hg)
        out = _dot(a.astype(BF16), wd_ref[0]) + bd_ref[0]
        gslot = jnp.sum(jnp.where(hit, gate, 0.0), axis=1, keepdims=True)
        outg = (out * gslot).astype(BF16)
        y_ref[...] += lax.dot_general(onehot, outg, (((0,), (0,)), ((), ())), preferred_element_type=F32)
        return carry

    lax.fori_loop(0, nch_ref[s * N_EXPERTS + e], body, 0)


def _moe(x1b, pos_t, gate_t, counts, wg, wl, bg, bl, wd, bd, s_tile, chunk):
    n = x1b.shape[0]
    ns = n // s_tile
    nch = ((counts.reshape(ns * N_EXPERTS) + (chunk - 1)) // chunk).astype(jnp.int32)
    pos4 = pos_t.reshape(N_EXPERTS, ns, 1, s_tile)
    gate4 = gate_t.reshape(N_EXPERTS, ns, 1, s_tile)
    wspec = pl.BlockSpec((1, D_MODEL, D_MODEL), lambda s, e, nch: (e, 0, 0))
    bspec = pl.BlockSpec((1, 1, D_MODEL), lambda s, e, nch: (e, 0, 0))
    rspec = pl.BlockSpec((1, 1, 1, s_tile), lambda s, e, nch: (e, s, 0, 0))
    return pl.pallas_call(
        functools.partial(_moe_kernel, chunk=chunk),
        grid_spec=pltpu.PrefetchScalarGridSpec(
            num_scalar_prefetch=1,
            grid=(ns, N_EXPERTS),
            in_specs=[pl.BlockSpec((s_tile, D_MODEL), lambda s, e, nch: (s, 0)),
                      rspec, rspec, wspec, wspec, bspec, bspec, wspec, bspec],
            out_specs=pl.BlockSpec((s_tile, D_MODEL), lambda s, e, nch: (s, 0))),
        out_shape=jax.ShapeDtypeStruct((n, D_MODEL), F32),
        compiler_params=_cparams("parallel", "arbitrary"),
        name="moe_experts",
    )(nch, x1b, pos4, gate4, wg, wl, bg, bl, wd, bd)


def _ln2_kernel(x_ref, f_ref, g_ref, b_ref, o_ref):
    o_ref[...] = _layer_norm(DEEPNORM_ALPHA * x_ref[...] + f_ref[...], g_ref[...], b_ref[...])


def _residual_ln(x1, f, g, b):
    n = x1.shape[0]
    tm = min(512, n)
    row = lambda i: (i, 0)
    fix = lambda i: (0, 0)
    return pl.pallas_call(
        _ln2_kernel,
        grid=(n // tm,),
        in_specs=[pl.BlockSpec((tm, D_MODEL), row), pl.BlockSpec((tm, D_MODEL), row),
                  pl.BlockSpec((1, D_MODEL), fix), pl.BlockSpec((1, D_MODEL), fix)],
        out_specs=pl.BlockSpec((tm, D_MODEL), row),
        out_shape=jax.ShapeDtypeStruct((n, D_MODEL), F32),
        compiler_params=_cparams("parallel"),
        name="residual_ln2",
    )(x1, f, g, b)


def _post_mixer(o_nsa, o_diff, x2d, p, s_tile, chunk):
    x1, x1b, sel_t, gate_t = _out_proj_ln_router(o_nsa, o_diff, x2d, p["wo_a"], p["wo_b"], p["ln1_g"], p["ln1_b"],
                                                 p["wr_t"], p["br"])
    pos_t, counts = _moe_positions(sel_t, s_tile)
    f = _moe(x1b, pos_t, gate_t, counts, p["wg"], p["wl"], p["bg"], p["bl"], p["wd"], p["bd"], s_tile, chunk)
    return _residual_ln(x1, f, p["ln2_g"], p["ln2_b"])


def _sample_kernel(pt_ref, qm_ref, qd_ref, gt_ref, nrow_ref, drow_ref, wrow_ref, win_ref,
                   w1_ref, pe_ref, w2_ref, m_ref, e_ref, r_ref, sn_ref, sd_ref, lam_ref, subg_ref, *rest,
                   n_pages, past_len, lam_init):
    nsa_pages = rest[:n_pages]
    diff_pages = rest[n_pages:2 * n_pages]
    onsa_ref, odiff_ref, c_ref = rest[2 * n_pages:]
    n16 = past_len // CMP_STRIDE
    cpp = PAGE_SIZE // CMP_STRIDE
    qpos = past_len

    def page_rows(pg, kind, first, count, step):
        return pg[pl.ds(4 * first + kind, count, stride=4 * step), :]

    for j in range(n_pages):
        for kind in range(2):
            for l in range(CMP_STRIDE):
                c_ref[kind, j * cpp:(j + 1) * cpp, l * LANES:(l + 1) * LANES] = (
                    page_rows(nsa_pages[j], kind, l, cpp, CMP_STRIDE))

    kv_cmp = []
    for kind in range(2):
        w1 = w1_ref[kind]
        cst = (_dot(pe_ref[kind, 0].astype(BF16), w1[:, 0:LANES]) + _dot(pe_ref[kind, 1].astype(BF16), w1[:, LANES:]))
        ab = _dot(c_ref[kind].astype(BF16), w1)
        nxt = pltpu.roll(ab[:, LANES:], n16 - 1, 0)
        hcmp = jax.nn.gelu(ab[:, 0:LANES] + nxt + cst[0:1])
        kv_cmp.append(_dot(hcmp.astype(BF16), w2_ref[kind]))

    qm = qm_ref[0]
    qmb = qm.astype(BF16)
    slope_n = sn_ref[:, 0:1]
    nrow = nrow_ref[0]
    wrow = wrow_ref[0]
    gt = gt_ref[0]

    def new_key_score(qrows, krow):
        return jnp.sum(qrows * krow, axis=1, keepdims=True) * ATTN_SCALE

    cmp_end = lax.broadcasted_iota(jnp.int32, (1, n16), 1) * CMP_STRIDE + (CMP_LEN - 1)
    dist_c = qpos - cmp_end
    s = _dot_nt(qmb, kv_cmp[0].astype(BF16)) * ATTN_SCALE - slope_n * dist_c.astype(F32)
    e, d = _softmax_parts(s, dist_c >= 0)
    p_cmp = e / d
    o_cmp = _dot(p_cmp.astype(BF16), kv_cmp[1].astype(BF16))
    pgrp = jnp.dot(r_ref[...], p_cmp, precision=HIGHEST, preferred_element_type=F32)
    imp = jnp.dot(pgrp, m_ref[...], precision=HIGHEST, preferred_element_type=F32)
    nbl = m_ref.shape[1]
    blk = lax.broadcasted_iota(jnp.int32, (NSA_HEADS, nbl), 1)
    sel = _select_blocks(imp, blk, jnp.full((NSA_HEADS, 1), qpos, jnp.int32), past_len // SEL_BLOCK + 1)
    mask_s = _dot(sel.astype(BF16), e_ref[...]) > 0.5

    kpos = lax.broadcasted_iota(jnp.int32, (1, past_len), 1)
    dist_k = (qpos - kpos).astype(F32)
    s = jnp.concatenate([_dot_nt(qmb, page_rows(pg, 2, 0, PAGE_SIZE, 1).astype(BF16)) for pg in nsa_pages], axis=1)
    s = jnp.where(mask_s, s * ATTN_SCALE - slope_n * dist_k, -jnp.inf)
    s_new = new_key_score(qm, nrow[:, 2 * LANES:3 * LANES])
    m = jnp.maximum(jnp.max(s, axis=1, keepdims=True), s_new)
    e = jnp.exp(s - m)
    e_new = jnp.exp(s_new - m)
    d = jnp.sum(e, axis=1, keepdims=True) + e_new
    eb = e.astype(BF16)
    acc = e_new * nrow[:, 3 * LANES:4 * LANES]
    for j, pg in enumerate(nsa_pages):
        acc = acc + _dot(eb[:, j * PAGE_SIZE:(j + 1) * PAGE_SIZE], page_rows(pg, 3, 0, PAGE_SIZE, 1).astype(BF16))
    o_sel = acc / d

    nwin = win_ref.shape[1]
    wpos = past_len - nwin + lax.broadcasted_iota(jnp.int32, (1, nwin), 1)
    dist_w = qpos - wpos
    mask_w = (dist_w >= 0) & (dist_w <= WINDOW)
    win = win_ref[0]
    s = _dot_nt(qmb, win[:, 0:LANES].astype(BF16)) * ATTN_SCALE - slope_n * dist_w.astype(F32)
    s = jnp.where(mask_w, s, -jnp.inf)
    s_new = new_key_score(qm, wrow[:, 0:LANES])
    m = jnp.maximum(jnp.max(s, axis=1, keepdims=True), s_new)
    e = jnp.exp(s - m)
    e_new = jnp.exp(s_new - m)
    d = jnp.sum(e, axis=1, keepdims=True) + e_new
    o_win = (_dot(e.astype(BF16), win[:, LANES:2 * LANES].astype(BF16)) + e_new * wrow[:, LANES:2 * LANES]) / d

    onsa_ref[0] = gt[:, 0:1] * o_cmp + gt[:, 1:2] * o_sel + gt[:, 2:3] * o_win

    qd = qd_ref[0]
    qdb = qd.astype(BF16)
    drow = drow_ref[0]
    kw = DIFF_HEADS * DIFF_VDIM
    s = jnp.concatenate([_dot_nt(qdb, pg[:, 0:kw].astype(BF16)) for pg in diff_pages], axis=1)
    s = s * ATTN_SCALE - sd_ref[:, 0:1] * dist_k
    s_new = new_key_score(qd, drow[:, 0:kw])
    m = jnp.maximum(jnp.max(s, axis=1, keepdims=True), s_new)
    e = jnp.exp(s - m)
    e_new = jnp.exp(s_new - m)
    d = jnp.sum(e, axis=1, keepdims=True) + e_new
    lam = lam_ref[0:1, 0:1]
    p = e / d
    p_new = e_new / d
    a = (p - lam * pltpu.roll(p, DIFF_HEADS, 0)).astype(BF16)
    a_new = p_new - lam * pltpu.roll(p_new, DIFF_HEADS, 0)
    acc = a_new * drow[:, kw:2 * kw]
    for j, pg in enumerate(diff_pages):
        acc = acc + _dot(a[:, j * PAGE_SIZE:(j + 1) * PAGE_SIZE], pg[:, kw:2 * kw].astype(BF16))
    lane_head = lax.broadcasted_iota(jnp.int32, acc.shape, 1) // DIFF_VDIM
    row = lax.broadcasted_iota(jnp.int32, acc.shape, 0)
    own = lane_head == row
    ms = jnp.sum(jnp.where(own, jnp.square(acc), 0.0), axis=1, keepdims=True) / DIFF_VDIM
    odiff_ref[0] = acc * lax.rsqrt(ms + RMS_EPS) * subg_ref[...] * (1.0 - lam_init)


def _sample_attention(page_table, qm, qd, gt8, nrow, drow, wrow, win_buf, pool_nsa, pool_diff,
                      w1ab, pe2, w2bd, m_mat, e_mat, r_mat, sn, sd, lam_tile, subg_tile, past_len, lam_init):
    B = qm.shape[0]
    n_pages = page_table.shape[1]
    nwin = win_buf.shape[1]
    n16 = past_len // CMP_STRIDE
    nsa_w = 4 * NSA_KV_HEADS * HEAD_DIM
    diff_w = 2 * DIFF_HEADS * DIFF_VDIM

    def per_b(shape):
        nd = len(shape)
        return pl.BlockSpec((1,) + shape, lambda b, pt: (b,) + (0,) * nd)

    def fixed(shape):
        nd = len(shape)
        return pl.BlockSpec(shape, lambda b, pt: (0,) * nd)

    def page_spec(pool, j):
        return pl.BlockSpec((None,) + pool.shape[1:], lambda b, pt: (pt[b, j], 0, 0))

    in_specs = [per_b((NSA_HEADS, LANES)), per_b((8, diff_w // 2)), per_b((NSA_HEADS, LANES)),
                per_b((1, nsa_w)), per_b((1, diff_w)), per_b((1, 2 * LANES)), per_b((nwin, 2 * LANES)),
                fixed(w1ab.shape), fixed(pe2.shape), fixed(w2bd.shape), fixed(m_mat.shape), fixed(e_mat.shape),
                fixed(r_mat.shape), fixed(sn.shape), fixed(sd.shape), fixed(lam_tile.shape), fixed(subg_tile.shape)]
    in_specs += [page_spec(pool_nsa, j) for j in range(n_pages)]
    in_specs += [page_spec(pool_diff, j) for j in range(n_pages)]
    return pl.pallas_call(
        functools.partial(_sample_kernel, n_pages=n_pages, past_len=past_len, lam_init=lam_init),
        grid_spec=pltpu.PrefetchScalarGridSpec(
            num_scalar_prefetch=1,
            grid=(B,),
            in_specs=in_specs,
            out_specs=(per_b((NSA_HEADS, LANES)), per_b((8, diff_w // 2))),
            scratch_shapes=[pltpu.VMEM((2, n16, CMP_STRIDE * LANES), F32)]),
        out_shape=(jax.ShapeDtypeStruct((B, NSA_HEADS, LANES), F32),
                   jax.ShapeDtypeStruct((B, 8, diff_w // 2), F32)),
        compiler_params=_cparams("arbitrary"),
        name="sample_attention",
    )(page_table, qm, qd, gt8, nrow, drow, wrow, win_buf, w1ab, pe2, w2bd, m_mat, e_mat, r_mat, sn, sd,
      lam_tile, subg_tile, *([pool_nsa] * n_pages), *([pool_diff] * n_pages))


def _cmp_to_sel(n_cmp, n_sel):
    c0 = np.arange(n_cmp)[:, None] * CMP_STRIDE
    s0 = np.arange(n_sel)[None, :] * SEL_BLOCK
    ov = np.clip(np.minimum(c0 + CMP_LEN, s0 + SEL_BLOCK) - np.maximum(c0, s0), 0, None)
    return (ov / CMP_LEN).astype(np.float32)


def _lambda_init(layer):
    return 0.8 - 0.6 * math.exp(-0.3 * layer)


def _prep_params(l, w_in, w_out, cmp_pe, cmp_w1, cmp_w2, diff_subln_g, ln1_g, ln1_b, ln2_g, ln2_b, w_router, b_router,
                 w_gate_up, b_gate_up, w_down, b_down):
    gate_end = 1280 + 3 * NSA_HEADS
    w_pad = jnp.concatenate([w_in[l][:, :gate_end], jnp.zeros((D_MODEL, _GATE_PAD), F32), w_in[l][:, gate_end:]],
                            axis=1).astype(BF16)
    wgu = w_gate_up[l]
    bgu = b_gate_up[l]
    return {
        "w_pad": w_pad,
        "wo_a": w_out[l][:512].astype(BF16), "wo_b": w_out[l][512:].astype(BF16),
        "ln1_g": ln1_g[l][None], "ln1_b": ln1_b[l][None], "ln2_g": ln2_g[l][None], "ln2_b": ln2_b[l][None],
        "wr_t": w_router[l].T, "br": b_router[l][:, None],
        "wg": wgu[:, :, 0::2].astype(BF16), "wl": wgu[:, :, 1::2].astype(BF16),
        "bg": bgu[:, None, 0::2], "bl": bgu[:, None, 1::2],
        "wd": w_down[l].astype(BF16), "bd": b_down[l][:, None, :],
        "subln_g": diff_subln_g[l][None],
    }


def _sample_cmp_params(cmp_pe, cmp_w1, cmp_w2):
    eye = jnp.eye(NSA_KV_HEADS, dtype=F32)
    halves = []
    for half in range(2):
        w = cmp_w1[:, half * CMP_STRIDE:(half + 1) * CMP_STRIDE]
        wb = jnp.einsum('klde,gh->klgdhe', w, eye)
        halves.append(wb.reshape(2, CMP_STRIDE * LANES, LANES))
    w1ab = jnp.concatenate(halves, axis=-1).astype(BF16)
    pe = cmp_pe.reshape(2, 2, CMP_STRIDE, 1, HEAD_DIM)
    pe2 = jnp.broadcast_to(pe, (2, 2, CMP_STRIDE, NSA_KV_HEADS, HEAD_DIM)).reshape(2, 2, 1, CMP_STRIDE * LANES)
    pe2 = jnp.broadcast_to(pe2, (2, 2, 8, CMP_STRIDE * LANES))
    w2bd = jnp.einsum('kde,gh->kgdhe', cmp_w2, eye).reshape(2, LANES, LANES).astype(BF16)
    return w1ab, pe2, w2bd


def _prompt_group(x, p, cmp_pe, cmp_w1, cmp_w2, lam_tile, lam_init):
    B, T, _ = x.shape
    x2d = x.reshape(B * T, D_MODEL)
    q_nsa, nsa_rows, win_rows, gates, dq, diff_rows = _in_proj(x2d, p["w_pad"])
    r3 = lambda a: a.reshape(B, T, a.shape[-1])
    q_nsa, nsa_rows3, win_rows3, gates, dq, diff_rows3 = map(r3, (q_nsa, nsa_rows, win_rows, gates, dq, diff_rows))
    kdim = CMP_LEN * HEAD_DIM
    kvc = _compress_prompt(nsa_rows3, cmp_pe.reshape(2, 1, kdim), cmp_w1.reshape(2, kdim, HEAD_DIM).astype(BF16),
                           cmp_w2.astype(BF16))
    n16 = T // CMP_STRIDE
    n_sel = -(-T // SEL_BLOCK)
    m_np = np.zeros((n16, n_sel), np.float32)
    m_np[:n16 - 1] = _cmp_to_sel(n16 - 1, n_sel)
    o_cmp, sel = _cmp_attn_prompt(q_nsa, kvc, jnp.asarray(m_np))
    e_np = np.zeros((NSA_KV_HEADS, LANES, T), np.float32)
    for g in range(NSA_KV_HEADS):
        e_np[g, g * n_sel + np.arange(T) // SEL_BLOCK, np.arange(T)] = 1.0
    o_nsa = _sel_win_prompt(q_nsa, nsa_rows3, win_rows3, sel, jnp.asarray(e_np, BF16), o_cmp, gates)
    o_diff = _diff_prompt(lam_tile, dq, diff_rows3, p["subln_g"], lam_init)
    y = _post_mixer(o_nsa.reshape(B * T, 512), o_diff.reshape(B * T, 512), x2d, p, s_tile=T, chunk=320)
    nwin = min(WINDOW, T)
    return (y.reshape(B, T, D_MODEL),
            nsa_rows3.reshape(B, T, 4, NSA_KV_HEADS, HEAD_DIM),
            diff_rows3.reshape(B, T, 2, DIFF_HEADS, DIFF_VDIM),
            win_rows3[:, T - nwin:].reshape(B, nwin, 2, NSA_KV_HEADS, HEAD_DIM))


def _sample_group(x, pool_nsa, pool_diff, win_buf, page_table, p, cmp_pe, cmp_w1, cmp_w2, lam_tile, lam_init):
    B, T, _ = x.shape
    past_len = page_table.shape[1] * PAGE_SIZE
    x2d = x.reshape(B, D_MODEL)
    q_nsa, nsa_rows, win_rows, gates, dq, diff_rows = _in_proj(x2d, p["w_pad"])
    qh = q_nsa.reshape(B, NSA_HEADS, HEAD_DIM)
    grp = (np.arange(NSA_HEADS) // NSA_GROUP)[None, :, None, None] == np.arange(NSA_KV_HEADS)[None, None, :, None]
    qm = (qh[:, :, None, :] * jnp.asarray(grp, F32)).reshape(B, NSA_HEADS, LANES)
    rows_hc = np.arange(8) % DIFF_HEADS * 2 + np.arange(8) // DIFF_HEADS
    slot = (np.arange(DIFF_HEADS * DIFF_VDIM) // HEAD_DIM)[None, :] == rows_hc[:, None]
    qd = dq[:, None, :] * jnp.asarray(slot, F32)[None]
    gt8 = jnp.pad(gates[:, :3 * NSA_HEADS].reshape(B, 3, NSA_HEADS).transpose(0, 2, 1), ((0, 0), (0, 0), (0, LANES - 3)))
    nwin = win_buf.shape[1]
    win2 = win_buf.reshape(B, nwin, 2 * LANES)
    n_pool = pool_nsa.shape[0]
    w1ab, pe2, w2bd = _sample_cmp_params(cmp_pe, cmp_w1, cmp_w2)
    n16 = past_len // CMP_STRIDE
    n_sel = past_len // SEL_BLOCK + 1
    m_np = np.zeros((n16, 64), np.float32)
    m_np[:n16 - 1, :n_sel] = _cmp_to_sel(n16 - 1, n_sel)
    e_np = np.zeros((64, past_len), np.float32)
    e_np[np.arange(past_len) // SEL_BLOCK, np.arange(past_len)] = 1.0
    r_np = (np.arange(8)[:, None] // NSA_GROUP == np.arange(8)[None, :] // NSA_GROUP).astype(np.float32)
    sn = jnp.asarray(np.tile(np.asarray(NSA_SLOPES, np.float32)[:, None], (1, LANES)))
    sd = jnp.asarray(np.tile(np.asarray(DIFF_SLOPES, np.float32)[np.arange(8) % DIFF_HEADS, None], (1, LANES)))
    subg_tile = jnp.tile(p["subln_g"], (1, DIFF_HEADS))
    o_nsa8, o_diff8 = _sample_attention(
        page_table, qm, qd, gt8, nsa_rows[:, None, :], diff_rows[:, None, :], win_rows[:, None, :], win2,
        pool_nsa.reshape(n_pool, PAGE_SIZE * 4, LANES), pool_diff.reshape(n_pool, PAGE_SIZE, -1),
        w1ab, pe2, w2bd, jnp.asarray(m_np), jnp.asarray(e_np, BF16), jnp.asarray(r_np), sn, sd, lam_tile, subg_tile,
        past_len, lam_init)
    o8 = o_nsa8.reshape(B, NSA_KV_HEADS, NSA_GROUP, NSA_KV_HEADS, HEAD_DIM)
    o_nsa = jnp.stack([o8[:, g, :, g] for g in range(NSA_KV_HEADS)], axis=1).reshape(B, 512)
    d8 = o_diff8[:, :DIFF_HEADS].reshape(B, DIFF_HEADS, DIFF_HEADS, DIFF_VDIM)
    o_diff = jnp.stack([d8[:, h, h] for h in range(DIFF_HEADS)], axis=1).reshape(B, 512)
    y = _post_mixer(o_nsa, o_diff, x2d, p, s_tile=B, chunk=B)
    new_win = jnp.concatenate([win2, win_rows[:, None, :]], axis=1)
    new_win = new_win[:, new_win.shape[1] - min(WINDOW, past_len + T):]
    return (y.reshape(B, T, D_MODEL),
            nsa_rows.reshape(B, T, 4, NSA_KV_HEADS, HEAD_DIM),
            diff_rows.reshape(B, T, 2, DIFF_HEADS, DIFF_VDIM),
            new_win.reshape(B, new_win.shape[1], 2, NSA_KV_HEADS, HEAD_DIM))


def kernel(x_prompt, x_sample, cache_nsa_kv, cache_diff_kv, state_nsa_win, page_table, w_in, w_out, cmp_pe, cmp_w1,
           cmp_w2, diff_lambda, diff_subln_g, ln1_g, ln1_b, ln2_g, ln2_b, w_router, b_router, w_gate_up, b_gate_up,
           w_down, b_down):
    depth = w_in.shape[0]
    xp, xs = x_prompt, x_sample
    outs = [[] for _ in range(6)]
    for l in range(depth):
        lam0 = _lambda_init(l)
        lv = diff_lambda[l].astype(F32)
        lam = jnp.exp(jnp.sum(lv[0] * lv[1])) - jnp.exp(jnp.sum(lv[2] * lv[3])) + lam0
        lam_tile = jnp.full((8, LANES), lam, F32)
        p = _prep_params(l, w_in, w_out, cmp_pe, cmp_w1, cmp_w2, diff_subln_g, ln1_g, ln1_b, ln2_g, ln2_b, w_router,
                         b_router, w_gate_up, b_gate_up, w_down, b_down)
        xp, r_nsa, r_diff, r_win = _prompt_group(xp, p, cmp_pe[l], cmp_w1[l], cmp_w2[l], lam_tile, lam0)
        xs, s_nsa, s_diff, s_win = _sample_group(xs, cache_nsa_kv[l], cache_diff_kv[l], state_nsa_win[l], page_table,
                                                 p, cmp_pe[l], cmp_w1[l], cmp_w2[l], lam_tile, lam0)
        for lst, v in zip(outs, (r_nsa, r_diff, r_win, s_nsa, s_diff, s_win)):
            lst.append(v)
    return (xp, xs) + tuple(jnp.stack(o) for o in outs)
```

```python
import functools
import math

import numpy as np
import jax
import jax.numpy as jnp
from jax import lax
from jax.experimental import pallas as pl
from jax.experimental.pallas import tpu as pltpu

F32 = jnp.float32
BF16 = jnp.bfloat16
HIGHEST = lax.Precision.HIGHEST

D_MODEL = 1024
HEAD_DIM = 64
NSA_HEADS = 8
NSA_KV_HEADS = 2
NSA_GROUP = NSA_HEADS // NSA_KV_HEADS
CMP_LEN = 32
CMP_STRIDE = 16
SEL_BLOCK = 64
SEL_TOPN = 16
WINDOW = 512
DIFF_HEADS = 4
DIFF_VDIM = 2 * HEAD_DIM
N_EXPERTS = 32
TOP_K = 4
SWIGLU_ALPHA = 1.702
SWIGLU_LIMIT = 7.0
LN_EPS = 1e-5
RMS_EPS = 1e-5
ATTN_SCALE = HEAD_DIM ** -0.5
DEPTH = 1
DEEPNORM_ALPHA = (2 * DEPTH) ** 0.25
PAGE_SIZE = 128

NSA_SLOPES = tuple(2.0 ** (-8.0 * (i + 1) / NSA_HEADS) for i in range(NSA_HEADS))
DIFF_SLOPES = tuple(2.0 ** (-8.0 * (i + 1) / DIFF_HEADS) for i in range(DIFF_HEADS))

VMEM_LIMIT_BYTES = 56 * 1024 * 1024
LANES = 128
MXU_DIM = 256

NSA_W = 4 * NSA_KV_HEADS * HEAD_DIM
WIN_W = 2 * NSA_KV_HEADS * HEAD_DIM
DIFF_W = 2 * DIFF_HEADS * DIFF_VDIM
DIFF_ROWS = DIFF_W // LANES
_GATE_PAD = LANES - 3 * NSA_HEADS
_Q0, _NSA0, _WIN0, _GT0, _DQ0, _DIFF0, _PROJ_ROWS = 0, 512, 1024, 1280, 1408, 1920, 2944


def _cparams(*sem):
    return pltpu.CompilerParams(dimension_semantics=sem, vmem_limit_bytes=VMEM_LIMIT_BYTES)


def _softmax_parts(s, mask):
    s = jnp.where(mask, s, -jnp.inf)
    m = jnp.max(s, axis=-1, keepdims=True)
    m = jnp.where(jnp.isfinite(m), m, 0.0)
    e = jnp.exp(s - m)
    d = jnp.sum(e, axis=-1, keepdims=True)
    return e, jnp.where(d > 0, d, 1.0)


def _dot_nt(a, b):
    return lax.dot_general(a, b, (((1,), (1,)), ((), ())), preferred_element_type=F32)


def _dot(a, b):
    return jnp.dot(a, b, preferred_element_type=F32)


def _pad_head(qh, g):
    z = jnp.zeros_like(qh)
    return jnp.concatenate([qh, z] if g == 0 else [z, qh], axis=1)


def _select_blocks(imp, blk, qpos, n_blk_lanes):
    cur = jnp.right_shift(qpos, int(math.log2(SEL_BLOCK)))
    valid = blk * SEL_BLOCK <= qpos
    forced = (blk == 0) | (blk == cur) | (blk == cur - 1)
    val = jnp.where(forced, jnp.inf, jnp.where(valid, imp, -jnp.inf))
    rank = jnp.zeros(val.shape, F32)
    for i in range(n_blk_lanes):
        ci = val[:, i:i + 1]
        beats = (ci > val) | ((ci == val) & (blk > i))
        rank = rank + jnp.where(beats, 1.0, 0.0)
    return jnp.where(rank < SEL_TOPN, 1.0, 0.0)


def _in_proj_kernel(x_ref, w_ref, q_ref, nsa_ref, nsat_ref, nsatb_ref, win_ref, wint_ref, wintb_ref, gt_ref, dq_ref,
                    diff_ref, diffb_ref):
    tm = x_ref.shape[0]
    xb = x_ref[...].astype(BF16)
    q_ref[...] = _dot_nt(xb, w_ref[_Q0:_NSA0, :])
    nsa_ref[...] = _dot_nt(xb, w_ref[_NSA0:_WIN0, :])
    r = _dot_nt(w_ref[_NSA0:_WIN0, :], xb)
    nsat_ref[0] = r
    nsatb_ref[0] = r.astype(BF16)
    win_ref[...] = _dot_nt(xb, w_ref[_WIN0:_GT0, :])
    r = _dot_nt(w_ref[_WIN0:_GT0, :], xb)
    wint_ref[0] = r
    wintb_ref[0] = r.astype(BF16)
    gt_ref[...] = jax.nn.sigmoid(_dot_nt(xb, w_ref[_GT0:_DQ0, :]))
    dq_ref[...] = _dot_nt(xb, w_ref[_DQ0:_DIFF0, :])
    r = _dot_nt(xb, w_ref[_DIFF0:_PROJ_ROWS, :])
    diffb_ref[...] = r.astype(BF16)
    for j in range(DIFF_ROWS):
        diff_ref[pl.ds(j, tm, stride=DIFF_ROWS), :] = r[:, j * LANES:(j + 1) * LANES]


def _in_proj(x2d, w_t, batch):
    n = x2d.shape[0]
    t = n // batch
    tm = min(256, t)
    nt = t // tm
    row = lambda b, i: (b * nt + i, 0)
    tr = lambda b, i: (b, 0, i)
    tok = lambda w: pl.BlockSpec((tm, w), row)
    return pl.pallas_call(
        _in_proj_kernel,
        grid=(batch, nt),
        in_specs=[pl.BlockSpec((tm, D_MODEL), row),
                  pl.BlockSpec((_PROJ_ROWS, D_MODEL), lambda b, i: (0, 0))],
        out_specs=(tok(512), tok(NSA_W), pl.BlockSpec((1, NSA_W, tm), tr), pl.BlockSpec((1, NSA_W, tm), tr),
                   tok(WIN_W), pl.BlockSpec((1, WIN_W, tm), tr), pl.BlockSpec((1, WIN_W, tm), tr),
                   tok(LANES), tok(512), pl.BlockSpec((tm * DIFF_ROWS, LANES), row), tok(DIFF_W)),
        out_shape=(jax.ShapeDtypeStruct((n, 512), F32), jax.ShapeDtypeStruct((n, NSA_W), F32),
                   jax.ShapeDtypeStruct((batch, NSA_W, t), F32), jax.ShapeDtypeStruct((batch, NSA_W, t), BF16),
                   jax.ShapeDtypeStruct((n, WIN_W), F32),
                   jax.ShapeDtypeStruct((batch, WIN_W, t), F32), jax.ShapeDtypeStruct((batch, WIN_W, t), BF16),
                   jax.ShapeDtypeStruct((n, LANES), F32), jax.ShapeDtypeStruct((n, 512), F32),
                   jax.ShapeDtypeStruct((n * DIFF_ROWS, LANES), F32), jax.ShapeDtypeStruct((n, DIFF_W), BF16)),
        compiler_params=_cparams("parallel", "parallel"),
        name="in_proj",
    )(x2d, w_t)


def _compress_kernel(blk_ref, pe_ref, w1_ref, w2_ref, o_ref):
    nb = blk_ref.shape[3]
    x = blk_ref[0, 0].reshape(2 * nb, CMP_LEN * HEAD_DIM) + pe_ref[0]
    h = jax.nn.gelu(_dot(x.astype(BF16), w1_ref[0]))
    o = _dot(h.astype(BF16), w2_ref[0])
    o_ref[0, 0, :, 0:HEAD_DIM] = o[0:nb]
    o_ref[0, 0, :, HEAD_DIM:2 * HEAD_DIM] = o[nb:2 * nb]


def _compress_prompt(nsa_rows, pe_flat, w1_flat, w2_b):
    B, T, _ = nsa_rows.shape
    n16 = T // CMP_STRIDE
    kv = nsa_rows[:, :, :2 * NSA_KV_HEADS * HEAD_DIM].reshape(B, n16, CMP_STRIDE, 2, NSA_KV_HEADS, HEAD_DIM)
    kv = kv.transpose(0, 3, 4, 1, 2, 5).reshape(B, 2, NSA_KV_HEADS, n16, CMP_STRIDE * HEAD_DIM)
    blocks = jnp.concatenate([kv[:, :, :, :-1], kv[:, :, :, 1:]], axis=-1)
    blocks = jnp.pad(blocks, ((0, 0), (0, 0), (0, 0), (0, 1), (0, 0)))
    kdim = CMP_LEN * HEAD_DIM
    return pl.pallas_call(
        _compress_kernel,
        grid=(B, 2),
        in_specs=[pl.BlockSpec((1, 1, NSA_KV_HEADS, n16, kdim), lambda b, k: (b, k, 0, 0, 0)),
                  pl.BlockSpec((1, 1, kdim), lambda b, k: (k, 0, 0)),
                  pl.BlockSpec((1, kdim, HEAD_DIM), lambda b, k: (k, 0, 0)),
                  pl.BlockSpec((1, HEAD_DIM, HEAD_DIM), lambda b, k: (k, 0, 0))],
        out_specs=pl.BlockSpec((1, 1, n16, 2 * HEAD_DIM), lambda b, k: (b, k, 0, 0)),
        out_shape=jax.ShapeDtypeStruct((B, 2, n16, 2 * HEAD_DIM), F32),
        compiler_params=_cparams("parallel", "parallel"),
        name="compress_prompt",
    )(blocks, pe_flat, w1_flat, w2_b)


def _cmp_attn_kernel(q_ref, kc_ref, vc_ref, m_ref, o_ref, sel_ref, *, tq, n_sel):
    i = pl.program_id(1)
    ncmp = kc_ref.shape[2]
    qpos = i * tq + lax.broadcasted_iota(jnp.int32, (tq, 1), 0)
    cmp_end = lax.broadcasted_iota(jnp.int32, (1, ncmp), 1) * CMP_STRIDE + (CMP_LEN - 1)
    dist = qpos - cmp_end
    mask = dist >= 0
    distf = dist.astype(F32)
    kcb = kc_ref[0, 0].astype(BF16)
    vcb = vc_ref[0, 0].astype(BF16)
    q = q_ref[0]
    psum = [None, None]
    for h in range(NSA_HEADS):
        g = h // NSA_GROUP
        qp = _pad_head(q[:, h * HEAD_DIM:(h + 1) * HEAD_DIM], g).astype(BF16)
        s = _dot_nt(qp, kcb) * ATTN_SCALE - NSA_SLOPES[h] * distf
        e, d = _softmax_parts(s, mask)
        p = e / d
        oh = _dot(p.astype(BF16), vcb)
        o_ref[0, :, h * HEAD_DIM:(h + 1) * HEAD_DIM] = oh[:, g * HEAD_DIM:(g + 1) * HEAD_DIM]
        psum[g] = p if psum[g] is None else psum[g] + p
    blk = lax.broadcasted_iota(jnp.int32, (tq, n_sel), 1)
    sels = []
    for g in range(NSA_KV_HEADS):
        imp = jnp.dot(psum[g], m_ref[...], precision=HIGHEST, preferred_element_type=F32)
        sels.append(_select_blocks(imp, blk, qpos, n_sel))
    sels.append(jnp.zeros((tq, LANES - NSA_KV_HEADS * n_sel), F32))
    sel_ref[0] = jnp.concatenate(sels, axis=1)


def _cmp_attn_prompt(q_nsa, kvc, m_mat, tq=256):
    B, T, _ = q_nsa.shape
    ncmp = kvc.shape[2]
    n_sel = m_mat.shape[1]
    return pl.pallas_call(
        functools.partial(_cmp_attn_kernel, tq=tq, n_sel=n_sel),
        grid=(B, T // tq),
        in_specs=[pl.BlockSpec((1, tq, 512), lambda b, i: (b, i, 0)),
                  pl.BlockSpec((1, 1, ncmp, LANES), lambda b, i: (b, 0, 0, 0)),
                  pl.BlockSpec((1, 1, ncmp, LANES), lambda b, i: (b, 1, 0, 0)),
                  pl.BlockSpec((ncmp, n_sel), lambda b, i: (0, 0))],
        out_specs=(pl.BlockSpec((1, tq, 512), lambda b, i: (b, i, 0)),
                   pl.BlockSpec((1, tq, LANES), lambda b, i: (b, i, 0))),
        out_shape=(jax.ShapeDtypeStruct((B, T, 512), F32), jax.ShapeDtypeStruct((B, T, LANES), F32)),
        compiler_params=_cparams("parallel", "parallel"),
        name="cmp_attn_prompt",
    )(q_nsa, kvc, kvc, m_mat)


def _sel_win_kernel(q_ref, ks_ref, vs_ref, kw_ref, vw_ref, sel_ref, e_ref, ocmp_ref, gt_ref, o_ref, *, tq, span):
    i = pl.program_id(1)
    T = ks_ref.shape[2]
    qpos = i * tq + lax.broadcasted_iota(jnp.int32, (tq, 1), 0)
    q = q_ref[0]
    gt = gt_ref[0]
    ocmp = ocmp_ref[0]
    kpos = lax.broadcasted_iota(jnp.int32, (1, T), 1)
    dist_s = qpos - kpos
    causal = dist_s >= 0
    dist_sf = dist_s.astype(F32)
    ksb = ks_ref[0]
    vsb = vs_ref[0]
    selb = sel_ref[0].astype(BF16)
    start = pl.multiple_of(jnp.maximum(i * tq - WINDOW, 0), LANES)
    wpos = start + lax.broadcasted_iota(jnp.int32, (1, span), 1)
    dist_w = qpos - wpos
    mask_w = (dist_w >= 0) & (dist_w <= WINDOW)
    dist_wf = dist_w.astype(F32)
    kwb = kw_ref[0, :, pl.ds(start, span)]
    vwb = vw_ref[0, :, pl.ds(start, span)]
    for g in range(NSA_KV_HEADS):
        mask_s = (_dot(selb, e_ref[g]) > 0.5) & causal
        for r in range(NSA_GROUP):
            h = g * NSA_GROUP + r
            lo, hi = h * HEAD_DIM, (h + 1) * HEAD_DIM
            qp = _pad_head(q[:, lo:hi], g).astype(BF16)
            s = _dot(qp, ksb) * ATTN_SCALE - NSA_SLOPES[h] * dist_sf
            e, d = _softmax_parts(s, mask_s)
            o_sel = _dot_nt(e.astype(BF16), vsb)[:, g * HEAD_DIM:(g + 1) * HEAD_DIM] / d
            s = _dot(qp, kwb) * ATTN_SCALE - NSA_SLOPES[h] * dist_wf
            e, d = _softmax_parts(s, mask_w)
            o_win = _dot_nt(e.astype(BF16), vwb)[:, g * HEAD_DIM:(g + 1) * HEAD_DIM] / d
            o_ref[0, :, lo:hi] = (gt[:, h:h + 1] * ocmp[:, lo:hi]
                                  + gt[:, NSA_HEADS + h:NSA_HEADS + h + 1] * o_sel
                                  + gt[:, 2 * NSA_HEADS + h:2 * NSA_HEADS + h + 1] * o_win)


def _sel_win_prompt(q_nsa, nsa_t, win_t, sel, e_mat, o_cmp, gates, tq=128):
    B, T, _ = q_nsa.shape
    span = WINDOW + tq
    kv = lambda j: pl.BlockSpec((1, LANES, T), lambda b, i: (b, j, 0))
    return pl.pallas_call(
        functools.partial(_sel_win_kernel, tq=tq, span=span),
        grid=(B, T // tq),
        in_specs=[pl.BlockSpec((1, tq, 512), lambda b, i: (b, i, 0)),
                  kv(2), kv(3), kv(0), kv(1),
                  pl.BlockSpec((1, tq, LANES), lambda b, i: (b, i, 0)),
                  pl.BlockSpec((NSA_KV_HEADS, LANES, T), lambda b, i: (0, 0, 0)),
                  pl.BlockSpec((1, tq, 512), lambda b, i: (b, i, 0)),
                  pl.BlockSpec((1, tq, LANES), lambda b, i: (b, i, 0))],
        out_specs=pl.BlockSpec((1, tq, 512), lambda b, i: (b, i, 0)),
        out_shape=jax.ShapeDtypeStruct((B, T, 512), F32),
        compiler_params=_cparams("parallel", "parallel"),
        name="sel_win_prompt",
    )(q_nsa, nsa_t, nsa_t, win_t, win_t, sel, e_mat, o_cmp, gates)


def _diff_kernel(lam_ref, sl_ref, q_ref, k_ref, v_ref, g_ref, o_ref, *, tq, lam_init):
    i = pl.program_id(2)
    T = k_ref.shape[1]
    lam = lam_ref[0:1, 0:1]
    qpos = i * tq + lax.broadcasted_iota(jnp.int32, (tq, 1), 0)
    kpos = lax.broadcasted_iota(jnp.int32, (1, T), 1)
    dist = qpos - kpos
    mask = dist >= 0
    bias = sl_ref[0, :, 0:1] * dist.astype(F32)
    q = q_ref[0]
    kb = k_ref[0]
    vb = v_ref[0]
    outs = []
    for c in range(2):
        qp = _pad_head(q[:, c * HEAD_DIM:(c + 1) * HEAD_DIM], c).astype(BF16)
        s = _dot_nt(qp, kb) * ATTN_SCALE - bias
        e, d = _softmax_parts(s, mask)
        outs.append(_dot(e.astype(BF16), vb) / d)
    od = outs[0] - lam * outs[1]
    od = od * lax.rsqrt(jnp.mean(jnp.square(od), axis=-1, keepdims=True) + RMS_EPS) * g_ref[...] * (1.0 - lam_init)
    o_ref[0] = od


def _diff_prompt(lam_tile, dq, diff_b, subln_g, lam_init, tq=256):
    B, T, _ = dq.shape
    slopes = jnp.asarray(np.tile(np.asarray(DIFF_SLOPES, np.float32)[:, None, None], (1, 1, LANES)))
    return pl.pallas_call(
        functools.partial(_diff_kernel, tq=tq, lam_init=lam_init),
        grid=(B, DIFF_HEADS, T // tq),
        in_specs=[pl.BlockSpec((8, LANES), lambda b, h, i: (0, 0)),
                  pl.BlockSpec((1, 1, LANES), lambda b, h, i: (h, 0, 0)),
                  pl.BlockSpec((1, tq, LANES), lambda b, h, i: (b, i, h)),
                  pl.BlockSpec((1, T, LANES), lambda b, h, i: (b, 0, h)),
                  pl.BlockSpec((1, T, LANES), lambda b, h, i: (b, 0, DIFF_HEADS + h)),
                  pl.BlockSpec((1, DIFF_VDIM), lambda b, h, i: (0, 0))],
        out_specs=pl.BlockSpec((1, tq, LANES), lambda b, h, i: (b, i, h)),
        out_shape=jax.ShapeDtypeStruct((B, T, DIFF_HEADS * DIFF_VDIM), F32),
        compiler_params=_cparams("parallel", "parallel", "parallel"),
        name="diff_prompt",
    )(lam_tile, slopes, dq, diff_b, diff_b, subln_g)


def _layer_norm(z, g, b):
    mu = jnp.mean(z, axis=-1, keepdims=True)
    zc = z - mu
    var = jnp.mean(jnp.square(zc), axis=-1, keepdims=True)
    return zc * lax.rsqrt(var + LN_EPS) * g + b


def _out_proj_kernel(on_ref, od_ref, x_ref, wa_ref, wb_ref, g_ref, b_ref, wr_ref, br_ref,
                     x1_ref, x1b_ref, sel_ref, gate_ref):
    y = _dot(on_ref[...].astype(BF16), wa_ref[...]) + _dot(od_ref[...].astype(BF16), wb_ref[...])
    x1 = _layer_norm(DEEPNORM_ALPHA * x_ref[...] + y, g_ref[...], b_ref[...])
    x1_ref[...] = x1
    x1b_ref[...] = x1.astype(BF16)
    logits = lax.dot_general(wr_ref[...], x1, (((1,), (1,)), ((), ())), precision=HIGHEST,
                             preferred_element_type=F32) + br_ref[...]
    eidx = lax.broadcasted_iota(jnp.int32, logits.shape, 0).astype(F32)
    sel = jnp.zeros(logits.shape, F32)
    vmax = None
    for k in range(TOP_K):
        cur = jnp.where(sel > 0.5, -jnp.inf, logits)
        m = jnp.max(cur, axis=0, keepdims=True)
        first = jnp.min(jnp.where(cur == m, eidx, float(N_EXPERTS)), axis=0, keepdims=True)
        sel = jnp.where(eidx == first, 1.0, sel)
        if k == 0:
            vmax = m
    ex = jnp.where(sel > 0.5, jnp.exp(logits - vmax), 0.0)
    sel_ref[...] = sel
    gate_ref[...] = ex / jnp.sum(ex, axis=0, keepdims=True)


def _out_proj_ln_router(o_nsa, o_diff, x2d, wa, wb, ln_g, ln_b, wr_t, br):
    n = x2d.shape[0]
    tm = min(256, n)
    row = lambda i: (i, 0)
    fix = lambda i: (0, 0)
    col = lambda i: (0, i)
    return pl.pallas_call(
        _out_proj_kernel,
        grid=(n // tm,),
        in_specs=[pl.BlockSpec((tm, 512), row), pl.BlockSpec((tm, 512), row), pl.BlockSpec((tm, D_MODEL), row),
                  pl.BlockSpec((512, D_MODEL), fix), pl.BlockSpec((512, D_MODEL), fix),
                  pl.BlockSpec((1, D_MODEL), fix), pl.BlockSpec((1, D_MODEL), fix),
                  pl.BlockSpec((N_EXPERTS, D_MODEL), fix), pl.BlockSpec((N_EXPERTS, 1), fix)],
        out_specs=(pl.BlockSpec((tm, D_MODEL), row), pl.BlockSpec((tm, D_MODEL), row),
                   pl.BlockSpec((N_EXPERTS, tm), col), pl.BlockSpec((N_EXPERTS, tm), col)),
        out_shape=(jax.ShapeDtypeStruct((n, D_MODEL), F32), jax.ShapeDtypeStruct((n, D_MODEL), BF16),
                   jax.ShapeDtypeStruct((N_EXPERTS, n), F32), jax.ShapeDtypeStruct((N_EXPERTS, n), F32)),
        compiler_params=_cparams("parallel"),
        name="out_proj_ln_router",
    )(o_nsa, o_diff, x2d, wa, wb, ln_g, ln_b, wr_t, br)


def _moe_weight_kernel(w_ref, p_ref, wg_ref, wl_ref):
    half = MXU_DIM // 2
    for m in range(w_ref.shape[2] // MXU_DIM):
        y = _dot(w_ref[0, :, m * MXU_DIM:(m + 1) * MXU_DIM].astype(BF16), p_ref[...])
        wg_ref[0, :, m * half:(m + 1) * half] = y[:, :half].astype(BF16)
        wl_ref[0, :, m * half:(m + 1) * half] = y[:, half:].astype(BF16)


def _moe_split_gate_up(w_gate_up):
    ne, dm, two_ff = w_gate_up.shape
    half = MXU_DIM // 2
    p_np = np.zeros((MXU_DIM, MXU_DIM), np.float32)
    p_np[2 * np.arange(half), np.arange(half)] = 1.0
    p_np[2 * np.arange(half) + 1, half + np.arange(half)] = 1.0
    out = jax.ShapeDtypeStruct((ne, dm, two_ff // 2), BF16)
    ospec = pl.BlockSpec((1, dm, two_ff // 2), lambda e: (e, 0, 0))
    return pl.pallas_call(
        _moe_weight_kernel,
        grid=(ne,),
        in_specs=[pl.BlockSpec((1, dm, two_ff), lambda e: (e, 0, 0)),
                  pl.BlockSpec((MXU_DIM, MXU_DIM), lambda e: (0, 0))],
        out_specs=(ospec, ospec),
        out_shape=(out, out),
        compiler_params=_cparams("parallel"),
        name="moe_split_gate_up",
    )(w_gate_up, jnp.asarray(p_np, BF16))


def _moe_pos_kernel(sel_ref, u_ref, pos_ref, cnt_ref):
    sel = sel_ref[...]
    rank = _dot(sel.astype(BF16), u_ref[...])
    pos_ref[...] = jnp.where(sel > 0.5, rank, -1.0)
    cnt_ref[0] = jnp.sum(sel, axis=1, keepdims=True)


def _moe_positions(sel_t, s_tile):
    n = sel_t.shape[1]
    ns = n // s_tile
    upper = jnp.asarray(np.triu(np.ones((s_tile, s_tile), np.float32), 1), BF16)
    return pl.pallas_call(
        _moe_pos_kernel,
        grid=(ns,),
        in_specs=[pl.BlockSpec((N_EXPERTS, s_tile), lambda s: (0, s)),
                  pl.BlockSpec((s_tile, s_tile), lambda s: (0, 0))],
        out_specs=(pl.BlockSpec((N_EXPERTS, s_tile), lambda s: (0, s)),
                   pl.BlockSpec((1, N_EXPERTS, 1), lambda s: (s, 0, 0))),
        out_shape=(jax.ShapeDtypeStruct((N_EXPERTS, n), F32), jax.ShapeDtypeStruct((ns, N_EXPERTS, 1), F32)),
        compiler_params=_cparams("parallel"),
        name="moe_positions",
    )(sel_t, upper)


def _moe_kernel(nch_ref, x_ref, pos_ref, gate_ref, wg_ref, wl_ref, bg_ref, bl_ref, wd_ref, bd_ref, y_ref, *, chunk):
    s = pl.program_id(0)
    e = pl.program_id(1)

    @pl.when(e == 0)
    def _():
        y_ref[...] = jnp.zeros_like(y_ref)

    pos = pos_ref[0, 0].astype(jnp.int32)
    gate = gate_ref[0, 0]
    slot0 = lax.broadcasted_iota(jnp.int32, (chunk, 1), 0)

    def body(j, carry):
        hit = pos == slot0 + j * chunk
        onehot = jnp.where(hit, 1.0, 0.0).astype(BF16)
        xs = _dot(onehot, x_ref[...]).astype(BF16)
        hg = jnp.minimum(_dot(xs, wg_ref[0]) + bg_ref[0], SWIGLU_LIMIT)
        hl = jnp.clip(_dot(xs, wl_ref[0]) + bl_ref[0], -SWIGLU_LIMIT, SWIGLU_LIMIT)
        a = (hl + 1.0) * hg * jax.nn.sigmoid(SWIGLU_ALPHA * hg)
        out = _dot(a.astype(BF16), wd_ref[0]) + bd_ref[0]
        gslot = jnp.sum(jnp.where(hit, gate, 0.0), axis=1, keepdims=True)
        outg = (out * gslot).astype(BF16)
        y_ref[...] += lax.dot_general(onehot, outg, (((0,), (0,)), ((), ())), preferred_element_type=F32)
        return carry

    lax.fori_loop(0, nch_ref[s * N_EXPERTS + e], body, 0)


def _moe(x1b, pos_t, gate_t, counts, wg, wl, bg, bl, wd, bd, s_tile, chunk):
    n = x1b.shape[0]
    ns = n // s_tile
    nch = ((counts.reshape(ns * N_EXPERTS) + (chunk - 1)) // chunk).astype(jnp.int32)
    pos4 = pos_t.reshape(N_EXPERTS, ns, 1, s_tile)
    gate4 = gate_t.reshape(N_EXPERTS, ns, 1, s_tile)
    wspec = pl.BlockSpec((1, D_MODEL, D_MODEL), lambda s, e, nch: (e, 0, 0))
    bspec = pl.BlockSpec((1, 1, D_MODEL), lambda s, e, nch: (e, 0, 0))
    rspec = pl.BlockSpec((1, 1, 1, s_tile), lambda s, e, nch: (e, s, 0, 0))
    return pl.pallas_call(
        functools.partial(_moe_kernel, chunk=chunk),
        grid_spec=pltpu.PrefetchScalarGridSpec(
            num_scalar_prefetch=1,
            grid=(ns, N_EXPERTS),
            in_specs=[pl.BlockSpec((s_tile, D_MODEL), lambda s, e, nch: (s, 0)),
                      rspec, rspec, wspec, wspec, bspec, bspec, wspec, bspec],
            out_specs=pl.BlockSpec((s_tile, D_MODEL), lambda s, e, nch: (s, 0))),
        out_shape=jax.ShapeDtypeStruct((n, D_MODEL), F32),
        compiler_params=_cparams("parallel", "arbitrary"),
        name="moe_experts",
    )(nch, x1b, pos4, gate4, wg, wl, bg, bl, wd, bd)


def _ln2_kernel(x_ref, f_ref, g_ref, b_ref, o_ref):
    o_ref[...] = _layer_norm(DEEPNORM_ALPHA * x_ref[...] + f_ref[...], g_ref[...], b_ref[...])


def _residual_ln(x1, f, g, b):
    n = x1.shape[0]
    tm = min(512, n)
    row = lambda i: (i, 0)
    fix = lambda i: (0, 0)
    return pl.pallas_call(
        _ln2_kernel,
        grid=(n // tm,),
        in_specs=[pl.BlockSpec((tm, D_MODEL), row), pl.BlockSpec((tm, D_MODEL), row),
                  pl.BlockSpec((1, D_MODEL), fix), pl.BlockSpec((1, D_MODEL), fix)],
        out_specs=pl.BlockSpec((tm, D_MODEL), row),
        out_shape=jax.ShapeDtypeStruct((n, D_MODEL), F32),
        compiler_params=_cparams("parallel"),
        name="residual_ln2",
    )(x1, f, g, b)


def _post_mixer(o_nsa, o_diff, x2d, p, s_tile, chunk):
    x1, x1b, sel_t, gate_t = _out_proj_ln_router(o_nsa, o_diff, x2d, p["wo_a"], p["wo_b"], p["ln1_g"], p["ln1_b"],
                                                 p["wr_t"], p["br"])
    pos_t, counts = _moe_positions(sel_t, s_tile)
    f = _moe(x1b, pos_t, gate_t, counts, p["wg"], p["wl"], p["bg"], p["bl"], p["wd"], p["bd"], s_tile, chunk)
    return _residual_ln(x1, f, p["ln2_g"], p["ln2_b"])


def _sample_kernel(pt_ref, qm_ref, qd_ref, gt_ref, nrow_ref, wrow_ref, wcol_ref, drow_ref, win_ref,
                   w1_ref, pe_ref, w2_ref, m_ref, e_ref, r_ref, sn_ref, sd_ref, lam_ref, subg_ref, *rest,
                   n_pages, past_len, lam_init):
    nsa_pages = rest[:n_pages]
    diff_pages = rest[n_pages:2 * n_pages]
    onsa_ref, odiff_ref, nwin_ref, x_ref = rest[2 * n_pages:]
    n16 = past_len // CMP_STRIDE
    qpos = past_len

    for j, pg in enumerate(nsa_pages):
        for kind in range(2):
            x_ref[kind, j * PAGE_SIZE:(j + 1) * PAGE_SIZE, :] = pg[kind * LANES:(kind + 1) * LANES, :].T

    kv_cmp = []
    for kind in range(2):
        w1 = w1_ref[kind]
        cst = (_dot(pe_ref[kind, 0].astype(BF16), w1[:, 0:LANES]) + _dot(pe_ref[kind, 1].astype(BF16), w1[:, LANES:]))
        chunks = jnp.concatenate([x_ref[kind, pl.ds(l, n16, stride=CMP_STRIDE), :] for l in range(CMP_STRIDE)], axis=1)
        ab = _dot(chunks.astype(BF16), w1)
        nxt = pltpu.roll(ab[:, LANES:], n16 - 1, 0)
        hcmp = jax.nn.gelu(ab[:, 0:LANES] + nxt + cst[0:1])
        kv_cmp.append(_dot(hcmp.astype(BF16), w2_ref[kind]))

    qm = qm_ref[0]
    qmb = qm.astype(BF16)
    slope_n = sn_ref[:, 0:1]
    nrow = nrow_ref[0]
    wrow = wrow_ref[0]
    gt = gt_ref[0]

    def new_key_score(qrows, krow):
        return jnp.sum(qrows * krow, axis=1, keepdims=True) * ATTN_SCALE

    cmp_end = lax.broadcasted_iota(jnp.int32, (1, n16), 1) * CMP_STRIDE + (CMP_LEN - 1)
    dist_c = qpos - cmp_end
    s = _dot_nt(qmb, kv_cmp[0].astype(BF16)) * ATTN_SCALE - slope_n * dist_c.astype(F32)
    e, d = _softmax_parts(s, dist_c >= 0)
    p_cmp = e / d
    o_cmp = _dot(p_cmp.astype(BF16), kv_cmp[1].astype(BF16))
    pgrp = jnp.dot(r_ref[...], p_cmp, precision=HIGHEST, preferred_element_type=F32)
    imp = jnp.dot(pgrp, m_ref[...], precision=HIGHEST, preferred_element_type=F32)
    nbl = m_ref.shape[1]
    blk = lax.broadcasted_iota(jnp.int32, (NSA_HEADS, nbl), 1)
    sel = _select_blocks(imp, blk, jnp.full((NSA_HEADS, 1), qpos, jnp.int32), past_len // SEL_BLOCK + 1)
    mask_s = _dot(sel.astype(BF16), e_ref[...]) > 0.5

    kpos = lax.broadcasted_iota(jnp.int32, (1, past_len), 1)
    dist_k = (qpos - kpos).astype(F32)
    s = jnp.concatenate([_dot(qmb, pg[2 * LANES:3 * LANES, :].astype(BF16)) for pg in nsa_pages], axis=1)
    s = jnp.where(mask_s, s * ATTN_SCALE - slope_n * dist_k, -jnp.inf)
    s_new = new_key_score(qm, nrow[:, 2 * LANES:3 * LANES])
    m = jnp.maximum(jnp.max(s, axis=1, keepdims=True), s_new)
    e = jnp.exp(s - m)
    e_new = jnp.exp(s_new - m)
    d = jnp.sum(e, axis=1, keepdims=True) + e_new
    eb = e.astype(BF16)
    acc = e_new * nrow[:, 3 * LANES:4 * LANES]
    for j, pg in enumerate(nsa_pages):
        acc = acc + _dot_nt(eb[:, j * PAGE_SIZE:(j + 1) * PAGE_SIZE], pg[3 * LANES:4 * LANES, :].astype(BF16))
    o_sel = acc / d

    nwin = win_ref.shape[2]
    wpos = past_len - nwin + lax.broadcasted_iota(jnp.int32, (1, nwin), 1)
    dist_w = qpos - wpos
    mask_w = (dist_w >= 0) & (dist_w <= WINDOW)
    win = win_ref[0]
    s = _dot(qmb, win[0:LANES, :].astype(BF16)) * ATTN_SCALE - slope_n * dist_w.astype(F32)
    s = jnp.where(mask_w, s, -jnp.inf)
    s_new = new_key_score(qm, wrow[:, 0:LANES])
    m = jnp.maximum(jnp.max(s, axis=1, keepdims=True), s_new)
    e = jnp.exp(s - m)
    e_new = jnp.exp(s_new - m)
    d = jnp.sum(e, axis=1, keepdims=True) + e_new
    o_win = (_dot_nt(e.astype(BF16), win[LANES:2 * LANES, :].astype(BF16)) + e_new * wrow[:, LANES:2 * LANES]) / d

    onsa_ref[0] = gt[:, 0:1] * o_cmp + gt[:, 1:2] * o_sel + gt[:, 2:3] * o_win

    lane = lax.broadcasted_iota(jnp.int32, win.shape, 1)
    nwin_ref[0] = jnp.where(lane == nwin - 1, wcol_ref[0], pltpu.roll(win, nwin - 1, 1))

    drow = drow_ref[0]

    def diff_rows(pg, j):
        return pg[pl.ds(j, PAGE_SIZE, stride=DIFF_ROWS), :]

    qds = [qd_ref[0, h] for h in range(DIFF_HEADS)]
    qdb = [qh.astype(BF16) for qh in qds]
    s_pages = []
    for pg in diff_pages:
        sp = _dot_nt(qdb[0], diff_rows(pg, 0).astype(BF16))
        for h in range(1, DIFF_HEADS):
            sp = sp + _dot_nt(qdb[h], diff_rows(pg, h).astype(BF16))
        s_pages.append(sp)
    s = jnp.concatenate(s_pages, axis=1) * ATTN_SCALE - sd_ref[:, 0:1] * dist_k
    prod = qds[0] * drow[0:1]
    for h in range(1, DIFF_HEADS):
        prod = prod + qds[h] * drow[h:h + 1]
    s_new = jnp.sum(prod, axis=1, keepdims=True) * ATTN_SCALE
    m = jnp.maximum(jnp.max(s, axis=1, keepdims=True), s_new)
    e = jnp.exp(s - m)
    e_new = jnp.exp(s_new - m)
    d = jnp.sum(e, axis=1, keepdims=True) + e_new
    lam = lam_ref[0:1, 0:1]
    p = e / d
    p_new = e_new / d
    a = (p - lam * pltpu.roll(p, DIFF_HEADS, 0)).astype(BF16)
    a_new = p_new - lam * pltpu.roll(p_new, DIFF_HEADS, 0)
    accs = []
    for h in range(DIFF_HEADS):
        acc = a_new * drow[DIFF_HEADS + h:DIFF_HEADS + h + 1]
        for j, pg in enumerate(diff_pages):
            acc = acc + _dot(a[:, j * PAGE_SIZE:(j + 1) * PAGE_SIZE], diff_rows(pg, DIFF_HEADS + h).astype(BF16))
        accs.append(acc)
    acc = jnp.concatenate(accs, axis=1)
    lane_head = lax.broadcasted_iota(jnp.int32, acc.shape, 1) // DIFF_VDIM
    row = lax.broadcasted_iota(jnp.int32, acc.shape, 0)
    own = lane_head == row
    ms = jnp.sum(jnp.where(own, jnp.square(acc), 0.0), axis=1, keepdims=True) / DIFF_VDIM
    odiff_ref[0] = acc * lax.rsqrt(ms + RMS_EPS) * subg_ref[...] * (1.0 - lam_init)


def _sample_attention(page_table, qm, qd4, gt8, nrow, wrow, wcol, drow, win_t, pool_nsa, pool_diff,
                      w1ab, pe2, w2bd, m_mat, e_mat, r_mat, sn, sd, lam_tile, subg_tile, past_len, lam_init):
    B = qm.shape[0]
    n_pages = page_table.shape[1]
    nwin = win_t.shape[2]

    def per_b(shape):
        nd = len(shape)
        return pl.BlockSpec((1,) + shape, lambda b, pt: (b,) + (0,) * nd)

    def fixed(shape):
        nd = len(shape)
        return pl.BlockSpec(shape, lambda b, pt: (0,) * nd)

    def page_spec(pool, j):
        return pl.BlockSpec((None,) + pool.shape[1:], lambda b, pt: (pt[b, j], 0, 0))

    in_specs = [per_b((NSA_HEADS, LANES)), per_b((DIFF_HEADS, 8, LANES)), per_b((NSA_HEADS, LANES)),
                per_b((1, NSA_W)), per_b((1, WIN_W)), per_b((WIN_W, 1)), per_b((DIFF_ROWS, LANES)),
                per_b((WIN_W, nwin)),
                fixed(w1ab.shape), fixed(pe2.shape), fixed(w2bd.shape), fixed(m_mat.shape), fixed(e_mat.shape),
                fixed(r_mat.shape), fixed(sn.shape), fixed(sd.shape), fixed(lam_tile.shape), fixed(subg_tile.shape)]
    in_specs += [page_spec(pool_nsa, j) for j in range(n_pages)]
    in_specs += [page_spec(pool_diff, j) for j in range(n_pages)]
    return pl.pallas_call(
        functools.partial(_sample_kernel, n_pages=n_pages, past_len=past_len, lam_init=lam_init),
        grid_spec=pltpu.PrefetchScalarGridSpec(
            num_scalar_prefetch=1,
            grid=(B,),
            in_specs=in_specs,
            out_specs=(per_b((NSA_HEADS, LANES)), per_b((8, DIFF_HEADS * DIFF_VDIM)), per_b((WIN_W, nwin))),
            scratch_shapes=[pltpu.VMEM((2, past_len, LANES), F32)]),
        out_shape=(jax.ShapeDtypeStruct((B, NSA_HEADS, LANES), F32),
                   jax.ShapeDtypeStruct((B, 8, DIFF_HEADS * DIFF_VDIM), F32),
                   jax.ShapeDtypeStruct((B, WIN_W, nwin), F32)),
        compiler_params=_cparams("arbitrary"),
        name="sample_attention",
    )(page_table, qm, qd4, gt8, nrow, wrow, wcol, drow, win_t, w1ab, pe2, w2bd, m_mat, e_mat, r_mat, sn, sd,
      lam_tile, subg_tile, *([pool_nsa] * n_pages), *([pool_diff] * n_pages))


def _cmp_to_sel(n_cmp, n_sel):
    c0 = np.arange(n_cmp)[:, None] * CMP_STRIDE
    s0 = np.arange(n_sel)[None, :] * SEL_BLOCK
    ov = np.clip(np.minimum(c0 + CMP_LEN, s0 + SEL_BLOCK) - np.maximum(c0, s0), 0, None)
    return (ov / CMP_LEN).astype(np.float32)


def _lambda_init(layer):
    return 0.8 - 0.6 * math.exp(-0.3 * layer)


def _prep_params(l, w_in, w_out, diff_subln_g, ln1_g, ln1_b, ln2_g, ln2_b, w_router, b_router,
                 w_gate_up, b_gate_up, w_down, b_down):
    gate_end = _GT0 + 3 * NSA_HEADS
    wt = w_in[l].T
    w_t = jnp.concatenate([wt[:gate_end], jnp.zeros((_GATE_PAD, D_MODEL), F32), wt[gate_end:]], axis=0).astype(BF16)
    wg, wl = _moe_split_gate_up(w_gate_up[l])
    bgu = b_gate_up[l]
    return {
        "w_t": w_t,
        "wo_a": w_out[l][:512].astype(BF16), "wo_b": w_out[l][512:].astype(BF16),
        "ln1_g": ln1_g[l][None], "ln1_b": ln1_b[l][None], "ln2_g": ln2_g[l][None], "ln2_b": ln2_b[l][None],
        "wr_t": w_router[l].T, "br": b_router[l][:, None],
        "wg": wg, "wl": wl, "bg": bgu[:, None, 0::2], "bl": bgu[:, None, 1::2],
        "wd": w_down[l].astype(BF16), "bd": b_down[l][:, None, :],
        "subln_g": diff_subln_g[l][None],
    }


def _sample_cmp_params(cmp_pe, cmp_w1, cmp_w2):
    eye = jnp.eye(NSA_KV_HEADS, dtype=F32)
    halves = []
    for half in range(2):
        w = cmp_w1[:, half * CMP_STRIDE:(half + 1) * CMP_STRIDE]
        wb = jnp.einsum('klde,gh->klgdhe', w, eye)
        halves.append(wb.reshape(2, CMP_STRIDE * LANES, LANES))
    w1ab = jnp.concatenate(halves, axis=-1).astype(BF16)
    pe = cmp_pe.reshape(2, 2, CMP_STRIDE, 1, HEAD_DIM)
    pe2 = jnp.broadcast_to(pe, (2, 2, CMP_STRIDE, NSA_KV_HEADS, HEAD_DIM)).reshape(2, 2, 1, CMP_STRIDE * LANES)
    pe2 = jnp.broadcast_to(pe2, (2, 2, 8, CMP_STRIDE * LANES))
    w2bd = jnp.einsum('kde,gh->kgdhe', cmp_w2, eye).reshape(2, LANES, LANES).astype(BF16)
    return w1ab, pe2, w2bd


def _feature_major_to_rows(a_t, kinds):
    B, _, T = a_t.shape
    return a_t.reshape(B, kinds, NSA_KV_HEADS, HEAD_DIM, T).transpose(0, 4, 1, 2, 3)


def _prompt_group(x, p, cmp_pe, cmp_w1, cmp_w2, lam_tile, lam_init):
    B, T, _ = x.shape
    x2d = x.reshape(B * T, D_MODEL)
    q_nsa, nsa_rows, nsa_t, nsa_tb, _, win_t, win_tb, gates, dq, diff8, diff_b = _in_proj(x2d, p["w_t"], B)
    r3 = lambda a: a.reshape(B, T, a.shape[-1])
    q_nsa, nsa_rows3, gates, dq = map(r3, (q_nsa, nsa_rows, gates, dq))
    kdim = CMP_LEN * HEAD_DIM
    kvc = _compress_prompt(nsa_rows3, cmp_pe.reshape(2, 1, kdim), cmp_w1.reshape(2, kdim, HEAD_DIM).astype(BF16),
                           cmp_w2.astype(BF16))
    n16 = T // CMP_STRIDE
    n_sel = -(-T // SEL_BLOCK)
    m_np = np.zeros((n16, n_sel), np.float32)
    m_np[:n16 - 1] = _cmp_to_sel(n16 - 1, n_sel)
    o_cmp, sel = _cmp_attn_prompt(q_nsa, kvc, jnp.asarray(m_np))
    e_np = np.zeros((NSA_KV_HEADS, LANES, T), np.float32)
    for g in range(NSA_KV_HEADS):
        e_np[g, g * n_sel + np.arange(T) // SEL_BLOCK, np.arange(T)] = 1.0
    o_nsa = _sel_win_prompt(q_nsa, nsa_tb, win_tb, sel, jnp.asarray(e_np, BF16), o_cmp, gates)
    o_diff = _diff_prompt(lam_tile, dq, diff_b.reshape(B, T, DIFF_W), p["subln_g"], lam_init)
    y = _post_mixer(o_nsa.reshape(B * T, 512), o_diff.reshape(B * T, 512), x2d, p, s_tile=T, chunk=320)
    nwin = min(WINDOW, T)
    return (y.reshape(B, T, D_MODEL),
            _feature_major_to_rows(nsa_t, 4),
            diff8.reshape(B, T, 2, DIFF_HEADS, DIFF_VDIM),
            _feature_major_to_rows(win_t[:, :, T - nwin:], 2))


def _sample_group(x, pool_nsa, pool_diff, win_buf, page_table, p, cmp_pe, cmp_w1, cmp_w2, lam_tile, lam_init):
    B, T, _ = x.shape
    past_len = page_table.shape[1] * PAGE_SIZE
    x2d = x.reshape(B, D_MODEL)
    q_nsa, nsa_rows, nsa_t, _, win_rows, _, _, gates, dq, diff8, _ = _in_proj(x2d, p["w_t"], 1)
    qh = q_nsa.reshape(B, NSA_HEADS, HEAD_DIM)
    grp = (np.arange(NSA_HEADS) // NSA_GROUP)[None, :, None, None] == np.arange(NSA_KV_HEADS)[None, None, :, None]
    qm = (qh[:, :, None, :] * jnp.asarray(grp, F32)).reshape(B, NSA_HEADS, LANES)
    dq4 = dq.reshape(B, DIFF_HEADS, 1, 2 * HEAD_DIM)
    rr = np.arange(8)
    rowmask = (rr[None, :, None] % DIFF_HEADS == np.arange(DIFF_HEADS)[:, None, None]) & (
        rr[None, :, None] // DIFF_HEADS == (np.arange(LANES) // HEAD_DIM)[None, None, :])
    qd4 = dq4 * jnp.asarray(rowmask, F32)[None]
    gt8 = jnp.pad(gates[:, :3 * NSA_HEADS].reshape(B, 3, NSA_HEADS).transpose(0, 2, 1), ((0, 0), (0, 0), (0, LANES - 3)))
    nwin = win_buf.shape[1]
    win_t = win_buf.transpose(0, 2, 3, 4, 1).reshape(B, WIN_W, nwin)
    n_pool = pool_nsa.shape[0]
    pool_nsa_t = pool_nsa.transpose(0, 2, 3, 4, 1).reshape(n_pool, NSA_W, PAGE_SIZE)
    pool_diff_r = pool_diff.reshape(n_pool, PAGE_SIZE * DIFF_ROWS, LANES)
    w1ab, pe2, w2bd = _sample_cmp_params(cmp_pe, cmp_w1, cmp_w2)
    n16 = past_len // CMP_STRIDE
    n_sel = past_len // SEL_BLOCK + 1
    m_np = np.zeros((n16, 64), np.float32)
    m_np[:n16 - 1, :n_sel] = _cmp_to_sel(n16 - 1, n_sel)
    e_np = np.zeros((64, past_len), np.float32)
    e_np[np.arange(past_len) // SEL_BLOCK, np.arange(past_len)] = 1.0
    r_np = (np.arange(8)[:, None] // NSA_GROUP == np.arange(8)[None, :] // NSA_GROUP).astype(np.float32)
    sn = jnp.asarray(np.tile(np.asarray(NSA_SLOPES, np.float32)[:, None], (1, LANES)))
    sd = jnp.asarray(np.tile(np.asarray(DIFF_SLOPES, np.float32)[np.arange(8) % DIFF_HEADS, None], (1, LANES)))
    subg_tile = jnp.tile(p["subln_g"], (1, DIFF_HEADS))
    o_nsa8, o_diff8, new_win_t = _sample_attention(
        page_table, qm, qd4, gt8, nsa_rows[:, None, :], win_rows[:, None, :], win_rows[:, :, None],
        diff8.reshape(B, DIFF_ROWS, LANES), win_t, pool_nsa_t, pool_diff_r,
        w1ab, pe2, w2bd, jnp.asarray(m_np), jnp.asarray(e_np, BF16), jnp.asarray(r_np), sn, sd, lam_tile, subg_tile,
        past_len, lam_init)
    o8 = o_nsa8.reshape(B, NSA_KV_HEADS, NSA_GROUP, NSA_KV_HEADS, HEAD_DIM)
    o_nsa = jnp.stack([o8[:, g, :, g] for g in range(NSA_KV_HEADS)], axis=1).reshape(B, 512)
    d8 = o_diff8[:, :DIFF_HEADS].reshape(B, DIFF_HEADS, DIFF_HEADS, DIFF_VDIM)
    o_diff = jnp.stack([d8[:, h, h] for h in range(DIFF_HEADS)], axis=1).reshape(B, 512)
    y = _post_mixer(o_nsa, o_diff, x2d, p, s_tile=B, chunk=B)
    return (y.reshape(B, T, D_MODEL),
            _feature_major_to_rows(nsa_t, 4).reshape(B, T, 4, NSA_KV_HEADS, HEAD_DIM),
            diff8.reshape(B, T, 2, DIFF_HEADS, DIFF_VDIM),
            _feature_major_to_rows(new_win_t, 2))


def kernel(x_prompt, x_sample, cache_nsa_kv, cache_diff_kv, state_nsa_win, page_table, w_in, w_out, cmp_pe, cmp_w1,
           cmp_w2, diff_lambda, diff_subln_g, ln1_g, ln1_b, ln2_g, ln2_b, w_router, b_router, w_gate_up, b_gate_up,
           w_down, b_down):
    depth = w_in.shape[0]
    xp, xs = x_prompt, x_sample
    outs = [[] for _ in range(6)]
    for l in range(depth):
        lam0 = _lambda_init(l)
        lv = diff_lambda[l].astype(F32)
        lam = jnp.exp(jnp.sum(lv[0] * lv[1])) - jnp.exp(jnp.sum(lv[2] * lv[3])) + lam0
        lam_tile = jnp.full((8, LANES), lam, F32)
        p = _prep_params(l, w_in, w_out, diff_subln_g, ln1_g, ln1_b, ln2_g, ln2_b, w_router, b_router,
                         w_gate_up, b_gate_up, w_down, b_down)
        xp, r_nsa, r_diff, r_win = _prompt_group(xp, p, cmp_pe[l], cmp_w1[l], cmp_w2[l], lam_tile, lam0)
        xs, s_nsa, s_diff, s_win = _sample_group(xs, cache_nsa_kv[l], cache_diff_kv[l], state_nsa_win[l], page_table,
                                                 p, cmp_pe[l], cmp_w1[l], cmp_w2[l], lam_tile, lam0)
        for lst, v in zip(outs, (r_nsa, r_diff, r_win, s_nsa, s_diff, s_win)):
            lst.append(v)
    return (xp, xs) + tuple(jnp.stack(o) for o in outs)
```

```python
import functools
import math

import numpy as np
import jax
import jax.numpy as jnp
from jax import lax
from jax.experimental import pallas as pl
from jax.experimental.pallas import tpu as pltpu

F32 = jnp.float32
BF16 = jnp.bfloat16
HIGHEST = lax.Precision.HIGHEST

D_MODEL = 1024
HEAD_DIM = 64
NSA_HEADS = 8
NSA_KV_HEADS = 2
NSA_GROUP = NSA_HEADS // NSA_KV_HEADS
CMP_LEN = 32
CMP_STRIDE = 16
SEL_BLOCK = 64
SEL_TOPN = 16
WINDOW = 512
DIFF_HEADS = 4
DIFF_VDIM = 2 * HEAD_DIM
N_EXPERTS = 32
TOP_K = 4
SWIGLU_ALPHA = 1.702
SWIGLU_LIMIT = 7.0
LN_EPS = 1e-5
RMS_EPS = 1e-5
ATTN_SCALE = HEAD_DIM ** -0.5
LOG2E = math.log2(math.e)
DEPTH = 1
DEEPNORM_ALPHA = (2 * DEPTH) ** 0.25
PAGE_SIZE = 128

NSA_SLOPES = tuple(2.0 ** (-8.0 * (i + 1) / NSA_HEADS) for i in range(NSA_HEADS))
DIFF_SLOPES = tuple(2.0 ** (-8.0 * (i + 1) / DIFF_HEADS) for i in range(DIFF_HEADS))

VMEM_LIMIT_BYTES = 56 * 1024 * 1024
LANES = 128
MXU_DIM = 256

NSA_W = 4 * NSA_KV_HEADS * HEAD_DIM
WIN_W = 2 * NSA_KV_HEADS * HEAD_DIM
DIFF_W = 2 * DIFF_HEADS * DIFF_VDIM
DIFF_ROWS = DIFF_W // LANES
_GATE_PAD = LANES - 3 * NSA_HEADS
_Q0, _NSA0, _WIN0, _GT0, _DQ0, _DIFF0, _PROJ_ROWS = 0, 512, 1024, 1280, 1408, 1920, 2944


def _cparams(*sem):
    return pltpu.CompilerParams(dimension_semantics=sem, vmem_limit_bytes=VMEM_LIMIT_BYTES)


def _softmax_parts(s, mask):
    s = jnp.where(mask, s, -jnp.inf)
    m = jnp.max(s, axis=-1, keepdims=True)
    m = jnp.where(jnp.isfinite(m), m, 0.0)
    e = jnp.exp(s - m)
    d = jnp.sum(e, axis=-1, keepdims=True)
    return e, jnp.where(d > 0, d, 1.0)


def _softmax2_parts(s, mask):
    s = jnp.where(mask, s, -jnp.inf)
    m = jnp.max(s, axis=-1, keepdims=True)
    m = jnp.where(jnp.isfinite(m), m, 0.0)
    e = jnp.exp2(s - m)
    d = jnp.sum(e, axis=-1, keepdims=True)
    return e, jnp.where(d > 0, d, 1.0)


def _dot_nt(a, b):
    return lax.dot_general(a, b, (((1,), (1,)), ((), ())), preferred_element_type=F32)


def _dot(a, b):
    return jnp.dot(a, b, preferred_element_type=F32)


def _pad_head(qh, g):
    z = jnp.zeros_like(qh)
    return jnp.concatenate([qh, z] if g == 0 else [z, qh], axis=1)


def _select_blocks(imp, blk, qpos, n_blk_lanes):
    cur = jnp.right_shift(qpos, int(math.log2(SEL_BLOCK)))
    valid = blk * SEL_BLOCK <= qpos
    forced = (blk == 0) | (blk == cur) | (blk == cur - 1)
    val = jnp.where(forced, jnp.inf, jnp.where(valid, imp, -jnp.inf))
    rank = jnp.zeros(val.shape, F32)
    for i in range(n_blk_lanes):
        ci = val[:, i:i + 1]
        beats = (ci > val) | ((ci == val) & (blk > i))
        rank = rank + jnp.where(beats, 1.0, 0.0)
    return jnp.where(rank < SEL_TOPN, 1.0, 0.0)


def _in_proj_kernel(x_ref, w_ref, q_ref, qtb_ref, nsa_ref, nsat_ref, nsatb_ref, win_ref, wint_ref, wintb_ref, gt_ref,
                    dq_ref, diff_ref, diffb_ref):
    tm = x_ref.shape[0]
    xb = x_ref[...].astype(BF16)
    q_ref[...] = _dot_nt(xb, w_ref[_Q0:_NSA0, :])
    qtb_ref[0] = _dot_nt(w_ref[_Q0:_NSA0, :], xb).astype(BF16)
    nsa_ref[...] = _dot_nt(xb, w_ref[_NSA0:_WIN0, :])
    r = _dot_nt(w_ref[_NSA0:_WIN0, :], xb)
    nsat_ref[0] = r
    nsatb_ref[0] = r.astype(BF16)
    win_ref[...] = _dot_nt(xb, w_ref[_WIN0:_GT0, :])
    r = _dot_nt(w_ref[_WIN0:_GT0, :], xb)
    wint_ref[0] = r
    wintb_ref[0] = r.astype(BF16)
    gt_ref[...] = jax.nn.sigmoid(_dot_nt(xb, w_ref[_GT0:_DQ0, :]))
    dq_ref[...] = _dot_nt(xb, w_ref[_DQ0:_DIFF0, :])
    r = _dot_nt(xb, w_ref[_DIFF0:_PROJ_ROWS, :])
    diffb_ref[...] = r.astype(BF16)
    for j in range(DIFF_ROWS):
        diff_ref[pl.ds(j, tm, stride=DIFF_ROWS), :] = r[:, j * LANES:(j + 1) * LANES]


def _in_proj(x2d, w_t, batch):
    n = x2d.shape[0]
    t = n // batch
    tm = min(256, t)
    nt = t // tm
    row = lambda b, i: (b * nt + i, 0)
    tr = lambda b, i: (b, 0, i)
    tok = lambda w: pl.BlockSpec((tm, w), row)
    return pl.pallas_call(
        _in_proj_kernel,
        grid=(batch, nt),
        in_specs=[pl.BlockSpec((tm, D_MODEL), row),
                  pl.BlockSpec((_PROJ_ROWS, D_MODEL), lambda b, i: (0, 0))],
        out_specs=(tok(512), pl.BlockSpec((1, 512, tm), tr),
                   tok(NSA_W), pl.BlockSpec((1, NSA_W, tm), tr), pl.BlockSpec((1, NSA_W, tm), tr),
                   tok(WIN_W), pl.BlockSpec((1, WIN_W, tm), tr), pl.BlockSpec((1, WIN_W, tm), tr),
                   tok(LANES), tok(512), pl.BlockSpec((tm * DIFF_ROWS, LANES), row), tok(DIFF_W)),
        out_shape=(jax.ShapeDtypeStruct((n, 512), F32), jax.ShapeDtypeStruct((batch, 512, t), BF16),
                   jax.ShapeDtypeStruct((n, NSA_W), F32),
                   jax.ShapeDtypeStruct((batch, NSA_W, t), F32), jax.ShapeDtypeStruct((batch, NSA_W, t), BF16),
                   jax.ShapeDtypeStruct((n, WIN_W), F32),
                   jax.ShapeDtypeStruct((batch, WIN_W, t), F32), jax.ShapeDtypeStruct((batch, WIN_W, t), BF16),
                   jax.ShapeDtypeStruct((n, LANES), F32), jax.ShapeDtypeStruct((n, 512), F32),
                   jax.ShapeDtypeStruct((n * DIFF_ROWS, LANES), F32), jax.ShapeDtypeStruct((n, DIFF_W), BF16)),
        compiler_params=_cparams("parallel", "parallel"),
        name="in_proj",
    )(x2d, w_t)


def _compress_kernel(blk_ref, pe_ref, w1_ref, w2_ref, o_ref):
    nb = blk_ref.shape[3]
    x = blk_ref[0, 0].reshape(2 * nb, CMP_LEN * HEAD_DIM) + pe_ref[0]
    h = jax.nn.gelu(_dot(x.astype(BF16), w1_ref[0]))
    o = _dot(h.astype(BF16), w2_ref[0])
    o_ref[0, 0, :, 0:HEAD_DIM] = o[0:nb]
    o_ref[0, 0, :, HEAD_DIM:2 * HEAD_DIM] = o[nb:2 * nb]


def _compress_prompt(nsa_rows, pe_flat, w1_flat, w2_b):
    B, T, _ = nsa_rows.shape
    n16 = T // CMP_STRIDE
    kv = nsa_rows[:, :, :2 * NSA_KV_HEADS * HEAD_DIM].reshape(B, n16, CMP_STRIDE, 2, NSA_KV_HEADS, HEAD_DIM)
    kv = kv.transpose(0, 3, 4, 1, 2, 5).reshape(B, 2, NSA_KV_HEADS, n16, CMP_STRIDE * HEAD_DIM)
    blocks = jnp.concatenate([kv[:, :, :, :-1], kv[:, :, :, 1:]], axis=-1)
    blocks = jnp.pad(blocks, ((0, 0), (0, 0), (0, 0), (0, 1), (0, 0)))
    kdim = CMP_LEN * HEAD_DIM
    return pl.pallas_call(
        _compress_kernel,
        grid=(B, 2),
        in_specs=[pl.BlockSpec((1, 1, NSA_KV_HEADS, n16, kdim), lambda b, k: (b, k, 0, 0, 0)),
                  pl.BlockSpec((1, 1, kdim), lambda b, k: (k, 0, 0)),
                  pl.BlockSpec((1, kdim, HEAD_DIM), lambda b, k: (k, 0, 0)),
                  pl.BlockSpec((1, HEAD_DIM, HEAD_DIM), lambda b, k: (k, 0, 0))],
        out_specs=pl.BlockSpec((1, 1, n16, 2 * HEAD_DIM), lambda b, k: (b, k, 0, 0)),
        out_shape=jax.ShapeDtypeStruct((B, 2, n16, 2 * HEAD_DIM), F32),
        compiler_params=_cparams("parallel", "parallel"),
        name="compress_prompt",
    )(blocks, pe_flat, w1_flat, w2_b)


def _select_blocks_t(imp, qpos):
    n_blk = imp.shape[0]
    blk = lax.broadcasted_iota(jnp.int32, imp.shape, 0)
    cur = jnp.right_shift(qpos, int(math.log2(SEL_BLOCK)))
    valid = blk * SEL_BLOCK <= qpos
    forced = (blk == 0) | (blk == cur) | (blk == cur - 1)
    val = jnp.where(forced, jnp.inf, jnp.where(valid, imp, -jnp.inf))
    rank = jnp.zeros(val.shape, F32)
    for i in range(n_blk):
        ci = val[i:i + 1, :]
        beats = (ci > val) | ((ci == val) & (blk > i))
        rank = rank + jnp.where(beats, 1.0, 0.0)
    return jnp.where(rank < SEL_TOPN, 1.0, 0.0)


def _cmp_attn_kernel(q_ref, qt_ref, kc_ref, vc_ref, mt_ref, o_ref, selt_ref, *, tq):
    i = pl.program_id(1)
    ncmp = kc_ref.shape[2]
    kcb = kc_ref[0, 0].astype(BF16)
    vcb = vc_ref[0, 0].astype(BF16)
    qpos_c = i * tq + lax.broadcasted_iota(jnp.int32, (tq, 1), 0)
    end_r = lax.broadcasted_iota(jnp.int32, (1, ncmp), 1) * CMP_STRIDE + (CMP_LEN - 1)
    mask = end_r <= qpos_c
    end_rf = end_r.astype(F32)
    q = q_ref[0] * (ATTN_SCALE * LOG2E)
    for h in range(NSA_HEADS):
        g = h // NSA_GROUP
        qp = _pad_head(q[:, h * HEAD_DIM:(h + 1) * HEAD_DIM], g).astype(BF16)
        e, d = _softmax2_parts(_dot_nt(qp, kcb) + (NSA_SLOPES[h] * LOG2E) * end_rf, mask)
        oh = _dot(e.astype(BF16), vcb)[:, g * HEAD_DIM:(g + 1) * HEAD_DIM] / d
        o_ref[0, :, h * HEAD_DIM:(h + 1) * HEAD_DIM] = oh
    qpos_r = i * tq + lax.broadcasted_iota(jnp.int32, (1, tq), 1)
    end_c = lax.broadcasted_iota(jnp.int32, (ncmp, 1), 0) * CMP_STRIDE + (CMP_LEN - 1)
    mask_t = end_c <= qpos_r
    end_cf = end_c.astype(F32)
    qt = qt_ref[0]
    zero = jnp.zeros((HEAD_DIM, tq), BF16)
    psum = [None, None]
    for h in range(NSA_HEADS):
        g = h // NSA_GROUP
        qh = qt[h * HEAD_DIM:(h + 1) * HEAD_DIM, :]
        qpt = jnp.concatenate([qh, zero] if g == 0 else [zero, qh], axis=0)
        s = _dot(kcb, qpt) * (ATTN_SCALE * LOG2E) + (NSA_SLOPES[h] * LOG2E) * end_cf
        s = jnp.where(mask_t, s, -jnp.inf)
        m = jnp.max(s, axis=0, keepdims=True)
        m = jnp.where(jnp.isfinite(m), m, 0.0)
        e = jnp.exp2(s - m)
        d = jnp.sum(e, axis=0, keepdims=True)
        p = e / jnp.where(d > 0, d, 1.0)
        psum[g] = p if psum[g] is None else psum[g] + p
    sels = []
    for g in range(NSA_KV_HEADS):
        imp = jnp.dot(mt_ref[...], psum[g], precision=HIGHEST, preferred_element_type=F32)
        sels.append(_select_blocks_t(imp, qpos_r))
    sels.append(jnp.zeros((LANES - NSA_KV_HEADS * mt_ref.shape[0], tq), F32))
    selt_ref[0] = jnp.concatenate(sels, axis=0)


def _cmp_attn_prompt(q_nsa, q_tb, kvc, mt_mat, tq=256):
    B, T, _ = q_nsa.shape
    ncmp = kvc.shape[2]
    n_sel = mt_mat.shape[0]
    return pl.pallas_call(
        functools.partial(_cmp_attn_kernel, tq=tq),
        grid=(B, T // tq),
        in_specs=[pl.BlockSpec((1, tq, 512), lambda b, i: (b, i, 0)),
                  pl.BlockSpec((1, 512, tq), lambda b, i: (b, 0, i)),
                  pl.BlockSpec((1, 1, ncmp, LANES), lambda b, i: (b, 0, 0, 0)),
                  pl.BlockSpec((1, 1, ncmp, LANES), lambda b, i: (b, 1, 0, 0)),
                  pl.BlockSpec((n_sel, ncmp), lambda b, i: (0, 0))],
        out_specs=(pl.BlockSpec((1, tq, 512), lambda b, i: (b, i, 0)),
                   pl.BlockSpec((1, LANES, tq), lambda b, i: (b, 0, i))),
        out_shape=(jax.ShapeDtypeStruct((B, T, 512), F32), jax.ShapeDtypeStruct((B, LANES, T), F32)),
        compiler_params=_cparams("parallel", "parallel"),
        name="cmp_attn_prompt",
    )(q_nsa, q_tb, kvc, kvc, mt_mat)


def _sel_win_kernel(q_ref, ks_ref, vs_ref, kw_ref, vw_ref, sel_ref, e_ref, ocmp_ref, gt_ref, o_ref, *, tq, span,
                    kv_step):
    i = pl.program_id(1)
    T = ks_ref.shape[2]
    qpos = i * tq + lax.broadcasted_iota(jnp.int32, (tq, 1), 0)
    q = q_ref[0] * (ATTN_SCALE * LOG2E)
    gt = gt_ref[0]
    ocmp = ocmp_ref[0]
    selb = sel_ref[0].astype(BF16)
    heads = [(h, h // NSA_GROUP, h * HEAD_DIM, (h + 1) * HEAD_DIM) for h in range(NSA_HEADS)]
    qps = [_pad_head(q[:, lo:hi], g).astype(BF16) for _, g, lo, hi in heads]
    start = pl.multiple_of(jnp.maximum(i * tq - WINDOW, 0), LANES)
    wpos = start + lax.broadcasted_iota(jnp.int32, (1, span), 1)
    dist_w = qpos - wpos
    mask_w = (dist_w >= 0) & (dist_w <= WINDOW)
    wposf = wpos.astype(F32)
    kwb = kw_ref[0, :, pl.ds(start, span)]
    vwb = vw_ref[0, :, pl.ds(start, span)]
    partial = []
    for h, g, lo, hi in heads:
        s = _dot(qps[h], kwb) + (NSA_SLOPES[h] * LOG2E) * wposf
        e, d = _softmax2_parts(s, mask_w)
        o_win = _dot_nt(e.astype(BF16), vwb)[:, g * HEAD_DIM:(g + 1) * HEAD_DIM] / d
        partial.append(gt[:, h:h + 1] * ocmp[:, lo:hi] + gt[:, 2 * NSA_HEADS + h:2 * NSA_HEADS + h + 1] * o_win)
    tiles_per_step = kv_step // tq
    for br in range(T // kv_step):
        @pl.when((i >= br * tiles_per_step) & (i < (br + 1) * tiles_per_step))
        def _(br=br):
            kv = (br + 1) * kv_step
            kpos = lax.broadcasted_iota(jnp.int32, (1, kv), 1)
            causal = kpos <= qpos
            kposf = kpos.astype(F32)
            ksb = ks_ref[0, :, 0:kv]
            vsb = vs_ref[0, :, 0:kv]
            for g in range(NSA_KV_HEADS):
                expand = lax.dot_general(selb, e_ref[g, :, 0:kv], (((0,), (0,)), ((), ())), preferred_element_type=F32)
                mask_s = (expand > 0.5) & causal
                for h, _, lo, hi in heads[g * NSA_GROUP:(g + 1) * NSA_GROUP]:
                    s = _dot(qps[h], ksb) + (NSA_SLOPES[h] * LOG2E) * kposf
                    e, d = _softmax2_parts(s, mask_s)
                    o_sel = _dot_nt(e.astype(BF16), vsb)[:, g * HEAD_DIM:(g + 1) * HEAD_DIM] / d
                    o_ref[0, :, lo:hi] = partial[h] + gt[:, NSA_HEADS + h:NSA_HEADS + h + 1] * o_sel


def _sel_win_prompt(q_nsa, nsa_t, win_t, sel_t, e_mat, o_cmp, gates, tq=256):
    B, T, _ = q_nsa.shape
    span = WINDOW + tq
    kv = lambda j: pl.BlockSpec((1, LANES, T), lambda b, i: (b, j, 0))
    return pl.pallas_call(
        functools.partial(_sel_win_kernel, tq=tq, span=span, kv_step=min(2 * tq, T)),
        grid=(B, T // tq),
        in_specs=[pl.BlockSpec((1, tq, 512), lambda b, i: (b, i, 0)),
                  kv(2), kv(3), kv(0), kv(1),
                  pl.BlockSpec((1, LANES, tq), lambda b, i: (b, 0, i)),
                  pl.BlockSpec((NSA_KV_HEADS, LANES, T), lambda b, i: (0, 0, 0)),
                  pl.BlockSpec((1, tq, 512), lambda b, i: (b, i, 0)),
                  pl.BlockSpec((1, tq, LANES), lambda b, i: (b, i, 0))],
        out_specs=pl.BlockSpec((1, tq, 512), lambda b, i: (b, i, 0)),
        out_shape=jax.ShapeDtypeStruct((B, T, 512), F32),
        compiler_params=_cparams("parallel", "parallel"),
        name="sel_win_prompt",
    )(q_nsa, nsa_t, nsa_t, win_t, win_t, sel_t, e_mat, o_cmp, gates)


def _diff_kernel(lam_ref, sl_ref, q_ref, k_ref, v_ref, g_ref, o_ref, *, tq, lam_init):
    i = pl.program_id(2)
    T = k_ref.shape[1]
    lam = lam_ref[0:1, 0:1]
    q = q_ref[0] * (ATTN_SCALE * LOG2E)
    qps = [_pad_head(q[:, c * HEAD_DIM:(c + 1) * HEAD_DIM], c).astype(BF16) for c in range(2)]
    tri = lax.broadcasted_iota(jnp.int32, (tq, tq), 0) >= lax.broadcasted_iota(jnp.int32, (tq, tq), 1)
    for br in range(T // tq):
        @pl.when(i == br)
        def _(br=br):
            off = br * tq
            kv = off + tq
            kb = k_ref[0, 0:kv, :]
            vb = v_ref[0, 0:kv, :]
            col = (sl_ref[0, :, 0:1] * LOG2E) * lax.broadcasted_iota(jnp.int32, (1, kv), 1).astype(F32)
            outs = []
            for c in range(2):
                s = _dot_nt(qps[c], kb) + col
                s_d = jnp.where(tri, s[:, off:], -jnp.inf)
                m = jnp.max(s_d, axis=-1, keepdims=True)
                if off:
                    m = jnp.maximum(m, jnp.max(s[:, :off], axis=-1, keepdims=True))
                e_d = jnp.exp2(s_d - m)
                d = jnp.sum(e_d, axis=-1, keepdims=True)
                o = _dot(e_d.astype(BF16), vb[off:, :])
                if off:
                    e_o = jnp.exp2(s[:, :off] - m)
                    d = d + jnp.sum(e_o, axis=-1, keepdims=True)
                    o = o + _dot(e_o.astype(BF16), vb[:off, :])
                outs.append(o / d)
            od = outs[0] - lam * outs[1]
            od = od * lax.rsqrt(jnp.mean(jnp.square(od), axis=-1, keepdims=True) + RMS_EPS)
            o_ref[0] = od * g_ref[...] * (1.0 - lam_init)


def _diff_prompt(lam_tile, dq, diff_b, subln_g, lam_init, tq=256):
    B, T, _ = dq.shape
    slopes = jnp.asarray(np.tile(np.asarray(DIFF_SLOPES, np.float32)[:, None, None], (1, 1, LANES)))
    return pl.pallas_call(
        functools.partial(_diff_kernel, tq=tq, lam_init=lam_init),
        grid=(B, DIFF_HEADS, T // tq),
        in_specs=[pl.BlockSpec((8, LANES), lambda b, h, i: (0, 0)),
                  pl.BlockSpec((1, 1, LANES), lambda b, h, i: (h, 0, 0)),
                  pl.BlockSpec((1, tq, LANES), lambda b, h, i: (b, i, h)),
                  pl.BlockSpec((1, T, LANES), lambda b, h, i: (b, 0, h)),
                  pl.BlockSpec((1, T, LANES), lambda b, h, i: (b, 0, DIFF_HEADS + h)),
                  pl.BlockSpec((1, DIFF_VDIM), lambda b, h, i: (0, 0))],
        out_specs=pl.BlockSpec((1, tq, LANES), lambda b, h, i: (b, i, h)),
        out_shape=jax.ShapeDtypeStruct((B, T, DIFF_HEADS * DIFF_VDIM), F32),
        compiler_params=_cparams("parallel", "parallel", "parallel"),
        name="diff_prompt",
    )(lam_tile, slopes, dq, diff_b, diff_b, subln_g)


def _layer_norm(z, g, b):
    mu = jnp.mean(z, axis=-1, keepdims=True)
    zc = z - mu
    var = jnp.mean(jnp.square(zc), axis=-1, keepdims=True)
    return zc * lax.rsqrt(var + LN_EPS) * g + b


def _out_proj_kernel(on_ref, od_ref, x_ref, wa_ref, wb_ref, g_ref, b_ref, wr_ref, br_ref,
                     x1_ref, x1b_ref, sel_ref, gate_ref):
    y = _dot(on_ref[...].astype(BF16), wa_ref[...]) + _dot(od_ref[...].astype(BF16), wb_ref[...])
    x1 = _layer_norm(DEEPNORM_ALPHA * x_ref[...] + y, g_ref[...], b_ref[...])
    x1_ref[...] = x1
    x1b_ref[...] = x1.astype(BF16)
    logits = lax.dot_general(wr_ref[...], x1, (((1,), (1,)), ((), ())), precision=HIGHEST,
                             preferred_element_type=F32) + br_ref[...]
    eidx = lax.broadcasted_iota(jnp.int32, logits.shape, 0).astype(F32)
    sel = jnp.zeros(logits.shape, F32)
    vmax = None
    for k in range(TOP_K):
        cur = jnp.where(sel > 0.5, -jnp.inf, logits)
        m = jnp.max(cur, axis=0, keepdims=True)
        first = jnp.min(jnp.where(cur == m, eidx, float(N_EXPERTS)), axis=0, keepdims=True)
        sel = jnp.where(eidx == first, 1.0, sel)
        if k == 0:
            vmax = m
    ex = jnp.where(sel > 0.5, jnp.exp(logits - vmax), 0.0)
    sel_ref[...] = sel
    gate_ref[...] = ex / jnp.sum(ex, axis=0, keepdims=True)


def _out_proj_ln_router(o_nsa, o_diff, x2d, wa, wb, ln_g, ln_b, wr_t, br):
    n = x2d.shape[0]
    tm = min(256, n)
    row = lambda i: (i, 0)
    fix = lambda i: (0, 0)
    col = lambda i: (0, i)
    return pl.pallas_call(
        _out_proj_kernel,
        grid=(n // tm,),
        in_specs=[pl.BlockSpec((tm, 512), row), pl.BlockSpec((tm, 512), row), pl.BlockSpec((tm, D_MODEL), row),
                  pl.BlockSpec((512, D_MODEL), fix), pl.BlockSpec((512, D_MODEL), fix),
                  pl.BlockSpec((1, D_MODEL), fix), pl.BlockSpec((1, D_MODEL), fix),
                  pl.BlockSpec((N_EXPERTS, D_MODEL), fix), pl.BlockSpec((N_EXPERTS, 1), fix)],
        out_specs=(pl.BlockSpec((tm, D_MODEL), row), pl.BlockSpec((tm, D_MODEL), row),
                   pl.BlockSpec((N_EXPERTS, tm), col), pl.BlockSpec((N_EXPERTS, tm), col)),
        out_shape=(jax.ShapeDtypeStruct((n, D_MODEL), F32), jax.ShapeDtypeStruct((n, D_MODEL), BF16),
                   jax.ShapeDtypeStruct((N_EXPERTS, n), F32), jax.ShapeDtypeStruct((N_EXPERTS, n), F32)),
        compiler_params=_cparams("parallel"),
        name="out_proj_ln_router",
    )(o_nsa, o_diff, x2d, wa, wb, ln_g, ln_b, wr_t, br)


def _moe_weight_kernel(w_ref, p_ref, wg_ref, wl_ref):
    half = MXU_DIM // 2
    for m in range(w_ref.shape[2] // MXU_DIM):
        y = _dot(w_ref[0, :, m * MXU_DIM:(m + 1) * MXU_DIM].astype(BF16), p_ref[...])
        wg_ref[0, :, m * half:(m + 1) * half] = y[:, :half].astype(BF16)
        wl_ref[0, :, m * half:(m + 1) * half] = y[:, half:].astype(BF16)


def _moe_split_gate_up(w_gate_up):
    ne, dm, two_ff = w_gate_up.shape
    half = MXU_DIM // 2
    p_np = np.zeros((MXU_DIM, MXU_DIM), np.float32)
    p_np[2 * np.arange(half), np.arange(half)] = 1.0
    p_np[2 * np.arange(half) + 1, half + np.arange(half)] = 1.0
    out = jax.ShapeDtypeStruct((ne, dm, two_ff // 2), BF16)
    ospec = pl.BlockSpec((1, dm, two_ff // 2), lambda e: (e, 0, 0))
    return pl.pallas_call(
        _moe_weight_kernel,
        grid=(ne,),
        in_specs=[pl.BlockSpec((1, dm, two_ff), lambda e: (e, 0, 0)),
                  pl.BlockSpec((MXU_DIM, MXU_DIM), lambda e: (0, 0))],
        out_specs=(ospec, ospec),
        out_shape=(out, out),
        compiler_params=_cparams("parallel"),
        name="moe_split_gate_up",
    )(w_gate_up, jnp.asarray(p_np, BF16))


def _moe_pos_kernel(sel_ref, u_ref, pos_ref, cnt_ref):
    sel = sel_ref[...]
    rank = _dot(sel.astype(BF16), u_ref[...])
    pos_ref[...] = jnp.where(sel > 0.5, rank, -1.0)
    cnt_ref[0] = jnp.sum(sel, axis=1, keepdims=True)


def _moe_positions(sel_t, s_tile):
    n = sel_t.shape[1]
    ns = n // s_tile
    upper = jnp.asarray(np.triu(np.ones((s_tile, s_tile), np.float32), 1), BF16)
    return pl.pallas_call(
        _moe_pos_kernel,
        grid=(ns,),
        in_specs=[pl.BlockSpec((N_EXPERTS, s_tile), lambda s: (0, s)),
                  pl.BlockSpec((s_tile, s_tile), lambda s: (0, 0))],
        out_specs=(pl.BlockSpec((N_EXPERTS, s_tile), lambda s: (0, s)),
                   pl.BlockSpec((1, N_EXPERTS, 1), lambda s: (s, 0, 0))),
        out_shape=(jax.ShapeDtypeStruct((N_EXPERTS, n), F32), jax.ShapeDtypeStruct((ns, N_EXPERTS, 1), F32)),
        compiler_params=_cparams("parallel"),
        name="moe_positions",
    )(sel_t, upper)


def _moe_kernel(nch_ref, x_ref, pos_ref, gate_ref, wg_ref, wl_ref, bg_ref, bl_ref, wd_ref, bd_ref, y_ref, *, chunk):
    s = pl.program_id(0)
    e = pl.program_id(1)

    @pl.when(e == 0)
    def _():
        y_ref[...] = jnp.zeros_like(y_ref)

    pos = pos_ref[0, 0].astype(jnp.int32)
    gate = gate_ref[0, 0]
    slot0 = lax.broadcasted_iota(jnp.int32, (chunk, 1), 0)

    def body(j, carry):
        hit = pos == slot0 + j * chunk
        onehot = jnp.where(hit, 1.0, 0.0).astype(BF16)
        xs = _dot(onehot, x_ref[...]).astype(BF16)
        hg = jnp.minimum(_dot(xs, wg_ref[0]) + bg_ref[0], SWIGLU_LIMIT)
        hl = jnp.clip(_dot(xs, wl_ref[0]) + bl_ref[0], -SWIGLU_LIMIT, SWIGLU_LIMIT)
        a = (hl + 1.0) * hg * jax.nn.sigmoid(SWIGLU_ALPHA * hg)
        out = _dot(a.astype(BF16), wd_ref[0]) + bd_ref[0]
        gslot = jnp.sum(jnp.where(hit, gate, 0.0), axis=1, keepdims=True)
        outg = (out * gslot).astype(BF16)
        y_ref[...] += lax.dot_general(onehot, outg, (((0,), (0,)), ((), ())), preferred_element_type=F32)
        return carry

    lax.fori_loop(0, nch_ref[s * N_EXPERTS + e], body, 0)


def _moe(x1b, pos_t, gate_t, counts, wg, wl, bg, bl, wd, bd, s_tile, chunk):
    n = x1b.shape[0]
    ns = n // s_tile
    nch = ((counts.reshape(ns * N_EXPERTS) + (chunk - 1)) // chunk).astype(jnp.int32)
    pos4 = pos_t.reshape(N_EXPERTS, ns, 1, s_tile)
    gate4 = gate_t.reshape(N_EXPERTS, ns, 1, s_tile)
    wspec = pl.BlockSpec((1, D_MODEL, D_MODEL), lambda s, e, nch: (e, 0, 0))
    bspec = pl.BlockSpec((1, 1, D_MODEL), lambda s, e, nch: (e, 0, 0))
    rspec = pl.BlockSpec((1, 1, 1, s_tile), lambda s, e, nch: (e, s, 0, 0))
    return pl.pallas_call(
        functools.partial(_moe_kernel, chunk=chunk),
        grid_spec=pltpu.PrefetchScalarGridSpec(
            num_scalar_prefetch=1,
            grid=(ns, N_EXPERTS),
            in_specs=[pl.BlockSpec((s_tile, D_MODEL), lambda s, e, nch: (s, 0)),
                      rspec, rspec, wspec, wspec, bspec, bspec, wspec, bspec],
            out_specs=pl.BlockSpec((s_tile, D_MODEL), lambda s, e, nch: (s, 0))),
        out_shape=jax.ShapeDtypeStruct((n, D_MODEL), F32),
        compiler_params=_cparams("parallel", "arbitrary"),
        name="moe_experts",
    )(nch, x1b, pos4, gate4, wg, wl, bg, bl, wd, bd)


def _ln2_kernel(x_ref, f_ref, g_ref, b_ref, o_ref):
    o_ref[...] = _layer_norm(DEEPNORM_ALPHA * x_ref[...] + f_ref[...], g_ref[...], b_ref[...])


def _residual_ln(x1, f, g, b):
    n = x1.shape[0]
    tm = min(512, n)
    row = lambda i: (i, 0)
    fix = lambda i: (0, 0)
    return pl.pallas_call(
        _ln2_kernel,
        grid=(n // tm,),
        in_specs=[pl.BlockSpec((tm, D_MODEL), row), pl.BlockSpec((tm, D_MODEL), row),
                  pl.BlockSpec((1, D_MODEL), fix), pl.BlockSpec((1, D_MODEL), fix)],
        out_specs=pl.BlockSpec((tm, D_MODEL), row),
        out_shape=jax.ShapeDtypeStruct((n, D_MODEL), F32),
        compiler_params=_cparams("parallel"),
        name="residual_ln2",
    )(x1, f, g, b)


def _post_mixer(o_nsa, o_diff, x2d, p, s_tile, chunk):
    x1, x1b, sel_t, gate_t = _out_proj_ln_router(o_nsa, o_diff, x2d, p["wo_a"], p["wo_b"], p["ln1_g"], p["ln1_b"],
                                                 p["wr_t"], p["br"])
    pos_t, counts = _moe_positions(sel_t, s_tile)
    f = _moe(x1b, pos_t, gate_t, counts, p["wg"], p["wl"], p["bg"], p["bl"], p["wd"], p["bd"], s_tile, chunk)
    return _residual_ln(x1, f, p["ln2_g"], p["ln2_b"])


def _sample_kernel(pt_ref, qm_ref, qd_ref, gt_ref, nrow_ref, wrow_ref, wcol_ref, drow_ref, win_ref,
                   w1_ref, pe_ref, w2_ref, m_ref, e_ref, r_ref, sn_ref, sd_ref, lam_ref, subg_ref, *rest,
                   n_pages, past_len, lam_init):
    nsa_pages = rest[:n_pages]
    diff_pages = rest[n_pages:2 * n_pages]
    onsa_ref, odiff_ref, nwin_ref, x_ref = rest[2 * n_pages:]
    n16 = past_len // CMP_STRIDE
    qpos = past_len

    for j, pg in enumerate(nsa_pages):
        for kind in range(2):
            x_ref[kind, j * PAGE_SIZE:(j + 1) * PAGE_SIZE, :] = pg[kind * LANES:(kind + 1) * LANES, :].T

    kv_cmp = []
    for kind in range(2):
        w1 = w1_ref[kind]
        cst = (_dot(pe_ref[kind, 0].astype(BF16), w1[:, 0:LANES]) + _dot(pe_ref[kind, 1].astype(BF16), w1[:, LANES:]))
        chunks = jnp.concatenate([x_ref[kind, pl.ds(l, n16, stride=CMP_STRIDE), :] for l in range(CMP_STRIDE)], axis=1)
        ab = _dot(chunks.astype(BF16), w1)
        nxt = pltpu.roll(ab[:, LANES:], n16 - 1, 0)
        hcmp = jax.nn.gelu(ab[:, 0:LANES] + nxt + cst[0:1])
        kv_cmp.append(_dot(hcmp.astype(BF16), w2_ref[kind]))

    qm = qm_ref[0]
    qmb = qm.astype(BF16)
    slope_n = sn_ref[:, 0:1]
    nrow = nrow_ref[0]
    wrow = wrow_ref[0]
    gt = gt_ref[0]

    def new_key_score(qrows, krow):
        return jnp.sum(qrows * krow, axis=1, keepdims=True) * ATTN_SCALE

    cmp_end = lax.broadcasted_iota(jnp.int32, (1, n16), 1) * CMP_STRIDE + (CMP_LEN - 1)
    dist_c = qpos - cmp_end
    s = _dot_nt(qmb, kv_cmp[0].astype(BF16)) * ATTN_SCALE - slope_n * dist_c.astype(F32)
    e, d = _softmax_parts(s, dist_c >= 0)
    p_cmp = e / d
    o_cmp = _dot(p_cmp.astype(BF16), kv_cmp[1].astype(BF16))
    pgrp = jnp.dot(r_ref[...], p_cmp, precision=HIGHEST, preferred_element_type=F32)
    imp = jnp.dot(pgrp, m_ref[...], precision=HIGHEST, preferred_element_type=F32)
    nbl = m_ref.shape[1]
    blk = lax.broadcasted_iota(jnp.int32, (NSA_HEADS, nbl), 1)
    sel = _select_blocks(imp, blk, jnp.full((NSA_HEADS, 1), qpos, jnp.int32), past_len // SEL_BLOCK + 1)
    mask_s = _dot(sel.astype(BF16), e_ref[...]) > 0.5

    kpos = lax.broadcasted_iota(jnp.int32, (1, past_len), 1)
    dist_k = (qpos - kpos).astype(F32)
    s = jnp.concatenate([_dot(qmb, pg[2 * LANES:3 * LANES, :].astype(BF16)) for pg in nsa_pages], axis=1)
    s = jnp.where(mask_s, s * ATTN_SCALE - slope_n * dist_k, -jnp.inf)
    s_new = new_key_score(qm, nrow[:, 2 * LANES:3 * LANES])
    m = jnp.maximum(jnp.max(s, axis=1, keepdims=True), s_new)
    e = jnp.exp(s - m)
    e_new = jnp.exp(s_new - m)
    d = jnp.sum(e, axis=1, keepdims=True) + e_new
    eb = e.astype(BF16)
    acc = e_new * nrow[:, 3 * LANES:4 * LANES]
    for j, pg in enumerate(nsa_pages):
        acc = acc + _dot_nt(eb[:, j * PAGE_SIZE:(j + 1) * PAGE_SIZE], pg[3 * LANES:4 * LANES, :].astype(BF16))
    o_sel = acc / d

    nwin = win_ref.shape[2]
    wpos = past_len - nwin + lax.broadcasted_iota(jnp.int32, (1, nwin), 1)
    dist_w = qpos - wpos
    mask_w = (dist_w >= 0) & (dist_w <= WINDOW)
    win = win_ref[0]
    s = _dot(qmb, win[0:LANES, :].astype(BF16)) * ATTN_SCALE - slope_n * dist_w.astype(F32)
    s = jnp.where(mask_w, s, -jnp.inf)
    s_new = new_key_score(qm, wrow[:, 0:LANES])
    m = jnp.maximum(jnp.max(s, axis=1, keepdims=True), s_new)
    e = jnp.exp(s - m)
    e_new = jnp.exp(s_new - m)
    d = jnp.sum(e, axis=1, keepdims=True) + e_new
    o_win = (_dot_nt(e.astype(BF16), win[LANES:2 * LANES, :].astype(BF16)) + e_new * wrow[:, LANES:2 * LANES]) / d

    onsa_ref[0] = gt[:, 0:1] * o_cmp + gt[:, 1:2] * o_sel + gt[:, 2:3] * o_win

    lane = lax.broadcasted_iota(jnp.int32, win.shape, 1)
    nwin_ref[0] = jnp.where(lane == nwin - 1, wcol_ref[0], pltpu.roll(win, nwin - 1, 1))

    drow = drow_ref[0]

    def diff_rows(pg, j):
        return pg[pl.ds(j, PAGE_SIZE, stride=DIFF_ROWS), :]

    qds = [qd_ref[0, h] for h in range(DIFF_HEADS)]
    qdb = [qh.astype(BF16) for qh in qds]
    s_pages = []
    for pg in diff_pages:
        sp = _dot_nt(qdb[0], diff_rows(pg, 0).astype(BF16))
        for h in range(1, DIFF_HEADS):
            sp = sp + _dot_nt(qdb[h], diff_rows(pg, h).astype(BF16))
        s_pages.append(sp)
    s = jnp.concatenate(s_pages, axis=1) * ATTN_SCALE - sd_ref[:, 0:1] * dist_k
    prod = qds[0] * drow[0:1]
    for h in range(1, DIFF_HEADS):
        prod = prod + qds[h] * drow[h:h + 1]
    s_new = jnp.sum(prod, axis=1, keepdims=True) * ATTN_SCALE
    m = jnp.maximum(jnp.max(s, axis=1, keepdims=True), s_new)
    e = jnp.exp(s - m)
    e_new = jnp.exp(s_new - m)
    d = jnp.sum(e, axis=1, keepdims=True) + e_new
    lam = lam_ref[0:1, 0:1]
    p = e / d
    p_new = e_new / d
    a = (p - lam * pltpu.roll(p, DIFF_HEADS, 0)).astype(BF16)
    a_new = p_new - lam * pltpu.roll(p_new, DIFF_HEADS, 0)
    accs = []
    for h in range(DIFF_HEADS):
        acc = a_new * drow[DIFF_HEADS + h:DIFF_HEADS + h + 1]
        for j, pg in enumerate(diff_pages):
            acc = acc + _dot(a[:, j * PAGE_SIZE:(j + 1) * PAGE_SIZE], diff_rows(pg, DIFF_HEADS + h).astype(BF16))
        accs.append(acc)
    acc = jnp.concatenate(accs, axis=1)
    lane_head = lax.broadcasted_iota(jnp.int32, acc.shape, 1) // DIFF_VDIM
    row = lax.broadcasted_iota(jnp.int32, acc.shape, 0)
    own = lane_head == row
    ms = jnp.sum(jnp.where(own, jnp.square(acc), 0.0), axis=1, keepdims=True) / DIFF_VDIM
    odiff_ref[0] = acc * lax.rsqrt(ms + RMS_EPS) * subg_ref[...] * (1.0 - lam_init)


def _sample_attention(page_table, qm, qd4, gt8, nrow, wrow, wcol, drow, win_t, pool_nsa, pool_diff,
                      w1ab, pe2, w2bd, m_mat, e_mat, r_mat, sn, sd, lam_tile, subg_tile, past_len, lam_init):
    B = qm.shape[0]
    n_pages = page_table.shape[1]
    nwin = win_t.shape[2]

    def per_b(shape):
        nd = len(shape)
        return pl.BlockSpec((1,) + shape, lambda b, pt: (b,) + (0,) * nd)

    def fixed(shape):
        nd = len(shape)
        return pl.BlockSpec(shape, lambda b, pt: (0,) * nd)

    def page_spec(pool, j):
        return pl.BlockSpec((None,) + pool.shape[1:], lambda b, pt: (pt[b, j], 0, 0))

    in_specs = [per_b((NSA_HEADS, LANES)), per_b((DIFF_HEADS, 8, LANES)), per_b((NSA_HEADS, LANES)),
                per_b((1, NSA_W)), per_b((1, WIN_W)), per_b((WIN_W, 1)), per_b((DIFF_ROWS, LANES)),
                per_b((WIN_W, nwin)),
                fixed(w1ab.shape), fixed(pe2.shape), fixed(w2bd.shape), fixed(m_mat.shape), fixed(e_mat.shape),
                fixed(r_mat.shape), fixed(sn.shape), fixed(sd.shape), fixed(lam_tile.shape), fixed(subg_tile.shape)]
    in_specs += [page_spec(pool_nsa, j) for j in range(n_pages)]
    in_specs += [page_spec(pool_diff, j) for j in range(n_pages)]
    return pl.pallas_call(
        functools.partial(_sample_kernel, n_pages=n_pages, past_len=past_len, lam_init=lam_init),
        grid_spec=pltpu.PrefetchScalarGridSpec(
            num_scalar_prefetch=1,
            grid=(B,),
            in_specs=in_specs,
            out_specs=(per_b((NSA_HEADS, LANES)), per_b((8, DIFF_HEADS * DIFF_VDIM)), per_b((WIN_W, nwin))),
            scratch_shapes=[pltpu.VMEM((2, past_len, LANES), F32)]),
        out_shape=(jax.ShapeDtypeStruct((B, NSA_HEADS, LANES), F32),
                   jax.ShapeDtypeStruct((B, 8, DIFF_HEADS * DIFF_VDIM), F32),
                   jax.ShapeDtypeStruct((B, WIN_W, nwin), F32)),
        compiler_params=_cparams("arbitrary"),
        name="sample_attention",
    )(page_table, qm, qd4, gt8, nrow, wrow, wcol, drow, win_t, w1ab, pe2, w2bd, m_mat, e_mat, r_mat, sn, sd,
      lam_tile, subg_tile, *([pool_nsa] * n_pages), *([pool_diff] * n_pages))


def _cmp_to_sel(n_cmp, n_sel):
    c0 = np.arange(n_cmp)[:, None] * CMP_STRIDE
    s0 = np.arange(n_sel)[None, :] * SEL_BLOCK
    ov = np.clip(np.minimum(c0 + CMP_LEN, s0 + SEL_BLOCK) - np.maximum(c0, s0), 0, None)
    return (ov / CMP_LEN).astype(np.float32)


def _lambda_init(layer):
    return 0.8 - 0.6 * math.exp(-0.3 * layer)


def _prep_params(l, w_in, w_out, diff_subln_g, ln1_g, ln1_b, ln2_g, ln2_b, w_router, b_router,
                 w_gate_up, b_gate_up, w_down, b_down):
    gate_end = _GT0 + 3 * NSA_HEADS
    wt = w_in[l].T
    w_t = jnp.concatenate([wt[:gate_end], jnp.zeros((_GATE_PAD, D_MODEL), F32), wt[gate_end:]], axis=0).astype(BF16)
    wg, wl = _moe_split_gate_up(w_gate_up[l])
    bgu = b_gate_up[l]
    return {
        "w_t": w_t,
        "wo_a": w_out[l][:512].astype(BF16), "wo_b": w_out[l][512:].astype(BF16),
        "ln1_g": ln1_g[l][None], "ln1_b": ln1_b[l][None], "ln2_g": ln2_g[l][None], "ln2_b": ln2_b[l][None],
        "wr_t": w_router[l].T, "br": b_router[l][:, None],
        "wg": wg, "wl": wl, "bg": bgu[:, None, 0::2], "bl": bgu[:, None, 1::2],
        "wd": w_down[l].astype(BF16), "bd": b_down[l][:, None, :],
        "subln_g": diff_subln_g[l][None],
    }


def _sample_cmp_params(cmp_pe, cmp_w1, cmp_w2):
    eye = jnp.eye(NSA_KV_HEADS, dtype=F32)
    halves = []
    for half in range(2):
        w = cmp_w1[:, half * CMP_STRIDE:(half + 1) * CMP_STRIDE]
        wb = jnp.einsum('klde,gh->klgdhe', w, eye)
        halves.append(wb.reshape(2, CMP_STRIDE * LANES, LANES))
    w1ab = jnp.concatenate(halves, axis=-1).astype(BF16)
    pe = cmp_pe.reshape(2, 2, CMP_STRIDE, 1, HEAD_DIM)
    pe2 = jnp.broadcast_to(pe, (2, 2, CMP_STRIDE, NSA_KV_HEADS, HEAD_DIM)).reshape(2, 2, 1, CMP_STRIDE * LANES)
    pe2 = jnp.broadcast_to(pe2, (2, 2, 8, CMP_STRIDE * LANES))
    w2bd = jnp.einsum('kde,gh->kgdhe', cmp_w2, eye).reshape(2, LANES, LANES).astype(BF16)
    return w1ab, pe2, w2bd


def _feature_major_to_rows(a_t, kinds):
    B, _, T = a_t.shape
    return a_t.reshape(B, kinds, NSA_KV_HEADS, HEAD_DIM, T).transpose(0, 4, 1, 2, 3)


def _prompt_group(x, p, cmp_pe, cmp_w1, cmp_w2, lam_tile, lam_init):
    B, T, _ = x.shape
    x2d = x.reshape(B * T, D_MODEL)
    q_nsa, q_tb, nsa_rows, nsa_t, nsa_tb, _, win_t, win_tb, gates, dq, diff8, diff_b = _in_proj(x2d, p["w_t"], B)
    r3 = lambda a: a.reshape(B, T, a.shape[-1])
    q_nsa, nsa_rows3, gates, dq = map(r3, (q_nsa, nsa_rows, gates, dq))
    kdim = CMP_LEN * HEAD_DIM
    kvc = _compress_prompt(nsa_rows3, cmp_pe.reshape(2, 1, kdim), cmp_w1.reshape(2, kdim, HEAD_DIM).astype(BF16),
                           cmp_w2.astype(BF16))
    n16 = T // CMP_STRIDE
    n_sel = -(-T // SEL_BLOCK)
    m_np = np.zeros((n16, n_sel), np.float32)
    m_np[:n16 - 1] = _cmp_to_sel(n16 - 1, n_sel)
    o_cmp, sel = _cmp_attn_prompt(q_nsa, q_tb, kvc, jnp.asarray(m_np.T))
    e_np = np.zeros((NSA_KV_HEADS, LANES, T), np.float32)
    for g in range(NSA_KV_HEADS):
        e_np[g, g * n_sel + np.arange(T) // SEL_BLOCK, np.arange(T)] = 1.0
    o_nsa = _sel_win_prompt(q_nsa, nsa_tb, win_tb, sel, jnp.asarray(e_np, BF16), o_cmp, gates)
    o_diff = _diff_prompt(lam_tile, dq, diff_b.reshape(B, T, DIFF_W), p["subln_g"], lam_init)
    y = _post_mixer(o_nsa.reshape(B * T, 512), o_diff.reshape(B * T, 512), x2d, p, s_tile=T, chunk=288)
    nwin = min(WINDOW, T)
    return (y.reshape(B, T, D_MODEL),
            _feature_major_to_rows(nsa_t, 4),
            diff8.reshape(B, T, 2, DIFF_HEADS, DIFF_VDIM),
            _feature_major_to_rows(win_t[:, :, T - nwin:], 2))


def _sample_group(x, pool_nsa, pool_diff, win_buf, page_table, p, cmp_pe, cmp_w1, cmp_w2, lam_tile, lam_init):
    B, T, _ = x.shape
    past_len = page_table.shape[1] * PAGE_SIZE
    x2d = x.reshape(B, D_MODEL)
    q_nsa, _, nsa_rows, nsa_t, _, win_rows, _, _, gates, dq, diff8, _ = _in_proj(x2d, p["w_t"], 1)
    qh = q_nsa.reshape(B, NSA_HEADS, HEAD_DIM)
    grp = (np.arange(NSA_HEADS) // NSA_GROUP)[None, :, None, None] == np.arange(NSA_KV_HEADS)[None, None, :, None]
    qm = (qh[:, :, None, :] * jnp.asarray(grp, F32)).reshape(B, NSA_HEADS, LANES)
    dq4 = dq.reshape(B, DIFF_HEADS, 1, 2 * HEAD_DIM)
    rr = np.arange(8)
    rowmask = (rr[None, :, None] % DIFF_HEADS == np.arange(DIFF_HEADS)[:, None, None]) & (
        rr[None, :, None] // DIFF_HEADS == (np.arange(LANES) // HEAD_DIM)[None, None, :])
    qd4 = dq4 * jnp.asarray(rowmask, F32)[None]
    gt8 = jnp.pad(gates[:, :3 * NSA_HEADS].reshape(B, 3, NSA_HEADS).transpose(0, 2, 1), ((0, 0), (0, 0), (0, LANES - 3)))
    nwin = win_buf.shape[1]
    win_t = win_buf.transpose(0, 2, 3, 4, 1).reshape(B, WIN_W, nwin)
    n_pool = pool_nsa.shape[0]
    pool_nsa_t = pool_nsa.transpose(0, 2, 3, 4, 1).reshape(n_pool, NSA_W, PAGE_SIZE)
    pool_diff_r = pool_diff.reshape(n_pool, PAGE_SIZE * DIFF_ROWS, LANES)
    w1ab, pe2, w2bd = _sample_cmp_params(cmp_pe, cmp_w1, cmp_w2)
    n16 = past_len // CMP_STRIDE
    n_sel = past_len // SEL_BLOCK + 1
    m_np = np.zeros((n16, 64), np.float32)
    m_np[:n16 - 1, :n_sel] = _cmp_to_sel(n16 - 1, n_sel)
    e_np = np.zeros((64, past_len), np.float32)
    e_np[np.arange(past_len) // SEL_BLOCK, np.arange(past_len)] = 1.0
    r_np = (np.arange(8)[:, None] // NSA_GROUP == np.arange(8)[None, :] // NSA_GROUP).astype(np.float32)
    sn = jnp.asarray(np.tile(np.asarray(NSA_SLOPES, np.float32)[:, None], (1, LANES)))
    sd = jnp.asarray(np.tile(np.asarray(DIFF_SLOPES, np.float32)[np.arange(8) % DIFF_HEADS, None], (1, LANES)))
    subg_tile = jnp.tile(p["subln_g"], (1, DIFF_HEADS))
    o_nsa8, o_diff8, new_win_t = _sample_attention(
        page_table, qm, qd4, gt8, nsa_rows[:, None, :], win_rows[:, None, :], win_rows[:, :, None],
        diff8.reshape(B, DIFF_ROWS, LANES), win_t, pool_nsa_t, pool_diff_r,
        w1ab, pe2, w2bd, jnp.asarray(m_np), jnp.asarray(e_np, BF16), jnp.asarray(r_np), sn, sd, lam_tile, subg_tile,
        past_len, lam_init)
    o8 = o_nsa8.reshape(B, NSA_KV_HEADS, NSA_GROUP, NSA_KV_HEADS, HEAD_DIM)
    o_nsa = jnp.stack([o8[:, g, :, g] for g in range(NSA_KV_HEADS)], axis=1).reshape(B, 512)
    d8 = o_diff8[:, :DIFF_HEADS].reshape(B, DIFF_HEADS, DIFF_HEADS, DIFF_VDIM)
    o_diff = jnp.stack([d8[:, h, h] for h in range(DIFF_HEADS)], axis=1).reshape(B, 512)
    y = _post_mixer(o_nsa, o_diff, x2d, p, s_tile=B, chunk=B)
    return (y.reshape(B, T, D_MODEL),
            _feature_major_to_rows(nsa_t, 4).reshape(B, T, 4, NSA_KV_HEADS, HEAD_DIM),
            diff8.reshape(B, T, 2, DIFF_HEADS, DIFF_VDIM),
            _feature_major_to_rows(new_win_t, 2))


def kernel(x_prompt, x_sample, cache_nsa_kv, cache_diff_kv, state_nsa_win, page_table, w_in, w_out, cmp_pe, cmp_w1,
           cmp_w2, diff_lambda, diff_subln_g, ln1_g, ln1_b, ln2_g, ln2_b, w_router, b_router, w_gate_up, b_gate_up,
           w_down, b_down):
    depth = w_in.shape[0]
    xp, xs = x_prompt, x_sample
    outs = [[] for _ in range(6)]
    for l in range(depth):
        lam0 = _lambda_init(l)
        lv = diff_lambda[l].astype(F32)
        lam = jnp.exp(jnp.sum(lv[0] * lv[1])) - jnp.exp(jnp.sum(lv[2] * lv[3])) + lam0
        lam_tile = jnp.full((8, LANES), lam, F32)
        p = _prep_params(l, w_in, w_out, diff_subln_g, ln1_g, ln1_b, ln2_g, ln2_b, w_router, b_router,
                         w_gate_up, b_gate_up, w_down, b_down)
        xp, r_nsa, r_diff, r_win = _prompt_group(xp, p, cmp_pe[l], cmp_w1[l], cmp_w2[l], lam_tile, lam0)
        xs, s_nsa, s_diff, s_win = _sample_group(xs, cache_nsa_kv[l], cache_diff_kv[l], state_nsa_win[l], page_table,
                                                 p, cmp_pe[l], cmp_w1[l], cmp_w2[l], lam_tile, lam0)
        for lst, v in zip(outs, (r_nsa, r_diff, r_win, s_nsa, s_diff, s_win)):
            lst.append(v)
    return (xp, xs) + tuple(jnp.stack(o) for o in outs)
```

```python
import functools
import math

import numpy as np
import jax
import jax.numpy as jnp
from jax import lax
from jax.experimental import pallas as pl
from jax.experimental.pallas import tpu as pltpu

F32 = jnp.float32
BF16 = jnp.bfloat16
HIGHEST = lax.Precision.HIGHEST

D_MODEL = 1024
HEAD_DIM = 64
NSA_HEADS = 8
NSA_KV_HEADS = 2
NSA_GROUP = NSA_HEADS // NSA_KV_HEADS
CMP_LEN = 32
CMP_STRIDE = 16
SEL_BLOCK = 64
SEL_TOPN = 16
WINDOW = 512
DIFF_HEADS = 4
DIFF_VDIM = 2 * HEAD_DIM
N_EXPERTS = 32
TOP_K = 4
SWIGLU_ALPHA = 1.702
SWIGLU_LIMIT = 7.0
LN_EPS = 1e-5
RMS_EPS = 1e-5
ATTN_SCALE = HEAD_DIM ** -0.5
LOG2E = math.log2(math.e)
DEPTH = 1
DEEPNORM_ALPHA = (2 * DEPTH) ** 0.25
PAGE_SIZE = 128

NSA_SLOPES = tuple(2.0 ** (-8.0 * (i + 1) / NSA_HEADS) for i in range(NSA_HEADS))
DIFF_SLOPES = tuple(2.0 ** (-8.0 * (i + 1) / DIFF_HEADS) for i in range(DIFF_HEADS))

VMEM_LIMIT_BYTES = 56 * 1024 * 1024
LANES = 128
MXU_DIM = 256

MOE_CHUNK = MXU_DIM
MOE_SUPERTILE = 13 * LANES

NSA_W = 4 * NSA_KV_HEADS * HEAD_DIM
WIN_W = 2 * NSA_KV_HEADS * HEAD_DIM
DIFF_W = 2 * DIFF_HEADS * DIFF_VDIM
DIFF_ROWS = DIFF_W // LANES
_GATE_PAD = LANES - 3 * NSA_HEADS
_Q0, _NSA0, _WIN0, _GT0, _DQ0, _DIFF0, _PROJ_ROWS = 0, 512, 1024, 1280, 1408, 1920, 2944


def _cparams(*sem):
    return pltpu.CompilerParams(dimension_semantics=sem, vmem_limit_bytes=VMEM_LIMIT_BYTES)


def _softmax_parts(s, mask):
    s = jnp.where(mask, s, -jnp.inf)
    m = jnp.max(s, axis=-1, keepdims=True)
    m = jnp.where(jnp.isfinite(m), m, 0.0)
    e = jnp.exp(s - m)
    d = jnp.sum(e, axis=-1, keepdims=True)
    return e, jnp.where(d > 0, d, 1.0)


def _softmax2_parts(s, mask):
    s = jnp.where(mask, s, -jnp.inf)
    m = jnp.max(s, axis=-1, keepdims=True)
    m = jnp.where(jnp.isfinite(m), m, 0.0)
    e = jnp.exp2(s - m)
    d = jnp.sum(e, axis=-1, keepdims=True)
    return e, jnp.where(d > 0, d, 1.0)


def _dot_nt(a, b):
    return lax.dot_general(a, b, (((1,), (1,)), ((), ())), preferred_element_type=F32)


def _dot(a, b):
    return jnp.dot(a, b, preferred_element_type=F32)


def _pad_head(qh, g):
    z = jnp.zeros_like(qh)
    return jnp.concatenate([qh, z] if g == 0 else [z, qh], axis=1)


def _select_blocks(imp, blk, qpos, n_blk_lanes):
    cur = jnp.right_shift(qpos, int(math.log2(SEL_BLOCK)))
    valid = blk * SEL_BLOCK <= qpos
    forced = (blk == 0) | (blk == cur) | (blk == cur - 1)
    val = jnp.where(forced, jnp.inf, jnp.where(valid, imp, -jnp.inf))
    rank = jnp.zeros(val.shape, F32)
    for i in range(n_blk_lanes):
        ci = val[:, i:i + 1]
        beats = (ci > val) | ((ci == val) & (blk > i))
        rank = rank + jnp.where(beats, 1.0, 0.0)
    return jnp.where(rank < SEL_TOPN, 1.0, 0.0)


def _in_proj_kernel(x_ref, w_ref, q_ref, qtb_ref, nsa_ref, nsat_ref, nsatb_ref, win_ref, wint_ref, wintb_ref, gt_ref,
                    dq_ref, diff_ref, diffb_ref):
    tm = x_ref.shape[0]
    xb = x_ref[...].astype(BF16)
    q_ref[...] = _dot_nt(xb, w_ref[_Q0:_NSA0, :])
    qtb_ref[0] = _dot_nt(w_ref[_Q0:_NSA0, :], xb).astype(BF16)
    nsa_ref[...] = _dot_nt(xb, w_ref[_NSA0:_WIN0, :])
    r = _dot_nt(w_ref[_NSA0:_WIN0, :], xb)
    nsat_ref[0] = r
    nsatb_ref[0] = r.astype(BF16)
    win_ref[...] = _dot_nt(xb, w_ref[_WIN0:_GT0, :])
    r = _dot_nt(w_ref[_WIN0:_GT0, :], xb)
    wint_ref[0] = r
    wintb_ref[0] = r.astype(BF16)
    gt_ref[...] = jax.nn.sigmoid(_dot_nt(xb, w_ref[_GT0:_DQ0, :]))
    dq_ref[...] = _dot_nt(xb, w_ref[_DQ0:_DIFF0, :])
    r = _dot_nt(xb, w_ref[_DIFF0:_PROJ_ROWS, :])
    diffb_ref[...] = r.astype(BF16)
    for j in range(DIFF_ROWS):
        diff_ref[pl.ds(j, tm, stride=DIFF_ROWS), :] = r[:, j * LANES:(j + 1) * LANES]


def _in_proj(x2d, w_t, batch):
    n = x2d.shape[0]
    t = n // batch
    tm = min(256, t)
    nt = t // tm
    row = lambda b, i: (b * nt + i, 0)
    tr = lambda b, i: (b, 0, i)
    tok = lambda w: pl.BlockSpec((tm, w), row)
    return pl.pallas_call(
        _in_proj_kernel,
        grid=(batch, nt),
        in_specs=[pl.BlockSpec((tm, D_MODEL), row),
                  pl.BlockSpec((_PROJ_ROWS, D_MODEL), lambda b, i: (0, 0))],
        out_specs=(tok(512), pl.BlockSpec((1, 512, tm), tr),
                   tok(NSA_W), pl.BlockSpec((1, NSA_W, tm), tr), pl.BlockSpec((1, NSA_W, tm), tr),
                   tok(WIN_W), pl.BlockSpec((1, WIN_W, tm), tr), pl.BlockSpec((1, WIN_W, tm), tr),
                   tok(LANES), tok(512), pl.BlockSpec((tm * DIFF_ROWS, LANES), row), tok(DIFF_W)),
        out_shape=(jax.ShapeDtypeStruct((n, 512), F32), jax.ShapeDtypeStruct((batch, 512, t), BF16),
                   jax.ShapeDtypeStruct((n, NSA_W), F32),
                   jax.ShapeDtypeStruct((batch, NSA_W, t), F32), jax.ShapeDtypeStruct((batch, NSA_W, t), BF16),
                   jax.ShapeDtypeStruct((n, WIN_W), F32),
                   jax.ShapeDtypeStruct((batch, WIN_W, t), F32), jax.ShapeDtypeStruct((batch, WIN_W, t), BF16),
                   jax.ShapeDtypeStruct((n, LANES), F32), jax.ShapeDtypeStruct((n, 512), F32),
                   jax.ShapeDtypeStruct((n * DIFF_ROWS, LANES), F32), jax.ShapeDtypeStruct((n, DIFF_W), BF16)),
        compiler_params=_cparams("parallel", "parallel"),
        name="in_proj",
    )(x2d, w_t)


def _compress_chunks(load_rows, w1, pe, w2, n16):
    cst = _dot(pe[0].astype(BF16), w1[:, 0:LANES]) + _dot(pe[1].astype(BF16), w1[:, LANES:])
    chunks = jnp.concatenate([load_rows(l) for l in range(CMP_STRIDE)], axis=1)
    ab = _dot(chunks.astype(BF16), w1)
    nxt = pltpu.roll(ab[:, LANES:], n16 - 1, 0)
    h = jax.nn.gelu(ab[:, 0:LANES] + nxt + cst[0:1])
    return _dot(h.astype(BF16), w2)


def _compress_kernel(rows_ref, w1_ref, pe_ref, w2_ref, o_ref):
    n16 = rows_ref.shape[1] // CMP_STRIDE
    o_ref[0, 0] = _compress_chunks(lambda l: rows_ref[0, pl.ds(l, n16, stride=CMP_STRIDE), :],
                                   w1_ref[0], pe_ref[0], w2_ref[0], n16)


def _compress_prompt(nsa_rows, w1ab, pe2, w2bd):
    B, T, _ = nsa_rows.shape
    n16 = T // CMP_STRIDE
    return pl.pallas_call(
        _compress_kernel,
        grid=(B, 2),
        in_specs=[pl.BlockSpec((1, T, LANES), lambda b, k: (b, 0, k)),
                  pl.BlockSpec((1,) + w1ab.shape[1:], lambda b, k: (k, 0, 0)),
                  pl.BlockSpec((1,) + pe2.shape[1:], lambda b, k: (k, 0, 0, 0)),
                  pl.BlockSpec((1,) + w2bd.shape[1:], lambda b, k: (k, 0, 0))],
        out_specs=pl.BlockSpec((1, 1, n16, LANES), lambda b, k: (b, k, 0, 0)),
        out_shape=jax.ShapeDtypeStruct((B, 2, n16, LANES), F32),
        compiler_params=_cparams("parallel", "parallel"),
        name="compress_prompt",
    )(nsa_rows, w1ab, pe2, w2bd)


def _select_blocks_t(imp, qpos):
    n_blk = imp.shape[0]
    blk = lax.broadcasted_iota(jnp.int32, imp.shape, 0)
    cur = jnp.right_shift(qpos, int(math.log2(SEL_BLOCK)))
    valid = blk * SEL_BLOCK <= qpos
    forced = (blk == 0) | (blk == cur) | (blk == cur - 1)
    val = jnp.where(forced, jnp.inf, jnp.where(valid, imp, -jnp.inf))
    rank = jnp.zeros(val.shape, F32)
    for i in range(n_blk):
        ci = val[i:i + 1, :]
        beats = (ci > val) | ((ci == val) & (blk > i))
        rank = rank + jnp.where(beats, 1.0, 0.0)
    return jnp.where(rank < SEL_TOPN, 1.0, 0.0)


def _cmp_attn_kernel(q_ref, qt_ref, kc_ref, vc_ref, mt_ref, o_ref, selt_ref, *, tq):
    i = pl.program_id(1)
    ncmp = kc_ref.shape[2]
    kcb = kc_ref[0, 0].astype(BF16)
    vcb = vc_ref[0, 0].astype(BF16)
    qpos_c = i * tq + lax.broadcasted_iota(jnp.int32, (tq, 1), 0)
    end_r = lax.broadcasted_iota(jnp.int32, (1, ncmp), 1) * CMP_STRIDE + (CMP_LEN - 1)
    mask = end_r <= qpos_c
    end_rf = end_r.astype(F32)
    q = q_ref[0] * (ATTN_SCALE * LOG2E)
    for h in range(NSA_HEADS):
        g = h // NSA_GROUP
        qp = _pad_head(q[:, h * HEAD_DIM:(h + 1) * HEAD_DIM], g).astype(BF16)
        e, d = _softmax2_parts(_dot_nt(qp, kcb) + (NSA_SLOPES[h] * LOG2E) * end_rf, mask)
        oh = _dot(e.astype(BF16), vcb)[:, g * HEAD_DIM:(g + 1) * HEAD_DIM] / d
        o_ref[0, :, h * HEAD_DIM:(h + 1) * HEAD_DIM] = oh
    qpos_r = i * tq + lax.broadcasted_iota(jnp.int32, (1, tq), 1)
    end_c = lax.broadcasted_iota(jnp.int32, (ncmp, 1), 0) * CMP_STRIDE + (CMP_LEN - 1)
    mask_t = end_c <= qpos_r
    end_cf = end_c.astype(F32)
    qt = qt_ref[0]
    zero = jnp.zeros((HEAD_DIM, tq), BF16)
    psum = [None, None]
    for h in range(NSA_HEADS):
        g = h // NSA_GROUP
        qh = qt[h * HEAD_DIM:(h + 1) * HEAD_DIM, :]
        qpt = jnp.concatenate([qh, zero] if g == 0 else [zero, qh], axis=0)
        s = _dot(kcb, qpt) * (ATTN_SCALE * LOG2E) + (NSA_SLOPES[h] * LOG2E) * end_cf
        s = jnp.where(mask_t, s, -jnp.inf)
        m = jnp.max(s, axis=0, keepdims=True)
        m = jnp.where(jnp.isfinite(m), m, 0.0)
        e = jnp.exp2(s - m)
        d = jnp.sum(e, axis=0, keepdims=True)
        p = e / jnp.where(d > 0, d, 1.0)
        psum[g] = p if psum[g] is None else psum[g] + p
    sels = []
    for g in range(NSA_KV_HEADS):
        imp = jnp.dot(mt_ref[...], psum[g], precision=HIGHEST, preferred_element_type=F32)
        sels.append(_select_blocks_t(imp, qpos_r))
    sels.append(jnp.zeros((LANES - NSA_KV_HEADS * mt_ref.shape[0], tq), F32))
    selt_ref[0] = jnp.concatenate(sels, axis=0)


def _cmp_attn_prompt(q_nsa, q_tb, kvc, mt_mat, tq=256):
    B, T, _ = q_nsa.shape
    ncmp = kvc.shape[2]
    n_sel = mt_mat.shape[0]
    return pl.pallas_call(
        functools.partial(_cmp_attn_kernel, tq=tq),
        grid=(B, T // tq),
        in_specs=[pl.BlockSpec((1, tq, 512), lambda b, i: (b, i, 0)),
                  pl.BlockSpec((1, 512, tq), lambda b, i: (b, 0, i)),
                  pl.BlockSpec((1, 1, ncmp, LANES), lambda b, i: (b, 0, 0, 0)),
                  pl.BlockSpec((1, 1, ncmp, LANES), lambda b, i: (b, 1, 0, 0)),
                  pl.BlockSpec((n_sel, ncmp), lambda b, i: (0, 0))],
        out_specs=(pl.BlockSpec((1, tq, 512), lambda b, i: (b, i, 0)),
                   pl.BlockSpec((1, LANES, tq), lambda b, i: (b, 0, i))),
        out_shape=(jax.ShapeDtypeStruct((B, T, 512), F32), jax.ShapeDtypeStruct((B, LANES, T), F32)),
        compiler_params=_cparams("parallel", "parallel"),
        name="cmp_attn_prompt",
    )(q_nsa, q_tb, kvc, kvc, mt_mat)


def _sel_win_kernel(q_ref, ks_ref, vs_ref, kw_ref, vw_ref, sel_ref, e_ref, ocmp_ref, gt_ref, o_ref, *, tq, span,
                    kv_step):
    i = pl.program_id(1)
    T = ks_ref.shape[2]
    qpos = i * tq + lax.broadcasted_iota(jnp.int32, (tq, 1), 0)
    q = q_ref[0] * (ATTN_SCALE * LOG2E)
    gt = gt_ref[0]
    ocmp = ocmp_ref[0]
    selb = sel_ref[0].astype(BF16)
    heads = [(h, h // NSA_GROUP, h * HEAD_DIM, (h + 1) * HEAD_DIM) for h in range(NSA_HEADS)]
    qps = [_pad_head(q[:, lo:hi], g).astype(BF16) for _, g, lo, hi in heads]
    start = pl.multiple_of(jnp.maximum(i * tq - WINDOW, 0), LANES)
    wpos = start + lax.broadcasted_iota(jnp.int32, (1, span), 1)
    dist_w = qpos - wpos
    mask_w = (dist_w >= 0) & (dist_w <= WINDOW)
    wposf = wpos.astype(F32)
    kwb = kw_ref[0, :, pl.ds(start, span)]
    vwb = vw_ref[0, :, pl.ds(start, span)]
    partial = []
    for h, g, lo, hi in heads:
        s = _dot(qps[h], kwb) + (NSA_SLOPES[h] * LOG2E) * wposf
        e, d = _softmax2_parts(s, mask_w)
        o_win = _dot_nt(e.astype(BF16), vwb)[:, g * HEAD_DIM:(g + 1) * HEAD_DIM] / d
        partial.append(gt[:, h:h + 1] * ocmp[:, lo:hi] + gt[:, 2 * NSA_HEADS + h:2 * NSA_HEADS + h + 1] * o_win)
    tiles_per_step = kv_step // tq
    for br in range(T // kv_step):
        @pl.when((i >= br * tiles_per_step) & (i < (br + 1) * tiles_per_step))
        def _(br=br):
            kv = (br + 1) * kv_step
            kpos = lax.broadcasted_iota(jnp.int32, (1, kv), 1)
            causal = kpos <= qpos
            kposf = kpos.astype(F32)
            ksb = ks_ref[0, :, 0:kv]
            vsb = vs_ref[0, :, 0:kv]
            for g in range(NSA_KV_HEADS):
                expand = lax.dot_general(selb, e_ref[g, :, 0:kv], (((0,), (0,)), ((), ())), preferred_element_type=F32)
                mask_s = (expand > 0.5) & causal
                for h, _, lo, hi in heads[g * NSA_GROUP:(g + 1) * NSA_GROUP]:
                    s = _dot(qps[h], ksb) + (NSA_SLOPES[h] * LOG2E) * kposf
                    e, d = _softmax2_parts(s, mask_s)
                    o_sel = _dot_nt(e.astype(BF16), vsb)[:, g * HEAD_DIM:(g + 1) * HEAD_DIM] / d
                    o_ref[0, :, lo:hi] = partial[h] + gt[:, NSA_HEADS + h:NSA_HEADS + h + 1] * o_sel


def _sel_win_prompt(q_nsa, nsa_t, win_t, sel_t, e_mat, o_cmp, gates, tq=256):
    B, T, _ = q_nsa.shape
    span = WINDOW + tq
    kv = lambda j: pl.BlockSpec((1, LANES, T), lambda b, i: (b, j, 0))
    return pl.pallas_call(
        functools.partial(_sel_win_kernel, tq=tq, span=span, kv_step=min(2 * tq, T)),
        grid=(B, T // tq),
        in_specs=[pl.BlockSpec((1, tq, 512), lambda b, i: (b, i, 0)),
                  kv(2), kv(3), kv(0), kv(1),
                  pl.BlockSpec((1, LANES, tq), lambda b, i: (b, 0, i)),
                  pl.BlockSpec((NSA_KV_HEADS, LANES, T), lambda b, i: (0, 0, 0)),
                  pl.BlockSpec((1, tq, 512), lambda b, i: (b, i, 0)),
                  pl.BlockSpec((1, tq, LANES), lambda b, i: (b, i, 0))],
        out_specs=pl.BlockSpec((1, tq, 512), lambda b, i: (b, i, 0)),
        out_shape=jax.ShapeDtypeStruct((B, T, 512), F32),
        compiler_params=_cparams("parallel", "parallel"),
        name="sel_win_prompt",
    )(q_nsa, nsa_t, nsa_t, win_t, win_t, sel_t, e_mat, o_cmp, gates)


def _diff_kernel(lam_ref, sl_ref, q_ref, k_ref, v_ref, g_ref, o_ref, *, tq, lam_init):
    i = pl.program_id(2)
    T = k_ref.shape[1]
    lam = lam_ref[0:1, 0:1]
    q = q_ref[0] * (ATTN_SCALE * LOG2E)
    qps = [_pad_head(q[:, c * HEAD_DIM:(c + 1) * HEAD_DIM], c).astype(BF16) for c in range(2)]
    tri = lax.broadcasted_iota(jnp.int32, (tq, tq), 0) >= lax.broadcasted_iota(jnp.int32, (tq, tq), 1)
    for br in range(T // tq):
        @pl.when(i == br)
        def _(br=br):
            off = br * tq
            kv = off + tq
            kb = k_ref[0, 0:kv, :]
            vb = v_ref[0, 0:kv, :]
            col = (sl_ref[0, :, 0:1] * LOG2E) * lax.broadcasted_iota(jnp.int32, (1, kv), 1).astype(F32)
            outs = []
            for c in range(2):
                s = _dot_nt(qps[c], kb) + col
                s_d = jnp.where(tri, s[:, off:], -jnp.inf)
                m = jnp.max(s_d, axis=-1, keepdims=True)
                if off:
                    m = jnp.maximum(m, jnp.max(s[:, :off], axis=-1, keepdims=True))
                e_d = jnp.exp2(s_d - m)
                d = jnp.sum(e_d, axis=-1, keepdims=True)
                o = _dot(e_d.astype(BF16), vb[off:, :])
                if off:
                    e_o = jnp.exp2(s[:, :off] - m)
                    d = d + jnp.sum(e_o, axis=-1, keepdims=True)
                    o = o + _dot(e_o.astype(BF16), vb[:off, :])
                outs.append(o / d)
            od = outs[0] - lam * outs[1]
            od = od * lax.rsqrt(jnp.mean(jnp.square(od), axis=-1, keepdims=True) + RMS_EPS)
            o_ref[0] = od * g_ref[...] * (1.0 - lam_init)


def _diff_prompt(lam_tile, dq, diff_b, subln_g, lam_init, tq=256):
    B, T, _ = dq.shape
    slopes = jnp.asarray(np.tile(np.asarray(DIFF_SLOPES, np.float32)[:, None, None], (1, 1, LANES)))
    return pl.pallas_call(
        functools.partial(_diff_kernel, tq=tq, lam_init=lam_init),
        grid=(B, DIFF_HEADS, T // tq),
        in_specs=[pl.BlockSpec((8, LANES), lambda b, h, i: (0, 0)),
                  pl.BlockSpec((1, 1, LANES), lambda b, h, i: (h, 0, 0)),
                  pl.BlockSpec((1, tq, LANES), lambda b, h, i: (b, i, h)),
                  pl.BlockSpec((1, T, LANES), lambda b, h, i: (b, 0, h)),
                  pl.BlockSpec((1, T, LANES), lambda b, h, i: (b, 0, DIFF_HEADS + h)),
                  pl.BlockSpec((1, DIFF_VDIM), lambda b, h, i: (0, 0))],
        out_specs=pl.BlockSpec((1, tq, LANES), lambda b, h, i: (b, i, h)),
        out_shape=jax.ShapeDtypeStruct((B, T, DIFF_HEADS * DIFF_VDIM), F32),
        compiler_params=_cparams("parallel", "parallel", "parallel"),
        name="diff_prompt",
    )(lam_tile, slopes, dq, diff_b, diff_b, subln_g)


def _layer_norm(z, g, b):
    mu = jnp.mean(z, axis=-1, keepdims=True)
    zc = z - mu
    var = jnp.mean(jnp.square(zc), axis=-1, keepdims=True)
    return zc * lax.rsqrt(var + LN_EPS) * g + b


def _out_proj_kernel(on_ref, od_ref, x_ref, wa_ref, wb_ref, g_ref, b_ref, wr_ref, br_ref,
                     x1_ref, x1b_ref, sel_ref, gate_ref):
    y = _dot(on_ref[...].astype(BF16), wa_ref[...]) + _dot(od_ref[...].astype(BF16), wb_ref[...])
    x1 = _layer_norm(DEEPNORM_ALPHA * x_ref[...] + y, g_ref[...], b_ref[...])
    x1_ref[...] = x1
    x1b_ref[...] = x1.astype(BF16)
    logits = lax.dot_general(wr_ref[...], x1, (((1,), (1,)), ((), ())), precision=HIGHEST,
                             preferred_element_type=F32) + br_ref[...]
    eidx = lax.broadcasted_iota(jnp.int32, logits.shape, 0).astype(F32)
    sel = jnp.zeros(logits.shape, F32)
    vmax = None
    for k in range(TOP_K):
        cur = jnp.where(sel > 0.5, -jnp.inf, logits)
        m = jnp.max(cur, axis=0, keepdims=True)
        first = jnp.min(jnp.where(cur == m, eidx, float(N_EXPERTS)), axis=0, keepdims=True)
        sel = jnp.where(eidx == first, 1.0, sel)
        if k == 0:
            vmax = m
    ex = jnp.where(sel > 0.5, jnp.exp(logits - vmax), 0.0)
    sel_ref[...] = sel
    gate_ref[...] = ex / jnp.sum(ex, axis=0, keepdims=True)


def _out_proj_ln_router(o_nsa, o_diff, x2d, wa, wb, ln_g, ln_b, wr_t, br):
    n = x2d.shape[0]
    tm = min(256, n)
    row = lambda i: (i, 0)
    fix = lambda i: (0, 0)
    col = lambda i: (0, i)
    return pl.pallas_call(
        _out_proj_kernel,
        grid=(n // tm,),
        in_specs=[pl.BlockSpec((tm, 512), row), pl.BlockSpec((tm, 512), row), pl.BlockSpec((tm, D_MODEL), row),
                  pl.BlockSpec((512, D_MODEL), fix), pl.BlockSpec((512, D_MODEL), fix),
                  pl.BlockSpec((1, D_MODEL), fix), pl.BlockSpec((1, D_MODEL), fix),
                  pl.BlockSpec((N_EXPERTS, D_MODEL), fix), pl.BlockSpec((N_EXPERTS, 1), fix)],
        out_specs=(pl.BlockSpec((tm, D_MODEL), row), pl.BlockSpec((tm, D_MODEL), row),
                   pl.BlockSpec((N_EXPERTS, tm), col), pl.BlockSpec((N_EXPERTS, tm), col)),
        out_shape=(jax.ShapeDtypeStruct((n, D_MODEL), F32), jax.ShapeDtypeStruct((n, D_MODEL), BF16),
                   jax.ShapeDtypeStruct((N_EXPERTS, n), F32), jax.ShapeDtypeStruct((N_EXPERTS, n), F32)),
        compiler_params=_cparams("parallel"),
        name="out_proj_ln_router",
    )(o_nsa, o_diff, x2d, wa, wb, ln_g, ln_b, wr_t, br)


def _moe_weight_kernel(w_ref, p_ref, wg_ref, wl_ref):
    half = MXU_DIM // 2
    for m in range(w_ref.shape[2] // MXU_DIM):
        y = _dot(w_ref[0, :, m * MXU_DIM:(m + 1) * MXU_DIM].astype(BF16), p_ref[...])
        wg_ref[0, :, m * half:(m + 1) * half] = y[:, :half].astype(BF16)
        wl_ref[0, :, m * half:(m + 1) * half] = y[:, half:].astype(BF16)


def _moe_split_gate_up(w_gate_up):
    ne, dm, two_ff = w_gate_up.shape
    half = MXU_DIM // 2
    p_np = np.zeros((MXU_DIM, MXU_DIM), np.float32)
    p_np[2 * np.arange(half), np.arange(half)] = 1.0
    p_np[2 * np.arange(half) + 1, half + np.arange(half)] = 1.0
    out = jax.ShapeDtypeStruct((ne, dm, two_ff // 2), BF16)
    ospec = pl.BlockSpec((1, dm, two_ff // 2), lambda e: (e, 0, 0))
    return pl.pallas_call(
        _moe_weight_kernel,
        grid=(ne,),
        in_specs=[pl.BlockSpec((1, dm, two_ff), lambda e: (e, 0, 0)),
                  pl.BlockSpec((MXU_DIM, MXU_DIM), lambda e: (0, 0))],
        out_specs=(ospec, ospec),
        out_shape=(out, out),
        compiler_params=_cparams("parallel"),
        name="moe_split_gate_up",
    )(w_gate_up, jnp.asarray(p_np, BF16))


def _moe_pos_kernel(sel_ref, u_ref, pos_ref, cnt_ref, *, n_valid):
    s_tile = sel_ref.shape[1]
    tok = pl.program_id(0) * s_tile + lax.broadcasted_iota(jnp.int32, (1, s_tile), 1)
    sel = jnp.where(tok < n_valid, sel_ref[...], 0.0)
    rank = _dot(sel.astype(BF16), u_ref[...])
    pos_ref[...] = jnp.where(sel > 0.5, rank, -1.0)
    cnt_ref[0] = jnp.sum(sel, axis=1, keepdims=True)


def _moe_positions(sel_t, s_tile):
    n = sel_t.shape[1]
    ns = pl.cdiv(n, s_tile)
    upper = jnp.asarray(np.triu(np.ones((s_tile, s_tile), np.float32), 1), BF16)
    return pl.pallas_call(
        functools.partial(_moe_pos_kernel, n_valid=n),
        grid=(ns,),
        in_specs=[pl.BlockSpec((N_EXPERTS, s_tile), lambda s: (0, s)),
                  pl.BlockSpec((s_tile, s_tile), lambda s: (0, 0))],
        out_specs=(pl.BlockSpec((N_EXPERTS, s_tile), lambda s: (0, s)),
                   pl.BlockSpec((1, N_EXPERTS, 1), lambda s: (s, 0, 0))),
        out_shape=(jax.ShapeDtypeStruct((N_EXPERTS, ns * s_tile), F32),
                   jax.ShapeDtypeStruct((ns, N_EXPERTS, 1), F32)),
        compiler_params=_cparams("parallel"),
        name="moe_positions",
    )(sel_t, upper)


def _moe_kernel(nch_ref, x_ref, pos_ref, gate_ref, wg_ref, wl_ref, bg_ref, bl_ref, wd_ref, bd_ref, y_ref, *, chunk,
                n_valid):
    s = pl.program_id(0)
    e = pl.program_id(1)
    s_tile = x_ref.shape[0]

    @pl.when(e == 0)
    def _():
        y_ref[...] = jnp.zeros_like(y_ref)

    pos = pos_ref[0, 0].astype(jnp.int32)
    gate = gate_ref[0, 0]
    slot0 = lax.broadcasted_iota(jnp.int32, (chunk, 1), 0)

    def body(j, carry):
        hit = pos == slot0 + j * chunk
        onehot = jnp.where(hit, 1.0, 0.0).astype(BF16)
        x = x_ref[...]
        if n_valid % s_tile:
            row = s * s_tile + lax.broadcasted_iota(jnp.int32, (s_tile, 1), 0)
            x = jnp.where(row < n_valid, x, jnp.zeros_like(x))
        xs = _dot(onehot, x).astype(BF16)
        hg = jnp.minimum(_dot(xs, wg_ref[0]) + bg_ref[0], SWIGLU_LIMIT)
        hl = jnp.clip(_dot(xs, wl_ref[0]) + bl_ref[0], -SWIGLU_LIMIT, SWIGLU_LIMIT)
        a = (hl + 1.0) * hg * jax.nn.sigmoid(SWIGLU_ALPHA * hg)
        out = _dot(a.astype(BF16), wd_ref[0]) + bd_ref[0]
        gslot = jnp.sum(jnp.where(hit, gate, 0.0), axis=1, keepdims=True)
        outg = (out * gslot).astype(BF16)
        y_ref[...] += lax.dot_general(onehot, outg, (((0,), (0,)), ((), ())), preferred_element_type=F32)
        return carry

    lax.fori_loop(0, nch_ref[s * N_EXPERTS + e], body, 0)


def _moe(x1b, pos_t, gate_t, counts, wg, wl, bg, bl, wd, bd, s_tile, chunk):
    n = x1b.shape[0]
    ns = pl.cdiv(n, s_tile)
    nch = ((counts.reshape(ns * N_EXPERTS) + (chunk - 1)) // chunk).astype(jnp.int32)
    pos4 = pos_t.reshape(N_EXPERTS, ns, 1, s_tile)
    gate4 = jnp.pad(gate_t, ((0, 0), (0, ns * s_tile - n))).reshape(N_EXPERTS, ns, 1, s_tile)
    wspec = pl.BlockSpec((1, D_MODEL, D_MODEL), lambda s, e, nch: (e, 0, 0))
    bspec = pl.BlockSpec((1, 1, D_MODEL), lambda s, e, nch: (e, 0, 0))
    rspec = pl.BlockSpec((1, 1, 1, s_tile), lambda s, e, nch: (e, s, 0, 0))
    return pl.pallas_call(
        functools.partial(_moe_kernel, chunk=chunk, n_valid=n),
        grid_spec=pltpu.PrefetchScalarGridSpec(
            num_scalar_prefetch=1,
            grid=(ns, N_EXPERTS),
            in_specs=[pl.BlockSpec((s_tile, D_MODEL), lambda s, e, nch: (s, 0)),
                      rspec, rspec, wspec, wspec, bspec, bspec, wspec, bspec],
            out_specs=pl.BlockSpec((s_tile, D_MODEL), lambda s, e, nch: (s, 0))),
        out_shape=jax.ShapeDtypeStruct((n, D_MODEL), F32),
        compiler_params=_cparams("parallel", "arbitrary"),
        name="moe_experts",
    )(nch, x1b, pos4, gate4, wg, wl, bg, bl, wd, bd)


def _ln2_kernel(x_ref, f_ref, g_ref, b_ref, o_ref):
    o_ref[...] = _layer_norm(DEEPNORM_ALPHA * x_ref[...] + f_ref[...], g_ref[...], b_ref[...])


def _residual_ln(x1, f, g, b):
    n = x1.shape[0]
    tm = min(512, n)
    row = lambda i: (i, 0)
    fix = lambda i: (0, 0)
    return pl.pallas_call(
        _ln2_kernel,
        grid=(n // tm,),
        in_specs=[pl.BlockSpec((tm, D_MODEL), row), pl.BlockSpec((tm, D_MODEL), row),
                  pl.BlockSpec((1, D_MODEL), fix), pl.BlockSpec((1, D_MODEL), fix)],
        out_specs=pl.BlockSpec((tm, D_MODEL), row),
        out_shape=jax.ShapeDtypeStruct((n, D_MODEL), F32),
        compiler_params=_cparams("parallel"),
        name="residual_ln2",
    )(x1, f, g, b)


def _post_mixer(o_nsa, o_diff, x2d, p, s_tile, chunk):
    x1, x1b, sel_t, gate_t = _out_proj_ln_router(o_nsa, o_diff, x2d, p["wo_a"], p["wo_b"], p["ln1_g"], p["ln1_b"],
                                                 p["wr_t"], p["br"])
    pos_t, counts = _moe_positions(sel_t, s_tile)
    f = _moe(x1b, pos_t, gate_t, counts, p["wg"], p["wl"], p["bg"], p["bl"], p["wd"], p["bd"], s_tile, chunk)
    return _residual_ln(x1, f, p["ln2_g"], p["ln2_b"])


def _sample_kernel(pt_ref, qm_ref, qd_ref, gt_ref, nrow_ref, wrow_ref, wcol_ref, drow_ref, win_ref,
                   w1_ref, pe_ref, w2_ref, m_ref, e_ref, r_ref, sn_ref, sd_ref, lam_ref, subg_ref, *rest,
                   n_pages, past_len, lam_init):
    nsa_pages = rest[:n_pages]
    diff_pages = rest[n_pages:2 * n_pages]
    onsa_ref, odiff_ref, nwin_ref, x_ref = rest[2 * n_pages:]
    n16 = past_len // CMP_STRIDE
    qpos = past_len

    for j, pg in enumerate(nsa_pages):
        for kind in range(2):
            x_ref[kind, j * PAGE_SIZE:(j + 1) * PAGE_SIZE, :] = pg[kind * LANES:(kind + 1) * LANES, :].T

    kv_cmp = [_compress_chunks(lambda l, kind=kind: x_ref[kind, pl.ds(l, n16, stride=CMP_STRIDE), :],
                               w1_ref[kind], pe_ref[kind], w2_ref[kind], n16) for kind in range(2)]

    qm = qm_ref[0]
    qmb = qm.astype(BF16)
    slope_n = sn_ref[:, 0:1]
    nrow = nrow_ref[0]
    wrow = wrow_ref[0]
    gt = gt_ref[0]

    def new_key_score(qrows, krow):
        return jnp.sum(qrows * krow, axis=1, keepdims=True) * ATTN_SCALE

    cmp_end = lax.broadcasted_iota(jnp.int32, (1, n16), 1) * CMP_STRIDE + (CMP_LEN - 1)
    dist_c = qpos - cmp_end
    s = _dot_nt(qmb, kv_cmp[0].astype(BF16)) * ATTN_SCALE - slope_n * dist_c.astype(F32)
    e, d = _softmax_parts(s, dist_c >= 0)
    p_cmp = e / d
    o_cmp = _dot(p_cmp.astype(BF16), kv_cmp[1].astype(BF16))
    pgrp = jnp.dot(r_ref[...], p_cmp, precision=HIGHEST, preferred_element_type=F32)
    imp = jnp.dot(pgrp, m_ref[...], precision=HIGHEST, preferred_element_type=F32)
    nbl = m_ref.shape[1]
    blk = lax.broadcasted_iota(jnp.int32, (NSA_HEADS, nbl), 1)
    sel = _select_blocks(imp, blk, jnp.full((NSA_HEADS, 1), qpos, jnp.int32), past_len // SEL_BLOCK + 1)
    mask_s = _dot(sel.astype(BF16), e_ref[...]) > 0.5

    kpos = lax.broadcasted_iota(jnp.int32, (1, past_len), 1)
    dist_k = (qpos - kpos).astype(F32)
    s = jnp.concatenate([_dot(qmb, pg[2 * LANES:3 * LANES, :].astype(BF16)) for pg in nsa_pages], axis=1)
    s = jnp.where(mask_s, s * ATTN_SCALE - slope_n * dist_k, -jnp.inf)
    s_new = new_key_score(qm, nrow[:, 2 * LANES:3 * LANES])
    m = jnp.maximum(jnp.max(s, axis=1, keepdims=True), s_new)
    e = jnp.exp(s - m)
    e_new = jnp.exp(s_new - m)
    d = jnp.sum(e, axis=1, keepdims=True) + e_new
    eb = e.astype(BF16)
    acc = e_new * nrow[:, 3 * LANES:4 * LANES]
    for j, pg in enumerate(nsa_pages):
        acc = acc + _dot_nt(eb[:, j * PAGE_SIZE:(j + 1) * PAGE_SIZE], pg[3 * LANES:4 * LANES, :].astype(BF16))
    o_sel = acc / d

    nwin = win_ref.shape[2]
    wpos = past_len - nwin + lax.broadcasted_iota(jnp.int32, (1, nwin), 1)
    dist_w = qpos - wpos
    mask_w = (dist_w >= 0) & (dist_w <= WINDOW)
    win = win_ref[0]
    s = _dot(qmb, win[0:LANES, :].astype(BF16)) * ATTN_SCALE - slope_n * dist_w.astype(F32)
    s = jnp.where(mask_w, s, -jnp.inf)
    s_new = new_key_score(qm, wrow[:, 0:LANES])
    m = jnp.maximum(jnp.max(s, axis=1, keepdims=True), s_new)
    e = jnp.exp(s - m)
    e_new = jnp.exp(s_new - m)
    d = jnp.sum(e, axis=1, keepdims=True) + e_new
    o_win = (_dot_nt(e.astype(BF16), win[LANES:2 * LANES, :].astype(BF16)) + e_new * wrow[:, LANES:2 * LANES]) / d

    onsa_ref[0] = gt[:, 0:1] * o_cmp + gt[:, 1:2] * o_sel + gt[:, 2:3] * o_win

    lane = lax.broadcasted_iota(jnp.int32, win.shape, 1)
    nwin_ref[0] = jnp.where(lane == nwin - 1, wcol_ref[0], pltpu.roll(win, nwin - 1, 1))

    drow = drow_ref[0]

    def diff_rows(pg, j):
        return pg[pl.ds(j, PAGE_SIZE, stride=DIFF_ROWS), :]

    qds = [qd_ref[0, h] for h in range(DIFF_HEADS)]
    qdb = [qh.astype(BF16) for qh in qds]
    s_pages = []
    for pg in diff_pages:
        sp = _dot_nt(qdb[0], diff_rows(pg, 0).astype(BF16))
        for h in range(1, DIFF_HEADS):
            sp = sp + _dot_nt(qdb[h], diff_rows(pg, h).astype(BF16))
        s_pages.append(sp)
    s = jnp.concatenate(s_pages, axis=1) * ATTN_SCALE - sd_ref[:, 0:1] * dist_k
    prod = qds[0] * drow[0:1]
    for h in range(1, DIFF_HEADS):
        prod = prod + qds[h] * drow[h:h + 1]
    s_new = jnp.sum(prod, axis=1, keepdims=True) * ATTN_SCALE
    m = jnp.maximum(jnp.max(s, axis=1, keepdims=True), s_new)
    e = jnp.exp(s - m)
    e_new = jnp.exp(s_new - m)
    d = jnp.sum(e, axis=1, keepdims=True) + e_new
    lam = lam_ref[0:1, 0:1]
    p = e / d
    p_new = e_new / d
    a = (p - lam * pltpu.roll(p, DIFF_HEADS, 0)).astype(BF16)
    a_new = p_new - lam * pltpu.roll(p_new, DIFF_HEADS, 0)
    accs = []
    for h in range(DIFF_HEADS):
        acc = a_new * drow[DIFF_HEADS + h:DIFF_HEADS + h + 1]
        for j, pg in enumerate(diff_pages):
            acc = acc + _dot(a[:, j * PAGE_SIZE:(j + 1) * PAGE_SIZE], diff_rows(pg, DIFF_HEADS + h).astype(BF16))
        accs.append(acc)
    acc = jnp.concatenate(accs, axis=1)
    lane_head = lax.broadcasted_iota(jnp.int32, acc.shape, 1) // DIFF_VDIM
    row = lax.broadcasted_iota(jnp.int32, acc.shape, 0)
    own = lane_head == row
    ms = jnp.sum(jnp.where(own, jnp.square(acc), 0.0), axis=1, keepdims=True) / DIFF_VDIM
    odiff_ref[0] = acc * lax.rsqrt(ms + RMS_EPS) * subg_ref[...] * (1.0 - lam_init)


def _sample_attention(page_table, qm, qd4, gt8, nrow, wrow, wcol, drow, win_t, pool_nsa, pool_diff,
                      w1ab, pe2, w2bd, m_mat, e_mat, r_mat, sn, sd, lam_tile, subg_tile, past_len, lam_init):
    B = qm.shape[0]
    n_pages = page_table.shape[1]
    nwin = win_t.shape[2]

    def per_b(shape):
        nd = len(shape)
        return pl.BlockSpec((1,) + shape, lambda b, pt: (b,) + (0,) * nd)

    def fixed(shape):
        nd = len(shape)
        return pl.BlockSpec(shape, lambda b, pt: (0,) * nd)

    def page_spec(pool, j):
        return pl.BlockSpec((None,) + pool.shape[1:], lambda b, pt: (pt[b, j], 0, 0))

    in_specs = [per_b((NSA_HEADS, LANES)), per_b((DIFF_HEADS, 8, LANES)), per_b((NSA_HEADS, LANES)),
                per_b((1, NSA_W)), per_b((1, WIN_W)), per_b((WIN_W, 1)), per_b((DIFF_ROWS, LANES)),
                per_b((WIN_W, nwin)),
                fixed(w1ab.shape), fixed(pe2.shape), fixed(w2bd.shape), fixed(m_mat.shape), fixed(e_mat.shape),
                fixed(r_mat.shape), fixed(sn.shape), fixed(sd.shape), fixed(lam_tile.shape), fixed(subg_tile.shape)]
    in_specs += [page_spec(pool_nsa, j) for j in range(n_pages)]
    in_specs += [page_spec(pool_diff, j) for j in range(n_pages)]
    return pl.pallas_call(
        functools.partial(_sample_kernel, n_pages=n_pages, past_len=past_len, lam_init=lam_init),
        grid_spec=pltpu.PrefetchScalarGridSpec(
            num_scalar_prefetch=1,
            grid=(B,),
            in_specs=in_specs,
            out_specs=(per_b((NSA_HEADS, LANES)), per_b((8, DIFF_HEADS * DIFF_VDIM)), per_b((WIN_W, nwin))),
            scratch_shapes=[pltpu.VMEM((2, past_len, LANES), F32)]),
        out_shape=(jax.ShapeDtypeStruct((B, NSA_HEADS, LANES), F32),
                   jax.ShapeDtypeStruct((B, 8, DIFF_HEADS * DIFF_VDIM), F32),
                   jax.ShapeDtypeStruct((B, WIN_W, nwin), F32)),
        compiler_params=_cparams("arbitrary"),
        name="sample_attention",
    )(page_table, qm, qd4, gt8, nrow, wrow, wcol, drow, win_t, w1ab, pe2, w2bd, m_mat, e_mat, r_mat, sn, sd,
      lam_tile, subg_tile, *([pool_nsa] * n_pages), *([pool_diff] * n_pages))


def _cmp_to_sel(n_cmp, n_sel):
    c0 = np.arange(n_cmp)[:, None] * CMP_STRIDE
    s0 = np.arange(n_sel)[None, :] * SEL_BLOCK
    ov = np.clip(np.minimum(c0 + CMP_LEN, s0 + SEL_BLOCK) - np.maximum(c0, s0), 0, None)
    return (ov / CMP_LEN).astype(np.float32)


def _lambda_init(layer):
    return 0.8 - 0.6 * math.exp(-0.3 * layer)


def _prep_params(l, w_in, w_out, diff_subln_g, ln1_g, ln1_b, ln2_g, ln2_b, w_router, b_router,
                 w_gate_up, b_gate_up, w_down, b_down):
    gate_end = _GT0 + 3 * NSA_HEADS
    wt = w_in[l].T
    w_t = jnp.concatenate([wt[:gate_end], jnp.zeros((_GATE_PAD, D_MODEL), F32), wt[gate_end:]], axis=0).astype(BF16)
    wg, wl = _moe_split_gate_up(w_gate_up[l])
    bgu = b_gate_up[l]
    return {
        "w_t": w_t,
        "wo_a": w_out[l][:512].astype(BF16), "wo_b": w_out[l][512:].astype(BF16),
        "ln1_g": ln1_g[l][None], "ln1_b": ln1_b[l][None], "ln2_g": ln2_g[l][None], "ln2_b": ln2_b[l][None],
        "wr_t": w_router[l].T, "br": b_router[l][:, None],
        "wg": wg, "wl": wl, "bg": bgu[:, None, 0::2], "bl": bgu[:, None, 1::2],
        "wd": w_down[l].astype(BF16), "bd": b_down[l][:, None, :],
        "subln_g": diff_subln_g[l][None],
    }


def _cmp_params(cmp_pe, cmp_w1, cmp_w2):
    eye = jnp.eye(NSA_KV_HEADS, dtype=F32)
    halves = []
    for half in range(2):
        w = cmp_w1[:, half * CMP_STRIDE:(half + 1) * CMP_STRIDE]
        wb = jnp.einsum('klde,gh->klgdhe', w, eye)
        halves.append(wb.reshape(2, CMP_STRIDE * LANES, LANES))
    w1ab = jnp.concatenate(halves, axis=-1).astype(BF16)
    pe = cmp_pe.reshape(2, 2, CMP_STRIDE, 1, HEAD_DIM)
    pe2 = jnp.broadcast_to(pe, (2, 2, CMP_STRIDE, NSA_KV_HEADS, HEAD_DIM)).reshape(2, 2, 1, CMP_STRIDE * LANES)
    pe2 = jnp.broadcast_to(pe2, (2, 2, 8, CMP_STRIDE * LANES))
    w2bd = jnp.einsum('kde,gh->kgdhe', cmp_w2, eye).reshape(2, LANES, LANES).astype(BF16)
    return w1ab, pe2, w2bd


def _feature_major_to_rows(a_t, kinds):
    B, _, T = a_t.shape
    return a_t.reshape(B, kinds, NSA_KV_HEADS, HEAD_DIM, T).transpose(0, 4, 1, 2, 3)


def _prompt_group(x, p, cmp_pe, cmp_w1, cmp_w2, lam_tile, lam_init):
    B, T, _ = x.shape
    x2d = x.reshape(B * T, D_MODEL)
    q_nsa, q_tb, nsa_rows, nsa_t, nsa_tb, _, win_t, win_tb, gates, dq, diff8, diff_b = _in_proj(x2d, p["w_t"], B)
    r3 = lambda a: a.reshape(B, T, a.shape[-1])
    q_nsa, nsa_rows3, gates, dq = map(r3, (q_nsa, nsa_rows, gates, dq))
    kvc = _compress_prompt(nsa_rows3, *_cmp_params(cmp_pe, cmp_w1, cmp_w2))
    n16 = T // CMP_STRIDE
    n_sel = -(-T // SEL_BLOCK)
    m_np = np.zeros((n16, n_sel), np.float32)
    m_np[:n16 - 1] = _cmp_to_sel(n16 - 1, n_sel)
    o_cmp, sel = _cmp_attn_prompt(q_nsa, q_tb, kvc, jnp.asarray(m_np.T))
    e_np = np.zeros((NSA_KV_HEADS, LANES, T), np.float32)
    for g in range(NSA_KV_HEADS):
        e_np[g, g * n_sel + np.arange(T) // SEL_BLOCK, np.arange(T)] = 1.0
    o_nsa = _sel_win_prompt(q_nsa, nsa_tb, win_tb, sel, jnp.asarray(e_np, BF16), o_cmp, gates)
    o_diff = _diff_prompt(lam_tile, dq, diff_b.reshape(B, T, DIFF_W), p["subln_g"], lam_init)
    y = _post_mixer(o_nsa.reshape(B * T, 512), o_diff.reshape(B * T, 512), x2d, p, s_tile=MOE_SUPERTILE, chunk=MOE_CHUNK)
    nwin = min(WINDOW, T)
    return (y.reshape(B, T, D_MODEL),
            _feature_major_to_rows(nsa_t, 4),
            diff8.reshape(B, T, 2, DIFF_HEADS, DIFF_VDIM),
            _feature_major_to_rows(win_t[:, :, T - nwin:], 2))


def _sample_group(x, pool_nsa, pool_diff, win_buf, page_table, p, cmp_pe, cmp_w1, cmp_w2, lam_tile, lam_init):
    B, T, _ = x.shape
    past_len = page_table.shape[1] * PAGE_SIZE
    x2d = x.reshape(B, D_MODEL)
    q_nsa, _, nsa_rows, nsa_t, _, win_rows, _, _, gates, dq, diff8, _ = _in_proj(x2d, p["w_t"], 1)
    qh = q_nsa.reshape(B, NSA_HEADS, HEAD_DIM)
    grp = (np.arange(NSA_HEADS) // NSA_GROUP)[None, :, None, None] == np.arange(NSA_KV_HEADS)[None, None, :, None]
    qm = (qh[:, :, None, :] * jnp.asarray(grp, F32)).reshape(B, NSA_HEADS, LANES)
    dq4 = dq.reshape(B, DIFF_HEADS, 1, 2 * HEAD_DIM)
    rr = np.arange(8)
    rowmask = (rr[None, :, None] % DIFF_HEADS == np.arange(DIFF_HEADS)[:, None, None]) & (
        rr[None, :, None] // DIFF_HEADS == (np.arange(LANES) // HEAD_DIM)[None, None, :])
    qd4 = dq4 * jnp.asarray(rowmask, F32)[None]
    gt8 = jnp.pad(gates[:, :3 * NSA_HEADS].reshape(B, 3, NSA_HEADS).transpose(0, 2, 1), ((0, 0), (0, 0), (0, LANES - 3)))
    nwin = win_buf.shape[1]
    win_t = win_buf.transpose(0, 2, 3, 4, 1).reshape(B, WIN_W, nwin)
    n_pool = pool_nsa.shape[0]
    pool_nsa_t = pool_nsa.transpose(0, 2, 3, 4, 1).reshape(n_pool, NSA_W, PAGE_SIZE)
    pool_diff_r = pool_diff.reshape(n_pool, PAGE_SIZE * DIFF_ROWS, LANES)
    w1ab, pe2, w2bd = _cmp_params(cmp_pe, cmp_w1, cmp_w2)
    n16 = past_len // CMP_STRIDE
    n_sel = past_len // SEL_BLOCK + 1
    m_np = np.zeros((n16, 64), np.float32)
    m_np[:n16 - 1, :n_sel] = _cmp_to_sel(n16 - 1, n_sel)
    e_np = np.zeros((64, past_len), np.float32)
    e_np[np.arange(past_len) // SEL_BLOCK, np.arange(past_len)] = 1.0
    r_np = (np.arange(8)[:, None] // NSA_GROUP == np.arange(8)[None, :] // NSA_GROUP).astype(np.float32)
    sn = jnp.asarray(np.tile(np.asarray(NSA_SLOPES, np.float32)[:, None], (1, LANES)))
    sd = jnp.asarray(np.tile(np.asarray(DIFF_SLOPES, np.float32)[np.arange(8) % DIFF_HEADS, None], (1, LANES)))
    subg_tile = jnp.tile(p["subln_g"], (1, DIFF_HEADS))
    o_nsa8, o_diff8, new_win_t = _sample_attention(
        page_table, qm, qd4, gt8, nsa_rows[:, None, :], win_rows[:, None, :], win_rows[:, :, None],
        diff8.reshape(B, DIFF_ROWS, LANES), win_t, pool_nsa_t, pool_diff_r,
        w1ab, pe2, w2bd, jnp.asarray(m_np), jnp.asarray(e_np, BF16), jnp.asarray(r_np), sn, sd, lam_tile, subg_tile,
        past_len, lam_init)
    o8 = o_nsa8.reshape(B, NSA_KV_HEADS, NSA_GROUP, NSA_KV_HEADS, HEAD_DIM)
    o_nsa = jnp.stack([o8[:, g, :, g] for g in range(NSA_KV_HEADS)], axis=1).reshape(B, 512)
    d8 = o_diff8[:, :DIFF_HEADS].reshape(B, DIFF_HEADS, DIFF_HEADS, DIFF_VDIM)
    o_diff = jnp.stack([d8[:, h, h] for h in range(DIFF_HEADS)], axis=1).reshape(B, 512)
    y = _post_mixer(o_nsa, o_diff, x2d, p, s_tile=B, chunk=B)
    return (y.reshape(B, T, D_MODEL),
            _feature_major_to_rows(nsa_t, 4).reshape(B, T, 4, NSA_KV_HEADS, HEAD_DIM),
            diff8.reshape(B, T, 2, DIFF_HEADS, DIFF_VDIM),
            _feature_major_to_rows(new_win_t, 2))


def kernel(x_prompt, x_sample, cache_nsa_kv, cache_diff_kv, state_nsa_win, page_table, w_in, w_out, cmp_pe, cmp_w1,
           cmp_w2, diff_lambda, diff_subln_g, ln1_g, ln1_b, ln2_g, ln2_b, w_router, b_router, w_gate_up, b_gate_up,
           w_down, b_down):
    depth = w_in.shape[0]
    xp, xs = x_prompt, x_sample
    outs = [[] for _ in range(6)]
    for l in range(depth):
        lam0 = _lambda_init(l)
        lv = diff_lambda[l].astype(F32)
        lam = jnp.exp(jnp.sum(lv[0] * lv[1])) - jnp.exp(jnp.sum(lv[2] * lv[3])) + lam0
        lam_tile = jnp.full((8, LANES), lam, F32)
        p = _prep_params(l, w_in, w_out, diff_subln_g, ln1_g, ln1_b, ln2_g, ln2_b, w_router, b_router,
                         w_gate_up, b_gate_up, w_down, b_down)
        xp, r_nsa, r_diff, r_win = _prompt_group(xp, p, cmp_pe[l], cmp_w1[l], cmp_w2[l], lam_tile, lam0)
        xs, s_nsa, s_diff, s_win = _sample_group(xs, cache_nsa_kv[l], cache_diff_kv[l], state_nsa_win[l], page_table,
                                                 p, cmp_pe[l], cmp_w1[l], cmp_w2[l], lam_tile, lam0)
        for lst, v in zip(outs, (r_nsa, r_diff, r_win, s_nsa, s_diff, s_win)):
            lst.append(v)
    return (xp, xs) + tuple(jnp.stack(o) for o in outs)
```

```python
import functools
import math

import numpy as np
import jax
import jax.numpy as jnp
from jax import lax
from jax.experimental import pallas as pl
from jax.experimental.pallas import tpu as pltpu
from jax.experimental.pallas import tpu_sc as plsc

F32 = jnp.float32
BF16 = jnp.bfloat16
HIGHEST = lax.Precision.HIGHEST

D_MODEL = 1024
HEAD_DIM = 64
NSA_HEADS = 8
NSA_KV_HEADS = 2
NSA_GROUP = NSA_HEADS // NSA_KV_HEADS
CMP_LEN = 32
CMP_STRIDE = 16
SEL_BLOCK = 64
SEL_TOPN = 16
WINDOW = 512
DIFF_HEADS = 4
DIFF_VDIM = 2 * HEAD_DIM
N_EXPERTS = 32
TOP_K = 4
SWIGLU_ALPHA = 1.702
SWIGLU_LIMIT = 7.0
LN_EPS = 1e-5
RMS_EPS = 1e-5
ATTN_SCALE = HEAD_DIM ** -0.5
LOG2E = math.log2(math.e)
DEPTH = 1
DEEPNORM_ALPHA = (2 * DEPTH) ** 0.25
PAGE_SIZE = 128

NSA_SLOPES = tuple(2.0 ** (-8.0 * (i + 1) / NSA_HEADS) for i in range(NSA_HEADS))
DIFF_SLOPES = tuple(2.0 ** (-8.0 * (i + 1) / DIFF_HEADS) for i in range(DIFF_HEADS))

VMEM_LIMIT_BYTES = 56 * 1024 * 1024
LANES = 128
MXU_DIM = 256

MOE_CHUNK = MXU_DIM
MOE_SUPERTILE = 13 * LANES
SC_CORES = 2
SC_SUBCORES = 16
SC_WINDOW = 128

NSA_W = 4 * NSA_KV_HEADS * HEAD_DIM
WIN_W = 2 * NSA_KV_HEADS * HEAD_DIM
DIFF_W = 2 * DIFF_HEADS * DIFF_VDIM
DIFF_ROWS = DIFF_W // LANES
_GATE_PAD = LANES - 3 * NSA_HEADS
_Q0, _NSA0, _WIN0, _GT0, _DQ0, _DIFF0, _PROJ_ROWS = 0, 512, 1024, 1280, 1408, 1920, 2944


def _cparams(*sem):
    return pltpu.CompilerParams(dimension_semantics=sem, vmem_limit_bytes=VMEM_LIMIT_BYTES)


def _softmax_parts(s, mask):
    s = jnp.where(mask, s, -jnp.inf)
    m = jnp.max(s, axis=-1, keepdims=True)
    m = jnp.where(jnp.isfinite(m), m, 0.0)
    e = jnp.exp(s - m)
    d = jnp.sum(e, axis=-1, keepdims=True)
    return e, jnp.where(d > 0, d, 1.0)


def _softmax2_parts(s, mask):
    s = jnp.where(mask, s, -jnp.inf)
    m = jnp.max(s, axis=-1, keepdims=True)
    m = jnp.where(jnp.isfinite(m), m, 0.0)
    e = jnp.exp2(s - m)
    d = jnp.sum(e, axis=-1, keepdims=True)
    return e, jnp.where(d > 0, d, 1.0)


def _dot_nt(a, b):
    return lax.dot_general(a, b, (((1,), (1,)), ((), ())), preferred_element_type=F32)


def _dot(a, b):
    return jnp.dot(a, b, preferred_element_type=F32)


def _pad_head(qh, g):
    z = jnp.zeros_like(qh)
    return jnp.concatenate([qh, z] if g == 0 else [z, qh], axis=1)


def _select_blocks(imp, blk, qpos, n_blk_lanes):
    cur = jnp.right_shift(qpos, int(math.log2(SEL_BLOCK)))
    valid = blk * SEL_BLOCK <= qpos
    forced = (blk == 0) | (blk == cur) | (blk == cur - 1)
    val = jnp.where(forced, jnp.inf, jnp.where(valid, imp, -jnp.inf))
    rank = jnp.zeros(val.shape, F32)
    for i in range(n_blk_lanes):
        ci = val[:, i:i + 1]
        beats = (ci > val) | ((ci == val) & (blk > i))
        rank = rank + jnp.where(beats, 1.0, 0.0)
    return jnp.where(rank < SEL_TOPN, 1.0, 0.0)


def _in_proj_kernel(x_ref, w_ref, q_ref, qtb_ref, nsa_ref, nsat_ref, nsatb_ref, win_ref, wint_ref, wintb_ref, gt_ref,
                    dq_ref, diff_ref, diffb_ref):
    tm = x_ref.shape[0]
    xb = x_ref[...].astype(BF16)
    q_ref[...] = _dot_nt(xb, w_ref[_Q0:_NSA0, :])
    qtb_ref[0] = _dot_nt(w_ref[_Q0:_NSA0, :], xb).astype(BF16)
    nsa_ref[...] = _dot_nt(xb, w_ref[_NSA0:_WIN0, :])
    r = _dot_nt(w_ref[_NSA0:_WIN0, :], xb)
    nsat_ref[0] = r
    nsatb_ref[0] = r.astype(BF16)
    win_ref[...] = _dot_nt(xb, w_ref[_WIN0:_GT0, :])
    r = _dot_nt(w_ref[_WIN0:_GT0, :], xb)
    wint_ref[0] = r
    wintb_ref[0] = r.astype(BF16)
    gt_ref[...] = jax.nn.sigmoid(_dot_nt(xb, w_ref[_GT0:_DQ0, :]))
    dq_ref[...] = _dot_nt(xb, w_ref[_DQ0:_DIFF0, :])
    r = _dot_nt(xb, w_ref[_DIFF0:_PROJ_ROWS, :])
    diffb_ref[...] = r.astype(BF16)
    for j in range(DIFF_ROWS):
        diff_ref[pl.ds(j, tm, stride=DIFF_ROWS), :] = r[:, j * LANES:(j + 1) * LANES]


def _in_proj(x2d, w_t, batch):
    n = x2d.shape[0]
    t = n // batch
    tm = min(256, t)
    nt = t // tm
    row = lambda b, i: (b * nt + i, 0)
    tr = lambda b, i: (b, 0, i)
    tok = lambda w: pl.BlockSpec((tm, w), row)
    return pl.pallas_call(
        _in_proj_kernel,
        grid=(batch, nt),
        in_specs=[pl.BlockSpec((tm, D_MODEL), row),
                  pl.BlockSpec((_PROJ_ROWS, D_MODEL), lambda b, i: (0, 0))],
        out_specs=(tok(512), pl.BlockSpec((1, 512, tm), tr),
                   tok(NSA_W), pl.BlockSpec((1, NSA_W, tm), tr), pl.BlockSpec((1, NSA_W, tm), tr),
                   tok(WIN_W), pl.BlockSpec((1, WIN_W, tm), tr), pl.BlockSpec((1, WIN_W, tm), tr),
                   tok(LANES), tok(512), pl.BlockSpec((tm * DIFF_ROWS, LANES), row), tok(DIFF_W)),
        out_shape=(jax.ShapeDtypeStruct((n, 512), F32), jax.ShapeDtypeStruct((batch, 512, t), BF16),
                   jax.ShapeDtypeStruct((n, NSA_W), F32),
                   jax.ShapeDtypeStruct((batch, NSA_W, t), F32), jax.ShapeDtypeStruct((batch, NSA_W, t), BF16),
                   jax.ShapeDtypeStruct((n, WIN_W), F32),
                   jax.ShapeDtypeStruct((batch, WIN_W, t), F32), jax.ShapeDtypeStruct((batch, WIN_W, t), BF16),
                   jax.ShapeDtypeStruct((n, LANES), F32), jax.ShapeDtypeStruct((n, 512), F32),
                   jax.ShapeDtypeStruct((n * DIFF_ROWS, LANES), F32), jax.ShapeDtypeStruct((n, DIFF_W), BF16)),
        compiler_params=_cparams("parallel", "parallel"),
        name="in_proj",
    )(x2d, w_t)


def _compress_chunks(load_rows, w1, pe, w2, n16):
    cst = _dot(pe[0].astype(BF16), w1[:, 0:LANES]) + _dot(pe[1].astype(BF16), w1[:, LANES:])
    chunks = jnp.concatenate([load_rows(l) for l in range(CMP_STRIDE)], axis=1)
    ab = _dot(chunks.astype(BF16), w1)
    nxt = pltpu.roll(ab[:, LANES:], n16 - 1, 0)
    h = jax.nn.gelu(ab[:, 0:LANES] + nxt + cst[0:1])
    return _dot(h.astype(BF16), w2)


def _compress_kernel(rows_ref, w1_ref, pe_ref, w2_ref, o_ref):
    n16 = rows_ref.shape[1] // CMP_STRIDE
    o_ref[0, 0] = _compress_chunks(lambda l: rows_ref[0, pl.ds(l, n16, stride=CMP_STRIDE), :],
                                   w1_ref[0], pe_ref[0], w2_ref[0], n16)


def _compress_prompt(nsa_rows, w1ab, pe2, w2bd):
    B, T, _ = nsa_rows.shape
    n16 = T // CMP_STRIDE
    return pl.pallas_call(
        _compress_kernel,
        grid=(B, 2),
        in_specs=[pl.BlockSpec((1, T, LANES), lambda b, k: (b, 0, k)),
                  pl.BlockSpec((1,) + w1ab.shape[1:], lambda b, k: (k, 0, 0)),
                  pl.BlockSpec((1,) + pe2.shape[1:], lambda b, k: (k, 0, 0, 0)),
                  pl.BlockSpec((1,) + w2bd.shape[1:], lambda b, k: (k, 0, 0))],
        out_specs=pl.BlockSpec((1, 1, n16, LANES), lambda b, k: (b, k, 0, 0)),
        out_shape=jax.ShapeDtypeStruct((B, 2, n16, LANES), F32),
        compiler_params=_cparams("parallel", "parallel"),
        name="compress_prompt",
    )(nsa_rows, w1ab, pe2, w2bd)


def _select_blocks_t(imp, qpos):
    n_blk = imp.shape[0]
    blk = lax.broadcasted_iota(jnp.int32, imp.shape, 0)
    cur = jnp.right_shift(qpos, int(math.log2(SEL_BLOCK)))
    valid = blk * SEL_BLOCK <= qpos
    forced = (blk == 0) | (blk == cur) | (blk == cur - 1)
    val = jnp.where(forced, jnp.inf, jnp.where(valid, imp, -jnp.inf))
    rank = jnp.zeros(val.shape, F32)
    for i in range(n_blk):
        ci = val[i:i + 1, :]
        beats = (ci > val) | ((ci == val) & (blk > i))
        rank = rank + jnp.where(beats, 1.0, 0.0)
    return jnp.where(rank < SEL_TOPN, 1.0, 0.0)


def _cmp_attn_kernel(q_ref, qt_ref, kc_ref, vc_ref, mt_ref, o_ref, selt_ref, *, tq):
    i = pl.program_id(1)
    ncmp = kc_ref.shape[2]
    kcb = kc_ref[0, 0].astype(BF16)
    vcb = vc_ref[0, 0].astype(BF16)
    qpos_c = i * tq + lax.broadcasted_iota(jnp.int32, (tq, 1), 0)
    end_r = lax.broadcasted_iota(jnp.int32, (1, ncmp), 1) * CMP_STRIDE + (CMP_LEN - 1)
    mask = end_r <= qpos_c
    end_rf = end_r.astype(F32)
    q = q_ref[0] * (ATTN_SCALE * LOG2E)
    for h in range(NSA_HEADS):
        g = h // NSA_GROUP
        qp = _pad_head(q[:, h * HEAD_DIM:(h + 1) * HEAD_DIM], g).astype(BF16)
        e, d = _softmax2_parts(_dot_nt(qp, kcb) + (NSA_SLOPES[h] * LOG2E) * end_rf, mask)
        oh = _dot(e.astype(BF16), vcb)[:, g * HEAD_DIM:(g + 1) * HEAD_DIM] / d
        o_ref[0, :, h * HEAD_DIM:(h + 1) * HEAD_DIM] = oh
    qpos_r = i * tq + lax.broadcasted_iota(jnp.int32, (1, tq), 1)
    end_c = lax.broadcasted_iota(jnp.int32, (ncmp, 1), 0) * CMP_STRIDE + (CMP_LEN - 1)
    mask_t = end_c <= qpos_r
    end_cf = end_c.astype(F32)
    qt = qt_ref[0]
    zero = jnp.zeros((HEAD_DIM, tq), BF16)
    psum = [None, None]
    for h in range(NSA_HEADS):
        g = h // NSA_GROUP
        qh = qt[h * HEAD_DIM:(h + 1) * HEAD_DIM, :]
        qpt = jnp.concatenate([qh, zero] if g == 0 else [zero, qh], axis=0)
        s = _dot(kcb, qpt) * (ATTN_SCALE * LOG2E) + (NSA_SLOPES[h] * LOG2E) * end_cf
        s = jnp.where(mask_t, s, -jnp.inf)
        m = jnp.max(s, axis=0, keepdims=True)
        m = jnp.where(jnp.isfinite(m), m, 0.0)
        e = jnp.exp2(s - m)
        d = jnp.sum(e, axis=0, keepdims=True)
        p = e / jnp.where(d > 0, d, 1.0)
        psum[g] = p if psum[g] is None else psum[g] + p
    sels = []
    for g in range(NSA_KV_HEADS):
        imp = jnp.dot(mt_ref[...], psum[g], precision=HIGHEST, preferred_element_type=F32)
        sels.append(_select_blocks_t(imp, qpos_r))
    sels.append(jnp.zeros((LANES - NSA_KV_HEADS * mt_ref.shape[0], tq), F32))
    selt_ref[0] = jnp.concatenate(sels, axis=0)


def _cmp_attn_prompt(q_nsa, q_tb, kvc, mt_mat, tq=256):
    B, T, _ = q_nsa.shape
    ncmp = kvc.shape[2]
    n_sel = mt_mat.shape[0]
    return pl.pallas_call(
        functools.partial(_cmp_attn_kernel, tq=tq),
        grid=(B, T // tq),
        in_specs=[pl.BlockSpec((1, tq, 512), lambda b, i: (b, i, 0)),
                  pl.BlockSpec((1, 512, tq), lambda b, i: (b, 0, i)),
                  pl.BlockSpec((1, 1, ncmp, LANES), lambda b, i: (b, 0, 0, 0)),
                  pl.BlockSpec((1, 1, ncmp, LANES), lambda b, i: (b, 1, 0, 0)),
                  pl.BlockSpec((n_sel, ncmp), lambda b, i: (0, 0))],
        out_specs=(pl.BlockSpec((1, tq, 512), lambda b, i: (b, i, 0)),
                   pl.BlockSpec((1, LANES, tq), lambda b, i: (b, 0, i))),
        out_shape=(jax.ShapeDtypeStruct((B, T, 512), F32), jax.ShapeDtypeStruct((B, LANES, T), F32)),
        compiler_params=_cparams("parallel", "parallel"),
        name="cmp_attn_prompt",
    )(q_nsa, q_tb, kvc, kvc, mt_mat)


def _sel_win_kernel(q_ref, ks_ref, vs_ref, kw_ref, vw_ref, sel_ref, e_ref, ocmp_ref, gt_ref, o_ref, *, tq, span,
                    kv_step):
    i = pl.program_id(1)
    T = ks_ref.shape[2]
    qpos = i * tq + lax.broadcasted_iota(jnp.int32, (tq, 1), 0)
    q = q_ref[0] * (ATTN_SCALE * LOG2E)
    gt = gt_ref[0]
    ocmp = ocmp_ref[0]
    selb = sel_ref[0].astype(BF16)
    heads = [(h, h // NSA_GROUP, h * HEAD_DIM, (h + 1) * HEAD_DIM) for h in range(NSA_HEADS)]
    qps = [_pad_head(q[:, lo:hi], g).astype(BF16) for _, g, lo, hi in heads]
    start = pl.multiple_of(jnp.maximum(i * tq - WINDOW, 0), LANES)
    wpos = start + lax.broadcasted_iota(jnp.int32, (1, span), 1)
    dist_w = qpos - wpos
    mask_w = (dist_w >= 0) & (dist_w <= WINDOW)
    wposf = wpos.astype(F32)
    kwb = kw_ref[0, :, pl.ds(start, span)]
    vwb = vw_ref[0, :, pl.ds(start, span)]
    partial = []
    for h, g, lo, hi in heads:
        s = _dot(qps[h], kwb) + (NSA_SLOPES[h] * LOG2E) * wposf
        e, d = _softmax2_parts(s, mask_w)
        o_win = _dot_nt(e.astype(BF16), vwb)[:, g * HEAD_DIM:(g + 1) * HEAD_DIM] / d
        partial.append(gt[:, h:h + 1] * ocmp[:, lo:hi] + gt[:, 2 * NSA_HEADS + h:2 * NSA_HEADS + h + 1] * o_win)
    tiles_per_step = kv_step // tq
    for br in range(T // kv_step):
        @pl.when((i >= br * tiles_per_step) & (i < (br + 1) * tiles_per_step))
        def _(br=br):
            kv = (br + 1) * kv_step
            kpos = lax.broadcasted_iota(jnp.int32, (1, kv), 1)
            causal = kpos <= qpos
            kposf = kpos.astype(F32)
            ksb = ks_ref[0, :, 0:kv]
            vsb = vs_ref[0, :, 0:kv]
            for g in range(NSA_KV_HEADS):
                expand = lax.dot_general(selb, e_ref[g, :, 0:kv], (((0,), (0,)), ((), ())), preferred_element_type=F32)
                mask_s = (expand > 0.5) & causal
                for h, _, lo, hi in heads[g * NSA_GROUP:(g + 1) * NSA_GROUP]:
                    s = _dot(qps[h], ksb) + (NSA_SLOPES[h] * LOG2E) * kposf
                    e, d = _softmax2_parts(s, mask_s)
                    o_sel = _dot_nt(e.astype(BF16), vsb)[:, g * HEAD_DIM:(g + 1) * HEAD_DIM] / d
                    o_ref[0, :, lo:hi] = partial[h] + gt[:, NSA_HEADS + h:NSA_HEADS + h + 1] * o_sel


def _sel_win_prompt(q_nsa, nsa_t, win_t, sel_t, e_mat, o_cmp, gates, tq=256):
    B, T, _ = q_nsa.shape
    span = WINDOW + tq
    kv = lambda j: pl.BlockSpec((1, LANES, T), lambda b, i: (b, j, 0))
    return pl.pallas_call(
        functools.partial(_sel_win_kernel, tq=tq, span=span, kv_step=min(2 * tq, T)),
        grid=(B, T // tq),
        in_specs=[pl.BlockSpec((1, tq, 512), lambda b, i: (b, i, 0)),
                  kv(2), kv(3), kv(0), kv(1),
                  pl.BlockSpec((1, LANES, tq), lambda b, i: (b, 0, i)),
                  pl.BlockSpec((NSA_KV_HEADS, LANES, T), lambda b, i: (0, 0, 0)),
                  pl.BlockSpec((1, tq, 512), lambda b, i: (b, i, 0)),
                  pl.BlockSpec((1, tq, LANES), lambda b, i: (b, i, 0))],
        out_specs=pl.BlockSpec((1, tq, 512), lambda b, i: (b, i, 0)),
        out_shape=jax.ShapeDtypeStruct((B, T, 512), F32),
        compiler_params=_cparams("parallel", "parallel"),
        name="sel_win_prompt",
    )(q_nsa, nsa_t, nsa_t, win_t, win_t, sel_t, e_mat, o_cmp, gates)


def _diff_kernel(lam_ref, sl_ref, q_ref, k_ref, v_ref, g_ref, o_ref, *, tq, lam_init):
    i = pl.program_id(2)
    T = k_ref.shape[1]
    lam = lam_ref[0:1, 0:1]
    q = q_ref[0] * (ATTN_SCALE * LOG2E)
    qps = [_pad_head(q[:, c * HEAD_DIM:(c + 1) * HEAD_DIM], c).astype(BF16) for c in range(2)]
    tri = lax.broadcasted_iota(jnp.int32, (tq, tq), 0) >= lax.broadcasted_iota(jnp.int32, (tq, tq), 1)
    for br in range(T // tq):
        @pl.when(i == br)
        def _(br=br):
            off = br * tq
            kv = off + tq
            kb = k_ref[0, 0:kv, :]
            vb = v_ref[0, 0:kv, :]
            col = (sl_ref[0, :, 0:1] * LOG2E) * lax.broadcasted_iota(jnp.int32, (1, kv), 1).astype(F32)
            outs = []
            for c in range(2):
                s = _dot_nt(qps[c], kb) + col
                s_d = jnp.where(tri, s[:, off:], -jnp.inf)
                m = jnp.max(s_d, axis=-1, keepdims=True)
                if off:
                    m = jnp.maximum(m, jnp.max(s[:, :off], axis=-1, keepdims=True))
                e_d = jnp.exp2(s_d - m)
                d = jnp.sum(e_d, axis=-1, keepdims=True)
                o = _dot(e_d.astype(BF16), vb[off:, :])
                if off:
                    e_o = jnp.exp2(s[:, :off] - m)
                    d = d + jnp.sum(e_o, axis=-1, keepdims=True)
                    o = o + _dot(e_o.astype(BF16), vb[:off, :])
                outs.append(o / d)
            od = outs[0] - lam * outs[1]
            od = od * lax.rsqrt(jnp.mean(jnp.square(od), axis=-1, keepdims=True) + RMS_EPS)
            o_ref[0] = od * g_ref[...] * (1.0 - lam_init)


def _diff_prompt(lam_tile, dq, diff_b, subln_g, lam_init, tq=256):
    B, T, _ = dq.shape
    slopes = jnp.asarray(np.tile(np.asarray(DIFF_SLOPES, np.float32)[:, None, None], (1, 1, LANES)))
    return pl.pallas_call(
        functools.partial(_diff_kernel, tq=tq, lam_init=lam_init),
        grid=(B, DIFF_HEADS, T // tq),
        in_specs=[pl.BlockSpec((8, LANES), lambda b, h, i: (0, 0)),
                  pl.BlockSpec((1, 1, LANES), lambda b, h, i: (h, 0, 0)),
                  pl.BlockSpec((1, tq, LANES), lambda b, h, i: (b, i, h)),
                  pl.BlockSpec((1, T, LANES), lambda b, h, i: (b, 0, h)),
                  pl.BlockSpec((1, T, LANES), lambda b, h, i: (b, 0, DIFF_HEADS + h)),
                  pl.BlockSpec((1, DIFF_VDIM), lambda b, h, i: (0, 0))],
        out_specs=pl.BlockSpec((1, tq, LANES), lambda b, h, i: (b, i, h)),
        out_shape=jax.ShapeDtypeStruct((B, T, DIFF_HEADS * DIFF_VDIM), F32),
        compiler_params=_cparams("parallel", "parallel", "parallel"),
        name="diff_prompt",
    )(lam_tile, slopes, dq, diff_b, diff_b, subln_g)


def _layer_norm(z, g, b):
    mu = jnp.mean(z, axis=-1, keepdims=True)
    zc = z - mu
    var = jnp.mean(jnp.square(zc), axis=-1, keepdims=True)
    return zc * lax.rsqrt(var + LN_EPS) * g + b


def _pack_bf16_pairs(x):
    w = x.shape[1] // 2
    hi = lax.bitcast_convert_type(x[:, :w].astype(BF16).astype(F32), jnp.uint32)
    lo = lax.bitcast_convert_type(x[:, w:].astype(BF16).astype(F32), jnp.uint32)
    return hi | (lo >> 16)


def _unpack_bf16_pairs(u):
    hi = lax.bitcast_convert_type(u & jnp.uint32(0xFFFF0000), F32)
    lo = lax.bitcast_convert_type(u << 16, F32)
    return jnp.concatenate([hi, lo], axis=1)


def _out_proj_kernel(on_ref, od_ref, x_ref, wa_ref, wb_ref, g_ref, b_ref, wr_ref, br_ref,
                     x1_ref, x1b_ref, x1p_ref, sel_ref, gate_ref, tope_ref, gate8_ref, cnt_ref):
    y = _dot(on_ref[...].astype(BF16), wa_ref[...]) + _dot(od_ref[...].astype(BF16), wb_ref[...])
    x1 = _layer_norm(DEEPNORM_ALPHA * x_ref[...] + y, g_ref[...], b_ref[...])
    x1_ref[...] = x1
    x1b_ref[...] = x1.astype(BF16)
    x1p_ref[...] = _pack_bf16_pairs(x1)
    logits = lax.dot_general(wr_ref[...], x1, (((1,), (1,)), ((), ())), precision=HIGHEST,
                             preferred_element_type=F32) + br_ref[...]
    eidx = lax.broadcasted_iota(jnp.int32, logits.shape, 0).astype(F32)
    sel = jnp.zeros(logits.shape, F32)
    picked, vals = [], []
    for k in range(TOP_K):
        cur = jnp.where(sel > 0.5, -jnp.inf, logits)
        m = jnp.max(cur, axis=0, keepdims=True)
        first = jnp.min(jnp.where(cur == m, eidx, float(N_EXPERTS)), axis=0, keepdims=True)
        sel = jnp.where(eidx == first, 1.0, sel)
        picked.append(first)
        vals.append(m)
    ex = jnp.where(sel > 0.5, jnp.exp(logits - vals[0]), 0.0)
    denom = jnp.sum(ex, axis=0, keepdims=True)
    sel_ref[...] = sel
    gate_ref[...] = ex / denom
    pad = [jnp.zeros_like(denom)] * (8 - TOP_K)
    tope_ref[...] = jnp.concatenate(picked + pad, axis=0)
    gate8_ref[...] = jnp.concatenate([jnp.exp(v - vals[0]) / denom for v in vals] + pad, axis=0)
    cnt_ref[0] = jnp.sum(sel, axis=1, keepdims=True)


def _out_proj_ln_router(o_nsa, o_diff, x2d, wa, wb, ln_g, ln_b, wr_t, br):
    n = x2d.shape[0]
    tm = min(256, n)
    row = lambda i: (i, 0)
    fix = lambda i: (0, 0)
    col = lambda i: (0, i)
    return pl.pallas_call(
        _out_proj_kernel,
        grid=(n // tm,),
        in_specs=[pl.BlockSpec((tm, 512), row), pl.BlockSpec((tm, 512), row), pl.BlockSpec((tm, D_MODEL), row),
                  pl.BlockSpec((512, D_MODEL), fix), pl.BlockSpec((512, D_MODEL), fix),
                  pl.BlockSpec((1, D_MODEL), fix), pl.BlockSpec((1, D_MODEL), fix),
                  pl.BlockSpec((N_EXPERTS, D_MODEL), fix), pl.BlockSpec((N_EXPERTS, 1), fix)],
        out_specs=(pl.BlockSpec((tm, D_MODEL), row), pl.BlockSpec((tm, D_MODEL), row),
                   pl.BlockSpec((tm, D_MODEL // 2), row),
                   pl.BlockSpec((N_EXPERTS, tm), col), pl.BlockSpec((N_EXPERTS, tm), col),
                   pl.BlockSpec((8, tm), col), pl.BlockSpec((8, tm), col),
                   pl.BlockSpec((1, N_EXPERTS, 1), lambda i: (i, 0, 0))),
        out_shape=(jax.ShapeDtypeStruct((n, D_MODEL), F32), jax.ShapeDtypeStruct((n, D_MODEL), BF16),
                   jax.ShapeDtypeStruct((n, D_MODEL // 2), jnp.uint32),
                   jax.ShapeDtypeStruct((N_EXPERTS, n), F32), jax.ShapeDtypeStruct((N_EXPERTS, n), F32),
                   jax.ShapeDtypeStruct((8, n), F32), jax.ShapeDtypeStruct((8, n), F32),
                   jax.ShapeDtypeStruct((n // tm, N_EXPERTS, 1), F32)),
        compiler_params=_cparams("parallel"),
        name="out_proj_ln_router",
    )(o_nsa, o_diff, x2d, wa, wb, ln_g, ln_b, wr_t, br)


def _moe_weight_kernel(w_ref, p_ref, wg_ref, wl_ref):
    half = MXU_DIM // 2
    for m in range(w_ref.shape[2] // MXU_DIM):
        y = _dot(w_ref[0, :, m * MXU_DIM:(m + 1) * MXU_DIM].astype(BF16), p_ref[...])
        wg_ref[0, :, m * half:(m + 1) * half] = y[:, :half].astype(BF16)
        wl_ref[0, :, m * half:(m + 1) * half] = y[:, half:].astype(BF16)


def _moe_split_gate_up(w_gate_up):
    ne, dm, two_ff = w_gate_up.shape
    half = MXU_DIM // 2
    p_np = np.zeros((MXU_DIM, MXU_DIM), np.float32)
    p_np[2 * np.arange(half), np.arange(half)] = 1.0
    p_np[2 * np.arange(half) + 1, half + np.arange(half)] = 1.0
    out = jax.ShapeDtypeStruct((ne, dm, two_ff // 2), BF16)
    ospec = pl.BlockSpec((1, dm, two_ff // 2), lambda e: (e, 0, 0))
    return pl.pallas_call(
        _moe_weight_kernel,
        grid=(ne,),
        in_specs=[pl.BlockSpec((1, dm, two_ff), lambda e: (e, 0, 0)),
                  pl.BlockSpec((MXU_DIM, MXU_DIM), lambda e: (0, 0))],
        out_specs=(ospec, ospec),
        out_shape=(out, out),
        compiler_params=_cparams("parallel"),
        name="moe_split_gate_up",
    )(w_gate_up, jnp.asarray(p_np, BF16))


def _moe_pos_kernel(sel_ref, u_ref, pos_ref, cnt_ref, *, n_valid):
    s_tile = sel_ref.shape[1]
    tok = pl.program_id(0) * s_tile + lax.broadcasted_iota(jnp.int32, (1, s_tile), 1)
    sel = jnp.where(tok < n_valid, sel_ref[...], 0.0)
    rank = _dot(sel.astype(BF16), u_ref[...])
    pos_ref[...] = jnp.where(sel > 0.5, rank, -1.0)
    cnt_ref[0] = jnp.sum(sel, axis=1, keepdims=True)


def _moe_positions(sel_t, s_tile):
    n = sel_t.shape[1]
    ns = pl.cdiv(n, s_tile)
    upper = jnp.asarray(np.triu(np.ones((s_tile, s_tile), np.float32), 1), BF16)
    return pl.pallas_call(
        functools.partial(_moe_pos_kernel, n_valid=n),
        grid=(ns,),
        in_specs=[pl.BlockSpec((N_EXPERTS, s_tile), lambda s: (0, s)),
                  pl.BlockSpec((s_tile, s_tile), lambda s: (0, 0))],
        out_specs=(pl.BlockSpec((N_EXPERTS, s_tile), lambda s: (0, s)),
                   pl.BlockSpec((1, N_EXPERTS, 1), lambda s: (s, 0, 0))),
        out_shape=(jax.ShapeDtypeStruct((N_EXPERTS, ns * s_tile), F32),
                   jax.ShapeDtypeStruct((ns, N_EXPERTS, 1), F32)),
        compiler_params=_cparams("parallel"),
        name="moe_positions",
    )(sel_t, upper)


def _moe_kernel(nch_ref, x_ref, pos_ref, gate_ref, wg_ref, wl_ref, bg_ref, bl_ref, wd_ref, bd_ref, y_ref, *, chunk,
                n_valid):
    s = pl.program_id(0)
    e = pl.program_id(1)
    s_tile = x_ref.shape[0]

    @pl.when(e == 0)
    def _():
        y_ref[...] = jnp.zeros_like(y_ref)

    pos = pos_ref[0, 0].astype(jnp.int32)
    gate = gate_ref[0, 0]
    slot0 = lax.broadcasted_iota(jnp.int32, (chunk, 1), 0)

    def body(j, carry):
        hit = pos == slot0 + j * chunk
        onehot = jnp.where(hit, 1.0, 0.0).astype(BF16)
        x = x_ref[...]
        if n_valid % s_tile:
            row = s * s_tile + lax.broadcasted_iota(jnp.int32, (s_tile, 1), 0)
            x = jnp.where(row < n_valid, x, jnp.zeros_like(x))
        xs = _dot(onehot, x).astype(BF16)
        hg = jnp.minimum(_dot(xs, wg_ref[0]) + bg_ref[0], SWIGLU_LIMIT)
        hl = jnp.clip(_dot(xs, wl_ref[0]) + bl_ref[0], -SWIGLU_LIMIT, SWIGLU_LIMIT)
        a = (hl + 1.0) * hg * jax.nn.sigmoid(SWIGLU_ALPHA * hg)
        out = _dot(a.astype(BF16), wd_ref[0]) + bd_ref[0]
        gslot = jnp.sum(jnp.where(hit, gate, 0.0), axis=1, keepdims=True)
        outg = (out * gslot).astype(BF16)
        y_ref[...] += lax.dot_general(onehot, outg, (((0,), (0,)), ((), ())), preferred_element_type=F32)
        return carry

    lax.fori_loop(0, nch_ref[s * N_EXPERTS + e], body, 0)


def _moe(x1b, pos_t, gate_t, counts, wg, wl, bg, bl, wd, bd, s_tile, chunk):
    n = x1b.shape[0]
    ns = pl.cdiv(n, s_tile)
    nch = ((counts.reshape(ns * N_EXPERTS) + (chunk - 1)) // chunk).astype(jnp.int32)
    pos4 = pos_t.reshape(N_EXPERTS, ns, 1, s_tile)
    gate4 = jnp.pad(gate_t, ((0, 0), (0, ns * s_tile - n))).reshape(N_EXPERTS, ns, 1, s_tile)
    wspec = pl.BlockSpec((1, D_MODEL, D_MODEL), lambda s, e, nch: (e, 0, 0))
    bspec = pl.BlockSpec((1, 1, D_MODEL), lambda s, e, nch: (e, 0, 0))
    rspec = pl.BlockSpec((1, 1, 1, s_tile), lambda s, e, nch: (e, s, 0, 0))
    return pl.pallas_call(
        functools.partial(_moe_kernel, chunk=chunk, n_valid=n),
        grid_spec=pltpu.PrefetchScalarGridSpec(
            num_scalar_prefetch=1,
            grid=(ns, N_EXPERTS),
            in_specs=[pl.BlockSpec((s_tile, D_MODEL), lambda s, e, nch: (s, 0)),
                      rspec, rspec, wspec, wspec, bspec, bspec, wspec, bspec],
            out_specs=pl.BlockSpec((s_tile, D_MODEL), lambda s, e, nch: (s, 0))),
        out_shape=jax.ShapeDtypeStruct((n, D_MODEL), F32),
        compiler_params=_cparams("parallel", "arbitrary"),
        name="moe_experts",
    )(nch, x1b, pos4, gate4, wg, wl, bg, bl, wd, bd)


def _ln2_kernel(x_ref, f_ref, g_ref, b_ref, o_ref):
    o_ref[...] = _layer_norm(DEEPNORM_ALPHA * x_ref[...] + f_ref[...], g_ref[...], b_ref[...])


def _residual_ln(x1, f, g, b):
    n = x1.shape[0]
    tm = min(512, n)
    row = lambda i: (i, 0)
    fix = lambda i: (0, 0)
    return pl.pallas_call(
        _ln2_kernel,
        grid=(n // tm,),
        in_specs=[pl.BlockSpec((tm, D_MODEL), row), pl.BlockSpec((tm, D_MODEL), row),
                  pl.BlockSpec((1, D_MODEL), fix), pl.BlockSpec((1, D_MODEL), fix)],
        out_specs=pl.BlockSpec((tm, D_MODEL), row),
        out_shape=jax.ShapeDtypeStruct((n, D_MODEL), F32),
        compiler_params=_cparams("parallel"),
        name="residual_ln2",
    )(x1, f, g, b)


def _post_mixer(o_nsa, o_diff, x2d, p, s_tile, chunk):
    x1, x1b, _, sel_t, gate_t, _, _, _ = _out_proj_ln_router(o_nsa, o_diff, x2d, p["wo_a"], p["wo_b"], p["ln1_g"],
                                                             p["ln1_b"], p["wr_t"], p["br"])
    pos_t, counts = _moe_positions(sel_t, s_tile)
    f = _moe(x1b, pos_t, gate_t, counts, p["wg"], p["wl"], p["bg"], p["bl"], p["wd"], p["bd"], s_tile, chunk)
    return _residual_ln(x1, f, p["ln2_g"], p["ln2_b"])


def _moe_slots_kernel(sel_ref, tope_ref, pstart_ref, u_ref, slot_ref, carry_ref):
    @pl.when(pl.program_id(0) == 0)
    def _():
        carry_ref[...] = jnp.zeros_like(carry_ref)

    sel = sel_ref[...]
    slot_all = pstart_ref[...] + carry_ref[...] + _dot(sel.astype(BF16), u_ref[...])
    eidx = lax.broadcasted_iota(jnp.int32, sel.shape, 0).astype(F32)
    tope = tope_ref[...]
    rows = [jnp.sum(jnp.where(eidx == tope[k:k + 1], slot_all, 0.0), axis=0, keepdims=True) for k in range(TOP_K)]
    rows += [jnp.zeros_like(rows[0])] * (8 - TOP_K)
    slot_ref[...] = jnp.concatenate(rows, axis=0).astype(jnp.int32)
    carry_ref[...] += jnp.sum(sel, axis=1, keepdims=True)


def _moe_slots(sel_t, tope, pstart, tm=2048):
    n = sel_t.shape[1]
    upper = jnp.asarray(np.triu(np.ones((tm, tm), np.float32), 1), BF16)
    return pl.pallas_call(
        _moe_slots_kernel,
        grid=(n // tm,),
        in_specs=[pl.BlockSpec((N_EXPERTS, tm), lambda i: (0, i)), pl.BlockSpec((8, tm), lambda i: (0, i)),
                  pl.BlockSpec((N_EXPERTS, 1), lambda i: (0, 0)), pl.BlockSpec((tm, tm), lambda i: (0, 0))],
        out_specs=pl.BlockSpec((8, tm), lambda i: (0, i)),
        out_shape=jax.ShapeDtypeStruct((8, n), jnp.int32),
        scratch_shapes=[pltpu.VMEM((N_EXPERTS, 1), F32)],
        compiler_params=_cparams("arbitrary"),
        name="moe_slots",
    )(sel_t, tope, pstart, upper)


def _sc_mesh():
    return plsc.VectorSubcoreMesh(core_axis_name="c", subcore_axis_name="s", num_cores=SC_CORES,
                                  num_subcores=SC_SUBCORES)


def _sc_scatter_rows(src, idx, n_out):
    n_rows, width = src.shape
    n_idx = idx.shape[0]
    per_core = n_rows // SC_WINDOW // SC_CORES

    @functools.partial(pl.kernel, out_type=jax.ShapeDtypeStruct((n_out, width), src.dtype), mesh=_sc_mesh(),
                       scratch_types=[], name="moe_dispatch_rows")
    def scatter(src_hbm, idx_hbm, out_hbm):
        def body(src_vmem, idx_vmem):
            for k in range(n_idx):
                pltpu.sync_copy(src_vmem, out_hbm.at[idx_vmem.at[k]])

        pltpu.emit_pipeline(
            body,
            grid=(SC_CORES, per_core),
            in_specs=[pl.BlockSpec((SC_WINDOW, width), lambda c, j: (c * per_core + j, 0)),
                      pl.BlockSpec((n_idx, SC_WINDOW), lambda c, j: (0, c * per_core + j))],
            out_specs=[],
            core_axis_name=("c", "s"),
            dimension_semantics=(pltpu.PARALLEL, pltpu.PARALLEL),
        )(src_hbm, idx_hbm)

    return scatter(src, idx)


def _sc_gather_rows(table, idx):
    n_rows = idx.shape[1]
    width = table.shape[1]
    per_core = n_rows // SC_WINDOW // SC_CORES

    @functools.partial(pl.kernel, out_type=jax.ShapeDtypeStruct((n_rows, width), table.dtype), mesh=_sc_mesh(),
                       scratch_types=[], name="moe_return_rows")
    def gather(table_hbm, idx_hbm, out_hbm):
        def body(idx_vmem, out_vmem):
            pltpu.sync_copy(table_hbm.at[idx_vmem.at[0]], out_vmem)

        pltpu.emit_pipeline(
            body,
            grid=(SC_CORES, per_core),
            in_specs=[pl.BlockSpec((1, SC_WINDOW), lambda c, j: (0, c * per_core + j))],
            out_specs=[pl.BlockSpec((SC_WINDOW, width), lambda c, j: (c * per_core + j, 0))],
            core_axis_name=("c", "s"),
            dimension_semantics=(pltpu.PARALLEL, pltpu.PARALLEL),
        )(idx_hbm, out_hbm)

    return gather(table, idx)


def _moe_ffn_kernel(blk_e_ref, nvalid_ref, xs_ref, wg_ref, wl_ref, bg_ref, bl_ref, wd_ref, bd_ref, o_ref):
    nv = nvalid_ref[pl.program_id(0)]

    @pl.when(nv > 0)
    def _():
        live = lax.broadcasted_iota(jnp.int32, (o_ref.shape[0], 1), 0) < nv
        xs = jnp.where(live, _unpack_bf16_pairs(xs_ref[...]), 0.0).astype(BF16)
        hg = jnp.minimum(_dot(xs, wg_ref[0]) + bg_ref[0], SWIGLU_LIMIT)
        hl = jnp.clip(_dot(xs, wl_ref[0]) + bl_ref[0], -SWIGLU_LIMIT, SWIGLU_LIMIT)
        a = (hl + 1.0) * hg * jax.nn.sigmoid(SWIGLU_ALPHA * hg)
        o_ref[...] = _pack_bf16_pairs(_dot(a.astype(BF16), wd_ref[0]) + bd_ref[0])

    @pl.when(nv == 0)
    def _():
        o_ref[...] = jnp.zeros_like(o_ref)


def _moe_ffn(blk_e, nvalid, xs, wg, wl, bg, bl, wd, bd):
    n_slots, half = xs.shape
    wspec = pl.BlockSpec((1, D_MODEL, D_MODEL), lambda i, be, nv: (be[i], 0, 0))
    bspec = pl.BlockSpec((1, 1, D_MODEL), lambda i, be, nv: (be[i], 0, 0))
    rows = pl.BlockSpec((MOE_CHUNK, half), lambda i, be, nv: (i, 0))
    return pl.pallas_call(
        _moe_ffn_kernel,
        grid_spec=pltpu.PrefetchScalarGridSpec(
            num_scalar_prefetch=2,
            grid=(n_slots // MOE_CHUNK,),
            in_specs=[rows, wspec, wspec, bspec, bspec, wspec, bspec],
            out_specs=rows),
        out_shape=jax.ShapeDtypeStruct((n_slots, half), jnp.uint32),
        compiler_params=_cparams("arbitrary"),
        name="moe_ffn_sorted",
    )(blk_e, nvalid, xs, wg, wl, bg, bl, wd, bd)


def _ln2_combine_kernel(x_ref, r_ref, gate_ref, g_ref, b_ref, o_ref):
    gate = gate_ref[...]
    f = gate[:, 0:1] * _unpack_bf16_pairs(r_ref[0])
    for k in range(1, TOP_K):
        f = f + gate[:, k:k + 1] * _unpack_bf16_pairs(r_ref[k])
    o_ref[...] = _layer_norm(DEEPNORM_ALPHA * x_ref[...] + f, g_ref[...], b_ref[...])


def _ln2_combine(x1, returned, gate_tok, g, b, tm=256):
    n = x1.shape[0]
    row = lambda i: (i, 0)
    fix = lambda i: (0, 0)
    return pl.pallas_call(
        _ln2_combine_kernel,
        grid=(n // tm,),
        in_specs=[pl.BlockSpec((tm, D_MODEL), row), pl.BlockSpec((TOP_K, tm, D_MODEL // 2), lambda i: (0, i, 0)),
                  pl.BlockSpec((tm, 8), row), pl.BlockSpec((1, D_MODEL), fix), pl.BlockSpec((1, D_MODEL), fix)],
        out_specs=pl.BlockSpec((tm, D_MODEL), row),
        out_shape=jax.ShapeDtypeStruct((n, D_MODEL), F32),
        compiler_params=_cparams("parallel"),
        name="combine_ln2",
    )(x1, returned, gate_tok, g, b)


def _post_mixer_sorted(o_nsa, o_diff, x2d, p):
    n = x2d.shape[0]
    x1, _, x1p, sel_t, _, tope, gate8, cnt = _out_proj_ln_router(o_nsa, o_diff, x2d, p["wo_a"], p["wo_b"], p["ln1_g"],
                                                                 p["ln1_b"], p["wr_t"], p["br"])
    counts = jnp.sum(cnt[:, :, 0], axis=0).astype(jnp.int32)
    padded = (counts + (MOE_CHUNK - 1)) // MOE_CHUNK * MOE_CHUNK
    pend = jnp.cumsum(padded)
    pstart = pend - padded
    n_blocks = n * TOP_K // MOE_CHUNK + N_EXPERTS
    blk0 = jnp.arange(n_blocks, dtype=jnp.int32) * MOE_CHUNK
    blk_e = jnp.minimum(jnp.searchsorted(pend, blk0, side="right"), N_EXPERTS - 1).astype(jnp.int32)
    nvalid = jnp.clip(pstart[blk_e] + counts[blk_e] - blk0, 0, MOE_CHUNK).astype(jnp.int32)
    slot = _moe_slots(sel_t, tope, pstart.astype(F32)[:, None])[:TOP_K]
    half = D_MODEL // 4
    slot2 = (2 * slot[:, :, None] + jnp.arange(2, dtype=jnp.int32)).reshape(TOP_K, 2 * n)
    xs = _sc_scatter_rows(x1p.reshape(2 * n, half), slot2, 2 * n_blocks * MOE_CHUNK)
    outs = _moe_ffn(blk_e, nvalid, xs.reshape(n_blocks * MOE_CHUNK, 2 * half), p["wg"], p["wl"], p["bg"], p["bl"],
                    p["wd"], p["bd"])
    returned = _sc_gather_rows(outs.reshape(2 * n_blocks * MOE_CHUNK, half), slot2.reshape(1, TOP_K * 2 * n))
    return _ln2_combine(x1, returned.reshape(TOP_K, n, 2 * half), gate8.T, p["ln2_g"], p["ln2_b"])


def _sample_kernel(pt_ref, qm_ref, qd_ref, gt_ref, nrow_ref, wrow_ref, wcol_ref, drow_ref, win_ref,
                   w1_ref, pe_ref, w2_ref, m_ref, e_ref, r_ref, sn_ref, sd_ref, lam_ref, subg_ref, *rest,
                   n_pages, past_len, lam_init):
    nsa_pages = rest[:n_pages]
    diff_pages = rest[n_pages:2 * n_pages]
    onsa_ref, odiff_ref, nwin_ref, x_ref = rest[2 * n_pages:]
    n16 = past_len // CMP_STRIDE
    qpos = past_len

    for j, pg in enumerate(nsa_pages):
        for kind in range(2):
            x_ref[kind, j * PAGE_SIZE:(j + 1) * PAGE_SIZE, :] = pg[kind * LANES:(kind + 1) * LANES, :].T

    kv_cmp = [_compress_chunks(lambda l, kind=kind: x_ref[kind, pl.ds(l, n16, stride=CMP_STRIDE), :],
                               w1_ref[kind], pe_ref[kind], w2_ref[kind], n16) for kind in range(2)]

    qm = qm_ref[0]
    qmb = qm.astype(BF16)
    slope_n = sn_ref[:, 0:1]
    nrow = nrow_ref[0]
    wrow = wrow_ref[0]
    gt = gt_ref[0]

    def new_key_score(qrows, krow):
        return jnp.sum(qrows * krow, axis=1, keepdims=True) * ATTN_SCALE

    cmp_end = lax.broadcasted_iota(jnp.int32, (1, n16), 1) * CMP_STRIDE + (CMP_LEN - 1)
    dist_c = qpos - cmp_end
    s = _dot_nt(qmb, kv_cmp[0].astype(BF16)) * ATTN_SCALE - slope_n * dist_c.astype(F32)
    e, d = _softmax_parts(s, dist_c >= 0)
    p_cmp = e / d
    o_cmp = _dot(p_cmp.astype(BF16), kv_cmp[1].astype(BF16))
    pgrp = jnp.dot(r_ref[...], p_cmp, precision=HIGHEST, preferred_element_type=F32)
    imp = jnp.dot(pgrp, m_ref[...], precision=HIGHEST, preferred_element_type=F32)
    nbl = m_ref.shape[1]
    blk = lax.broadcasted_iota(jnp.int32, (NSA_HEADS, nbl), 1)
    sel = _select_blocks(imp, blk, jnp.full((NSA_HEADS, 1), qpos, jnp.int32), past_len // SEL_BLOCK + 1)
    mask_s = _dot(sel.astype(BF16), e_ref[...]) > 0.5

    kpos = lax.broadcasted_iota(jnp.int32, (1, past_len), 1)
    dist_k = (qpos - kpos).astype(F32)
    s = jnp.concatenate([_dot(qmb, pg[2 * LANES:3 * LANES, :].astype(BF16)) for pg in nsa_pages], axis=1)
    s = jnp.where(mask_s, s * ATTN_SCALE - slope_n * dist_k, -jnp.inf)
    s_new = new_key_score(qm, nrow[:, 2 * LANES:3 * LANES])
    m = jnp.maximum(jnp.max(s, axis=1, keepdims=True), s_new)
    e = jnp.exp(s - m)
    e_new = jnp.exp(s_new - m)
    d = jnp.sum(e, axis=1, keepdims=True) + e_new
    eb = e.astype(BF16)
    acc = e_new * nrow[:, 3 * LANES:4 * LANES]
    for j, pg in enumerate(nsa_pages):
        acc = acc + _dot_nt(eb[:, j * PAGE_SIZE:(j + 1) * PAGE_SIZE], pg[3 * LANES:4 * LANES, :].astype(BF16))
    o_sel = acc / d

    nwin = win_ref.shape[2]
    wpos = past_len - nwin + lax.broadcasted_iota(jnp.int32, (1, nwin), 1)
    dist_w = qpos - wpos
    mask_w = (dist_w >= 0) & (dist_w <= WINDOW)
    win = win_ref[0]
    s = _dot(qmb, win[0:LANES, :].astype(BF16)) * ATTN_SCALE - slope_n * dist_w.astype(F32)
    s = jnp.where(mask_w, s, -jnp.inf)
    s_new = new_key_score(qm, wrow[:, 0:LANES])
    m = jnp.maximum(jnp.max(s, axis=1, keepdims=True), s_new)
    e = jnp.exp(s - m)
    e_new = jnp.exp(s_new - m)
    d = jnp.sum(e, axis=1, keepdims=True) + e_new
    o_win = (_dot_nt(e.astype(BF16), win[LANES:2 * LANES, :].astype(BF16)) + e_new * wrow[:, LANES:2 * LANES]) / d

    onsa_ref[0] = gt[:, 0:1] * o_cmp + gt[:, 1:2] * o_sel + gt[:, 2:3] * o_win

    lane = lax.broadcasted_iota(jnp.int32, win.shape, 1)
    nwin_ref[0] = jnp.where(lane == nwin - 1, wcol_ref[0], pltpu.roll(win, nwin - 1, 1))

    drow = drow_ref[0]

    def diff_rows(pg, j):
        return pg[pl.ds(j, PAGE_SIZE, stride=DIFF_ROWS), :]

    qds = [qd_ref[0, h] for h in range(DIFF_HEADS)]
    qdb = [qh.astype(BF16) for qh in qds]
    s_pages = []
    for pg in diff_pages:
        sp = _dot_nt(qdb[0], diff_rows(pg, 0).astype(BF16))
        for h in range(1, DIFF_HEADS):
            sp = sp + _dot_nt(qdb[h], diff_rows(pg, h).astype(BF16))
        s_pages.append(sp)
    s = jnp.concatenate(s_pages, axis=1) * ATTN_SCALE - sd_ref[:, 0:1] * dist_k
    prod = qds[0] * drow[0:1]
    for h in range(1, DIFF_HEADS):
        prod = prod + qds[h] * drow[h:h + 1]
    s_new = jnp.sum(prod, axis=1, keepdims=True) * ATTN_SCALE
    m = jnp.maximum(jnp.max(s, axis=1, keepdims=True), s_new)
    e = jnp.exp(s - m)
    e_new = jnp.exp(s_new - m)
    d = jnp.sum(e, axis=1, keepdims=True) + e_new
    lam = lam_ref[0:1, 0:1]
    p = e / d
    p_new = e_new / d
    a = (p - lam * pltpu.roll(p, DIFF_HEADS, 0)).astype(BF16)
    a_new = p_new - lam * pltpu.roll(p_new, DIFF_HEADS, 0)
    accs = []
    for h in range(DIFF_HEADS):
        acc = a_new * drow[DIFF_HEADS + h:DIFF_HEADS + h + 1]
        for j, pg in enumerate(diff_pages):
            acc = acc + _dot(a[:, j * PAGE_SIZE:(j + 1) * PAGE_SIZE], diff_rows(pg, DIFF_HEADS + h).astype(BF16))
        accs.append(acc)
    acc = jnp.concatenate(accs, axis=1)
    lane_head = lax.broadcasted_iota(jnp.int32, acc.shape, 1) // DIFF_VDIM
    row = lax.broadcasted_iota(jnp.int32, acc.shape, 0)
    own = lane_head == row
    ms = jnp.sum(jnp.where(own, jnp.square(acc), 0.0), axis=1, keepdims=True) / DIFF_VDIM
    odiff_ref[0] = acc * lax.rsqrt(ms + RMS_EPS) * subg_ref[...] * (1.0 - lam_init)


def _sample_attention(page_table, qm, qd4, gt8, nrow, wrow, wcol, drow, win_t, pool_nsa, pool_diff,
                      w1ab, pe2, w2bd, m_mat, e_mat, r_mat, sn, sd, lam_tile, subg_tile, past_len, lam_init):
    B = qm.shape[0]
    n_pages = page_table.shape[1]
    nwin = win_t.shape[2]

    def per_b(shape):
        nd = len(shape)
        return pl.BlockSpec((1,) + shape, lambda b, pt: (b,) + (0,) * nd)

    def fixed(shape):
        nd = len(shape)
        return pl.BlockSpec(shape, lambda b, pt: (0,) * nd)

    def page_spec(pool, j):
        return pl.BlockSpec((None,) + pool.shape[1:], lambda b, pt: (pt[b, j], 0, 0))

    in_specs = [per_b((NSA_HEADS, LANES)), per_b((DIFF_HEADS, 8, LANES)), per_b((NSA_HEADS, LANES)),
                per_b((1, NSA_W)), per_b((1, WIN_W)), per_b((WIN_W, 1)), per_b((DIFF_ROWS, LANES)),
                per_b((WIN_W, nwin)),
                fixed(w1ab.shape), fixed(pe2.shape), fixed(w2bd.shape), fixed(m_mat.shape), fixed(e_mat.shape),
                fixed(r_mat.shape), fixed(sn.shape), fixed(sd.shape), fixed(lam_tile.shape), fixed(subg_tile.shape)]
    in_specs += [page_spec(pool_nsa, j) for j in range(n_pages)]
    in_specs += [page_spec(pool_diff, j) for j in range(n_pages)]
    return pl.pallas_call(
        functools.partial(_sample_kernel, n_pages=n_pages, past_len=past_len, lam_init=lam_init),
        grid_spec=pltpu.PrefetchScalarGridSpec(
            num_scalar_prefetch=1,
            grid=(B,),
            in_specs=in_specs,
            out_specs=(per_b((NSA_HEADS, LANES)), per_b((8, DIFF_HEADS * DIFF_VDIM)), per_b((WIN_W, nwin))),
            scratch_shapes=[pltpu.VMEM((2, past_len, LANES), F32)]),
        out_shape=(jax.ShapeDtypeStruct((B, NSA_HEADS, LANES), F32),
                   jax.ShapeDtypeStruct((B, 8, DIFF_HEADS * DIFF_VDIM), F32),
                   jax.ShapeDtypeStruct((B, WIN_W, nwin), F32)),
        compiler_params=_cparams("arbitrary"),
        name="sample_attention",
    )(page_table, qm, qd4, gt8, nrow, wrow, wcol, drow, win_t, w1ab, pe2, w2bd, m_mat, e_mat, r_mat, sn, sd,
      lam_tile, subg_tile, *([pool_nsa] * n_pages), *([pool_diff] * n_pages))


def _cmp_to_sel(n_cmp, n_sel):
    c0 = np.arange(n_cmp)[:, None] * CMP_STRIDE
    s0 = np.arange(n_sel)[None, :] * SEL_BLOCK
    ov = np.clip(np.minimum(c0 + CMP_LEN, s0 + SEL_BLOCK) - np.maximum(c0, s0), 0, None)
    return (ov / CMP_LEN).astype(np.float32)


def _lambda_init(layer):
    return 0.8 - 0.6 * math.exp(-0.3 * layer)


def _prep_params(l, w_in, w_out, diff_subln_g, ln1_g, ln1_b, ln2_g, ln2_b, w_router, b_router,
                 w_gate_up, b_gate_up, w_down, b_down):
    gate_end = _GT0 + 3 * NSA_HEADS
    wt = w_in[l].T
    w_t = jnp.concatenate([wt[:gate_end], jnp.zeros((_GATE_PAD, D_MODEL), F32), wt[gate_end:]], axis=0).astype(BF16)
    wg, wl = _moe_split_gate_up(w_gate_up[l])
    bgu = b_gate_up[l]
    return {
        "w_t": w_t,
        "wo_a": w_out[l][:512].astype(BF16), "wo_b": w_out[l][512:].astype(BF16),
        "ln1_g": ln1_g[l][None], "ln1_b": ln1_b[l][None], "ln2_g": ln2_g[l][None], "ln2_b": ln2_b[l][None],
        "wr_t": w_router[l].T, "br": b_router[l][:, None],
        "wg": wg, "wl": wl, "bg": bgu[:, None, 0::2], "bl": bgu[:, None, 1::2],
        "wd": w_down[l].astype(BF16), "bd": b_down[l][:, None, :],
        "subln_g": diff_subln_g[l][None],
    }


def _cmp_params(cmp_pe, cmp_w1, cmp_w2):
    eye = jnp.eye(NSA_KV_HEADS, dtype=F32)
    halves = []
    for half in range(2):
        w = cmp_w1[:, half * CMP_STRIDE:(half + 1) * CMP_STRIDE]
        wb = jnp.einsum('klde,gh->klgdhe', w, eye)
        halves.append(wb.reshape(2, CMP_STRIDE * LANES, LANES))
    w1ab = jnp.concatenate(halves, axis=-1).astype(BF16)
    pe = cmp_pe.reshape(2, 2, CMP_STRIDE, 1, HEAD_DIM)
    pe2 = jnp.broadcast_to(pe, (2, 2, CMP_STRIDE, NSA_KV_HEADS, HEAD_DIM)).reshape(2, 2, 1, CMP_STRIDE * LANES)
    pe2 = jnp.broadcast_to(pe2, (2, 2, 8, CMP_STRIDE * LANES))
    w2bd = jnp.einsum('kde,gh->kgdhe', cmp_w2, eye).reshape(2, LANES, LANES).astype(BF16)
    return w1ab, pe2, w2bd


def _feature_major_to_rows(a_t, kinds):
    B, _, T = a_t.shape
    return a_t.reshape(B, kinds, NSA_KV_HEADS, HEAD_DIM, T).transpose(0, 4, 1, 2, 3)


def _prompt_group(x, p, cmp_pe, cmp_w1, cmp_w2, lam_tile, lam_init):
    B, T, _ = x.shape
    x2d = x.reshape(B * T, D_MODEL)
    q_nsa, q_tb, nsa_rows, nsa_t, nsa_tb, _, win_t, win_tb, gates, dq, diff8, diff_b = _in_proj(x2d, p["w_t"], B)
    r3 = lambda a: a.reshape(B, T, a.shape[-1])
    q_nsa, nsa_rows3, gates, dq = map(r3, (q_nsa, nsa_rows, gates, dq))
    kvc = _compress_prompt(nsa_rows3, *_cmp_params(cmp_pe, cmp_w1, cmp_w2))
    n16 = T // CMP_STRIDE
    n_sel = -(-T // SEL_BLOCK)
    m_np = np.zeros((n16, n_sel), np.float32)
    m_np[:n16 - 1] = _cmp_to_sel(n16 - 1, n_sel)
    o_cmp, sel = _cmp_attn_prompt(q_nsa, q_tb, kvc, jnp.asarray(m_np.T))
    e_np = np.zeros((NSA_KV_HEADS, LANES, T), np.float32)
    for g in range(NSA_KV_HEADS):
        e_np[g, g * n_sel + np.arange(T) // SEL_BLOCK, np.arange(T)] = 1.0
    o_nsa = _sel_win_prompt(q_nsa, nsa_tb, win_tb, sel, jnp.asarray(e_np, BF16), o_cmp, gates)
    o_diff = _diff_prompt(lam_tile, dq, diff_b.reshape(B, T, DIFF_W), p["subln_g"], lam_init)
    y = _post_mixer_sorted(o_nsa.reshape(B * T, 512), o_diff.reshape(B * T, 512), x2d, p)
    nwin = min(WINDOW, T)
    return (y.reshape(B, T, D_MODEL),
            _feature_major_to_rows(nsa_t, 4),
            diff8.reshape(B, T, 2, DIFF_HEADS, DIFF_VDIM),
            _feature_major_to_rows(win_t[:, :, T - nwin:], 2))


def _sample_group(x, pool_nsa, pool_diff, win_buf, page_table, p, cmp_pe, cmp_w1, cmp_w2, lam_tile, lam_init):
    B, T, _ = x.shape
    past_len = page_table.shape[1] * PAGE_SIZE
    x2d = x.reshape(B, D_MODEL)
    q_nsa, _, nsa_rows, nsa_t, _, win_rows, _, _, gates, dq, diff8, _ = _in_proj(x2d, p["w_t"], 1)
    qh = q_nsa.reshape(B, NSA_HEADS, HEAD_DIM)
    grp = (np.arange(NSA_HEADS) // NSA_GROUP)[None, :, None, None] == np.arange(NSA_KV_HEADS)[None, None, :, None]
    qm = (qh[:, :, None, :] * jnp.asarray(grp, F32)).reshape(B, NSA_HEADS, LANES)
    dq4 = dq.reshape(B, DIFF_HEADS, 1, 2 * HEAD_DIM)
    rr = np.arange(8)
    rowmask = (rr[None, :, None] % DIFF_HEADS == np.arange(DIFF_HEADS)[:, None, None]) & (
        rr[None, :, None] // DIFF_HEADS == (np.arange(LANES) // HEAD_DIM)[None, None, :])
    qd4 = dq4 * jnp.asarray(rowmask, F32)[None]
    gt8 = jnp.pad(gates[:, :3 * NSA_HEADS].reshape(B, 3, NSA_HEADS).transpose(0, 2, 1), ((0, 0), (0, 0), (0, LANES - 3)))
    nwin = win_buf.shape[1]
    win_t = win_buf.transpose(0, 2, 3, 4, 1).reshape(B, WIN_W, nwin)
    n_pool = pool_nsa.shape[0]
    pool_nsa_t = pool_nsa.transpose(0, 2, 3, 4, 1).reshape(n_pool, NSA_W, PAGE_SIZE)
    pool_diff_r = pool_diff.reshape(n_pool, PAGE_SIZE * DIFF_ROWS, LANES)
    w1ab, pe2, w2bd = _cmp_params(cmp_pe, cmp_w1, cmp_w2)
    n16 = past_len // CMP_STRIDE
    n_sel = past_len // SEL_BLOCK + 1
    m_np = np.zeros((n16, 64), np.float32)
    m_np[:n16 - 1, :n_sel] = _cmp_to_sel(n16 - 1, n_sel)
    e_np = np.zeros((64, past_len), np.float32)
    e_np[np.arange(past_len) // SEL_BLOCK, np.arange(past_len)] = 1.0
    r_np = (np.arange(8)[:, None] // NSA_GROUP == np.arange(8)[None, :] // NSA_GROUP).astype(np.float32)
    sn = jnp.asarray(np.tile(np.asarray(NSA_SLOPES, np.float32)[:, None], (1, LANES)))
    sd = jnp.asarray(np.tile(np.asarray(DIFF_SLOPES, np.float32)[np.arange(8) % DIFF_HEADS, None], (1, LANES)))
    subg_tile = jnp.tile(p["subln_g"], (1, DIFF_HEADS))
    o_nsa8, o_diff8, new_win_t = _sample_attention(
        page_table, qm, qd4, gt8, nsa_rows[:, None, :], win_rows[:, None, :], win_rows[:, :, None],
        diff8.reshape(B, DIFF_ROWS, LANES), win_t, pool_nsa_t, pool_diff_r,
        w1ab, pe2, w2bd, jnp.asarray(m_np), jnp.asarray(e_np, BF16), jnp.asarray(r_np), sn, sd, lam_tile, subg_tile,
        past_len, lam_init)
    o8 = o_nsa8.reshape(B, NSA_KV_HEADS, NSA_GROUP, NSA_KV_HEADS, HEAD_DIM)
    o_nsa = jnp.stack([o8[:, g, :, g] for g in range(NSA_KV_HEADS)], axis=1).reshape(B, 512)
    d8 = o_diff8[:, :DIFF_HEADS].reshape(B, DIFF_HEADS, DIFF_HEADS, DIFF_VDIM)
    o_diff = jnp.stack([d8[:, h, h] for h in range(DIFF_HEADS)], axis=1).reshape(B, 512)
    y = _post_mixer(o_nsa, o_diff, x2d, p, s_tile=B, chunk=B)
    return (y.reshape(B, T, D_MODEL),
            _feature_major_to_rows(nsa_t, 4).reshape(B, T, 4, NSA_KV_HEADS, HEAD_DIM),
            diff8.reshape(B, T, 2, DIFF_HEADS, DIFF_VDIM),
            _feature_major_to_rows(new_win_t, 2))


def kernel(x_prompt, x_sample, cache_nsa_kv, cache_diff_kv, state_nsa_win, page_table, w_in, w_out, cmp_pe, cmp_w1,
           cmp_w2, diff_lambda, diff_subln_g, ln1_g, ln1_b, ln2_g, ln2_b, w_router, b_router, w_gate_up, b_gate_up,
           w_down, b_down):
    depth = w_in.shape[0]
    xp, xs = x_prompt, x_sample
    outs = [[] for _ in range(6)]
    for l in range(depth):
        lam0 = _lambda_init(l)
        lv = diff_lambda[l].astype(F32)
        lam = jnp.exp(jnp.sum(lv[0] * lv[1])) - jnp.exp(jnp.sum(lv[2] * lv[3])) + lam0
        lam_tile = jnp.full((8, LANES), lam, F32)
        p = _prep_params(l, w_in, w_out, diff_subln_g, ln1_g, ln1_b, ln2_g, ln2_b, w_router, b_router,
                         w_gate_up, b_gate_up, w_down, b_down)
        xp, r_nsa, r_diff, r_win = _prompt_group(xp, p, cmp_pe[l], cmp_w1[l], cmp_w2[l], lam_tile, lam0)
        xs, s_nsa, s_diff, s_win = _sample_group(xs, cache_nsa_kv[l], cache_diff_kv[l], state_nsa_win[l], page_table,
                                                 p, cmp_pe[l], cmp_w1[l], cmp_w2[l], lam_tile, lam0)
        for lst, v in zip(outs, (r_nsa, r_diff, r_win, s_nsa, s_diff, s_win)):
            lst.append(v)
    return (xp, xs) + tuple(jnp.stack(o) for o in outs)
```

```python
import functools
import math

import numpy as np
import jax
import jax.numpy as jnp
from jax import lax
from jax.experimental import pallas as pl
from jax.experimental.pallas import tpu as pltpu
from jax.experimental.pallas import tpu_sc as plsc

F32 = jnp.float32
BF16 = jnp.bfloat16
HIGHEST = lax.Precision.HIGHEST

D_MODEL = 1024
HEAD_DIM = 64
NSA_HEADS = 8
NSA_KV_HEADS = 2
NSA_GROUP = NSA_HEADS // NSA_KV_HEADS
CMP_LEN = 32
CMP_STRIDE = 16
SEL_BLOCK = 64
SEL_TOPN = 16
WINDOW = 512
DIFF_HEADS = 4
DIFF_VDIM = 2 * HEAD_DIM
N_EXPERTS = 32
TOP_K = 4
SWIGLU_ALPHA = 1.702
SWIGLU_LIMIT = 7.0
LN_EPS = 1e-5
RMS_EPS = 1e-5
ATTN_SCALE = HEAD_DIM ** -0.5
LOG2E = math.log2(math.e)
DEPTH = 1
DEEPNORM_ALPHA = (2 * DEPTH) ** 0.25
PAGE_SIZE = 128

NSA_SLOPES = tuple(2.0 ** (-8.0 * (i + 1) / NSA_HEADS) for i in range(NSA_HEADS))
DIFF_SLOPES = tuple(2.0 ** (-8.0 * (i + 1) / DIFF_HEADS) for i in range(DIFF_HEADS))

VMEM_LIMIT_BYTES = 56 * 1024 * 1024
LANES = 128
MXU_DIM = 256

MOE_CHUNK = MXU_DIM
MOE_SUPERTILE = 13 * LANES
SC_CORES = 2
SC_SUBCORES = 16
SC_WINDOW = 128

NSA_W = 4 * NSA_KV_HEADS * HEAD_DIM
WIN_W = 2 * NSA_KV_HEADS * HEAD_DIM
DIFF_W = 2 * DIFF_HEADS * DIFF_VDIM
DIFF_ROWS = DIFF_W // LANES
_GATE_PAD = LANES - 3 * NSA_HEADS
_Q0, _NSA0, _WIN0, _GT0, _DQ0, _DIFF0, _PROJ_ROWS = 0, 512, 1024, 1280, 1408, 1920, 2944


def _cparams(*sem):
    return pltpu.CompilerParams(dimension_semantics=sem, vmem_limit_bytes=VMEM_LIMIT_BYTES)


def _softmax_parts(s, mask):
    s = jnp.where(mask, s, -jnp.inf)
    m = jnp.max(s, axis=-1, keepdims=True)
    m = jnp.where(jnp.isfinite(m), m, 0.0)
    e = jnp.exp(s - m)
    d = jnp.sum(e, axis=-1, keepdims=True)
    return e, jnp.where(d > 0, d, 1.0)


def _softmax2_parts(s, mask):
    s = jnp.where(mask, s, -jnp.inf)
    m = jnp.max(s, axis=-1, keepdims=True)
    m = jnp.where(jnp.isfinite(m), m, 0.0)
    e = jnp.exp2(s - m)
    d = jnp.sum(e, axis=-1, keepdims=True)
    return e, jnp.where(d > 0, d, 1.0)


def _dot_nt(a, b):
    return lax.dot_general(a, b, (((1,), (1,)), ((), ())), preferred_element_type=F32)


def _dot(a, b):
    return jnp.dot(a, b, preferred_element_type=F32)


def _pad_head(qh, g):
    z = jnp.zeros_like(qh)
    return jnp.concatenate([qh, z] if g == 0 else [z, qh], axis=1)


def _select_blocks(imp, blk, qpos, n_blk_lanes):
    cur = jnp.right_shift(qpos, int(math.log2(SEL_BLOCK)))
    valid = blk * SEL_BLOCK <= qpos
    forced = (blk == 0) | (blk == cur) | (blk == cur - 1)
    val = jnp.where(forced, jnp.inf, jnp.where(valid, imp, -jnp.inf))
    rank = jnp.zeros(val.shape, F32)
    for i in range(n_blk_lanes):
        ci = val[:, i:i + 1]
        beats = (ci > val) | ((ci == val) & (blk > i))
        rank = rank + jnp.where(beats, 1.0, 0.0)
    return jnp.where(rank < SEL_TOPN, 1.0, 0.0)


def _in_proj_kernel(x_ref, w_ref, q_ref, qtb_ref, nsa_ref, nsat_ref, nsatb_ref, win_ref, wint_ref, wintb_ref, gt_ref,
                    dq_ref, diff_ref, diffb_ref):
    tm = x_ref.shape[0]
    xb = x_ref[...].astype(BF16)
    q_ref[...] = _dot_nt(xb, w_ref[_Q0:_NSA0, :])
    qtb_ref[0] = _dot_nt(w_ref[_Q0:_NSA0, :], xb).astype(BF16)
    nsa_ref[...] = _dot_nt(xb, w_ref[_NSA0:_WIN0, :])
    r = _dot_nt(w_ref[_NSA0:_WIN0, :], xb)
    nsat_ref[0] = r
    nsatb_ref[0] = r.astype(BF16)
    win_ref[...] = _dot_nt(xb, w_ref[_WIN0:_GT0, :])
    r = _dot_nt(w_ref[_WIN0:_GT0, :], xb)
    wint_ref[0] = r
    wintb_ref[0] = r.astype(BF16)
    gt_ref[...] = jax.nn.sigmoid(_dot_nt(xb, w_ref[_GT0:_DQ0, :]))
    dq_ref[...] = _dot_nt(xb, w_ref[_DQ0:_DIFF0, :])
    r = _dot_nt(xb, w_ref[_DIFF0:_PROJ_ROWS, :])
    diffb_ref[...] = r.astype(BF16)
    for j in range(DIFF_ROWS):
        diff_ref[pl.ds(j, tm, stride=DIFF_ROWS), :] = r[:, j * LANES:(j + 1) * LANES]


def _in_proj(x2d, w_t, batch):
    n = x2d.shape[0]
    t = n // batch
    tm = min(256, t)
    nt = t // tm
    row = lambda b, i: (b * nt + i, 0)
    tr = lambda b, i: (b, 0, i)
    tok = lambda w: pl.BlockSpec((tm, w), row)
    return pl.pallas_call(
        _in_proj_kernel,
        grid=(batch, nt),
        in_specs=[pl.BlockSpec((tm, D_MODEL), row),
                  pl.BlockSpec((_PROJ_ROWS, D_MODEL), lambda b, i: (0, 0))],
        out_specs=(tok(512), pl.BlockSpec((1, 512, tm), tr),
                   tok(NSA_W), pl.BlockSpec((1, NSA_W, tm), tr), pl.BlockSpec((1, NSA_W, tm), tr),
                   tok(WIN_W), pl.BlockSpec((1, WIN_W, tm), tr), pl.BlockSpec((1, WIN_W, tm), tr),
                   tok(LANES), tok(512), pl.BlockSpec((tm * DIFF_ROWS, LANES), row), tok(DIFF_W)),
        out_shape=(jax.ShapeDtypeStruct((n, 512), F32), jax.ShapeDtypeStruct((batch, 512, t), BF16),
                   jax.ShapeDtypeStruct((n, NSA_W), F32),
                   jax.ShapeDtypeStruct((batch, NSA_W, t), F32), jax.ShapeDtypeStruct((batch, NSA_W, t), BF16),
                   jax.ShapeDtypeStruct((n, WIN_W), F32),
                   jax.ShapeDtypeStruct((batch, WIN_W, t), F32), jax.ShapeDtypeStruct((batch, WIN_W, t), BF16),
                   jax.ShapeDtypeStruct((n, LANES), F32), jax.ShapeDtypeStruct((n, 512), F32),
                   jax.ShapeDtypeStruct((n * DIFF_ROWS, LANES), F32), jax.ShapeDtypeStruct((n, DIFF_W), BF16)),
        compiler_params=_cparams("parallel", "parallel"),
        name="in_proj",
    )(x2d, w_t)


def _compress_chunks(load_rows, w1, pe, w2, n16):
    cst = _dot(pe[0].astype(BF16), w1[:, 0:LANES]) + _dot(pe[1].astype(BF16), w1[:, LANES:])
    chunks = jnp.concatenate([load_rows(l) for l in range(CMP_STRIDE)], axis=1)
    ab = _dot(chunks.astype(BF16), w1)
    nxt = pltpu.roll(ab[:, LANES:], n16 - 1, 0)
    h = jax.nn.gelu(ab[:, 0:LANES] + nxt + cst[0:1])
    return _dot(h.astype(BF16), w2)


def _compress_kernel(rows_ref, w1_ref, pe_ref, w2_ref, o_ref):
    n16 = rows_ref.shape[1] // CMP_STRIDE
    o_ref[0, 0] = _compress_chunks(lambda l: rows_ref[0, pl.ds(l, n16, stride=CMP_STRIDE), :],
                                   w1_ref[0], pe_ref[0], w2_ref[0], n16)


def _compress_prompt(nsa_rows, w1ab, pe2, w2bd):
    B, T, _ = nsa_rows.shape
    n16 = T // CMP_STRIDE
    return pl.pallas_call(
        _compress_kernel,
        grid=(B, 2),
        in_specs=[pl.BlockSpec((1, T, LANES), lambda b, k: (b, 0, k)),
                  pl.BlockSpec((1,) + w1ab.shape[1:], lambda b, k: (k, 0, 0)),
                  pl.BlockSpec((1,) + pe2.shape[1:], lambda b, k: (k, 0, 0, 0)),
                  pl.BlockSpec((1,) + w2bd.shape[1:], lambda b, k: (k, 0, 0))],
        out_specs=pl.BlockSpec((1, 1, n16, LANES), lambda b, k: (b, k, 0, 0)),
        out_shape=jax.ShapeDtypeStruct((B, 2, n16, LANES), F32),
        compiler_params=_cparams("parallel", "parallel"),
        name="compress_prompt",
    )(nsa_rows, w1ab, pe2, w2bd)


def _select_blocks_t(imp, qpos):
    n_blk = imp.shape[0]
    blk = lax.broadcasted_iota(jnp.int32, imp.shape, 0)
    cur = jnp.right_shift(qpos, int(math.log2(SEL_BLOCK)))
    valid = blk * SEL_BLOCK <= qpos
    forced = (blk == 0) | (blk == cur) | (blk == cur - 1)
    val = jnp.where(forced, jnp.inf, jnp.where(valid, imp, -jnp.inf))
    rank = jnp.zeros(val.shape, F32)
    for i in range(n_blk):
        ci = val[i:i + 1, :]
        beats = (ci > val) | ((ci == val) & (blk > i))
        rank = rank + jnp.where(beats, 1.0, 0.0)
    return jnp.where(rank < SEL_TOPN, 1.0, 0.0)


def _cmp_attn_kernel(q_ref, qt_ref, kc_ref, vc_ref, mt_ref, o_ref, selt_ref, *, tq):
    i = pl.program_id(1)
    ncmp = kc_ref.shape[2]
    kcb = kc_ref[0, 0].astype(BF16)
    vcb = vc_ref[0, 0].astype(BF16)
    qpos_c = i * tq + lax.broadcasted_iota(jnp.int32, (tq, 1), 0)
    end_r = lax.broadcasted_iota(jnp.int32, (1, ncmp), 1) * CMP_STRIDE + (CMP_LEN - 1)
    mask = end_r <= qpos_c
    end_rf = end_r.astype(F32)
    q = q_ref[0] * (ATTN_SCALE * LOG2E)
    for h in range(NSA_HEADS):
        g = h // NSA_GROUP
        qp = _pad_head(q[:, h * HEAD_DIM:(h + 1) * HEAD_DIM], g).astype(BF16)
        e, d = _softmax2_parts(_dot_nt(qp, kcb) + (NSA_SLOPES[h] * LOG2E) * end_rf, mask)
        oh = _dot(e.astype(BF16), vcb)[:, g * HEAD_DIM:(g + 1) * HEAD_DIM] / d
        o_ref[0, :, h * HEAD_DIM:(h + 1) * HEAD_DIM] = oh
    qpos_r = i * tq + lax.broadcasted_iota(jnp.int32, (1, tq), 1)
    end_c = lax.broadcasted_iota(jnp.int32, (ncmp, 1), 0) * CMP_STRIDE + (CMP_LEN - 1)
    mask_t = end_c <= qpos_r
    end_cf = end_c.astype(F32)
    qt = qt_ref[0]
    zero = jnp.zeros((HEAD_DIM, tq), BF16)
    psum = [None, None]
    for h in range(NSA_HEADS):
        g = h // NSA_GROUP
        qh = qt[h * HEAD_DIM:(h + 1) * HEAD_DIM, :]
        qpt = jnp.concatenate([qh, zero] if g == 0 else [zero, qh], axis=0)
        s = _dot(kcb, qpt) * (ATTN_SCALE * LOG2E) + (NSA_SLOPES[h] * LOG2E) * end_cf
        s = jnp.where(mask_t, s, -jnp.inf)
        m = jnp.max(s, axis=0, keepdims=True)
        m = jnp.where(jnp.isfinite(m), m, 0.0)
        e = jnp.exp2(s - m)
        d = jnp.sum(e, axis=0, keepdims=True)
        p = e / jnp.where(d > 0, d, 1.0)
        psum[g] = p if psum[g] is None else psum[g] + p
    sels = []
    for g in range(NSA_KV_HEADS):
        imp = jnp.dot(mt_ref[...], psum[g], precision=HIGHEST, preferred_element_type=F32)
        sels.append(_select_blocks_t(imp, qpos_r))
    sels.append(jnp.zeros((LANES - NSA_KV_HEADS * mt_ref.shape[0], tq), F32))
    selt_ref[0] = jnp.concatenate(sels, axis=0)


def _cmp_attn_prompt(q_nsa, q_tb, kvc, mt_mat, tq=256):
    B, T, _ = q_nsa.shape
    ncmp = kvc.shape[2]
    n_sel = mt_mat.shape[0]
    return pl.pallas_call(
        functools.partial(_cmp_attn_kernel, tq=tq),
        grid=(B, T // tq),
        in_specs=[pl.BlockSpec((1, tq, 512), lambda b, i: (b, i, 0)),
                  pl.BlockSpec((1, 512, tq), lambda b, i: (b, 0, i)),
                  pl.BlockSpec((1, 1, ncmp, LANES), lambda b, i: (b, 0, 0, 0)),
                  pl.BlockSpec((1, 1, ncmp, LANES), lambda b, i: (b, 1, 0, 0)),
                  pl.BlockSpec((n_sel, ncmp), lambda b, i: (0, 0))],
        out_specs=(pl.BlockSpec((1, tq, 512), lambda b, i: (b, i, 0)),
                   pl.BlockSpec((1, LANES, tq), lambda b, i: (b, 0, i))),
        out_shape=(jax.ShapeDtypeStruct((B, T, 512), F32), jax.ShapeDtypeStruct((B, LANES, T), F32)),
        compiler_params=_cparams("parallel", "parallel"),
        name="cmp_attn_prompt",
    )(q_nsa, q_tb, kvc, kvc, mt_mat)


def _sel_win_kernel(q_ref, ks_ref, vs_ref, kw_ref, vw_ref, sel_ref, e_ref, ocmp_ref, gt_ref, o_ref, *, tq, span,
                    kv_step):
    i = pl.program_id(1)
    T = ks_ref.shape[2]
    qpos = i * tq + lax.broadcasted_iota(jnp.int32, (tq, 1), 0)
    q = q_ref[0] * (ATTN_SCALE * LOG2E)
    gt = gt_ref[0]
    ocmp = ocmp_ref[0]
    selb = sel_ref[0].astype(BF16)
    heads = [(h, h // NSA_GROUP, h * HEAD_DIM, (h + 1) * HEAD_DIM) for h in range(NSA_HEADS)]
    qps = [_pad_head(q[:, lo:hi], g).astype(BF16) for _, g, lo, hi in heads]
    start = pl.multiple_of(jnp.maximum(i * tq - WINDOW, 0), LANES)
    wpos = start + lax.broadcasted_iota(jnp.int32, (1, span), 1)
    dist_w = qpos - wpos
    mask_w = (dist_w >= 0) & (dist_w <= WINDOW)
    wposf = wpos.astype(F32)
    kwb = kw_ref[0, :, pl.ds(start, span)]
    vwb = vw_ref[0, :, pl.ds(start, span)]
    partial = []
    for h, g, lo, hi in heads:
        s = _dot(qps[h], kwb) + (NSA_SLOPES[h] * LOG2E) * wposf
        e, d = _softmax2_parts(s, mask_w)
        o_win = _dot_nt(e.astype(BF16), vwb)[:, g * HEAD_DIM:(g + 1) * HEAD_DIM] / d
        partial.append(gt[:, h:h + 1] * ocmp[:, lo:hi] + gt[:, 2 * NSA_HEADS + h:2 * NSA_HEADS + h + 1] * o_win)
    tiles_per_step = kv_step // tq
    for br in range(T // kv_step):
        @pl.when((i >= br * tiles_per_step) & (i < (br + 1) * tiles_per_step))
        def _(br=br):
            kv = (br + 1) * kv_step
            kpos = lax.broadcasted_iota(jnp.int32, (1, kv), 1)
            causal = kpos <= qpos
            kposf = kpos.astype(F32)
            ksb = ks_ref[0, :, 0:kv]
            vsb = vs_ref[0, :, 0:kv]
            for g in range(NSA_KV_HEADS):
                expand = lax.dot_general(selb, e_ref[g, :, 0:kv], (((0,), (0,)), ((), ())), preferred_element_type=F32)
                mask_s = (expand > 0.5) & causal
                for h, _, lo, hi in heads[g * NSA_GROUP:(g + 1) * NSA_GROUP]:
                    s = _dot(qps[h], ksb) + (NSA_SLOPES[h] * LOG2E) * kposf
                    e, d = _softmax2_parts(s, mask_s)
                    o_sel = _dot_nt(e.astype(BF16), vsb)[:, g * HEAD_DIM:(g + 1) * HEAD_DIM] / d
                    o_ref[0, :, lo:hi] = partial[h] + gt[:, NSA_HEADS + h:NSA_HEADS + h + 1] * o_sel


def _sel_win_prompt(q_nsa, nsa_t, win_t, sel_t, e_mat, o_cmp, gates, tq=256):
    B, T, _ = q_nsa.shape
    span = WINDOW + tq
    kv = lambda j: pl.BlockSpec((1, LANES, T), lambda b, i: (b, j, 0))
    return pl.pallas_call(
        functools.partial(_sel_win_kernel, tq=tq, span=span, kv_step=min(2 * tq, T)),
        grid=(B, T // tq),
        in_specs=[pl.BlockSpec((1, tq, 512), lambda b, i: (b, i, 0)),
                  kv(2), kv(3), kv(0), kv(1),
                  pl.BlockSpec((1, LANES, tq), lambda b, i: (b, 0, i)),
                  pl.BlockSpec((NSA_KV_HEADS, LANES, T), lambda b, i: (0, 0, 0)),
                  pl.BlockSpec((1, tq, 512), lambda b, i: (b, i, 0)),
                  pl.BlockSpec((1, tq, LANES), lambda b, i: (b, i, 0))],
        out_specs=pl.BlockSpec((1, tq, 512), lambda b, i: (b, i, 0)),
        out_shape=jax.ShapeDtypeStruct((B, T, 512), F32),
        compiler_params=_cparams("parallel", "parallel"),
        name="sel_win_prompt",
    )(q_nsa, nsa_t, nsa_t, win_t, win_t, sel_t, e_mat, o_cmp, gates)


def _diff_kernel(lam_ref, sl_ref, q_ref, k_ref, v_ref, g_ref, o_ref, *, tq, lam_init):
    i = pl.program_id(2)
    T = k_ref.shape[1]
    lam = lam_ref[0:1, 0:1]
    q = q_ref[0] * (ATTN_SCALE * LOG2E)
    qps = [_pad_head(q[:, c * HEAD_DIM:(c + 1) * HEAD_DIM], c).astype(BF16) for c in range(2)]
    tri = lax.broadcasted_iota(jnp.int32, (tq, tq), 0) >= lax.broadcasted_iota(jnp.int32, (tq, tq), 1)
    for br in range(T // tq):
        @pl.when(i == br)
        def _(br=br):
            off = br * tq
            kv = off + tq
            kb = k_ref[0, 0:kv, :]
            vb = v_ref[0, 0:kv, :]
            col = (sl_ref[0, :, 0:1] * LOG2E) * lax.broadcasted_iota(jnp.int32, (1, kv), 1).astype(F32)
            outs = []
            for c in range(2):
                s = _dot_nt(qps[c], kb) + col
                s_d = jnp.where(tri, s[:, off:], -jnp.inf)
                m = jnp.max(s_d, axis=-1, keepdims=True)
                if off:
                    m = jnp.maximum(m, jnp.max(s[:, :off], axis=-1, keepdims=True))
                e_d = jnp.exp2(s_d - m)
                d = jnp.sum(e_d, axis=-1, keepdims=True)
                o = _dot(e_d.astype(BF16), vb[off:, :])
                if off:
                    e_o = jnp.exp2(s[:, :off] - m)
                    d = d + jnp.sum(e_o, axis=-1, keepdims=True)
                    o = o + _dot(e_o.astype(BF16), vb[:off, :])
                outs.append(o / d)
            od = outs[0] - lam * outs[1]
            od = od * lax.rsqrt(jnp.mean(jnp.square(od), axis=-1, keepdims=True) + RMS_EPS)
            o_ref[0] = od * g_ref[...] * (1.0 - lam_init)


def _diff_prompt(lam_tile, dq, diff_b, subln_g, lam_init, tq=256):
    B, T, _ = dq.shape
    slopes = jnp.asarray(np.tile(np.asarray(DIFF_SLOPES, np.float32)[:, None, None], (1, 1, LANES)))
    return pl.pallas_call(
        functools.partial(_diff_kernel, tq=tq, lam_init=lam_init),
        grid=(B, DIFF_HEADS, T // tq),
        in_specs=[pl.BlockSpec((8, LANES), lambda b, h, i: (0, 0)),
                  pl.BlockSpec((1, 1, LANES), lambda b, h, i: (h, 0, 0)),
                  pl.BlockSpec((1, tq, LANES), lambda b, h, i: (b, i, h)),
                  pl.BlockSpec((1, T, LANES), lambda b, h, i: (b, 0, h)),
                  pl.BlockSpec((1, T, LANES), lambda b, h, i: (b, 0, DIFF_HEADS + h)),
                  pl.BlockSpec((1, DIFF_VDIM), lambda b, h, i: (0, 0))],
        out_specs=pl.BlockSpec((1, tq, LANES), lambda b, h, i: (b, i, h)),
        out_shape=jax.ShapeDtypeStruct((B, T, DIFF_HEADS * DIFF_VDIM), F32),
        compiler_params=_cparams("parallel", "parallel", "parallel"),
        name="diff_prompt",
    )(lam_tile, slopes, dq, diff_b, diff_b, subln_g)


def _layer_norm(z, g, b):
    mu = jnp.mean(z, axis=-1, keepdims=True)
    zc = z - mu
    var = jnp.mean(jnp.square(zc), axis=-1, keepdims=True)
    return zc * lax.rsqrt(var + LN_EPS) * g + b


def _pack_rows(x, ref):
    w = x.shape[1] // 2
    hi = lax.bitcast_convert_type(x[:, :w].astype(BF16).astype(F32), jnp.uint32)
    lo = lax.bitcast_convert_type(x[:, w:].astype(BF16).astype(F32), jnp.uint32)
    packed = hi | (lo >> 16)
    ref[0] = packed[:, :w // 2]
    ref[1] = packed[:, w // 2:]


def _unpack_rows(u0, u1):
    hi = [lax.bitcast_convert_type(u & jnp.uint32(0xFFFF0000), F32) for u in (u0, u1)]
    lo = [lax.bitcast_convert_type(u << 16, F32) for u in (u0, u1)]
    return jnp.concatenate(hi + lo, axis=1)


def _out_proj_kernel(on_ref, od_ref, x_ref, wa_ref, wb_ref, g_ref, b_ref, wr_ref, br_ref,
                     x1_ref, x1b_ref, x1p_ref, sel_ref, gate_ref, tope_ref, gate8_ref, cnt_ref):
    y = _dot(on_ref[...].astype(BF16), wa_ref[...]) + _dot(od_ref[...].astype(BF16), wb_ref[...])
    x1 = _layer_norm(DEEPNORM_ALPHA * x_ref[...] + y, g_ref[...], b_ref[...])
    x1_ref[...] = x1
    x1b_ref[...] = x1.astype(BF16)
    _pack_rows(x1, x1p_ref)
    logits = lax.dot_general(wr_ref[...], x1, (((1,), (1,)), ((), ())), precision=HIGHEST,
                             preferred_element_type=F32) + br_ref[...]
    eidx = lax.broadcasted_iota(jnp.int32, logits.shape, 0).astype(F32)
    sel = jnp.zeros(logits.shape, F32)
    picked, vals = [], []
    for k in range(TOP_K):
        cur = jnp.where(sel > 0.5, -jnp.inf, logits)
        m = jnp.max(cur, axis=0, keepdims=True)
        first = jnp.min(jnp.where(cur == m, eidx, float(N_EXPERTS)), axis=0, keepdims=True)
        sel = jnp.where(eidx == first, 1.0, sel)
        picked.append(first)
        vals.append(m)
    ex = jnp.where(sel > 0.5, jnp.exp(logits - vals[0]), 0.0)
    denom = jnp.sum(ex, axis=0, keepdims=True)
    sel_ref[...] = sel
    gate_ref[...] = ex / denom
    pad = [jnp.zeros_like(denom)] * (8 - TOP_K)
    tope_ref[...] = jnp.concatenate(picked + pad, axis=0)
    gate8_ref[...] = jnp.concatenate([jnp.exp(v - vals[0]) / denom for v in vals] + pad, axis=0)
    cnt_ref[0] = jnp.sum(sel, axis=1, keepdims=True)


def _out_proj_ln_router(o_nsa, o_diff, x2d, wa, wb, ln_g, ln_b, wr_t, br):
    n = x2d.shape[0]
    tm = min(256, n)
    row = lambda i: (i, 0)
    fix = lambda i: (0, 0)
    col = lambda i: (0, i)
    return pl.pallas_call(
        _out_proj_kernel,
        grid=(n // tm,),
        in_specs=[pl.BlockSpec((tm, 512), row), pl.BlockSpec((tm, 512), row), pl.BlockSpec((tm, D_MODEL), row),
                  pl.BlockSpec((512, D_MODEL), fix), pl.BlockSpec((512, D_MODEL), fix),
                  pl.BlockSpec((1, D_MODEL), fix), pl.BlockSpec((1, D_MODEL), fix),
                  pl.BlockSpec((N_EXPERTS, D_MODEL), fix), pl.BlockSpec((N_EXPERTS, 1), fix)],
        out_specs=(pl.BlockSpec((tm, D_MODEL), row), pl.BlockSpec((tm, D_MODEL), row),
                   pl.BlockSpec((2, tm, D_MODEL // 4), lambda i: (0, i, 0)),
                   pl.BlockSpec((N_EXPERTS, tm), col), pl.BlockSpec((N_EXPERTS, tm), col),
                   pl.BlockSpec((8, tm), col), pl.BlockSpec((8, tm), col),
                   pl.BlockSpec((1, N_EXPERTS, 1), lambda i: (i, 0, 0))),
        out_shape=(jax.ShapeDtypeStruct((n, D_MODEL), F32), jax.ShapeDtypeStruct((n, D_MODEL), BF16),
                   jax.ShapeDtypeStruct((2, n, D_MODEL // 4), jnp.uint32),
                   jax.ShapeDtypeStruct((N_EXPERTS, n), F32), jax.ShapeDtypeStruct((N_EXPERTS, n), F32),
                   jax.ShapeDtypeStruct((8, n), F32), jax.ShapeDtypeStruct((8, n), F32),
                   jax.ShapeDtypeStruct((n // tm, N_EXPERTS, 1), F32)),
        compiler_params=_cparams("parallel"),
        name="out_proj_ln_router",
    )(o_nsa, o_diff, x2d, wa, wb, ln_g, ln_b, wr_t, br)


def _moe_weight_kernel(w_ref, p_ref, wg_ref, wl_ref):
    half = MXU_DIM // 2
    for m in range(w_ref.shape[2] // MXU_DIM):
        y = _dot(w_ref[0, :, m * MXU_DIM:(m + 1) * MXU_DIM].astype(BF16), p_ref[...])
        wg_ref[0, :, m * half:(m + 1) * half] = y[:, :half].astype(BF16)
        wl_ref[0, :, m * half:(m + 1) * half] = y[:, half:].astype(BF16)


def _moe_split_gate_up(w_gate_up):
    ne, dm, two_ff = w_gate_up.shape
    half = MXU_DIM // 2
    p_np = np.zeros((MXU_DIM, MXU_DIM), np.float32)
    p_np[2 * np.arange(half), np.arange(half)] = 1.0
    p_np[2 * np.arange(half) + 1, half + np.arange(half)] = 1.0
    out = jax.ShapeDtypeStruct((ne, dm, two_ff // 2), BF16)
    ospec = pl.BlockSpec((1, dm, two_ff // 2), lambda e: (e, 0, 0))
    return pl.pallas_call(
        _moe_weight_kernel,
        grid=(ne,),
        in_specs=[pl.BlockSpec((1, dm, two_ff), lambda e: (e, 0, 0)),
                  pl.BlockSpec((MXU_DIM, MXU_DIM), lambda e: (0, 0))],
        out_specs=(ospec, ospec),
        out_shape=(out, out),
        compiler_params=_cparams("parallel"),
        name="moe_split_gate_up",
    )(w_gate_up, jnp.asarray(p_np, BF16))


def _moe_pos_kernel(sel_ref, u_ref, pos_ref, cnt_ref, *, n_valid):
    s_tile = sel_ref.shape[1]
    tok = pl.program_id(0) * s_tile + lax.broadcasted_iota(jnp.int32, (1, s_tile), 1)
    sel = jnp.where(tok < n_valid, sel_ref[...], 0.0)
    rank = _dot(sel.astype(BF16), u_ref[...])
    pos_ref[...] = jnp.where(sel > 0.5, rank, -1.0)
    cnt_ref[0] = jnp.sum(sel, axis=1, keepdims=True)


def _moe_positions(sel_t, s_tile):
    n = sel_t.shape[1]
    ns = pl.cdiv(n, s_tile)
    upper = jnp.asarray(np.triu(np.ones((s_tile, s_tile), np.float32), 1), BF16)
    return pl.pallas_call(
        functools.partial(_moe_pos_kernel, n_valid=n),
        grid=(ns,),
        in_specs=[pl.BlockSpec((N_EXPERTS, s_tile), lambda s: (0, s)),
                  pl.BlockSpec((s_tile, s_tile), lambda s: (0, 0))],
        out_specs=(pl.BlockSpec((N_EXPERTS, s_tile), lambda s: (0, s)),
                   pl.BlockSpec((1, N_EXPERTS, 1), lambda s: (s, 0, 0))),
        out_shape=(jax.ShapeDtypeStruct((N_EXPERTS, ns * s_tile), F32),
                   jax.ShapeDtypeStruct((ns, N_EXPERTS, 1), F32)),
        compiler_params=_cparams("parallel"),
        name="moe_positions",
    )(sel_t, upper)


def _moe_kernel(nch_ref, x_ref, pos_ref, gate_ref, wg_ref, wl_ref, bg_ref, bl_ref, wd_ref, bd_ref, y_ref, *, chunk,
                n_valid):
    s = pl.program_id(0)
    e = pl.program_id(1)
    s_tile = x_ref.shape[0]

    @pl.when(e == 0)
    def _():
        y_ref[...] = jnp.zeros_like(y_ref)

    pos = pos_ref[0, 0].astype(jnp.int32)
    gate = gate_ref[0, 0]
    slot0 = lax.broadcasted_iota(jnp.int32, (chunk, 1), 0)

    def body(j, carry):
        hit = pos == slot0 + j * chunk
        onehot = jnp.where(hit, 1.0, 0.0).astype(BF16)
        x = x_ref[...]
        if n_valid % s_tile:
            row = s * s_tile + lax.broadcasted_iota(jnp.int32, (s_tile, 1), 0)
            x = jnp.where(row < n_valid, x, jnp.zeros_like(x))
        xs = _dot(onehot, x).astype(BF16)
        hg = jnp.minimum(_dot(xs, wg_ref[0]) + bg_ref[0], SWIGLU_LIMIT)
        hl = jnp.clip(_dot(xs, wl_ref[0]) + bl_ref[0], -SWIGLU_LIMIT, SWIGLU_LIMIT)
        a = (hl + 1.0) * hg * jax.nn.sigmoid(SWIGLU_ALPHA * hg)
        out = _dot(a.astype(BF16), wd_ref[0]) + bd_ref[0]
        gslot = jnp.sum(jnp.where(hit, gate, 0.0), axis=1, keepdims=True)
        outg = (out * gslot).astype(BF16)
        y_ref[...] += lax.dot_general(onehot, outg, (((0,), (0,)), ((), ())), preferred_element_type=F32)
        return carry

    lax.fori_loop(0, nch_ref[s * N_EXPERTS + e], body, 0)


def _moe(x1b, pos_t, gate_t, counts, wg, wl, bg, bl, wd, bd, s_tile, chunk):
    n = x1b.shape[0]
    ns = pl.cdiv(n, s_tile)
    nch = ((counts.reshape(ns * N_EXPERTS) + (chunk - 1)) // chunk).astype(jnp.int32)
    pos4 = pos_t.reshape(N_EXPERTS, ns, 1, s_tile)
    gate4 = jnp.pad(gate_t, ((0, 0), (0, ns * s_tile - n))).reshape(N_EXPERTS, ns, 1, s_tile)
    wspec = pl.BlockSpec((1, D_MODEL, D_MODEL), lambda s, e, nch: (e, 0, 0))
    bspec = pl.BlockSpec((1, 1, D_MODEL), lambda s, e, nch: (e, 0, 0))
    rspec = pl.BlockSpec((1, 1, 1, s_tile), lambda s, e, nch: (e, s, 0, 0))
    return pl.pallas_call(
        functools.partial(_moe_kernel, chunk=chunk, n_valid=n),
        grid_spec=pltpu.PrefetchScalarGridSpec(
            num_scalar_prefetch=1,
            grid=(ns, N_EXPERTS),
            in_specs=[pl.BlockSpec((s_tile, D_MODEL), lambda s, e, nch: (s, 0)),
                      rspec, rspec, wspec, wspec, bspec, bspec, wspec, bspec],
            out_specs=pl.BlockSpec((s_tile, D_MODEL), lambda s, e, nch: (s, 0))),
        out_shape=jax.ShapeDtypeStruct((n, D_MODEL), F32),
        compiler_params=_cparams("parallel", "arbitrary"),
        name="moe_experts",
    )(nch, x1b, pos4, gate4, wg, wl, bg, bl, wd, bd)


def _ln2_kernel(x_ref, f_ref, g_ref, b_ref, o_ref):
    o_ref[...] = _layer_norm(DEEPNORM_ALPHA * x_ref[...] + f_ref[...], g_ref[...], b_ref[...])


def _residual_ln(x1, f, g, b):
    n = x1.shape[0]
    tm = min(512, n)
    row = lambda i: (i, 0)
    fix = lambda i: (0, 0)
    return pl.pallas_call(
        _ln2_kernel,
        grid=(n // tm,),
        in_specs=[pl.BlockSpec((tm, D_MODEL), row), pl.BlockSpec((tm, D_MODEL), row),
                  pl.BlockSpec((1, D_MODEL), fix), pl.BlockSpec((1, D_MODEL), fix)],
        out_specs=pl.BlockSpec((tm, D_MODEL), row),
        out_shape=jax.ShapeDtypeStruct((n, D_MODEL), F32),
        compiler_params=_cparams("parallel"),
        name="residual_ln2",
    )(x1, f, g, b)


def _post_mixer(o_nsa, o_diff, x2d, p, s_tile, chunk):
    x1, x1b, _, sel_t, gate_t, _, _, _ = _out_proj_ln_router(o_nsa, o_diff, x2d, p["wo_a"], p["wo_b"], p["ln1_g"],
                                                             p["ln1_b"], p["wr_t"], p["br"])
    pos_t, counts = _moe_positions(sel_t, s_tile)
    f = _moe(x1b, pos_t, gate_t, counts, p["wg"], p["wl"], p["bg"], p["bl"], p["wd"], p["bd"], s_tile, chunk)
    return _residual_ln(x1, f, p["ln2_g"], p["ln2_b"])


def _moe_slots_kernel(sel_ref, tope_ref, pstart_ref, u_ref, slot_ref, carry_ref):
    @pl.when(pl.program_id(0) == 0)
    def _():
        carry_ref[...] = jnp.zeros_like(carry_ref)

    sel = sel_ref[...]
    slot_all = pstart_ref[...] + carry_ref[...] + _dot(sel.astype(BF16), u_ref[...])
    eidx = lax.broadcasted_iota(jnp.int32, sel.shape, 0).astype(F32)
    tope = tope_ref[...]
    rows = [jnp.sum(jnp.where(eidx == tope[k:k + 1], slot_all, 0.0), axis=0, keepdims=True) for k in range(TOP_K)]
    rows += [jnp.zeros_like(rows[0])] * (8 - TOP_K)
    slot_ref[...] = jnp.concatenate(rows, axis=0).astype(jnp.int32)
    carry_ref[...] += jnp.sum(sel, axis=1, keepdims=True)


def _moe_slots(sel_t, tope, pstart, tm=2048):
    n = sel_t.shape[1]
    upper = jnp.asarray(np.triu(np.ones((tm, tm), np.float32), 1), BF16)
    return pl.pallas_call(
        _moe_slots_kernel,
        grid=(n // tm,),
        in_specs=[pl.BlockSpec((N_EXPERTS, tm), lambda i: (0, i)), pl.BlockSpec((8, tm), lambda i: (0, i)),
                  pl.BlockSpec((N_EXPERTS, 1), lambda i: (0, 0)), pl.BlockSpec((tm, tm), lambda i: (0, 0))],
        out_specs=pl.BlockSpec((8, tm), lambda i: (0, i)),
        out_shape=jax.ShapeDtypeStruct((8, n), jnp.int32),
        scratch_shapes=[pltpu.VMEM((N_EXPERTS, 1), F32)],
        compiler_params=_cparams("arbitrary"),
        name="moe_slots",
    )(sel_t, tope, pstart, upper)


def _sc_mesh():
    return plsc.VectorSubcoreMesh(core_axis_name="c", subcore_axis_name="s", num_cores=SC_CORES,
                                  num_subcores=SC_SUBCORES)


def _sc_scatter_rows(src, idx, n_out):
    n_rows, width = src.shape
    n_idx = idx.shape[0]
    per_core = n_rows // SC_WINDOW // SC_CORES

    @functools.partial(pl.kernel, out_type=jax.ShapeDtypeStruct((n_out, width), src.dtype), mesh=_sc_mesh(),
                       scratch_types=[], name="moe_dispatch_rows")
    def scatter(src_hbm, idx_hbm, out_hbm):
        def body(src_vmem, idx_vmem):
            for k in range(n_idx):
                pltpu.sync_copy(src_vmem, out_hbm.at[idx_vmem.at[k]])

        pltpu.emit_pipeline(
            body,
            grid=(SC_CORES, per_core),
            in_specs=[pl.BlockSpec((SC_WINDOW, width), lambda c, j: (c * per_core + j, 0)),
                      pl.BlockSpec((n_idx, SC_WINDOW), lambda c, j: (0, c * per_core + j))],
            out_specs=[],
            core_axis_name=("c", "s"),
            dimension_semantics=(pltpu.PARALLEL, pltpu.PARALLEL),
        )(src_hbm, idx_hbm)

    return scatter(src, idx)


def _sc_gather_rows(table, idx):
    n_rows = idx.shape[1]
    width = table.shape[1]
    per_core = n_rows // SC_WINDOW // SC_CORES

    @functools.partial(pl.kernel, out_type=jax.ShapeDtypeStruct((n_rows, width), table.dtype), mesh=_sc_mesh(),
                       scratch_types=[], name="moe_return_rows")
    def gather(table_hbm, idx_hbm, out_hbm):
        def body(idx_vmem, out_vmem):
            pltpu.sync_copy(table_hbm.at[idx_vmem.at[0]], out_vmem)

        pltpu.emit_pipeline(
            body,
            grid=(SC_CORES, per_core),
            in_specs=[pl.BlockSpec((1, SC_WINDOW), lambda c, j: (0, c * per_core + j))],
            out_specs=[pl.BlockSpec((SC_WINDOW, width), lambda c, j: (c * per_core + j, 0))],
            core_axis_name=("c", "s"),
            dimension_semantics=(pltpu.PARALLEL, pltpu.PARALLEL),
        )(idx_hbm, out_hbm)

    return gather(table, idx)


def _moe_ffn_kernel(blk_e_ref, nvalid_ref, xs_ref, wg_ref, wl_ref, bg_ref, bl_ref, wd_ref, bd_ref, o_ref):
    nv = nvalid_ref[pl.program_id(0)]

    @pl.when(nv > 0)
    def _():
        live = lax.broadcasted_iota(jnp.int32, (o_ref.shape[1], 1), 0) < nv
        xs = jnp.where(live, _unpack_rows(xs_ref[0], xs_ref[1]), 0.0).astype(BF16)
        hg = jnp.minimum(_dot(xs, wg_ref[0]) + bg_ref[0], SWIGLU_LIMIT)
        hl = jnp.clip(_dot(xs, wl_ref[0]) + bl_ref[0], -SWIGLU_LIMIT, SWIGLU_LIMIT)
        a = (hl + 1.0) * hg * jax.nn.sigmoid(SWIGLU_ALPHA * hg)
        _pack_rows(_dot(a.astype(BF16), wd_ref[0]) + bd_ref[0], o_ref)

    @pl.when(nv == 0)
    def _():
        o_ref[...] = jnp.zeros_like(o_ref)


def _moe_ffn(blk_e, nvalid, xs, wg, wl, bg, bl, wd, bd):
    _, n_slots, half = xs.shape
    wspec = pl.BlockSpec((1, D_MODEL, D_MODEL), lambda i, be, nv: (be[i], 0, 0))
    bspec = pl.BlockSpec((1, 1, D_MODEL), lambda i, be, nv: (be[i], 0, 0))
    rows = pl.BlockSpec((2, MOE_CHUNK, half), lambda i, be, nv: (0, i, 0))
    return pl.pallas_call(
        _moe_ffn_kernel,
        grid_spec=pltpu.PrefetchScalarGridSpec(
            num_scalar_prefetch=2,
            grid=(n_slots // MOE_CHUNK,),
            in_specs=[rows, wspec, wspec, bspec, bspec, wspec, bspec],
            out_specs=rows),
        out_shape=jax.ShapeDtypeStruct((2, n_slots, half), jnp.uint32),
        compiler_params=_cparams("arbitrary"),
        name="moe_ffn_sorted",
    )(blk_e, nvalid, xs, wg, wl, bg, bl, wd, bd)


def _ln2_combine_kernel(x_ref, r_ref, gate_ref, g_ref, b_ref, o_ref):
    gate = gate_ref[...]
    f = gate[:, 0:1] * _unpack_rows(r_ref[0, 0], r_ref[0, 1])
    for k in range(1, TOP_K):
        f = f + gate[:, k:k + 1] * _unpack_rows(r_ref[k, 0], r_ref[k, 1])
    o_ref[...] = _layer_norm(DEEPNORM_ALPHA * x_ref[...] + f, g_ref[...], b_ref[...])


def _ln2_combine(x1, returned, gate_tok, g, b, tm=256):
    n = x1.shape[0]
    row = lambda i: (i, 0)
    fix = lambda i: (0, 0)
    return pl.pallas_call(
        _ln2_combine_kernel,
        grid=(n // tm,),
        in_specs=[pl.BlockSpec((tm, D_MODEL), row),
                  pl.BlockSpec((TOP_K, 2, tm, D_MODEL // 4), lambda i: (0, 0, i, 0)),
                  pl.BlockSpec((tm, 8), row), pl.BlockSpec((1, D_MODEL), fix), pl.BlockSpec((1, D_MODEL), fix)],
        out_specs=pl.BlockSpec((tm, D_MODEL), row),
        out_shape=jax.ShapeDtypeStruct((n, D_MODEL), F32),
        compiler_params=_cparams("parallel"),
        name="combine_ln2",
    )(x1, returned, gate_tok, g, b)


def _post_mixer_sorted(o_nsa, o_diff, x2d, p):
    n = x2d.shape[0]
    x1, _, x1p, sel_t, _, tope, gate8, cnt = _out_proj_ln_router(o_nsa, o_diff, x2d, p["wo_a"], p["wo_b"], p["ln1_g"],
                                                                 p["ln1_b"], p["wr_t"], p["br"])
    counts = jnp.sum(cnt[:, :, 0], axis=0).astype(jnp.int32)
    padded = (counts + (MOE_CHUNK - 1)) // MOE_CHUNK * MOE_CHUNK
    pend = jnp.cumsum(padded)
    pstart = pend - padded
    n_blocks = n * TOP_K // MOE_CHUNK + N_EXPERTS
    n_slots = n_blocks * MOE_CHUNK
    blk0 = jnp.arange(n_blocks, dtype=jnp.int32)[:, None] * MOE_CHUNK
    blk_e = jnp.minimum(jnp.sum((blk0 >= pend[None, :]).astype(jnp.int32), axis=1), N_EXPERTS - 1)
    in_region = (blk0 >= pstart[None, :]) & (blk0 < pend[None, :])
    nvalid = jnp.sum(jnp.where(in_region, jnp.clip(pstart + counts - blk0, 0, MOE_CHUNK), 0), axis=1)
    slot = _moe_slots(sel_t, tope, pstart.astype(F32)[:, None])[:TOP_K]
    half = D_MODEL // 4
    slot2 = jnp.concatenate([slot, slot + n_slots], axis=1)
    xs = _sc_scatter_rows(x1p.reshape(2 * n, half), slot2, 2 * n_slots)
    outs = _moe_ffn(blk_e, nvalid, xs.reshape(2, n_slots, half), p["wg"], p["wl"], p["bg"], p["bl"], p["wd"], p["bd"])
    returned = _sc_gather_rows(outs.reshape(2 * n_slots, half), slot2.reshape(1, TOP_K * 2 * n))
    return _ln2_combine(x1, returned.reshape(TOP_K, 2, n, half), gate8.T, p["ln2_g"], p["ln2_b"])


def _sample_kernel(pt_ref, qm_ref, qd_ref, gt_ref, nrow_ref, wrow_ref, wcol_ref, drow_ref, win_ref,
                   w1_ref, pe_ref, w2_ref, m_ref, e_ref, r_ref, sn_ref, sd_ref, lam_ref, subg_ref, *rest,
                   n_pages, past_len, lam_init):
    nsa_pages = rest[:n_pages]
    diff_pages = rest[n_pages:2 * n_pages]
    onsa_ref, odiff_ref, nwin_ref, x_ref = rest[2 * n_pages:]
    n16 = past_len // CMP_STRIDE
    qpos = past_len

    for j, pg in enumerate(nsa_pages):
        for kind in range(2):
            x_ref[kind, j * PAGE_SIZE:(j + 1) * PAGE_SIZE, :] = pg[kind * LANES:(kind + 1) * LANES, :].T

    kv_cmp = [_compress_chunks(lambda l, kind=kind: x_ref[kind, pl.ds(l, n16, stride=CMP_STRIDE), :],
                               w1_ref[kind], pe_ref[kind], w2_ref[kind], n16) for kind in range(2)]

    qm = qm_ref[0]
    qmb = qm.astype(BF16)
    slope_n = sn_ref[:, 0:1]
    nrow = nrow_ref[0]
    wrow = wrow_ref[0]
    gt = gt_ref[0]

    def new_key_score(qrows, krow):
        return jnp.sum(qrows * krow, axis=1, keepdims=True) * ATTN_SCALE

    cmp_end = lax.broadcasted_iota(jnp.int32, (1, n16), 1) * CMP_STRIDE + (CMP_LEN - 1)
    dist_c = qpos - cmp_end
    s = _dot_nt(qmb, kv_cmp[0].astype(BF16)) * ATTN_SCALE - slope_n * dist_c.astype(F32)
    e, d = _softmax_parts(s, dist_c >= 0)
    p_cmp = e / d
    o_cmp = _dot(p_cmp.astype(BF16), kv_cmp[1].astype(BF16))
    pgrp = jnp.dot(r_ref[...], p_cmp, precision=HIGHEST, preferred_element_type=F32)
    imp = jnp.dot(pgrp, m_ref[...], precision=HIGHEST, preferred_element_type=F32)
    nbl = m_ref.shape[1]
    blk = lax.broadcasted_iota(jnp.int32, (NSA_HEADS, nbl), 1)
    sel = _select_blocks(imp, blk, jnp.full((NSA_HEADS, 1), qpos, jnp.int32), past_len // SEL_BLOCK + 1)
    mask_s = _dot(sel.astype(BF16), e_ref[...]) > 0.5

    kpos = lax.broadcasted_iota(jnp.int32, (1, past_len), 1)
    dist_k = (qpos - kpos).astype(F32)
    s = jnp.concatenate([_dot(qmb, pg[2 * LANES:3 * LANES, :].astype(BF16)) for pg in nsa_pages], axis=1)
    s = jnp.where(mask_s, s * ATTN_SCALE - slope_n * dist_k, -jnp.inf)
    s_new = new_key_score(qm, nrow[:, 2 * LANES:3 * LANES])
    m = jnp.maximum(jnp.max(s, axis=1, keepdims=True), s_new)
    e = jnp.exp(s - m)
    e_new = jnp.exp(s_new - m)
    d = jnp.sum(e, axis=1, keepdims=True) + e_new
    eb = e.astype(BF16)
    acc = e_new * nrow[:, 3 * LANES:4 * LANES]
    for j, pg in enumerate(nsa_pages):
        acc = acc + _dot_nt(eb[:, j * PAGE_SIZE:(j + 1) * PAGE_SIZE], pg[3 * LANES:4 * LANES, :].astype(BF16))
    o_sel = acc / d

    nwin = win_ref.shape[2]
    wpos = past_len - nwin + lax.broadcasted_iota(jnp.int32, (1, nwin), 1)
    dist_w = qpos - wpos
    mask_w = (dist_w >= 0) & (dist_w <= WINDOW)
    win = win_ref[0]
    s = _dot(qmb, win[0:LANES, :].astype(BF16)) * ATTN_SCALE - slope_n * dist_w.astype(F32)
    s = jnp.where(mask_w, s, -jnp.inf)
    s_new = new_key_score(qm, wrow[:, 0:LANES])
    m = jnp.maximum(jnp.max(s, axis=1, keepdims=True), s_new)
    e = jnp.exp(s - m)
    e_new = jnp.exp(s_new - m)
    d = jnp.sum(e, axis=1, keepdims=True) + e_new
    o_win = (_dot_nt(e.astype(BF16), win[LANES:2 * LANES, :].astype(BF16)) + e_new * wrow[:, LANES:2 * LANES]) / d

    onsa_ref[0] = gt[:, 0:1] * o_cmp + gt[:, 1:2] * o_sel + gt[:, 2:3] * o_win

    lane = lax.broadcasted_iota(jnp.int32, win.shape, 1)
    nwin_ref[0] = jnp.where(lane == nwin - 1, wcol_ref[0], pltpu.roll(win, nwin - 1, 1))

    drow = drow_ref[0]

    def diff_rows(pg, j):
        return pg[pl.ds(j, PAGE_SIZE, stride=DIFF_ROWS), :]

    qds = [qd_ref[0, h] for h in range(DIFF_HEADS)]
    qdb = [qh.astype(BF16) for qh in qds]
    s_pages = []
    for pg in diff_pages:
        sp = _dot_nt(qdb[0], diff_rows(pg, 0).astype(BF16))
        for h in range(1, DIFF_HEADS):
            sp = sp + _dot_nt(qdb[h], diff_rows(pg, h).astype(BF16))
        s_pages.append(sp)
    s = jnp.concatenate(s_pages, axis=1) * ATTN_SCALE - sd_ref[:, 0:1] * dist_k
    prod = qds[0] * drow[0:1]
    for h in range(1, DIFF_HEADS):
        prod = prod + qds[h] * drow[h:h + 1]
    s_new = jnp.sum(prod, axis=1, keepdims=True) * ATTN_SCALE
    m = jnp.maximum(jnp.max(s, axis=1, keepdims=True), s_new)
    e = jnp.exp(s - m)
    e_new = jnp.exp(s_new - m)
    d = jnp.sum(e, axis=1, keepdims=True) + e_new
    lam = lam_ref[0:1, 0:1]
    p = e / d
    p_new = e_new / d
    a = (p - lam * pltpu.roll(p, DIFF_HEADS, 0)).astype(BF16)
    a_new = p_new - lam * pltpu.roll(p_new, DIFF_HEADS, 0)
    accs = []
    for h in range(DIFF_HEADS):
        acc = a_new * drow[DIFF_HEADS + h:DIFF_HEADS + h + 1]
        for j, pg in enumerate(diff_pages):
            acc = acc + _dot(a[:, j * PAGE_SIZE:(j + 1) * PAGE_SIZE], diff_rows(pg, DIFF_HEADS + h).astype(BF16))
        accs.append(acc)
    acc = jnp.concatenate(accs, axis=1)
    lane_head = lax.broadcasted_iota(jnp.int32, acc.shape, 1) // DIFF_VDIM
    row = lax.broadcasted_iota(jnp.int32, acc.shape, 0)
    own = lane_head == row
    ms = jnp.sum(jnp.where(own, jnp.square(acc), 0.0), axis=1, keepdims=True) / DIFF_VDIM
    odiff_ref[0] = acc * lax.rsqrt(ms + RMS_EPS) * subg_ref[...] * (1.0 - lam_init)


def _sample_attention(page_table, qm, qd4, gt8, nrow, wrow, wcol, drow, win_t, pool_nsa, pool_diff,
                      w1ab, pe2, w2bd, m_mat, e_mat, r_mat, sn, sd, lam_tile, subg_tile, past_len, lam_init):
    B = qm.shape[0]
    n_pages = page_table.shape[1]
    nwin = win_t.shape[2]

    def per_b(shape):
        nd = len(shape)
        return pl.BlockSpec((1,) + shape, lambda b, pt: (b,) + (0,) * nd)

    def fixed(shape):
        nd = len(shape)
        return pl.BlockSpec(shape, lambda b, pt: (0,) * nd)

    def page_spec(pool, j):
        return pl.BlockSpec((None,) + pool.shape[1:], lambda b, pt: (pt[b, j], 0, 0))

    in_specs = [per_b((NSA_HEADS, LANES)), per_b((DIFF_HEADS, 8, LANES)), per_b((NSA_HEADS, LANES)),
                per_b((1, NSA_W)), per_b((1, WIN_W)), per_b((WIN_W, 1)), per_b((DIFF_ROWS, LANES)),
                per_b((WIN_W, nwin)),
                fixed(w1ab.shape), fixed(pe2.shape), fixed(w2bd.shape), fixed(m_mat.shape), fixed(e_mat.shape),
                fixed(r_mat.shape), fixed(sn.shape), fixed(sd.shape), fixed(lam_tile.shape), fixed(subg_tile.shape)]
    in_specs += [page_spec(pool_nsa, j) for j in range(n_pages)]
    in_specs += [page_spec(pool_diff, j) for j in range(n_pages)]
    return pl.pallas_call(
        functools.partial(_sample_kernel, n_pages=n_pages, past_len=past_len, lam_init=lam_init),
        grid_spec=pltpu.PrefetchScalarGridSpec(
            num_scalar_prefetch=1,
            grid=(B,),
            in_specs=in_specs,
            out_specs=(per_b((NSA_HEADS, LANES)), per_b((8, DIFF_HEADS * DIFF_VDIM)), per_b((WIN_W, nwin))),
            scratch_shapes=[pltpu.VMEM((2, past_len, LANES), F32)]),
        out_shape=(jax.ShapeDtypeStruct((B, NSA_HEADS, LANES), F32),
                   jax.ShapeDtypeStruct((B, 8, DIFF_HEADS * DIFF_VDIM), F32),
                   jax.ShapeDtypeStruct((B, WIN_W, nwin), F32)),
        compiler_params=_cparams("arbitrary"),
        name="sample_attention",
    )(page_table, qm, qd4, gt8, nrow, wrow, wcol, drow, win_t, w1ab, pe2, w2bd, m_mat, e_mat, r_mat, sn, sd,
      lam_tile, subg_tile, *([pool_nsa] * n_pages), *([pool_diff] * n_pages))


def _cmp_to_sel(n_cmp, n_sel):
    c0 = np.arange(n_cmp)[:, None] * CMP_STRIDE
    s0 = np.arange(n_sel)[None, :] * SEL_BLOCK
    ov = np.clip(np.minimum(c0 + CMP_LEN, s0 + SEL_BLOCK) - np.maximum(c0, s0), 0, None)
    return (ov / CMP_LEN).astype(np.float32)


def _lambda_init(layer):
    return 0.8 - 0.6 * math.exp(-0.3 * layer)


def _prep_params(l, w_in, w_out, diff_subln_g, ln1_g, ln1_b, ln2_g, ln2_b, w_router, b_router,
                 w_gate_up, b_gate_up, w_down, b_down):
    gate_end = _GT0 + 3 * NSA_HEADS
    wt = w_in[l].T
    w_t = jnp.concatenate([wt[:gate_end], jnp.zeros((_GATE_PAD, D_MODEL), F32), wt[gate_end:]], axis=0).astype(BF16)
    wg, wl = _moe_split_gate_up(w_gate_up[l])
    bgu = b_gate_up[l]
    return {
        "w_t": w_t,
        "wo_a": w_out[l][:512].astype(BF16), "wo_b": w_out[l][512:].astype(BF16),
        "ln1_g": ln1_g[l][None], "ln1_b": ln1_b[l][None], "ln2_g": ln2_g[l][None], "ln2_b": ln2_b[l][None],
        "wr_t": w_router[l].T, "br": b_router[l][:, None],
        "wg": wg, "wl": wl, "bg": bgu[:, None, 0::2], "bl": bgu[:, None, 1::2],
        "wd": w_down[l].astype(BF16), "bd": b_down[l][:, None, :],
        "subln_g": diff_subln_g[l][None],
    }


def _cmp_params(cmp_pe, cmp_w1, cmp_w2):
    eye = jnp.eye(NSA_KV_HEADS, dtype=F32)
    halves = []
    for half in range(2):
        w = cmp_w1[:, half * CMP_STRIDE:(half + 1) * CMP_STRIDE]
        wb = jnp.einsum('klde,gh->klgdhe', w, eye)
        halves.append(wb.reshape(2, CMP_STRIDE * LANES, LANES))
    w1ab = jnp.concatenate(halves, axis=-1).astype(BF16)
    pe = cmp_pe.reshape(2, 2, CMP_STRIDE, 1, HEAD_DIM)
    pe2 = jnp.broadcast_to(pe, (2, 2, CMP_STRIDE, NSA_KV_HEADS, HEAD_DIM)).reshape(2, 2, 1, CMP_STRIDE * LANES)
    pe2 = jnp.broadcast_to(pe2, (2, 2, 8, CMP_STRIDE * LANES))
    w2bd = jnp.einsum('kde,gh->kgdhe', cmp_w2, eye).reshape(2, LANES, LANES).astype(BF16)
    return w1ab, pe2, w2bd


def _feature_major_to_rows(a_t, kinds):
    B, _, T = a_t.shape
    return a_t.reshape(B, kinds, NSA_KV_HEADS, HEAD_DIM, T).transpose(0, 4, 1, 2, 3)


def _prompt_group(x, p, cmp_pe, cmp_w1, cmp_w2, lam_tile, lam_init):
    B, T, _ = x.shape
    x2d = x.reshape(B * T, D_MODEL)
    q_nsa, q_tb, nsa_rows, nsa_t, nsa_tb, _, win_t, win_tb, gates, dq, diff8, diff_b = _in_proj(x2d, p["w_t"], B)
    r3 = lambda a: a.reshape(B, T, a.shape[-1])
    q_nsa, nsa_rows3, gates, dq = map(r3, (q_nsa, nsa_rows, gates, dq))
    kvc = _compress_prompt(nsa_rows3, *_cmp_params(cmp_pe, cmp_w1, cmp_w2))
    n16 = T // CMP_STRIDE
    n_sel = -(-T // SEL_BLOCK)
    m_np = np.zeros((n16, n_sel), np.float32)
    m_np[:n16 - 1] = _cmp_to_sel(n16 - 1, n_sel)
    o_cmp, sel = _cmp_attn_prompt(q_nsa, q_tb, kvc, jnp.asarray(m_np.T))
    e_np = np.zeros((NSA_KV_HEADS, LANES, T), np.float32)
    for g in range(NSA_KV_HEADS):
        e_np[g, g * n_sel + np.arange(T) // SEL_BLOCK, np.arange(T)] = 1.0
    o_nsa = _sel_win_prompt(q_nsa, nsa_tb, win_tb, sel, jnp.asarray(e_np, BF16), o_cmp, gates)
    o_diff = _diff_prompt(lam_tile, dq, diff_b.reshape(B, T, DIFF_W), p["subln_g"], lam_init)
    y = _post_mixer_sorted(o_nsa.reshape(B * T, 512), o_diff.reshape(B * T, 512), x2d, p)
    nwin = min(WINDOW, T)
    return (y.reshape(B, T, D_MODEL),
            _feature_major_to_rows(nsa_t, 4),
            diff8.reshape(B, T, 2, DIFF_HEADS, DIFF_VDIM),
            _feature_major_to_rows(win_t[:, :, T - nwin:], 2))


def _sample_group(x, pool_nsa, pool_diff, win_buf, page_table, p, cmp_pe, cmp_w1, cmp_w2, lam_tile, lam_init):
    B, T, _ = x.shape
    past_len = page_table.shape[1] * PAGE_SIZE
    x2d = x.reshape(B, D_MODEL)
    q_nsa, _, nsa_rows, nsa_t, _, win_rows, _, _, gates, dq, diff8, _ = _in_proj(x2d, p["w_t"], 1)
    qh = q_nsa.reshape(B, NSA_HEADS, HEAD_DIM)
    grp = (np.arange(NSA_HEADS) // NSA_GROUP)[None, :, None, None] == np.arange(NSA_KV_HEADS)[None, None, :, None]
    qm = (qh[:, :, None, :] * jnp.asarray(grp, F32)).reshape(B, NSA_HEADS, LANES)
    dq4 = dq.reshape(B, DIFF_HEADS, 1, 2 * HEAD_DIM)
    rr = np.arange(8)
    rowmask = (rr[None, :, None] % DIFF_HEADS == np.arange(DIFF_HEADS)[:, None, None]) & (
        rr[None, :, None] // DIFF_HEADS == (np.arange(LANES) // HEAD_DIM)[None, None, :])
    qd4 = dq4 * jnp.asarray(rowmask, F32)[None]
    gt8 = jnp.pad(gates[:, :3 * NSA_HEADS].reshape(B, 3, NSA_HEADS).transpose(0, 2, 1), ((0, 0), (0, 0), (0, LANES - 3)))
    nwin = win_buf.shape[1]
    win_t = win_buf.transpose(0, 2, 3, 4, 1).reshape(B, WIN_W, nwin)
    n_pool = pool_nsa.shape[0]
    pool_nsa_t = pool_nsa.transpose(0, 2, 3, 4, 1).reshape(n_pool, NSA_W, PAGE_SIZE)
    pool_diff_r = pool_diff.reshape(n_pool, PAGE_SIZE * DIFF_ROWS, LANES)
    w1ab, pe2, w2bd = _cmp_params(cmp_pe, cmp_w1, cmp_w2)
    n16 = past_len // CMP_STRIDE
    n_sel = past_len // SEL_BLOCK + 1
    m_np = np.zeros((n16, 64), np.float32)
    m_np[:n16 - 1, :n_sel] = _cmp_to_sel(n16 - 1, n_sel)
    e_np = np.zeros((64, past_len), np.float32)
    e_np[np.arange(past_len) // SEL_BLOCK, np.arange(past_len)] = 1.0
    r_np = (np.arange(8)[:, None] // NSA_GROUP == np.arange(8)[None, :] // NSA_GROUP).astype(np.float32)
    sn = jnp.asarray(np.tile(np.asarray(NSA_SLOPES, np.float32)[:, None], (1, LANES)))
    sd = jnp.asarray(np.tile(np.asarray(DIFF_SLOPES, np.float32)[np.arange(8) % DIFF_HEADS, None], (1, LANES)))
    subg_tile = jnp.tile(p["subln_g"], (1, DIFF_HEADS))
    o_nsa8, o_diff8, new_win_t = _sample_attention(
        page_table, qm, qd4, gt8, nsa_rows[:, None, :], win_rows[:, None, :], win_rows[:, :, None],
        diff8.reshape(B, DIFF_ROWS, LANES), win_t, pool_nsa_t, pool_diff_r,
        w1ab, pe2, w2bd, jnp.asarray(m_np), jnp.asarray(e_np, BF16), jnp.asarray(r_np), sn, sd, lam_tile, subg_tile,
        past_len, lam_init)
    o8 = o_nsa8.reshape(B, NSA_KV_HEADS, NSA_GROUP, NSA_KV_HEADS, HEAD_DIM)
    o_nsa = jnp.stack([o8[:, g, :, g] for g in range(NSA_KV_HEADS)], axis=1).reshape(B, 512)
    d8 = o_diff8[:, :DIFF_HEADS].reshape(B, DIFF_HEADS, DIFF_HEADS, DIFF_VDIM)
    o_diff = jnp.stack([d8[:, h, h] for h in range(DIFF_HEADS)], axis=1).reshape(B, 512)
    y = _post_mixer(o_nsa, o_diff, x2d, p, s_tile=B, chunk=B)
    return (y.reshape(B, T, D_MODEL),
            _feature_major_to_rows(nsa_t, 4).reshape(B, T, 4, NSA_KV_HEADS, HEAD_DIM),
            diff8.reshape(B, T, 2, DIFF_HEADS, DIFF_VDIM),
            _feature_major_to_rows(new_win_t, 2))


def kernel(x_prompt, x_sample, cache_nsa_kv, cache_diff_kv, state_nsa_win, page_table, w_in, w_out, cmp_pe, cmp_w1,
           cmp_w2, diff_lambda, diff_subln_g, ln1_g, ln1_b, ln2_g, ln2_b, w_router, b_router, w_gate_up, b_gate_up,
           w_down, b_down):
    depth = w_in.shape[0]
    xp, xs = x_prompt, x_sample
    outs = [[] for _ in range(6)]
    for l in range(depth):
        lam0 = _lambda_init(l)
        lv = diff_lambda[l].astype(F32)
        lam = jnp.exp(jnp.sum(lv[0] * lv[1])) - jnp.exp(jnp.sum(lv[2] * lv[3])) + lam0
        lam_tile = jnp.full((8, LANES), lam, F32)
        p = _prep_params(l, w_in, w_out, diff_subln_g, ln1_g, ln1_b, ln2_g, ln2_b, w_router, b_router,
                         w_gate_up, b_gate_up, w_down, b_down)
        xp, r_nsa, r_diff, r_win = _prompt_group(xp, p, cmp_pe[l], cmp_w1[l], cmp_w2[l], lam_tile, lam0)
        xs, s_nsa, s_diff, s_win = _sample_group(xs, cache_nsa_kv[l], cache_diff_kv[l], state_nsa_win[l], page_table,
                                                 p, cmp_pe[l], cmp_w1[l], cmp_w2[l], lam_tile, lam0)
        for lst, v in zip(outs, (r_nsa, r_diff, r_win, s_nsa, s_diff, s_win)):
            lst.append(v)
    return (xp, xs) + tuple(jnp.stack(o) for o in outs)
```

```python
import functools
import math

import numpy as np
import jax
import jax.numpy as jnp
from jax import lax
from jax.experimental import pallas as pl
from jax.experimental.pallas import tpu as pltpu
from jax.experimental.pallas import tpu_sc as plsc

F32 = jnp.float32
BF16 = jnp.bfloat16
HIGHEST = lax.Precision.HIGHEST

D_MODEL = 1024
HEAD_DIM = 64
NSA_HEADS = 8
NSA_KV_HEADS = 2
NSA_GROUP = NSA_HEADS // NSA_KV_HEADS
CMP_LEN = 32
CMP_STRIDE = 16
SEL_BLOCK = 64
SEL_TOPN = 16
WINDOW = 512
DIFF_HEADS = 4
DIFF_VDIM = 2 * HEAD_DIM
N_EXPERTS = 32
TOP_K = 4
SWIGLU_ALPHA = 1.702
SWIGLU_LIMIT = 7.0
LN_EPS = 1e-5
RMS_EPS = 1e-5
ATTN_SCALE = HEAD_DIM ** -0.5
LOG2E = math.log2(math.e)
DEPTH = 1
DEEPNORM_ALPHA = (2 * DEPTH) ** 0.25
PAGE_SIZE = 128

NSA_SLOPES = tuple(2.0 ** (-8.0 * (i + 1) / NSA_HEADS) for i in range(NSA_HEADS))
DIFF_SLOPES = tuple(2.0 ** (-8.0 * (i + 1) / DIFF_HEADS) for i in range(DIFF_HEADS))

VMEM_LIMIT_BYTES = 56 * 1024 * 1024
LANES = 128
MXU_DIM = 256

MOE_CHUNK = MXU_DIM
MOE_SUPERTILE = 13 * LANES
SC_CORES = 2
SC_SUBCORES = 16
SC_WINDOW = 128

NSA_W = 4 * NSA_KV_HEADS * HEAD_DIM
WIN_W = 2 * NSA_KV_HEADS * HEAD_DIM
DIFF_W = 2 * DIFF_HEADS * DIFF_VDIM
DIFF_ROWS = DIFF_W // LANES
_GATE_PAD = LANES - 3 * NSA_HEADS
_Q0, _NSA0, _WIN0, _GT0, _DQ0, _DIFF0, _PROJ_ROWS = 0, 512, 1024, 1280, 1408, 1920, 2944


def _cparams(*sem):
    return pltpu.CompilerParams(dimension_semantics=sem, vmem_limit_bytes=VMEM_LIMIT_BYTES)


def _softmax_parts(s, mask):
    s = jnp.where(mask, s, -jnp.inf)
    m = jnp.max(s, axis=-1, keepdims=True)
    m = jnp.where(jnp.isfinite(m), m, 0.0)
    e = jnp.exp(s - m)
    d = jnp.sum(e, axis=-1, keepdims=True)
    return e, jnp.where(d > 0, d, 1.0)


def _softmax2_parts(s, mask):
    s = jnp.where(mask, s, -jnp.inf)
    m = jnp.max(s, axis=-1, keepdims=True)
    m = jnp.where(jnp.isfinite(m), m, 0.0)
    e = jnp.exp2(s - m)
    d = jnp.sum(e, axis=-1, keepdims=True)
    return e, jnp.where(d > 0, d, 1.0)


def _dot_nt(a, b):
    return lax.dot_general(a, b, (((1,), (1,)), ((), ())), preferred_element_type=F32)


def _dot(a, b):
    return jnp.dot(a, b, preferred_element_type=F32)


def _pad_head(qh, g):
    z = jnp.zeros_like(qh)
    return jnp.concatenate([qh, z] if g == 0 else [z, qh], axis=1)


def _select_blocks(imp, blk, qpos, n_blk_lanes):
    cur = jnp.right_shift(qpos, int(math.log2(SEL_BLOCK)))
    valid = blk * SEL_BLOCK <= qpos
    forced = (blk == 0) | (blk == cur) | (blk == cur - 1)
    val = jnp.where(forced, jnp.inf, jnp.where(valid, imp, -jnp.inf))
    rank = jnp.zeros(val.shape, F32)
    for i in range(n_blk_lanes):
        ci = val[:, i:i + 1]
        beats = (ci > val) | ((ci == val) & (blk > i))
        rank = rank + jnp.where(beats, 1.0, 0.0)
    return jnp.where(rank < SEL_TOPN, 1.0, 0.0)


def _in_proj_kernel(x_ref, w_ref, q_ref, qtb_ref, nsa_ref, nsat_ref, nsatb_ref, win_ref, wint_ref, wintb_ref, gt_ref,
                    dq_ref, diff_ref, diffb_ref):
    tm = x_ref.shape[0]
    xb = x_ref[...].astype(BF16)
    q_ref[...] = _dot_nt(xb, w_ref[_Q0:_NSA0, :])
    qtb_ref[0] = _dot_nt(w_ref[_Q0:_NSA0, :], xb).astype(BF16)
    nsa_ref[...] = _dot_nt(xb, w_ref[_NSA0:_WIN0, :])
    r = _dot_nt(w_ref[_NSA0:_WIN0, :], xb)
    nsat_ref[0] = r
    nsatb_ref[0] = r.astype(BF16)
    win_ref[...] = _dot_nt(xb, w_ref[_WIN0:_GT0, :])
    r = _dot_nt(w_ref[_WIN0:_GT0, :], xb)
    wint_ref[0] = r
    wintb_ref[0] = r.astype(BF16)
    gt_ref[...] = jax.nn.sigmoid(_dot_nt(xb, w_ref[_GT0:_DQ0, :]))
    dq_ref[...] = _dot_nt(xb, w_ref[_DQ0:_DIFF0, :])
    r = _dot_nt(xb, w_ref[_DIFF0:_PROJ_ROWS, :])
    diffb_ref[...] = r.astype(BF16)
    for j in range(DIFF_ROWS):
        diff_ref[pl.ds(j, tm, stride=DIFF_ROWS), :] = r[:, j * LANES:(j + 1) * LANES]


def _in_proj(x2d, w_t, batch):
    n = x2d.shape[0]
    t = n // batch
    tm = min(256, t)
    nt = t // tm
    row = lambda b, i: (b * nt + i, 0)
    tr = lambda b, i: (b, 0, i)
    tok = lambda w: pl.BlockSpec((tm, w), row)
    return pl.pallas_call(
        _in_proj_kernel,
        grid=(batch, nt),
        in_specs=[pl.BlockSpec((tm, D_MODEL), row),
                  pl.BlockSpec((_PROJ_ROWS, D_MODEL), lambda b, i: (0, 0))],
        out_specs=(tok(512), pl.BlockSpec((1, 512, tm), tr),
                   tok(NSA_W), pl.BlockSpec((1, NSA_W, tm), tr), pl.BlockSpec((1, NSA_W, tm), tr),
                   tok(WIN_W), pl.BlockSpec((1, WIN_W, tm), tr), pl.BlockSpec((1, WIN_W, tm), tr),
                   tok(LANES), tok(512), pl.BlockSpec((tm * DIFF_ROWS, LANES), row), tok(DIFF_W)),
        out_shape=(jax.ShapeDtypeStruct((n, 512), F32), jax.ShapeDtypeStruct((batch, 512, t), BF16),
                   jax.ShapeDtypeStruct((n, NSA_W), F32),
                   jax.ShapeDtypeStruct((batch, NSA_W, t), F32), jax.ShapeDtypeStruct((batch, NSA_W, t), BF16),
                   jax.ShapeDtypeStruct((n, WIN_W), F32),
                   jax.ShapeDtypeStruct((batch, WIN_W, t), F32), jax.ShapeDtypeStruct((batch, WIN_W, t), BF16),
                   jax.ShapeDtypeStruct((n, LANES), F32), jax.ShapeDtypeStruct((n, 512), F32),
                   jax.ShapeDtypeStruct((n * DIFF_ROWS, LANES), F32), jax.ShapeDtypeStruct((n, DIFF_W), BF16)),
        compiler_params=_cparams("parallel", "parallel"),
        name="in_proj",
    )(x2d, w_t)


def _compress_chunks(load_rows, w1, pe, w2, n16):
    cst = _dot(pe[0].astype(BF16), w1[:, 0:LANES]) + _dot(pe[1].astype(BF16), w1[:, LANES:])
    chunks = jnp.concatenate([load_rows(l) for l in range(CMP_STRIDE)], axis=1)
    ab = _dot(chunks.astype(BF16), w1)
    nxt = pltpu.roll(ab[:, LANES:], n16 - 1, 0)
    h = jax.nn.gelu(ab[:, 0:LANES] + nxt + cst[0:1])
    return _dot(h.astype(BF16), w2)


def _compress_kernel(rows_ref, w1_ref, pe_ref, w2_ref, o_ref):
    n16 = rows_ref.shape[1] // CMP_STRIDE
    o_ref[0, 0] = _compress_chunks(lambda l: rows_ref[0, pl.ds(l, n16, stride=CMP_STRIDE), :],
                                   w1_ref[0], pe_ref[0], w2_ref[0], n16)


def _compress_prompt(nsa_rows, w1ab, pe2, w2bd):
    B, T, _ = nsa_rows.shape
    n16 = T // CMP_STRIDE
    return pl.pallas_call(
        _compress_kernel,
        grid=(B, 2),
        in_specs=[pl.BlockSpec((1, T, LANES), lambda b, k: (b, 0, k)),
                  pl.BlockSpec((1,) + w1ab.shape[1:], lambda b, k: (k, 0, 0)),
                  pl.BlockSpec((1,) + pe2.shape[1:], lambda b, k: (k, 0, 0, 0)),
                  pl.BlockSpec((1,) + w2bd.shape[1:], lambda b, k: (k, 0, 0))],
        out_specs=pl.BlockSpec((1, 1, n16, LANES), lambda b, k: (b, k, 0, 0)),
        out_shape=jax.ShapeDtypeStruct((B, 2, n16, LANES), F32),
        compiler_params=_cparams("parallel", "parallel"),
        name="compress_prompt",
    )(nsa_rows, w1ab, pe2, w2bd)


def _select_blocks_t(imp, qpos):
    n_blk = imp.shape[0]
    blk = lax.broadcasted_iota(jnp.int32, imp.shape, 0)
    cur = jnp.right_shift(qpos, int(math.log2(SEL_BLOCK)))
    valid = blk * SEL_BLOCK <= qpos
    forced = (blk == 0) | (blk == cur) | (blk == cur - 1)
    val = jnp.where(forced, jnp.inf, jnp.where(valid, imp, -jnp.inf))
    rank = jnp.zeros(val.shape, F32)
    for i in range(n_blk):
        ci = val[i:i + 1, :]
        beats = (ci > val) | ((ci == val) & (blk > i))
        rank = rank + jnp.where(beats, 1.0, 0.0)
    return jnp.where(rank < SEL_TOPN, 1.0, 0.0)


def _cmp_attn_kernel(q_ref, qt_ref, kc_ref, vc_ref, mt_ref, o_ref, selt_ref, *, tq):
    i = pl.program_id(1)
    ncmp = kc_ref.shape[2]
    kcb = kc_ref[0, 0].astype(BF16)
    vcb = vc_ref[0, 0].astype(BF16)
    qpos_c = i * tq + lax.broadcasted_iota(jnp.int32, (tq, 1), 0)
    end_r = lax.broadcasted_iota(jnp.int32, (1, ncmp), 1) * CMP_STRIDE + (CMP_LEN - 1)
    mask = end_r <= qpos_c
    end_rf = end_r.astype(F32)
    q = q_ref[0] * (ATTN_SCALE * LOG2E)
    for h in range(NSA_HEADS):
        g = h // NSA_GROUP
        qp = _pad_head(q[:, h * HEAD_DIM:(h + 1) * HEAD_DIM], g).astype(BF16)
        e, d = _softmax2_parts(_dot_nt(qp, kcb) + (NSA_SLOPES[h] * LOG2E) * end_rf, mask)
        oh = _dot(e.astype(BF16), vcb)[:, g * HEAD_DIM:(g + 1) * HEAD_DIM] / d
        o_ref[0, :, h * HEAD_DIM:(h + 1) * HEAD_DIM] = oh
    qpos_r = i * tq + lax.broadcasted_iota(jnp.int32, (1, tq), 1)
    end_c = lax.broadcasted_iota(jnp.int32, (ncmp, 1), 0) * CMP_STRIDE + (CMP_LEN - 1)
    mask_t = end_c <= qpos_r
    end_cf = end_c.astype(F32)
    qt = qt_ref[0]
    zero = jnp.zeros((HEAD_DIM, tq), BF16)
    psum = [None, None]
    for h in range(NSA_HEADS):
        g = h // NSA_GROUP
        qh = qt[h * HEAD_DIM:(h + 1) * HEAD_DIM, :]
        qpt = jnp.concatenate([qh, zero] if g == 0 else [zero, qh], axis=0)
        s = _dot(kcb, qpt) * (ATTN_SCALE * LOG2E) + (NSA_SLOPES[h] * LOG2E) * end_cf
        s = jnp.where(mask_t, s, -jnp.inf)
        m = jnp.max(s, axis=0, keepdims=True)
        m = jnp.where(jnp.isfinite(m), m, 0.0)
        e = jnp.exp2(s - m)
        d = jnp.sum(e, axis=0, keepdims=True)
        p = e / jnp.where(d > 0, d, 1.0)
        psum[g] = p if psum[g] is None else psum[g] + p
    sels = []
    for g in range(NSA_KV_HEADS):
        imp = jnp.dot(mt_ref[...], psum[g], precision=HIGHEST, preferred_element_type=F32)
        sels.append(_select_blocks_t(imp, qpos_r))
    sels.append(jnp.zeros((LANES - NSA_KV_HEADS * mt_ref.shape[0], tq), F32))
    selt_ref[0] = jnp.concatenate(sels, axis=0)


def _cmp_attn_prompt(q_nsa, q_tb, kvc, mt_mat, tq=256):
    B, T, _ = q_nsa.shape
    ncmp = kvc.shape[2]
    n_sel = mt_mat.shape[0]
    return pl.pallas_call(
        functools.partial(_cmp_attn_kernel, tq=tq),
        grid=(B, T // tq),
        in_specs=[pl.BlockSpec((1, tq, 512), lambda b, i: (b, i, 0)),
                  pl.BlockSpec((1, 512, tq), lambda b, i: (b, 0, i)),
                  pl.BlockSpec((1, 1, ncmp, LANES), lambda b, i: (b, 0, 0, 0)),
                  pl.BlockSpec((1, 1, ncmp, LANES), lambda b, i: (b, 1, 0, 0)),
                  pl.BlockSpec((n_sel, ncmp), lambda b, i: (0, 0))],
        out_specs=(pl.BlockSpec((1, tq, 512), lambda b, i: (b, i, 0)),
                   pl.BlockSpec((1, LANES, tq), lambda b, i: (b, 0, i))),
        out_shape=(jax.ShapeDtypeStruct((B, T, 512), F32), jax.ShapeDtypeStruct((B, LANES, T), F32)),
        compiler_params=_cparams("parallel", "parallel"),
        name="cmp_attn_prompt",
    )(q_nsa, q_tb, kvc, kvc, mt_mat)


def _sel_win_kernel(q_ref, ks_ref, vs_ref, kw_ref, vw_ref, sel_ref, e_ref, ocmp_ref, gt_ref, o_ref, *, tq, span,
                    kv_step):
    i = pl.program_id(1)
    T = ks_ref.shape[2]
    qpos = i * tq + lax.broadcasted_iota(jnp.int32, (tq, 1), 0)
    q = q_ref[0] * (ATTN_SCALE * LOG2E)
    gt = gt_ref[0]
    ocmp = ocmp_ref[0]
    selb = sel_ref[0].astype(BF16)
    heads = [(h, h // NSA_GROUP, h * HEAD_DIM, (h + 1) * HEAD_DIM) for h in range(NSA_HEADS)]
    qps = [_pad_head(q[:, lo:hi], g).astype(BF16) for _, g, lo, hi in heads]
    start = pl.multiple_of(jnp.maximum(i * tq - WINDOW, 0), LANES)
    wpos = start + lax.broadcasted_iota(jnp.int32, (1, span), 1)
    dist_w = qpos - wpos
    mask_w = (dist_w >= 0) & (dist_w <= WINDOW)
    wposf = wpos.astype(F32)
    kwb = kw_ref[0, :, pl.ds(start, span)]
    vwb = vw_ref[0, :, pl.ds(start, span)]
    partial = []
    for h, g, lo, hi in heads:
        s = _dot(qps[h], kwb) + (NSA_SLOPES[h] * LOG2E) * wposf
        e, d = _softmax2_parts(s, mask_w)
        o_win = _dot_nt(e.astype(BF16), vwb)[:, g * HEAD_DIM:(g + 1) * HEAD_DIM] / d
        partial.append(gt[:, h:h + 1] * ocmp[:, lo:hi] + gt[:, 2 * NSA_HEADS + h:2 * NSA_HEADS + h + 1] * o_win)
    tiles_per_step = kv_step // tq
    for br in range(T // kv_step):
        @pl.when((i >= br * tiles_per_step) & (i < (br + 1) * tiles_per_step))
        def _(br=br):
            kv = (br + 1) * kv_step
            kpos = lax.broadcasted_iota(jnp.int32, (1, kv), 1)
            causal = kpos <= qpos
            kposf = kpos.astype(F32)
            ksb = ks_ref[0, :, 0:kv]
            vsb = vs_ref[0, :, 0:kv]
            for g in range(NSA_KV_HEADS):
                expand = lax.dot_general(selb, e_ref[g, :, 0:kv], (((0,), (0,)), ((), ())), preferred_element_type=F32)
                mask_s = (expand > 0.5) & causal
                for h, _, lo, hi in heads[g * NSA_GROUP:(g + 1) * NSA_GROUP]:
                    s = _dot(qps[h], ksb) + (NSA_SLOPES[h] * LOG2E) * kposf
                    e, d = _softmax2_parts(s, mask_s)
                    o_sel = _dot_nt(e.astype(BF16), vsb)[:, g * HEAD_DIM:(g + 1) * HEAD_DIM] / d
                    o_ref[0, :, lo:hi] = partial[h] + gt[:, NSA_HEADS + h:NSA_HEADS + h + 1] * o_sel


def _sel_win_prompt(q_nsa, nsa_t, win_t, sel_t, e_mat, o_cmp, gates, tq=256):
    B, T, _ = q_nsa.shape
    span = WINDOW + tq
    kv = lambda j: pl.BlockSpec((1, LANES, T), lambda b, i: (b, j, 0))
    return pl.pallas_call(
        functools.partial(_sel_win_kernel, tq=tq, span=span, kv_step=min(2 * tq, T)),
        grid=(B, T // tq),
        in_specs=[pl.BlockSpec((1, tq, 512), lambda b, i: (b, i, 0)),
                  kv(2), kv(3), kv(0), kv(1),
                  pl.BlockSpec((1, LANES, tq), lambda b, i: (b, 0, i)),
                  pl.BlockSpec((NSA_KV_HEADS, LANES, T), lambda b, i: (0, 0, 0)),
                  pl.BlockSpec((1, tq, 512), lambda b, i: (b, i, 0)),
                  pl.BlockSpec((1, tq, LANES), lambda b, i: (b, i, 0))],
        out_specs=pl.BlockSpec((1, tq, 512), lambda b, i: (b, i, 0)),
        out_shape=jax.ShapeDtypeStruct((B, T, 512), F32),
        compiler_params=_cparams("parallel", "parallel"),
        name="sel_win_prompt",
    )(q_nsa, nsa_t, nsa_t, win_t, win_t, sel_t, e_mat, o_cmp, gates)


def _diff_kernel(lam_ref, sl_ref, q_ref, k_ref, v_ref, g_ref, o_ref, *, tq, lam_init):
    i = pl.program_id(2)
    T = k_ref.shape[1]
    lam = lam_ref[0:1, 0:1]
    q = q_ref[0] * (ATTN_SCALE * LOG2E)
    qps = [_pad_head(q[:, c * HEAD_DIM:(c + 1) * HEAD_DIM], c).astype(BF16) for c in range(2)]
    tri = lax.broadcasted_iota(jnp.int32, (tq, tq), 0) >= lax.broadcasted_iota(jnp.int32, (tq, tq), 1)
    for br in range(T // tq):
        @pl.when(i == br)
        def _(br=br):
            off = br * tq
            kv = off + tq
            kb = k_ref[0, 0:kv, :]
            vb = v_ref[0, 0:kv, :]
            col = (sl_ref[0, :, 0:1] * LOG2E) * lax.broadcasted_iota(jnp.int32, (1, kv), 1).astype(F32)
            outs = []
            for c in range(2):
                s = _dot_nt(qps[c], kb) + col
                s_d = jnp.where(tri, s[:, off:], -jnp.inf)
                m = jnp.max(s_d, axis=-1, keepdims=True)
                if off:
                    m = jnp.maximum(m, jnp.max(s[:, :off], axis=-1, keepdims=True))
                e_d = jnp.exp2(s_d - m)
                d = jnp.sum(e_d, axis=-1, keepdims=True)
                o = _dot(e_d.astype(BF16), vb[off:, :])
                if off:
                    e_o = jnp.exp2(s[:, :off] - m)
                    d = d + jnp.sum(e_o, axis=-1, keepdims=True)
                    o = o + _dot(e_o.astype(BF16), vb[:off, :])
                outs.append(o / d)
            od = outs[0] - lam * outs[1]
            od = od * lax.rsqrt(jnp.mean(jnp.square(od), axis=-1, keepdims=True) + RMS_EPS)
            o_ref[0] = od * g_ref[...] * (1.0 - lam_init)


def _diff_prompt(lam_tile, dq, diff_b, subln_g, lam_init, tq=256):
    B, T, _ = dq.shape
    slopes = jnp.asarray(np.tile(np.asarray(DIFF_SLOPES, np.float32)[:, None, None], (1, 1, LANES)))
    return pl.pallas_call(
        functools.partial(_diff_kernel, tq=tq, lam_init=lam_init),
        grid=(B, DIFF_HEADS, T // tq),
        in_specs=[pl.BlockSpec((8, LANES), lambda b, h, i: (0, 0)),
                  pl.BlockSpec((1, 1, LANES), lambda b, h, i: (h, 0, 0)),
                  pl.BlockSpec((1, tq, LANES), lambda b, h, i: (b, i, h)),
                  pl.BlockSpec((1, T, LANES), lambda b, h, i: (b, 0, h)),
                  pl.BlockSpec((1, T, LANES), lambda b, h, i: (b, 0, DIFF_HEADS + h)),
                  pl.BlockSpec((1, DIFF_VDIM), lambda b, h, i: (0, 0))],
        out_specs=pl.BlockSpec((1, tq, LANES), lambda b, h, i: (b, i, h)),
        out_shape=jax.ShapeDtypeStruct((B, T, DIFF_HEADS * DIFF_VDIM), F32),
        compiler_params=_cparams("parallel", "parallel", "parallel"),
        name="diff_prompt",
    )(lam_tile, slopes, dq, diff_b, diff_b, subln_g)


def _layer_norm(z, g, b):
    mu = jnp.mean(z, axis=-1, keepdims=True)
    zc = z - mu
    var = jnp.mean(jnp.square(zc), axis=-1, keepdims=True)
    return zc * lax.rsqrt(var + LN_EPS) * g + b


def _pack_rows(x, ref):
    w = x.shape[1] // 2
    hi = lax.bitcast_convert_type(x[:, :w].astype(BF16).astype(F32), jnp.uint32)
    lo = lax.bitcast_convert_type(x[:, w:].astype(BF16).astype(F32), jnp.uint32)
    packed = hi | (lo >> 16)
    ref[0] = packed[:, :w // 2]
    ref[1] = packed[:, w // 2:]


def _unpack_rows(u0, u1):
    hi = [lax.bitcast_convert_type(u & jnp.uint32(0xFFFF0000), F32) for u in (u0, u1)]
    lo = [lax.bitcast_convert_type(u << 16, F32) for u in (u0, u1)]
    return jnp.concatenate(hi + lo, axis=1)


def _out_proj_kernel(on_ref, od_ref, x_ref, wa_ref, wb_ref, g_ref, b_ref, wr_ref, br_ref,
                     x1_ref, x1b_ref, x1p_ref, sel_ref, gate_ref, tope_ref, gate8_ref, cnt_ref):
    y = _dot(on_ref[...].astype(BF16), wa_ref[...]) + _dot(od_ref[...].astype(BF16), wb_ref[...])
    x1 = _layer_norm(DEEPNORM_ALPHA * x_ref[...] + y, g_ref[...], b_ref[...])
    x1_ref[...] = x1
    x1b_ref[...] = x1.astype(BF16)
    _pack_rows(x1, x1p_ref)
    logits = lax.dot_general(wr_ref[...], x1, (((1,), (1,)), ((), ())), precision=HIGHEST,
                             preferred_element_type=F32) + br_ref[...]
    eidx = lax.broadcasted_iota(jnp.int32, logits.shape, 0).astype(F32)
    sel = jnp.zeros(logits.shape, F32)
    picked, vals = [], []
    for k in range(TOP_K):
        cur = jnp.where(sel > 0.5, -jnp.inf, logits)
        m = jnp.max(cur, axis=0, keepdims=True)
        first = jnp.min(jnp.where(cur == m, eidx, float(N_EXPERTS)), axis=0, keepdims=True)
        sel = jnp.where(eidx == first, 1.0, sel)
        picked.append(first)
        vals.append(m)
    ex = jnp.where(sel > 0.5, jnp.exp(logits - vals[0]), 0.0)
    denom = jnp.sum(ex, axis=0, keepdims=True)
    sel_ref[...] = sel
    gate_ref[...] = ex / denom
    pad = [jnp.zeros_like(denom)] * (8 - TOP_K)
    tope_ref[...] = jnp.concatenate(picked + pad, axis=0)
    gate8_ref[...] = jnp.concatenate([jnp.exp(v - vals[0]) / denom for v in vals] + pad, axis=0)
    cnt_ref[0] = jnp.sum(sel, axis=1, keepdims=True)


def _out_proj_ln_router(o_nsa, o_diff, x2d, wa, wb, ln_g, ln_b, wr_t, br):
    n = x2d.shape[0]
    tm = min(256, n)
    row = lambda i: (i, 0)
    fix = lambda i: (0, 0)
    col = lambda i: (0, i)
    return pl.pallas_call(
        _out_proj_kernel,
        grid=(n // tm,),
        in_specs=[pl.BlockSpec((tm, 512), row), pl.BlockSpec((tm, 512), row), pl.BlockSpec((tm, D_MODEL), row),
                  pl.BlockSpec((512, D_MODEL), fix), pl.BlockSpec((512, D_MODEL), fix),
                  pl.BlockSpec((1, D_MODEL), fix), pl.BlockSpec((1, D_MODEL), fix),
                  pl.BlockSpec((N_EXPERTS, D_MODEL), fix), pl.BlockSpec((N_EXPERTS, 1), fix)],
        out_specs=(pl.BlockSpec((tm, D_MODEL), row), pl.BlockSpec((tm, D_MODEL), row),
                   pl.BlockSpec((2, tm, D_MODEL // 4), lambda i: (0, i, 0)),
                   pl.BlockSpec((N_EXPERTS, tm), col), pl.BlockSpec((N_EXPERTS, tm), col),
                   pl.BlockSpec((8, tm), col), pl.BlockSpec((8, tm), col),
                   pl.BlockSpec((1, N_EXPERTS, 1), lambda i: (i, 0, 0))),
        out_shape=(jax.ShapeDtypeStruct((n, D_MODEL), F32), jax.ShapeDtypeStruct((n, D_MODEL), BF16),
                   jax.ShapeDtypeStruct((2, n, D_MODEL // 4), jnp.uint32),
                   jax.ShapeDtypeStruct((N_EXPERTS, n), F32), jax.ShapeDtypeStruct((N_EXPERTS, n), F32),
                   jax.ShapeDtypeStruct((8, n), F32), jax.ShapeDtypeStruct((8, n), F32),
                   jax.ShapeDtypeStruct((n // tm, N_EXPERTS, 1), F32)),
        compiler_params=_cparams("parallel"),
        name="out_proj_ln_router",
    )(o_nsa, o_diff, x2d, wa, wb, ln_g, ln_b, wr_t, br)


def _load_expert_weights(wgu_ref, wd_ref, p_ref, wg_s, wl_s, wd_s):
    half = MXU_DIM // 2
    for m in range(wgu_ref.shape[2] // MXU_DIM):
        y = _dot(wgu_ref[0, :, m * MXU_DIM:(m + 1) * MXU_DIM].astype(BF16), p_ref[...])
        wg_s[:, m * half:(m + 1) * half] = y[:, :half].astype(BF16)
        wl_s[:, m * half:(m + 1) * half] = y[:, half:].astype(BF16)
    wd_s[...] = wd_ref[0].astype(BF16)


def _deinterleave_matrix():
    half = MXU_DIM // 2
    p_np = np.zeros((MXU_DIM, MXU_DIM), np.float32)
    p_np[2 * np.arange(half), np.arange(half)] = 1.0
    p_np[2 * np.arange(half) + 1, half + np.arange(half)] = 1.0
    return jnp.asarray(p_np, BF16)


_EXPERT_WEIGHT_SCRATCH = [pltpu.VMEM((D_MODEL, D_MODEL), BF16)] * 3


def _moe_pos_kernel(sel_ref, u_ref, pos_ref, cnt_ref, *, n_valid):
    s_tile = sel_ref.shape[1]
    tok = pl.program_id(0) * s_tile + lax.broadcasted_iota(jnp.int32, (1, s_tile), 1)
    sel = jnp.where(tok < n_valid, sel_ref[...], 0.0)
    rank = _dot(sel.astype(BF16), u_ref[...])
    pos_ref[...] = jnp.where(sel > 0.5, rank, -1.0)
    cnt_ref[0] = jnp.sum(sel, axis=1, keepdims=True)


def _moe_positions(sel_t, s_tile):
    n = sel_t.shape[1]
    ns = pl.cdiv(n, s_tile)
    upper = jnp.asarray(np.triu(np.ones((s_tile, s_tile), np.float32), 1), BF16)
    return pl.pallas_call(
        functools.partial(_moe_pos_kernel, n_valid=n),
        grid=(ns,),
        in_specs=[pl.BlockSpec((N_EXPERTS, s_tile), lambda s: (0, s)),
                  pl.BlockSpec((s_tile, s_tile), lambda s: (0, 0))],
        out_specs=(pl.BlockSpec((N_EXPERTS, s_tile), lambda s: (0, s)),
                   pl.BlockSpec((1, N_EXPERTS, 1), lambda s: (s, 0, 0))),
        out_shape=(jax.ShapeDtypeStruct((N_EXPERTS, ns * s_tile), F32),
                   jax.ShapeDtypeStruct((ns, N_EXPERTS, 1), F32)),
        compiler_params=_cparams("parallel"),
        name="moe_positions",
    )(sel_t, upper)


def _moe_kernel(nch_ref, x_ref, pos_ref, gate_ref, wgu_ref, wd_ref, p_ref, bg_ref, bl_ref, bd_ref, y_ref,
                wg_s, wl_s, wd_s, *, chunk, n_valid):
    s = pl.program_id(0)
    e = pl.program_id(1)
    s_tile = x_ref.shape[0]
    n_chunks = nch_ref[s * N_EXPERTS + e]

    @pl.when(e == 0)
    def _():
        y_ref[...] = jnp.zeros_like(y_ref)

    @pl.when(n_chunks > 0)
    def _():
        _load_expert_weights(wgu_ref, wd_ref, p_ref, wg_s, wl_s, wd_s)

    pos = pos_ref[0, 0].astype(jnp.int32)
    gate = gate_ref[0, 0]
    slot0 = lax.broadcasted_iota(jnp.int32, (chunk, 1), 0)

    def body(j, carry):
        hit = pos == slot0 + j * chunk
        onehot = jnp.where(hit, 1.0, 0.0).astype(BF16)
        x = x_ref[...]
        if n_valid % s_tile:
            row = s * s_tile + lax.broadcasted_iota(jnp.int32, (s_tile, 1), 0)
            x = jnp.where(row < n_valid, x, jnp.zeros_like(x))
        xs = _dot(onehot, x).astype(BF16)
        hg = jnp.minimum(_dot(xs, wg_s[...]) + bg_ref[0], SWIGLU_LIMIT)
        hl = jnp.clip(_dot(xs, wl_s[...]) + bl_ref[0], -SWIGLU_LIMIT, SWIGLU_LIMIT)
        a = (hl + 1.0) * hg * jax.nn.sigmoid(SWIGLU_ALPHA * hg)
        out = _dot(a.astype(BF16), wd_s[...]) + bd_ref[0]
        gslot = jnp.sum(jnp.where(hit, gate, 0.0), axis=1, keepdims=True)
        outg = (out * gslot).astype(BF16)
        y_ref[...] += lax.dot_general(onehot, outg, (((0,), (0,)), ((), ())), preferred_element_type=F32)
        return carry

    lax.fori_loop(0, n_chunks, body, 0)


def _moe(x1b, pos_t, gate_t, counts, wgu, wd, p_mat, bg, bl, bd, s_tile, chunk):
    n = x1b.shape[0]
    ns = pl.cdiv(n, s_tile)
    nch = ((counts.reshape(ns * N_EXPERTS) + (chunk - 1)) // chunk).astype(jnp.int32)
    pos4 = pos_t.reshape(N_EXPERTS, ns, 1, s_tile)
    gate4 = jnp.pad(gate_t, ((0, 0), (0, ns * s_tile - n))).reshape(N_EXPERTS, ns, 1, s_tile)
    wspec = lambda a: pl.BlockSpec((1,) + a.shape[1:], lambda s, e, nch: (e, 0, 0))
    bspec = pl.BlockSpec((1, 1, D_MODEL), lambda s, e, nch: (e, 0, 0))
    rspec = pl.BlockSpec((1, 1, 1, s_tile), lambda s, e, nch: (e, s, 0, 0))
    return pl.pallas_call(
        functools.partial(_moe_kernel, chunk=chunk, n_valid=n),
        grid_spec=pltpu.PrefetchScalarGridSpec(
            num_scalar_prefetch=1,
            grid=(ns, N_EXPERTS),
            in_specs=[pl.BlockSpec((s_tile, D_MODEL), lambda s, e, nch: (s, 0)),
                      rspec, rspec, wspec(wgu), wspec(wd), pl.BlockSpec(p_mat.shape, lambda s, e, nch: (0, 0)),
                      bspec, bspec, bspec],
            out_specs=pl.BlockSpec((s_tile, D_MODEL), lambda s, e, nch: (s, 0)),
            scratch_shapes=_EXPERT_WEIGHT_SCRATCH),
        out_shape=jax.ShapeDtypeStruct((n, D_MODEL), F32),
        compiler_params=_cparams("parallel", "arbitrary"),
        name="moe_experts",
    )(nch, x1b, pos4, gate4, wgu, wd, p_mat, bg, bl, bd)


def _ln2_kernel(x_ref, f_ref, g_ref, b_ref, o_ref):
    o_ref[...] = _layer_norm(DEEPNORM_ALPHA * x_ref[...] + f_ref[...], g_ref[...], b_ref[...])


def _residual_ln(x1, f, g, b):
    n = x1.shape[0]
    tm = min(512, n)
    row = lambda i: (i, 0)
    fix = lambda i: (0, 0)
    return pl.pallas_call(
        _ln2_kernel,
        grid=(n // tm,),
        in_specs=[pl.BlockSpec((tm, D_MODEL), row), pl.BlockSpec((tm, D_MODEL), row),
                  pl.BlockSpec((1, D_MODEL), fix), pl.BlockSpec((1, D_MODEL), fix)],
        out_specs=pl.BlockSpec((tm, D_MODEL), row),
        out_shape=jax.ShapeDtypeStruct((n, D_MODEL), F32),
        compiler_params=_cparams("parallel"),
        name="residual_ln2",
    )(x1, f, g, b)


def _post_mixer(o_nsa, o_diff, x2d, p, s_tile, chunk):
    x1, x1b, _, sel_t, gate_t, _, _, _ = _out_proj_ln_router(o_nsa, o_diff, x2d, p["wo_a"], p["wo_b"], p["ln1_g"],
                                                             p["ln1_b"], p["wr_t"], p["br"])
    pos_t, counts = _moe_positions(sel_t, s_tile)
    f = _moe(x1b, pos_t, gate_t, counts, p["wgu"], p["wd"], p["p_mat"], p["bg"], p["bl"], p["bd"], s_tile, chunk)
    return _residual_ln(x1, f, p["ln2_g"], p["ln2_b"])


def _moe_slots_kernel(sel_ref, tope_ref, pstart_ref, u_ref, slot_ref, carry_ref):
    @pl.when(pl.program_id(0) == 0)
    def _():
        carry_ref[...] = jnp.zeros_like(carry_ref)

    sel = sel_ref[...]
    slot_all = pstart_ref[...] + carry_ref[...] + _dot(sel.astype(BF16), u_ref[...])
    eidx = lax.broadcasted_iota(jnp.int32, sel.shape, 0).astype(F32)
    tope = tope_ref[...]
    rows = [jnp.sum(jnp.where(eidx == tope[k:k + 1], slot_all, 0.0), axis=0, keepdims=True) for k in range(TOP_K)]
    rows += [jnp.zeros_like(rows[0])] * (8 - TOP_K)
    slot_ref[...] = jnp.concatenate(rows, axis=0).astype(jnp.int32)
    carry_ref[...] += jnp.sum(sel, axis=1, keepdims=True)


def _moe_slots(sel_t, tope, pstart, tm=2048):
    n = sel_t.shape[1]
    upper = jnp.asarray(np.triu(np.ones((tm, tm), np.float32), 1), BF16)
    return pl.pallas_call(
        _moe_slots_kernel,
        grid=(n // tm,),
        in_specs=[pl.BlockSpec((N_EXPERTS, tm), lambda i: (0, i)), pl.BlockSpec((8, tm), lambda i: (0, i)),
                  pl.BlockSpec((N_EXPERTS, 1), lambda i: (0, 0)), pl.BlockSpec((tm, tm), lambda i: (0, 0))],
        out_specs=pl.BlockSpec((8, tm), lambda i: (0, i)),
        out_shape=jax.ShapeDtypeStruct((8, n), jnp.int32),
        scratch_shapes=[pltpu.VMEM((N_EXPERTS, 1), F32)],
        compiler_params=_cparams("arbitrary"),
        name="moe_slots",
    )(sel_t, tope, pstart, upper)


def _sc_mesh():
    return plsc.VectorSubcoreMesh(core_axis_name="c", subcore_axis_name="s", num_cores=SC_CORES,
                                  num_subcores=SC_SUBCORES)


def _sc_scatter_rows(src, idx, n_out):
    n_rows, width = src.shape
    n_idx = idx.shape[0]
    per_core = n_rows // SC_WINDOW // SC_CORES

    @functools.partial(pl.kernel, out_type=jax.ShapeDtypeStruct((n_out, width), src.dtype), mesh=_sc_mesh(),
                       scratch_types=[], name="moe_dispatch_rows")
    def scatter(src_hbm, idx_hbm, out_hbm):
        def body(src_vmem, idx_vmem):
            for k in range(n_idx):
                pltpu.sync_copy(src_vmem, out_hbm.at[idx_vmem.at[k]])

        pltpu.emit_pipeline(
            body,
            grid=(SC_CORES, per_core),
            in_specs=[pl.BlockSpec((SC_WINDOW, width), lambda c, j: (c * per_core + j, 0)),
                      pl.BlockSpec((n_idx, SC_WINDOW), lambda c, j: (0, c * per_core + j))],
            out_specs=[],
            core_axis_name=("c", "s"),
            dimension_semantics=(pltpu.PARALLEL, pltpu.PARALLEL),
        )(src_hbm, idx_hbm)

    return scatter(src, idx)


def _sc_gather_rows(table, idx):
    n_rows = idx.shape[1]
    width = table.shape[1]
    per_core = n_rows // SC_WINDOW // SC_CORES

    @functools.partial(pl.kernel, out_type=jax.ShapeDtypeStruct((n_rows, width), table.dtype), mesh=_sc_mesh(),
                       scratch_types=[], name="moe_return_rows")
    def gather(table_hbm, idx_hbm, out_hbm):
        def body(idx_vmem, out_vmem):
            pltpu.sync_copy(table_hbm.at[idx_vmem.at[0]], out_vmem)

        pltpu.emit_pipeline(
            body,
            grid=(SC_CORES, per_core),
            in_specs=[pl.BlockSpec((1, SC_WINDOW), lambda c, j: (0, c * per_core + j))],
            out_specs=[pl.BlockSpec((SC_WINDOW, width), lambda c, j: (c * per_core + j, 0))],
            core_axis_name=("c", "s"),
            dimension_semantics=(pltpu.PARALLEL, pltpu.PARALLEL),
        )(idx_hbm, out_hbm)

    return gather(table, idx)


def _moe_ffn_kernel(blk_e_ref, nvalid_ref, xs_ref, wgu_ref, wd_ref, p_ref, bg_ref, bl_ref, bd_ref, o_ref,
                    wg_s, wl_s, wd_s):
    i = pl.program_id(0)
    nv = nvalid_ref[i]
    new_expert = (i == 0) | (blk_e_ref[i] != blk_e_ref[jnp.maximum(i - 1, 0)])

    @pl.when(new_expert & (nv > 0))
    def _():
        _load_expert_weights(wgu_ref, wd_ref, p_ref, wg_s, wl_s, wd_s)

    @pl.when(nv > 0)
    def _():
        live = lax.broadcasted_iota(jnp.int32, (o_ref.shape[1], 1), 0) < nv
        xs = jnp.where(live, _unpack_rows(xs_ref[0], xs_ref[1]), 0.0).astype(BF16)
        hg = jnp.minimum(_dot(xs, wg_s[...]) + bg_ref[0], SWIGLU_LIMIT)
        hl = jnp.clip(_dot(xs, wl_s[...]) + bl_ref[0], -SWIGLU_LIMIT, SWIGLU_LIMIT)
        a = (hl + 1.0) * hg * jax.nn.sigmoid(SWIGLU_ALPHA * hg)
        _pack_rows(_dot(a.astype(BF16), wd_s[...]) + bd_ref[0], o_ref)

    @pl.when(nv == 0)
    def _():
        o_ref[...] = jnp.zeros_like(o_ref)


def _moe_ffn(blk_e, nvalid, xs, wgu, wd, p_mat, bg, bl, bd):
    _, n_slots, half = xs.shape
    wspec = lambda a: pl.BlockSpec((1,) + a.shape[1:], lambda i, be, nv: (be[i], 0, 0))
    bspec = pl.BlockSpec((1, 1, D_MODEL), lambda i, be, nv: (be[i], 0, 0))
    rows = pl.BlockSpec((2, MOE_CHUNK, half), lambda i, be, nv: (0, i, 0))
    return pl.pallas_call(
        _moe_ffn_kernel,
        grid_spec=pltpu.PrefetchScalarGridSpec(
            num_scalar_prefetch=2,
            grid=(n_slots // MOE_CHUNK,),
            in_specs=[rows, wspec(wgu), wspec(wd), pl.BlockSpec(p_mat.shape, lambda i, be, nv: (0, 0)),
                      bspec, bspec, bspec],
            out_specs=rows,
            scratch_shapes=_EXPERT_WEIGHT_SCRATCH),
        out_shape=jax.ShapeDtypeStruct((2, n_slots, half), jnp.uint32),
        compiler_params=_cparams("arbitrary"),
        name="moe_ffn_sorted",
    )(blk_e, nvalid, xs, wgu, wd, p_mat, bg, bl, bd)


def _ln2_combine_kernel(x_ref, r_ref, gate_ref, g_ref, b_ref, o_ref):
    gate = gate_ref[...]
    f = gate[:, 0:1] * _unpack_rows(r_ref[0, 0], r_ref[0, 1])
    for k in range(1, TOP_K):
        f = f + gate[:, k:k + 1] * _unpack_rows(r_ref[k, 0], r_ref[k, 1])
    o_ref[...] = _layer_norm(DEEPNORM_ALPHA * x_ref[...] + f, g_ref[...], b_ref[...])


def _ln2_combine(x1, returned, gate_tok, g, b, tm=256):
    n = x1.shape[0]
    row = lambda i: (i, 0)
    fix = lambda i: (0, 0)
    return pl.pallas_call(
        _ln2_combine_kernel,
        grid=(n // tm,),
        in_specs=[pl.BlockSpec((tm, D_MODEL), row),
                  pl.BlockSpec((TOP_K, 2, tm, D_MODEL // 4), lambda i: (0, 0, i, 0)),
                  pl.BlockSpec((tm, 8), row), pl.BlockSpec((1, D_MODEL), fix), pl.BlockSpec((1, D_MODEL), fix)],
        out_specs=pl.BlockSpec((tm, D_MODEL), row),
        out_shape=jax.ShapeDtypeStruct((n, D_MODEL), F32),
        compiler_params=_cparams("parallel"),
        name="combine_ln2",
    )(x1, returned, gate_tok, g, b)


def _post_mixer_sorted(o_nsa, o_diff, x2d, p):
    n = x2d.shape[0]
    x1, _, x1p, sel_t, _, tope, gate8, cnt = _out_proj_ln_router(o_nsa, o_diff, x2d, p["wo_a"], p["wo_b"], p["ln1_g"],
                                                                 p["ln1_b"], p["wr_t"], p["br"])
    counts = jnp.sum(cnt[:, :, 0], axis=0).astype(jnp.int32)
    padded = (counts + (MOE_CHUNK - 1)) // MOE_CHUNK * MOE_CHUNK
    pend = jnp.cumsum(padded)
    pstart = pend - padded
    n_blocks = n * TOP_K // MOE_CHUNK + N_EXPERTS
    n_slots = n_blocks * MOE_CHUNK
    blk0 = jnp.arange(n_blocks, dtype=jnp.int32)[:, None] * MOE_CHUNK
    blk_e = jnp.minimum(jnp.sum((blk0 >= pend[None, :]).astype(jnp.int32), axis=1), N_EXPERTS - 1)
    in_region = (blk0 >= pstart[None, :]) & (blk0 < pend[None, :])
    nvalid = jnp.sum(jnp.where(in_region, jnp.clip(pstart + counts - blk0, 0, MOE_CHUNK), 0), axis=1)
    slot = _moe_slots(sel_t, tope, pstart.astype(F32)[:, None])[:TOP_K]
    half = D_MODEL // 4
    slot2 = jnp.concatenate([slot, slot + n_slots], axis=1)
    xs = _sc_scatter_rows(x1p.reshape(2 * n, half), slot2, 2 * n_slots)
    outs = _moe_ffn(blk_e, nvalid, xs.reshape(2, n_slots, half), p["wgu"], p["wd"], p["p_mat"], p["bg"], p["bl"],
                    p["bd"])
    returned = _sc_gather_rows(outs.reshape(2 * n_slots, half), slot2.reshape(1, TOP_K * 2 * n))
    return _ln2_combine(x1, returned.reshape(TOP_K, 2, n, half), gate8.T, p["ln2_g"], p["ln2_b"])


def _sample_kernel(pt_ref, qm_ref, qd_ref, gt_ref, nrow_ref, wrow_ref, wcol_ref, drow_ref, win_ref,
                   w1_ref, pe_ref, w2_ref, m_ref, e_ref, r_ref, sn_ref, sd_ref, lam_ref, subg_ref, *rest,
                   n_pages, past_len, lam_init):
    nsa_pages = rest[:n_pages]
    diff_pages = rest[n_pages:2 * n_pages]
    onsa_ref, odiff_ref, nwin_ref, x_ref = rest[2 * n_pages:]
    n16 = past_len // CMP_STRIDE
    qpos = past_len

    for j, pg in enumerate(nsa_pages):
        for kind in range(2):
            x_ref[kind, j * PAGE_SIZE:(j + 1) * PAGE_SIZE, :] = pg[kind * LANES:(kind + 1) * LANES, :].T

    kv_cmp = [_compress_chunks(lambda l, kind=kind: x_ref[kind, pl.ds(l, n16, stride=CMP_STRIDE), :],
                               w1_ref[kind], pe_ref[kind], w2_ref[kind], n16) for kind in range(2)]

    qm = qm_ref[0]
    qmb = qm.astype(BF16)
    slope_n = sn_ref[:, 0:1]
    nrow = nrow_ref[0]
    wrow = wrow_ref[0]
    gt = gt_ref[0]

    def new_key_score(qrows, krow):
        return jnp.sum(qrows * krow, axis=1, keepdims=True) * ATTN_SCALE

    cmp_end = lax.broadcasted_iota(jnp.int32, (1, n16), 1) * CMP_STRIDE + (CMP_LEN - 1)
    dist_c = qpos - cmp_end
    s = _dot_nt(qmb, kv_cmp[0].astype(BF16)) * ATTN_SCALE - slope_n * dist_c.astype(F32)
    e, d = _softmax_parts(s, dist_c >= 0)
    p_cmp = e / d
    o_cmp = _dot(p_cmp.astype(BF16), kv_cmp[1].astype(BF16))
    pgrp = jnp.dot(r_ref[...], p_cmp, precision=HIGHEST, preferred_element_type=F32)
    imp = jnp.dot(pgrp, m_ref[...], precision=HIGHEST, preferred_element_type=F32)
    nbl = m_ref.shape[1]
    blk = lax.broadcasted_iota(jnp.int32, (NSA_HEADS, nbl), 1)
    sel = _select_blocks(imp, blk, jnp.full((NSA_HEADS, 1), qpos, jnp.int32), past_len // SEL_BLOCK + 1)
    mask_s = _dot(sel.astype(BF16), e_ref[...]) > 0.5

    kpos = lax.broadcasted_iota(jnp.int32, (1, past_len), 1)
    dist_k = (qpos - kpos).astype(F32)
    s = jnp.concatenate([_dot(qmb, pg[2 * LANES:3 * LANES, :].astype(BF16)) for pg in nsa_pages], axis=1)
    s = jnp.where(mask_s, s * ATTN_SCALE - slope_n * dist_k, -jnp.inf)
    s_new = new_key_score(qm, nrow[:, 2 * LANES:3 * LANES])
    m = jnp.maximum(jnp.max(s, axis=1, keepdims=True), s_new)
    e = jnp.exp(s - m)
    e_new = jnp.exp(s_new - m)
    d = jnp.sum(e, axis=1, keepdims=True) + e_new
    eb = e.astype(BF16)
    acc = e_new * nrow[:, 3 * LANES:4 * LANES]
    for j, pg in enumerate(nsa_pages):
        acc = acc + _dot_nt(eb[:, j * PAGE_SIZE:(j + 1) * PAGE_SIZE], pg[3 * LANES:4 * LANES, :].astype(BF16))
    o_sel = acc / d

    nwin = win_ref.shape[2]
    wpos = past_len - nwin + lax.broadcasted_iota(jnp.int32, (1, nwin), 1)
    dist_w = qpos - wpos
    mask_w = (dist_w >= 0) & (dist_w <= WINDOW)
    win = win_ref[0]
    s = _dot(qmb, win[0:LANES, :].astype(BF16)) * ATTN_SCALE - slope_n * dist_w.astype(F32)
    s = jnp.where(mask_w, s, -jnp.inf)
    s_new = new_key_score(qm, wrow[:, 0:LANES])
    m = jnp.maximum(jnp.max(s, axis=1, keepdims=True), s_new)
    e = jnp.exp(s - m)
    e_new = jnp.exp(s_new - m)
    d = jnp.sum(e, axis=1, keepdims=True) + e_new
    o_win = (_dot_nt(e.astype(BF16), win[LANES:2 * LANES, :].astype(BF16)) + e_new * wrow[:, LANES:2 * LANES]) / d

    onsa_ref[0] = gt[:, 0:1] * o_cmp + gt[:, 1:2] * o_sel + gt[:, 2:3] * o_win

    lane = lax.broadcasted_iota(jnp.int32, win.shape, 1)
    nwin_ref[0] = jnp.where(lane == nwin - 1, wcol_ref[0], pltpu.roll(win, nwin - 1, 1))

    drow = drow_ref[0]

    def diff_rows(pg, j):
        return pg[pl.ds(j, PAGE_SIZE, stride=DIFF_ROWS), :]

    qds = [qd_ref[0, h] for h in range(DIFF_HEADS)]
    qdb = [qh.astype(BF16) for qh in qds]
    s_pages = []
    for pg in diff_pages:
        sp = _dot_nt(qdb[0], diff_rows(pg, 0).astype(BF16))
        for h in range(1, DIFF_HEADS):
            sp = sp + _dot_nt(qdb[h], diff_rows(pg, h).astype(BF16))
        s_pages.append(sp)
    s = jnp.concatenate(s_pages, axis=1) * ATTN_SCALE - sd_ref[:, 0:1] * dist_k
    prod = qds[0] * drow[0:1]
    for h in range(1, DIFF_HEADS):
        prod = prod + qds[h] * drow[h:h + 1]
    s_new = jnp.sum(prod, axis=1, keepdims=True) * ATTN_SCALE
    m = jnp.maximum(jnp.max(s, axis=1, keepdims=True), s_new)
    e = jnp.exp(s - m)
    e_new = jnp.exp(s_new - m)
    d = jnp.sum(e, axis=1, keepdims=True) + e_new
    lam = lam_ref[0:1, 0:1]
    p = e / d
    p_new = e_new / d
    a = (p - lam * pltpu.roll(p, DIFF_HEADS, 0)).astype(BF16)
    a_new = p_new - lam * pltpu.roll(p_new, DIFF_HEADS, 0)
    accs = []
    for h in range(DIFF_HEADS):
        acc = a_new * drow[DIFF_HEADS + h:DIFF_HEADS + h + 1]
        for j, pg in enumerate(diff_pages):
            acc = acc + _dot(a[:, j * PAGE_SIZE:(j + 1) * PAGE_SIZE], diff_rows(pg, DIFF_HEADS + h).astype(BF16))
        accs.append(acc)
    acc = jnp.concatenate(accs, axis=1)
    lane_head = lax.broadcasted_iota(jnp.int32, acc.shape, 1) // DIFF_VDIM
    row = lax.broadcasted_iota(jnp.int32, acc.shape, 0)
    own = lane_head == row
    ms = jnp.sum(jnp.where(own, jnp.square(acc), 0.0), axis=1, keepdims=True) / DIFF_VDIM
    odiff_ref[0] = acc * lax.rsqrt(ms + RMS_EPS) * subg_ref[...] * (1.0 - lam_init)


def _sample_attention(page_table, qm, qd4, gt8, nrow, wrow, wcol, drow, win_t, pool_nsa, pool_diff,
                      w1ab, pe2, w2bd, m_mat, e_mat, r_mat, sn, sd, lam_tile, subg_tile, past_len, lam_init):
    B = qm.shape[0]
    n_pages = page_table.shape[1]
    nwin = win_t.shape[2]

    def per_b(shape):
        nd = len(shape)
        return pl.BlockSpec((1,) + shape, lambda b, pt: (b,) + (0,) * nd)

    def fixed(shape):
        nd = len(shape)
        return pl.BlockSpec(shape, lambda b, pt: (0,) * nd)

    def page_spec(pool, j):
        return pl.BlockSpec((None,) + pool.shape[1:], lambda b, pt: (pt[b, j], 0, 0))

    in_specs = [per_b((NSA_HEADS, LANES)), per_b((DIFF_HEADS, 8, LANES)), per_b((NSA_HEADS, LANES)),
                per_b((1, NSA_W)), per_b((1, WIN_W)), per_b((WIN_W, 1)), per_b((DIFF_ROWS, LANES)),
                per_b((WIN_W, nwin)),
                fixed(w1ab.shape), fixed(pe2.shape), fixed(w2bd.shape), fixed(m_mat.shape), fixed(e_mat.shape),
                fixed(r_mat.shape), fixed(sn.shape), fixed(sd.shape), fixed(lam_tile.shape), fixed(subg_tile.shape)]
    in_specs += [page_spec(pool_nsa, j) for j in range(n_pages)]
    in_specs += [page_spec(pool_diff, j) for j in range(n_pages)]
    return pl.pallas_call(
        functools.partial(_sample_kernel, n_pages=n_pages, past_len=past_len, lam_init=lam_init),
        grid_spec=pltpu.PrefetchScalarGridSpec(
            num_scalar_prefetch=1,
            grid=(B,),
            in_specs=in_specs,
            out_specs=(per_b((NSA_HEADS, LANES)), per_b((8, DIFF_HEADS * DIFF_VDIM)), per_b((WIN_W, nwin))),
            scratch_shapes=[pltpu.VMEM((2, past_len, LANES), F32)]),
        out_shape=(jax.ShapeDtypeStruct((B, NSA_HEADS, LANES), F32),
                   jax.ShapeDtypeStruct((B, 8, DIFF_HEADS * DIFF_VDIM), F32),
                   jax.ShapeDtypeStruct((B, WIN_W, nwin), F32)),
        compiler_params=_cparams("arbitrary"),
        name="sample_attention",
    )(page_table, qm, qd4, gt8, nrow, wrow, wcol, drow, win_t, w1ab, pe2, w2bd, m_mat, e_mat, r_mat, sn, sd,
      lam_tile, subg_tile, *([pool_nsa] * n_pages), *([pool_diff] * n_pages))


def _cmp_to_sel(n_cmp, n_sel):
    c0 = np.arange(n_cmp)[:, None] * CMP_STRIDE
    s0 = np.arange(n_sel)[None, :] * SEL_BLOCK
    ov = np.clip(np.minimum(c0 + CMP_LEN, s0 + SEL_BLOCK) - np.maximum(c0, s0), 0, None)
    return (ov / CMP_LEN).astype(np.float32)


def _lambda_init(layer):
    return 0.8 - 0.6 * math.exp(-0.3 * layer)


def _prep_params(l, w_in, w_out, diff_subln_g, ln1_g, ln1_b, ln2_g, ln2_b, w_router, b_router,
                 w_gate_up, b_gate_up, w_down, b_down):
    gate_end = _GT0 + 3 * NSA_HEADS
    wt = w_in[l].T
    w_t = jnp.concatenate([wt[:gate_end], jnp.zeros((_GATE_PAD, D_MODEL), F32), wt[gate_end:]], axis=0).astype(BF16)
    bgu = b_gate_up[l]
    return {
        "w_t": w_t,
        "wo_a": w_out[l][:512].astype(BF16), "wo_b": w_out[l][512:].astype(BF16),
        "ln1_g": ln1_g[l][None], "ln1_b": ln1_b[l][None], "ln2_g": ln2_g[l][None], "ln2_b": ln2_b[l][None],
        "wr_t": w_router[l].T, "br": b_router[l][:, None],
        "wgu": w_gate_up[l], "wd": w_down[l], "p_mat": _deinterleave_matrix(),
        "bg": bgu[:, None, 0::2], "bl": bgu[:, None, 1::2], "bd": b_down[l][:, None, :],
        "subln_g": diff_subln_g[l][None],
    }


def _cmp_params(cmp_pe, cmp_w1, cmp_w2):
    eye = jnp.eye(NSA_KV_HEADS, dtype=F32)
    halves = []
    for half in range(2):
        w = cmp_w1[:, half * CMP_STRIDE:(half + 1) * CMP_STRIDE]
        wb = jnp.einsum('klde,gh->klgdhe', w, eye)
        halves.append(wb.reshape(2, CMP_STRIDE * LANES, LANES))
    w1ab = jnp.concatenate(halves, axis=-1).astype(BF16)
    pe = cmp_pe.reshape(2, 2, CMP_STRIDE, 1, HEAD_DIM)
    pe2 = jnp.broadcast_to(pe, (2, 2, CMP_STRIDE, NSA_KV_HEADS, HEAD_DIM)).reshape(2, 2, 1, CMP_STRIDE * LANES)
    pe2 = jnp.broadcast_to(pe2, (2, 2, 8, CMP_STRIDE * LANES))
    w2bd = jnp.einsum('kde,gh->kgdhe', cmp_w2, eye).reshape(2, LANES, LANES).astype(BF16)
    return w1ab, pe2, w2bd


def _feature_major_to_rows(a_t, kinds):
    B, _, T = a_t.shape
    return a_t.reshape(B, kinds, NSA_KV_HEADS, HEAD_DIM, T).transpose(0, 4, 1, 2, 3)


def _prompt_group(x, p, cmp_pe, cmp_w1, cmp_w2, lam_tile, lam_init):
    B, T, _ = x.shape
    x2d = x.reshape(B * T, D_MODEL)
    q_nsa, q_tb, nsa_rows, nsa_t, nsa_tb, _, win_t, win_tb, gates, dq, diff8, diff_b = _in_proj(x2d, p["w_t"], B)
    r3 = lambda a: a.reshape(B, T, a.shape[-1])
    q_nsa, nsa_rows3, gates, dq = map(r3, (q_nsa, nsa_rows, gates, dq))
    kvc = _compress_prompt(nsa_rows3, *_cmp_params(cmp_pe, cmp_w1, cmp_w2))
    n16 = T // CMP_STRIDE
    n_sel = -(-T // SEL_BLOCK)
    m_np = np.zeros((n16, n_sel), np.float32)
    m_np[:n16 - 1] = _cmp_to_sel(n16 - 1, n_sel)
    o_cmp, sel = _cmp_attn_prompt(q_nsa, q_tb, kvc, jnp.asarray(m_np.T))
    e_np = np.zeros((NSA_KV_HEADS, LANES, T), np.float32)
    for g in range(NSA_KV_HEADS):
        e_np[g, g * n_sel + np.arange(T) // SEL_BLOCK, np.arange(T)] = 1.0
    o_nsa = _sel_win_prompt(q_nsa, nsa_tb, win_tb, sel, jnp.asarray(e_np, BF16), o_cmp, gates)
    o_diff = _diff_prompt(lam_tile, dq, diff_b.reshape(B, T, DIFF_W), p["subln_g"], lam_init)
    y = _post_mixer_sorted(o_nsa.reshape(B * T, 512), o_diff.reshape(B * T, 512), x2d, p)
    nwin = min(WINDOW, T)
    return (y.reshape(B, T, D_MODEL),
            _feature_major_to_rows(nsa_t, 4),
            diff8.reshape(B, T, 2, DIFF_HEADS, DIFF_VDIM),
            _feature_major_to_rows(win_t[:, :, T - nwin:], 2))


def _sample_group(x, pool_nsa, pool_diff, win_buf, page_table, p, cmp_pe, cmp_w1, cmp_w2, lam_tile, lam_init):
    B, T, _ = x.shape
    past_len = page_table.shape[1] * PAGE_SIZE
    x2d = x.reshape(B, D_MODEL)
    q_nsa, _, nsa_rows, nsa_t, _, win_rows, _, _, gates, dq, diff8, _ = _in_proj(x2d, p["w_t"], 1)
    qh = q_nsa.reshape(B, NSA_HEADS, HEAD_DIM)
    grp = (np.arange(NSA_HEADS) // NSA_GROUP)[None, :, None, None] == np.arange(NSA_KV_HEADS)[None, None, :, None]
    qm = (qh[:, :, None, :] * jnp.asarray(grp, F32)).reshape(B, NSA_HEADS, LANES)
    dq4 = dq.reshape(B, DIFF_HEADS, 1, 2 * HEAD_DIM)
    rr = np.arange(8)
    rowmask = (rr[None, :, None] % DIFF_HEADS == np.arange(DIFF_HEADS)[:, None, None]) & (
        rr[None, :, None] // DIFF_HEADS == (np.arange(LANES) // HEAD_DIM)[None, None, :])
    qd4 = dq4 * jnp.asarray(rowmask, F32)[None]
    gt8 = jnp.pad(gates[:, :3 * NSA_HEADS].reshape(B, 3, NSA_HEADS).transpose(0, 2, 1), ((0, 0), (0, 0), (0, LANES - 3)))
    nwin = win_buf.shape[1]
    win_t = win_buf.transpose(0, 2, 3, 4, 1).reshape(B, WIN_W, nwin)
    n_pool = pool_nsa.shape[0]
    pool_nsa_t = pool_nsa.transpose(0, 2, 3, 4, 1).reshape(n_pool, NSA_W, PAGE_SIZE)
    pool_diff_r = pool_diff.reshape(n_pool, PAGE_SIZE * DIFF_ROWS, LANES)
    w1ab, pe2, w2bd = _cmp_params(cmp_pe, cmp_w1, cmp_w2)
    n16 = past_len // CMP_STRIDE
    n_sel = past_len // SEL_BLOCK + 1
    m_np = np.zeros((n16, 64), np.float32)
    m_np[:n16 - 1, :n_sel] = _cmp_to_sel(n16 - 1, n_sel)
    e_np = np.zeros((64, past_len), np.float32)
    e_np[np.arange(past_len) // SEL_BLOCK, np.arange(past_len)] = 1.0
    r_np = (np.arange(8)[:, None] // NSA_GROUP == np.arange(8)[None, :] // NSA_GROUP).astype(np.float32)
    sn = jnp.asarray(np.tile(np.asarray(NSA_SLOPES, np.float32)[:, None], (1, LANES)))
    sd = jnp.asarray(np.tile(np.asarray(DIFF_SLOPES, np.float32)[np.arange(8) % DIFF_HEADS, None], (1, LANES)))
    subg_tile = jnp.tile(p["subln_g"], (1, DIFF_HEADS))
    o_nsa8, o_diff8, new_win_t = _sample_attention(
        page_table, qm, qd4, gt8, nsa_rows[:, None, :], win_rows[:, None, :], win_rows[:, :, None],
        diff8.reshape(B, DIFF_ROWS, LANES), win_t, pool_nsa_t, pool_diff_r,
        w1ab, pe2, w2bd, jnp.asarray(m_np), jnp.asarray(e_np, BF16), jnp.asarray(r_np), sn, sd, lam_tile, subg_tile,
        past_len, lam_init)
    o8 = o_nsa8.reshape(B, NSA_KV_HEADS, NSA_GROUP, NSA_KV_HEADS, HEAD_DIM)
    o_nsa = jnp.stack([o8[:, g, :, g] for g in range(NSA_KV_HEADS)], axis=1).reshape(B, 512)
    d8 = o_diff8[:, :DIFF_HEADS].reshape(B, DIFF_HEADS, DIFF_HEADS, DIFF_VDIM)
    o_diff = jnp.stack([d8[:, h, h] for h in range(DIFF_HEADS)], axis=1).reshape(B, 512)
    y = _post_mixer(o_nsa, o_diff, x2d, p, s_tile=B, chunk=B)
    return (y.reshape(B, T, D_MODEL),
            _feature_major_to_rows(nsa_t, 4).reshape(B, T, 4, NSA_KV_HEADS, HEAD_DIM),
            diff8.reshape(B, T, 2, DIFF_HEADS, DIFF_VDIM),
            _feature_major_to_rows(new_win_t, 2))


def kernel(x_prompt, x_sample, cache_nsa_kv, cache_diff_kv, state_nsa_win, page_table, w_in, w_out, cmp_pe, cmp_w1,
           cmp_w2, diff_lambda, diff_subln_g, ln1_g, ln1_b, ln2_g, ln2_b, w_router, b_router, w_gate_up, b_gate_up,
           w_down, b_down):
    depth = w_in.shape[0]
    xp, xs = x_prompt, x_sample
    outs = [[] for _ in range(6)]
    for l in range(depth):
        lam0 = _lambda_init(l)
        lv = diff_lambda[l].astype(F32)
        lam = jnp.exp(jnp.sum(lv[0] * lv[1])) - jnp.exp(jnp.sum(lv[2] * lv[3])) + lam0
        lam_tile = jnp.full((8, LANES), lam, F32)
        p = _prep_params(l, w_in, w_out, diff_subln_g, ln1_g, ln1_b, ln2_g, ln2_b, w_router, b_router,
                         w_gate_up, b_gate_up, w_down, b_down)
        xp, r_nsa, r_diff, r_win = _prompt_group(xp, p, cmp_pe[l], cmp_w1[l], cmp_w2[l], lam_tile, lam0)
        xs, s_nsa, s_diff, s_win = _sample_group(xs, cache_nsa_kv[l], cache_diff_kv[l], state_nsa_win[l], page_table,
                                                 p, cmp_pe[l], cmp_w1[l], cmp_w2[l], lam_tile, lam0)
        for lst, v in zip(outs, (r_nsa, r_diff, r_win, s_nsa, s_diff, s_win)):
            lst.append(v)
    return (xp, xs) + tuple(jnp.stack(o) for o in outs)
```

```python
import functools
import math

import numpy as np
import jax
import jax.numpy as jnp
from jax import lax
from jax.experimental import pallas as pl
from jax.experimental.pallas import tpu as pltpu
from jax.experimental.pallas import tpu_sc as plsc

F32 = jnp.float32
BF16 = jnp.bfloat16
HIGHEST = lax.Precision.HIGHEST

D_MODEL = 1024
HEAD_DIM = 64
NSA_HEADS = 8
NSA_KV_HEADS = 2
NSA_GROUP = NSA_HEADS // NSA_KV_HEADS
CMP_LEN = 32
CMP_STRIDE = 16
SEL_BLOCK = 64
SEL_TOPN = 16
WINDOW = 512
DIFF_HEADS = 4
DIFF_VDIM = 2 * HEAD_DIM
N_EXPERTS = 32
TOP_K = 4
SWIGLU_ALPHA = 1.702
SWIGLU_LIMIT = 7.0
LN_EPS = 1e-5
RMS_EPS = 1e-5
ATTN_SCALE = HEAD_DIM ** -0.5
LOG2E = math.log2(math.e)
DEPTH = 1
DEEPNORM_ALPHA = (2 * DEPTH) ** 0.25
PAGE_SIZE = 128

NSA_SLOPES = tuple(2.0 ** (-8.0 * (i + 1) / NSA_HEADS) for i in range(NSA_HEADS))
DIFF_SLOPES = tuple(2.0 ** (-8.0 * (i + 1) / DIFF_HEADS) for i in range(DIFF_HEADS))

VMEM_LIMIT_BYTES = 56 * 1024 * 1024
LANES = 128
MXU_DIM = 256

MOE_CHUNK = MXU_DIM
MOE_SUPERTILE = 13 * LANES
SC_CORES = 2
SC_SUBCORES = 16
SC_WINDOW = 128

NSA_W = 4 * NSA_KV_HEADS * HEAD_DIM
WIN_W = 2 * NSA_KV_HEADS * HEAD_DIM
DIFF_W = 2 * DIFF_HEADS * DIFF_VDIM
DIFF_ROWS = DIFF_W // LANES
_GATE_PAD = LANES - 3 * NSA_HEADS
_Q0, _NSA0, _WIN0, _GT0, _DQ0, _DIFF0, _PROJ_ROWS = 0, 512, 1024, 1280, 1408, 1920, 2944


def _cparams(*sem):
    return pltpu.CompilerParams(dimension_semantics=sem, vmem_limit_bytes=VMEM_LIMIT_BYTES)


def _softmax_parts(s, mask):
    s = jnp.where(mask, s, -jnp.inf)
    m = jnp.max(s, axis=-1, keepdims=True)
    m = jnp.where(jnp.isfinite(m), m, 0.0)
    e = jnp.exp(s - m)
    d = jnp.sum(e, axis=-1, keepdims=True)
    return e, jnp.where(d > 0, d, 1.0)


def _softmax2_parts(s, mask):
    s = jnp.where(mask, s, -jnp.inf)
    m = jnp.max(s, axis=-1, keepdims=True)
    m = jnp.where(jnp.isfinite(m), m, 0.0)
    e = jnp.exp2(s - m)
    d = jnp.sum(e, axis=-1, keepdims=True)
    return e, jnp.where(d > 0, d, 1.0)


def _dot_nt(a, b):
    return lax.dot_general(a, b, (((1,), (1,)), ((), ())), preferred_element_type=F32)


def _dot(a, b):
    return jnp.dot(a, b, preferred_element_type=F32)


def _pad_head(qh, g):
    z = jnp.zeros_like(qh)
    return jnp.concatenate([qh, z] if g == 0 else [z, qh], axis=1)


def _select_blocks(imp, blk, qpos, n_blk_lanes):
    cur = jnp.right_shift(qpos, int(math.log2(SEL_BLOCK)))
    valid = blk * SEL_BLOCK <= qpos
    forced = (blk == 0) | (blk == cur) | (blk == cur - 1)
    val = jnp.where(forced, jnp.inf, jnp.where(valid, imp, -jnp.inf))
    rank = jnp.zeros(val.shape, F32)
    for i in range(n_blk_lanes):
        ci = val[:, i:i + 1]
        beats = (ci > val) | ((ci == val) & (blk > i))
        rank = rank + jnp.where(beats, 1.0, 0.0)
    return jnp.where(rank < SEL_TOPN, 1.0, 0.0)


def _in_proj_kernel(x_ref, w_ref, q_ref, qtb_ref, nsa_ref, nsat_ref, nsatb_ref, win_ref, wint_ref, wintb_ref, gt_ref,
                    dq_ref, diff_ref, diffb_ref):
    tm = x_ref.shape[0]
    xb = x_ref[...].astype(BF16)
    q_ref[...] = _dot_nt(xb, w_ref[_Q0:_NSA0, :])
    qtb_ref[0] = _dot_nt(w_ref[_Q0:_NSA0, :], xb).astype(BF16)
    nsa_ref[...] = _dot_nt(xb, w_ref[_NSA0:_WIN0, :])
    r = _dot_nt(w_ref[_NSA0:_WIN0, :], xb)
    nsat_ref[0] = r
    nsatb_ref[0] = r.astype(BF16)
    win_ref[...] = _dot_nt(xb, w_ref[_WIN0:_GT0, :])
    r = _dot_nt(w_ref[_WIN0:_GT0, :], xb)
    wint_ref[0] = r
    wintb_ref[0] = r.astype(BF16)
    gt_ref[...] = jax.nn.sigmoid(_dot_nt(xb, w_ref[_GT0:_DQ0, :]))
    dq_ref[...] = _dot_nt(xb, w_ref[_DQ0:_DIFF0, :])
    r = _dot_nt(xb, w_ref[_DIFF0:_PROJ_ROWS, :])
    diffb_ref[...] = r.astype(BF16)
    for j in range(DIFF_ROWS):
        diff_ref[pl.ds(j, tm, stride=DIFF_ROWS), :] = r[:, j * LANES:(j + 1) * LANES]


def _in_proj(x2d, w_t, batch):
    n = x2d.shape[0]
    t = n // batch
    tm = min(512, t)
    nt = t // tm
    row = lambda b, i: (b * nt + i, 0)
    tr = lambda b, i: (b, 0, i)
    tok = lambda w: pl.BlockSpec((tm, w), row)
    return pl.pallas_call(
        _in_proj_kernel,
        grid=(batch, nt),
        in_specs=[pl.BlockSpec((tm, D_MODEL), row),
                  pl.BlockSpec((_PROJ_ROWS, D_MODEL), lambda b, i: (0, 0))],
        out_specs=(tok(512), pl.BlockSpec((1, 512, tm), tr),
                   tok(NSA_W), pl.BlockSpec((1, NSA_W, tm), tr), pl.BlockSpec((1, NSA_W, tm), tr),
                   tok(WIN_W), pl.BlockSpec((1, WIN_W, tm), tr), pl.BlockSpec((1, WIN_W, tm), tr),
                   tok(LANES), tok(512), pl.BlockSpec((tm * DIFF_ROWS, LANES), row), tok(DIFF_W)),
        out_shape=(jax.ShapeDtypeStruct((n, 512), F32), jax.ShapeDtypeStruct((batch, 512, t), BF16),
                   jax.ShapeDtypeStruct((n, NSA_W), F32),
                   jax.ShapeDtypeStruct((batch, NSA_W, t), F32), jax.ShapeDtypeStruct((batch, NSA_W, t), BF16),
                   jax.ShapeDtypeStruct((n, WIN_W), F32),
                   jax.ShapeDtypeStruct((batch, WIN_W, t), F32), jax.ShapeDtypeStruct((batch, WIN_W, t), BF16),
                   jax.ShapeDtypeStruct((n, LANES), F32), jax.ShapeDtypeStruct((n, 512), F32),
                   jax.ShapeDtypeStruct((n * DIFF_ROWS, LANES), F32), jax.ShapeDtypeStruct((n, DIFF_W), BF16)),
        compiler_params=_cparams("parallel", "parallel"),
        name="in_proj",
    )(x2d, w_t)


def _compress_chunks(load_rows, w1, pe, w2, n16):
    cst = _dot(pe[0].astype(BF16), w1[:, 0:LANES]) + _dot(pe[1].astype(BF16), w1[:, LANES:])
    chunks = jnp.concatenate([load_rows(l) for l in range(CMP_STRIDE)], axis=1)
    ab = _dot(chunks.astype(BF16), w1)
    nxt = pltpu.roll(ab[:, LANES:], n16 - 1, 0)
    h = jax.nn.gelu(ab[:, 0:LANES] + nxt + cst[0:1])
    return _dot(h.astype(BF16), w2)


def _compress_kernel(rows_ref, w1_ref, pe_ref, w2_ref, o_ref):
    n16 = rows_ref.shape[1] // CMP_STRIDE
    o_ref[0, 0] = _compress_chunks(lambda l: rows_ref[0, pl.ds(l, n16, stride=CMP_STRIDE), :],
                                   w1_ref[0], pe_ref[0], w2_ref[0], n16)


def _compress_prompt(nsa_rows, w1ab, pe2, w2bd):
    B, T, _ = nsa_rows.shape
    n16 = T // CMP_STRIDE
    return pl.pallas_call(
        _compress_kernel,
        grid=(B, 2),
        in_specs=[pl.BlockSpec((1, T, LANES), lambda b, k: (b, 0, k)),
                  pl.BlockSpec((1,) + w1ab.shape[1:], lambda b, k: (k, 0, 0)),
                  pl.BlockSpec((1,) + pe2.shape[1:], lambda b, k: (k, 0, 0, 0)),
                  pl.BlockSpec((1,) + w2bd.shape[1:], lambda b, k: (k, 0, 0))],
        out_specs=pl.BlockSpec((1, 1, n16, LANES), lambda b, k: (b, k, 0, 0)),
        out_shape=jax.ShapeDtypeStruct((B, 2, n16, LANES), F32),
        compiler_params=_cparams("parallel", "parallel"),
        name="compress_prompt",
    )(nsa_rows, w1ab, pe2, w2bd)


def _select_blocks_t(imp, qpos):
    n_blk = imp.shape[0]
    blk = lax.broadcasted_iota(jnp.int32, imp.shape, 0)
    cur = jnp.right_shift(qpos, int(math.log2(SEL_BLOCK)))
    valid = blk * SEL_BLOCK <= qpos
    forced = (blk == 0) | (blk == cur) | (blk == cur - 1)
    val = jnp.where(forced, jnp.inf, jnp.where(valid, imp, -jnp.inf))
    rank = jnp.zeros(val.shape, F32)
    for i in range(n_blk):
        ci = val[i:i + 1, :]
        beats = (ci > val) | ((ci == val) & (blk > i))
        rank = rank + jnp.where(beats, 1.0, 0.0)
    return jnp.where(rank < SEL_TOPN, 1.0, 0.0)


def _cmp_attn_kernel(q_ref, qt_ref, kc_ref, vc_ref, mt_ref, o_ref, selt_ref, *, tq):
    i = pl.program_id(1)
    ncmp = kc_ref.shape[2]
    kcb = kc_ref[0, 0].astype(BF16)
    vcb = vc_ref[0, 0].astype(BF16)
    qpos_c = i * tq + lax.broadcasted_iota(jnp.int32, (tq, 1), 0)
    end_r = lax.broadcasted_iota(jnp.int32, (1, ncmp), 1) * CMP_STRIDE + (CMP_LEN - 1)
    mask = end_r <= qpos_c
    end_rf = end_r.astype(F32)
    q = q_ref[0] * (ATTN_SCALE * LOG2E)
    for h in range(NSA_HEADS):
        g = h // NSA_GROUP
        qp = _pad_head(q[:, h * HEAD_DIM:(h + 1) * HEAD_DIM], g).astype(BF16)
        e, d = _softmax2_parts(_dot_nt(qp, kcb) + (NSA_SLOPES[h] * LOG2E) * end_rf, mask)
        oh = _dot(e.astype(BF16), vcb)[:, g * HEAD_DIM:(g + 1) * HEAD_DIM] / d
        o_ref[0, :, h * HEAD_DIM:(h + 1) * HEAD_DIM] = oh
    qpos_r = i * tq + lax.broadcasted_iota(jnp.int32, (1, tq), 1)
    end_c = lax.broadcasted_iota(jnp.int32, (ncmp, 1), 0) * CMP_STRIDE + (CMP_LEN - 1)
    mask_t = end_c <= qpos_r
    end_cf = end_c.astype(F32)
    qt = qt_ref[0]
    zero = jnp.zeros((HEAD_DIM, tq), BF16)
    psum = [None, None]
    for h in range(NSA_HEADS):
        g = h // NSA_GROUP
        qh = qt[h * HEAD_DIM:(h + 1) * HEAD_DIM, :]
        qpt = jnp.concatenate([qh, zero] if g == 0 else [zero, qh], axis=0)
        s = _dot(kcb, qpt) * (ATTN_SCALE * LOG2E) + (NSA_SLOPES[h] * LOG2E) * end_cf
        s = jnp.where(mask_t, s, -jnp.inf)
        m = jnp.max(s, axis=0, keepdims=True)
        m = jnp.where(jnp.isfinite(m), m, 0.0)
        e = jnp.exp2(s - m)
        d = jnp.sum(e, axis=0, keepdims=True)
        p = e / jnp.where(d > 0, d, 1.0)
        psum[g] = p if psum[g] is None else psum[g] + p
    sels = []
    for g in range(NSA_KV_HEADS):
        imp = jnp.dot(mt_ref[...], psum[g], precision=HIGHEST, preferred_element_type=F32)
        sels.append(_select_blocks_t(imp, qpos_r))
    sels.append(jnp.zeros((LANES - NSA_KV_HEADS * mt_ref.shape[0], tq), F32))
    selt_ref[0] = jnp.concatenate(sels, axis=0)


def _cmp_attn_prompt(q_nsa, q_tb, kvc, mt_mat, tq=512):
    B, T, _ = q_nsa.shape
    ncmp = kvc.shape[2]
    n_sel = mt_mat.shape[0]
    return pl.pallas_call(
        functools.partial(_cmp_attn_kernel, tq=tq),
        grid=(B, T // tq),
        in_specs=[pl.BlockSpec((1, tq, 512), lambda b, i: (b, i, 0)),
                  pl.BlockSpec((1, 512, tq), lambda b, i: (b, 0, i)),
                  pl.BlockSpec((1, 1, ncmp, LANES), lambda b, i: (b, 0, 0, 0)),
                  pl.BlockSpec((1, 1, ncmp, LANES), lambda b, i: (b, 1, 0, 0)),
                  pl.BlockSpec((n_sel, ncmp), lambda b, i: (0, 0))],
        out_specs=(pl.BlockSpec((1, tq, 512), lambda b, i: (b, i, 0)),
                   pl.BlockSpec((1, LANES, tq), lambda b, i: (b, 0, i))),
        out_shape=(jax.ShapeDtypeStruct((B, T, 512), F32), jax.ShapeDtypeStruct((B, LANES, T), F32)),
        compiler_params=_cparams("parallel", "parallel"),
        name="cmp_attn_prompt",
    )(q_nsa, q_tb, kvc, kvc, mt_mat)


def _sel_win_kernel(q_ref, ks_ref, vs_ref, kw_ref, vw_ref, sel_ref, e_ref, ocmp_ref, gt_ref, o_ref, *, tq, span,
                    kv_step):
    i = pl.program_id(1)
    T = ks_ref.shape[2]
    qpos = i * tq + lax.broadcasted_iota(jnp.int32, (tq, 1), 0)
    q = q_ref[0] * (ATTN_SCALE * LOG2E)
    gt = gt_ref[0]
    ocmp = ocmp_ref[0]
    selb = sel_ref[0].astype(BF16)
    heads = [(h, h // NSA_GROUP, h * HEAD_DIM, (h + 1) * HEAD_DIM) for h in range(NSA_HEADS)]
    qps = [_pad_head(q[:, lo:hi], g).astype(BF16) for _, g, lo, hi in heads]
    start = pl.multiple_of(jnp.maximum(i * tq - WINDOW, 0), LANES)
    wpos = start + lax.broadcasted_iota(jnp.int32, (1, span), 1)
    dist_w = qpos - wpos
    mask_w = (dist_w >= 0) & (dist_w <= WINDOW)
    wposf = wpos.astype(F32)
    kwb = kw_ref[0, :, pl.ds(start, span)]
    vwb = vw_ref[0, :, pl.ds(start, span)]
    partial = []
    for h, g, lo, hi in heads:
        s = _dot(qps[h], kwb) + (NSA_SLOPES[h] * LOG2E) * wposf
        e, d = _softmax2_parts(s, mask_w)
        o_win = _dot_nt(e.astype(BF16), vwb)[:, g * HEAD_DIM:(g + 1) * HEAD_DIM] / d
        partial.append(gt[:, h:h + 1] * ocmp[:, lo:hi] + gt[:, 2 * NSA_HEADS + h:2 * NSA_HEADS + h + 1] * o_win)
    tiles_per_step = kv_step // tq
    for br in range(T // kv_step):
        @pl.when((i >= br * tiles_per_step) & (i < (br + 1) * tiles_per_step))
        def _(br=br):
            kv = (br + 1) * kv_step
            kpos = lax.broadcasted_iota(jnp.int32, (1, kv), 1)
            causal = kpos <= qpos
            kposf = kpos.astype(F32)
            ksb = ks_ref[0, :, 0:kv]
            vsb = vs_ref[0, :, 0:kv]
            for g in range(NSA_KV_HEADS):
                expand = lax.dot_general(selb, e_ref[g, :, 0:kv], (((0,), (0,)), ((), ())), preferred_element_type=F32)
                mask_s = (expand > 0.5) & causal
                for h, _, lo, hi in heads[g * NSA_GROUP:(g + 1) * NSA_GROUP]:
                    s = _dot(qps[h], ksb) + (NSA_SLOPES[h] * LOG2E) * kposf
                    e, d = _softmax2_parts(s, mask_s)
                    o_sel = _dot_nt(e.astype(BF16), vsb)[:, g * HEAD_DIM:(g + 1) * HEAD_DIM] / d
                    o_ref[0, :, lo:hi] = partial[h] + gt[:, NSA_HEADS + h:NSA_HEADS + h + 1] * o_sel


def _sel_win_prompt(q_nsa, nsa_t, win_t, sel_t, e_mat, o_cmp, gates, tq=256):
    B, T, _ = q_nsa.shape
    span = WINDOW + tq
    kv = lambda j: pl.BlockSpec((1, LANES, T), lambda b, i: (b, j, 0))
    return pl.pallas_call(
        functools.partial(_sel_win_kernel, tq=tq, span=span, kv_step=min(2 * tq, T)),
        grid=(B, T // tq),
        in_specs=[pl.BlockSpec((1, tq, 512), lambda b, i: (b, i, 0)),
                  kv(2), kv(3), kv(0), kv(1),
                  pl.BlockSpec((1, LANES, tq), lambda b, i: (b, 0, i)),
                  pl.BlockSpec((NSA_KV_HEADS, LANES, T), lambda b, i: (0, 0, 0)),
                  pl.BlockSpec((1, tq, 512), lambda b, i: (b, i, 0)),
                  pl.BlockSpec((1, tq, LANES), lambda b, i: (b, i, 0))],
        out_specs=pl.BlockSpec((1, tq, 512), lambda b, i: (b, i, 0)),
        out_shape=jax.ShapeDtypeStruct((B, T, 512), F32),
        compiler_params=_cparams("parallel", "parallel"),
        name="sel_win_prompt",
    )(q_nsa, nsa_t, nsa_t, win_t, win_t, sel_t, e_mat, o_cmp, gates)


def _diff_kernel(lam_ref, sl_ref, q_ref, k_ref, v_ref, g_ref, o_ref, *, tq, lam_init):
    i = pl.program_id(2)
    T = k_ref.shape[1]
    lam = lam_ref[0:1, 0:1]
    q = q_ref[0] * (ATTN_SCALE * LOG2E)
    qps = [_pad_head(q[:, c * HEAD_DIM:(c + 1) * HEAD_DIM], c).astype(BF16) for c in range(2)]
    tri = lax.broadcasted_iota(jnp.int32, (tq, tq), 0) >= lax.broadcasted_iota(jnp.int32, (tq, tq), 1)
    for br in range(T // tq):
        @pl.when(i == br)
        def _(br=br):
            off = br * tq
            kv = off + tq
            kb = k_ref[0, 0:kv, :]
            vb = v_ref[0, 0:kv, :]
            col = (sl_ref[0, :, 0:1] * LOG2E) * lax.broadcasted_iota(jnp.int32, (1, kv), 1).astype(F32)
            outs = []
            for c in range(2):
                s = _dot_nt(qps[c], kb) + col
                s_d = jnp.where(tri, s[:, off:], -jnp.inf)
                m = jnp.max(s_d, axis=-1, keepdims=True)
                if off:
                    m = jnp.maximum(m, jnp.max(s[:, :off], axis=-1, keepdims=True))
                e_d = jnp.exp2(s_d - m)
                d = jnp.sum(e_d, axis=-1, keepdims=True)
                o = _dot(e_d.astype(BF16), vb[off:, :])
                if off:
                    e_o = jnp.exp2(s[:, :off] - m)
                    d = d + jnp.sum(e_o, axis=-1, keepdims=True)
                    o = o + _dot(e_o.astype(BF16), vb[:off, :])
                outs.append(o / d)
            od = outs[0] - lam * outs[1]
            od = od * lax.rsqrt(jnp.mean(jnp.square(od), axis=-1, keepdims=True) + RMS_EPS)
            o_ref[0] = od * g_ref[...] * (1.0 - lam_init)


def _diff_prompt(lam_tile, dq, diff_b, subln_g, lam_init, tq=256):
    B, T, _ = dq.shape
    slopes = jnp.asarray(np.tile(np.asarray(DIFF_SLOPES, np.float32)[:, None, None], (1, 1, LANES)))
    return pl.pallas_call(
        functools.partial(_diff_kernel, tq=tq, lam_init=lam_init),
        grid=(B, DIFF_HEADS, T // tq),
        in_specs=[pl.BlockSpec((8, LANES), lambda b, h, i: (0, 0)),
                  pl.BlockSpec((1, 1, LANES), lambda b, h, i: (h, 0, 0)),
                  pl.BlockSpec((1, tq, LANES), lambda b, h, i: (b, i, h)),
                  pl.BlockSpec((1, T, LANES), lambda b, h, i: (b, 0, h)),
                  pl.BlockSpec((1, T, LANES), lambda b, h, i: (b, 0, DIFF_HEADS + h)),
                  pl.BlockSpec((1, DIFF_VDIM), lambda b, h, i: (0, 0))],
        out_specs=pl.BlockSpec((1, tq, LANES), lambda b, h, i: (b, i, h)),
        out_shape=jax.ShapeDtypeStruct((B, T, DIFF_HEADS * DIFF_VDIM), F32),
        compiler_params=_cparams("parallel", "parallel", "parallel"),
        name="diff_prompt",
    )(lam_tile, slopes, dq, diff_b, diff_b, subln_g)


def _layer_norm(z, g, b):
    mu = jnp.mean(z, axis=-1, keepdims=True)
    zc = z - mu
    var = jnp.mean(jnp.square(zc), axis=-1, keepdims=True)
    return zc * lax.rsqrt(var + LN_EPS) * g + b


def _pack_rows(x, ref):
    w = x.shape[1] // 2
    hi = lax.bitcast_convert_type(x[:, :w].astype(BF16).astype(F32), jnp.uint32)
    lo = lax.bitcast_convert_type(x[:, w:].astype(BF16).astype(F32), jnp.uint32)
    packed = hi | (lo >> 16)
    ref[0] = packed[:, :w // 2]
    ref[1] = packed[:, w // 2:]


def _unpack_rows(u0, u1):
    hi = [lax.bitcast_convert_type(u & jnp.uint32(0xFFFF0000), F32) for u in (u0, u1)]
    lo = [lax.bitcast_convert_type(u << 16, F32) for u in (u0, u1)]
    return jnp.concatenate(hi + lo, axis=1)


def _out_proj_kernel(on_ref, od_ref, x_ref, wa_ref, wb_ref, g_ref, b_ref, wr_ref, br_ref,
                     x1_ref, x1b_ref, x1p_ref, sel_ref, gate_ref, tope_ref, gate8_ref, cnt_ref):
    y = _dot(on_ref[...].astype(BF16), wa_ref[...]) + _dot(od_ref[...].astype(BF16), wb_ref[...])
    x1 = _layer_norm(DEEPNORM_ALPHA * x_ref[...] + y, g_ref[...], b_ref[...])
    x1_ref[...] = x1
    x1b_ref[...] = x1.astype(BF16)
    _pack_rows(x1, x1p_ref)
    logits = lax.dot_general(wr_ref[...], x1, (((1,), (1,)), ((), ())), precision=HIGHEST,
                             preferred_element_type=F32) + br_ref[...]
    eidx = lax.broadcasted_iota(jnp.int32, logits.shape, 0).astype(F32)
    sel = jnp.zeros(logits.shape, F32)
    picked, vals = [], []
    for k in range(TOP_K):
        cur = jnp.where(sel > 0.5, -jnp.inf, logits)
        m = jnp.max(cur, axis=0, keepdims=True)
        first = jnp.min(jnp.where(cur == m, eidx, float(N_EXPERTS)), axis=0, keepdims=True)
        sel = jnp.where(eidx == first, 1.0, sel)
        picked.append(first)
        vals.append(m)
    ex = jnp.where(sel > 0.5, jnp.exp(logits - vals[0]), 0.0)
    denom = jnp.sum(ex, axis=0, keepdims=True)
    sel_ref[...] = sel
    gate_ref[...] = ex / denom
    pad = [jnp.zeros_like(denom)] * (8 - TOP_K)
    tope_ref[...] = jnp.concatenate(picked + pad, axis=0)
    gate8_ref[...] = jnp.concatenate([jnp.exp(v - vals[0]) / denom for v in vals] + pad, axis=0)
    cnt_ref[0] = jnp.sum(sel, axis=1, keepdims=True)


def _out_proj_ln_router(o_nsa, o_diff, x2d, wa, wb, ln_g, ln_b, wr_t, br):
    n = x2d.shape[0]
    tm = min(512, n)
    row = lambda i: (i, 0)
    fix = lambda i: (0, 0)
    col = lambda i: (0, i)
    return pl.pallas_call(
        _out_proj_kernel,
        grid=(n // tm,),
        in_specs=[pl.BlockSpec((tm, 512), row), pl.BlockSpec((tm, 512), row), pl.BlockSpec((tm, D_MODEL), row),
                  pl.BlockSpec((512, D_MODEL), fix), pl.BlockSpec((512, D_MODEL), fix),
                  pl.BlockSpec((1, D_MODEL), fix), pl.BlockSpec((1, D_MODEL), fix),
                  pl.BlockSpec((N_EXPERTS, D_MODEL), fix), pl.BlockSpec((N_EXPERTS, 1), fix)],
        out_specs=(pl.BlockSpec((tm, D_MODEL), row), pl.BlockSpec((tm, D_MODEL), row),
                   pl.BlockSpec((2, tm, D_MODEL // 4), lambda i: (0, i, 0)),
                   pl.BlockSpec((N_EXPERTS, tm), col), pl.BlockSpec((N_EXPERTS, tm), col),
                   pl.BlockSpec((8, tm), col), pl.BlockSpec((8, tm), col),
                   pl.BlockSpec((1, N_EXPERTS, 1), lambda i: (i, 0, 0))),
        out_shape=(jax.ShapeDtypeStruct((n, D_MODEL), F32), jax.ShapeDtypeStruct((n, D_MODEL), BF16),
                   jax.ShapeDtypeStruct((2, n, D_MODEL // 4), jnp.uint32),
                   jax.ShapeDtypeStruct((N_EXPERTS, n), F32), jax.ShapeDtypeStruct((N_EXPERTS, n), F32),
                   jax.ShapeDtypeStruct((8, n), F32), jax.ShapeDtypeStruct((8, n), F32),
                   jax.ShapeDtypeStruct((n // tm, N_EXPERTS, 1), F32)),
        compiler_params=_cparams("parallel"),
        name="out_proj_ln_router",
    )(o_nsa, o_diff, x2d, wa, wb, ln_g, ln_b, wr_t, br)


def _load_expert_weights(wgu_ref, wd_ref, p_ref, wg_s, wl_s, wd_s):
    half = MXU_DIM // 2
    for m in range(wgu_ref.shape[2] // MXU_DIM):
        y = _dot(wgu_ref[0, :, m * MXU_DIM:(m + 1) * MXU_DIM].astype(BF16), p_ref[...])
        wg_s[:, m * half:(m + 1) * half] = y[:, :half].astype(BF16)
        wl_s[:, m * half:(m + 1) * half] = y[:, half:].astype(BF16)
    wd_s[...] = wd_ref[0].astype(BF16)


def _deinterleave_matrix():
    half = MXU_DIM // 2
    p_np = np.zeros((MXU_DIM, MXU_DIM), np.float32)
    p_np[2 * np.arange(half), np.arange(half)] = 1.0
    p_np[2 * np.arange(half) + 1, half + np.arange(half)] = 1.0
    return jnp.asarray(p_np, BF16)


_EXPERT_WEIGHT_SCRATCH = [pltpu.VMEM((D_MODEL, D_MODEL), BF16)] * 3


def _moe_pos_kernel(sel_ref, u_ref, pos_ref, cnt_ref, *, n_valid):
    s_tile = sel_ref.shape[1]
    tok = pl.program_id(0) * s_tile + lax.broadcasted_iota(jnp.int32, (1, s_tile), 1)
    sel = jnp.where(tok < n_valid, sel_ref[...], 0.0)
    rank = _dot(sel.astype(BF16), u_ref[...])
    pos_ref[...] = jnp.where(sel > 0.5, rank, -1.0)
    cnt_ref[0] = jnp.sum(sel, axis=1, keepdims=True)


def _moe_positions(sel_t, s_tile):
    n = sel_t.shape[1]
    ns = pl.cdiv(n, s_tile)
    upper = jnp.asarray(np.triu(np.ones((s_tile, s_tile), np.float32), 1), BF16)
    return pl.pallas_call(
        functools.partial(_moe_pos_kernel, n_valid=n),
        grid=(ns,),
        in_specs=[pl.BlockSpec((N_EXPERTS, s_tile), lambda s: (0, s)),
                  pl.BlockSpec((s_tile, s_tile), lambda s: (0, 0))],
        out_specs=(pl.BlockSpec((N_EXPERTS, s_tile), lambda s: (0, s)),
                   pl.BlockSpec((1, N_EXPERTS, 1), lambda s: (s, 0, 0))),
        out_shape=(jax.ShapeDtypeStruct((N_EXPERTS, ns * s_tile), F32),
                   jax.ShapeDtypeStruct((ns, N_EXPERTS, 1), F32)),
        compiler_params=_cparams("parallel"),
        name="moe_positions",
    )(sel_t, upper)


def _moe_kernel(nch_ref, x_ref, pos_ref, gate_ref, wgu_ref, wd_ref, p_ref, bg_ref, bl_ref, bd_ref, y_ref,
                wg_s, wl_s, wd_s, *, chunk, n_valid):
    s = pl.program_id(0)
    e = pl.program_id(1)
    s_tile = x_ref.shape[0]
    n_chunks = nch_ref[s * N_EXPERTS + e]

    @pl.when(e == 0)
    def _():
        y_ref[...] = jnp.zeros_like(y_ref)

    @pl.when(n_chunks > 0)
    def _():
        _load_expert_weights(wgu_ref, wd_ref, p_ref, wg_s, wl_s, wd_s)

    pos = pos_ref[0, 0].astype(jnp.int32)
    gate = gate_ref[0, 0]
    slot0 = lax.broadcasted_iota(jnp.int32, (chunk, 1), 0)

    def body(j, carry):
        hit = pos == slot0 + j * chunk
        onehot = jnp.where(hit, 1.0, 0.0).astype(BF16)
        x = x_ref[...]
        if n_valid % s_tile:
            row = s * s_tile + lax.broadcasted_iota(jnp.int32, (s_tile, 1), 0)
            x = jnp.where(row < n_valid, x, jnp.zeros_like(x))
        xs = _dot(onehot, x).astype(BF16)
        hg = jnp.minimum(_dot(xs, wg_s[...]) + bg_ref[0], SWIGLU_LIMIT)
        hl = jnp.clip(_dot(xs, wl_s[...]) + bl_ref[0], -SWIGLU_LIMIT, SWIGLU_LIMIT)
        a = (hl + 1.0) * hg * jax.nn.sigmoid(SWIGLU_ALPHA * hg)
        out = _dot(a.astype(BF16), wd_s[...]) + bd_ref[0]
        gslot = jnp.sum(jnp.where(hit, gate, 0.0), axis=1, keepdims=True)
        outg = (out * gslot).astype(BF16)
        y_ref[...] += lax.dot_general(onehot, outg, (((0,), (0,)), ((), ())), preferred_element_type=F32)
        return carry

    lax.fori_loop(0, n_chunks, body, 0)


def _moe(x1b, pos_t, gate_t, counts, wgu, wd, p_mat, bg, bl, bd, s_tile, chunk):
    n = x1b.shape[0]
    ns = pl.cdiv(n, s_tile)
    nch = ((counts.reshape(ns * N_EXPERTS) + (chunk - 1)) // chunk).astype(jnp.int32)
    pos4 = pos_t.reshape(N_EXPERTS, ns, 1, s_tile)
    gate4 = jnp.pad(gate_t, ((0, 0), (0, ns * s_tile - n))).reshape(N_EXPERTS, ns, 1, s_tile)
    wspec = lambda a: pl.BlockSpec((1,) + a.shape[1:], lambda s, e, nch: (e, 0, 0))
    bspec = pl.BlockSpec((1, 1, D_MODEL), lambda s, e, nch: (e, 0, 0))
    rspec = pl.BlockSpec((1, 1, 1, s_tile), lambda s, e, nch: (e, s, 0, 0))
    return pl.pallas_call(
        functools.partial(_moe_kernel, chunk=chunk, n_valid=n),
        grid_spec=pltpu.PrefetchScalarGridSpec(
            num_scalar_prefetch=1,
            grid=(ns, N_EXPERTS),
            in_specs=[pl.BlockSpec((s_tile, D_MODEL), lambda s, e, nch: (s, 0)),
                      rspec, rspec, wspec(wgu), wspec(wd), pl.BlockSpec(p_mat.shape, lambda s, e, nch: (0, 0)),
                      bspec, bspec, bspec],
            out_specs=pl.BlockSpec((s_tile, D_MODEL), lambda s, e, nch: (s, 0)),
            scratch_shapes=_EXPERT_WEIGHT_SCRATCH),
        out_shape=jax.ShapeDtypeStruct((n, D_MODEL), F32),
        compiler_params=_cparams("parallel", "arbitrary"),
        name="moe_experts",
    )(nch, x1b, pos4, gate4, wgu, wd, p_mat, bg, bl, bd)


def _ln2_kernel(x_ref, f_ref, g_ref, b_ref, o_ref):
    o_ref[...] = _layer_norm(DEEPNORM_ALPHA * x_ref[...] + f_ref[...], g_ref[...], b_ref[...])


def _residual_ln(x1, f, g, b):
    n = x1.shape[0]
    tm = min(512, n)
    row = lambda i: (i, 0)
    fix = lambda i: (0, 0)
    return pl.pallas_call(
        _ln2_kernel,
        grid=(n // tm,),
        in_specs=[pl.BlockSpec((tm, D_MODEL), row), pl.BlockSpec((tm, D_MODEL), row),
                  pl.BlockSpec((1, D_MODEL), fix), pl.BlockSpec((1, D_MODEL), fix)],
        out_specs=pl.BlockSpec((tm, D_MODEL), row),
        out_shape=jax.ShapeDtypeStruct((n, D_MODEL), F32),
        compiler_params=_cparams("parallel"),
        name="residual_ln2",
    )(x1, f, g, b)


def _post_mixer(o_nsa, o_diff, x2d, p, s_tile, chunk):
    x1, x1b, _, sel_t, gate_t, _, _, _ = _out_proj_ln_router(o_nsa, o_diff, x2d, p["wo_a"], p["wo_b"], p["ln1_g"],
                                                             p["ln1_b"], p["wr_t"], p["br"])
    pos_t, counts = _moe_positions(sel_t, s_tile)
    f = _moe(x1b, pos_t, gate_t, counts, p["wgu"], p["wd"], p["p_mat"], p["bg"], p["bl"], p["bd"], s_tile, chunk)
    return _residual_ln(x1, f, p["ln2_g"], p["ln2_b"])


def _moe_slots_kernel(sel_ref, tope_ref, pstart_ref, u_ref, slot_ref, carry_ref):
    @pl.when(pl.program_id(0) == 0)
    def _():
        carry_ref[...] = jnp.zeros_like(carry_ref)

    sel = sel_ref[...]
    slot_all = pstart_ref[...] + carry_ref[...] + _dot(sel.astype(BF16), u_ref[...])
    eidx = lax.broadcasted_iota(jnp.int32, sel.shape, 0).astype(F32)
    tope = tope_ref[...]
    rows = [jnp.sum(jnp.where(eidx == tope[k:k + 1], slot_all, 0.0), axis=0, keepdims=True) for k in range(TOP_K)]
    rows += [jnp.zeros_like(rows[0])] * (8 - TOP_K)
    slot_ref[...] = jnp.concatenate(rows, axis=0).astype(jnp.int32)
    carry_ref[...] += jnp.sum(sel, axis=1, keepdims=True)


def _moe_slots(sel_t, tope, pstart, tm=2048):
    n = sel_t.shape[1]
    upper = jnp.asarray(np.triu(np.ones((tm, tm), np.float32), 1), BF16)
    return pl.pallas_call(
        _moe_slots_kernel,
        grid=(n // tm,),
        in_specs=[pl.BlockSpec((N_EXPERTS, tm), lambda i: (0, i)), pl.BlockSpec((8, tm), lambda i: (0, i)),
                  pl.BlockSpec((N_EXPERTS, 1), lambda i: (0, 0)), pl.BlockSpec((tm, tm), lambda i: (0, 0))],
        out_specs=pl.BlockSpec((8, tm), lambda i: (0, i)),
        out_shape=jax.ShapeDtypeStruct((8, n), jnp.int32),
        scratch_shapes=[pltpu.VMEM((N_EXPERTS, 1), F32)],
        compiler_params=_cparams("arbitrary"),
        name="moe_slots",
    )(sel_t, tope, pstart, upper)


def _sc_mesh():
    return plsc.VectorSubcoreMesh(core_axis_name="c", subcore_axis_name="s", num_cores=SC_CORES,
                                  num_subcores=SC_SUBCORES)


def _sc_scatter_rows(src, idx, n_out):
    n_rows, width = src.shape
    n_idx = idx.shape[0]
    per_core = n_rows // SC_WINDOW // SC_CORES

    @functools.partial(pl.kernel, out_type=jax.ShapeDtypeStruct((n_out, width), src.dtype), mesh=_sc_mesh(),
                       scratch_types=[], name="moe_dispatch_rows")
    def scatter(src_hbm, idx_hbm, out_hbm):
        def body(src_vmem, idx_vmem):
            for k in range(n_idx):
                pltpu.sync_copy(src_vmem, out_hbm.at[idx_vmem.at[k]])

        pltpu.emit_pipeline(
            body,
            grid=(SC_CORES, per_core),
            in_specs=[pl.BlockSpec((SC_WINDOW, width), lambda c, j: (c * per_core + j, 0)),
                      pl.BlockSpec((n_idx, SC_WINDOW), lambda c, j: (0, c * per_core + j))],
            out_specs=[],
            core_axis_name=("c", "s"),
            dimension_semantics=(pltpu.PARALLEL, pltpu.PARALLEL),
        )(src_hbm, idx_hbm)

    return scatter(src, idx)


def _sc_gather_rows(table, idx):
    n_rows = idx.shape[1]
    width = table.shape[1]
    per_core = n_rows // SC_WINDOW // SC_CORES

    @functools.partial(pl.kernel, out_type=jax.ShapeDtypeStruct((n_rows, width), table.dtype), mesh=_sc_mesh(),
                       scratch_types=[], name="moe_return_rows")
    def gather(table_hbm, idx_hbm, out_hbm):
        def body(idx_vmem, out_vmem):
            pltpu.sync_copy(table_hbm.at[idx_vmem.at[0]], out_vmem)

        pltpu.emit_pipeline(
            body,
            grid=(SC_CORES, per_core),
            in_specs=[pl.BlockSpec((1, SC_WINDOW), lambda c, j: (0, c * per_core + j))],
            out_specs=[pl.BlockSpec((SC_WINDOW, width), lambda c, j: (c * per_core + j, 0))],
            core_axis_name=("c", "s"),
            dimension_semantics=(pltpu.PARALLEL, pltpu.PARALLEL),
        )(idx_hbm, out_hbm)

    return gather(table, idx)


def _moe_ffn_kernel(blk_e_ref, nvalid_ref, xs_ref, wgu_ref, wd_ref, p_ref, bg_ref, bl_ref, bd_ref, o_ref,
                    wg_s, wl_s, wd_s):
    i = pl.program_id(0)
    nv = nvalid_ref[i]
    new_expert = (i == 0) | (blk_e_ref[i] != blk_e_ref[jnp.maximum(i - 1, 0)])

    @pl.when(new_expert & (nv > 0))
    def _():
        _load_expert_weights(wgu_ref, wd_ref, p_ref, wg_s, wl_s, wd_s)

    @pl.when(nv > 0)
    def _():
        live = lax.broadcasted_iota(jnp.int32, (o_ref.shape[1], 1), 0) < nv
        xs = jnp.where(live, _unpack_rows(xs_ref[0], xs_ref[1]), 0.0).astype(BF16)
        hg = jnp.minimum(_dot(xs, wg_s[...]) + bg_ref[0], SWIGLU_LIMIT)
        hl = jnp.clip(_dot(xs, wl_s[...]) + bl_ref[0], -SWIGLU_LIMIT, SWIGLU_LIMIT)
        a = (hl + 1.0) * hg * jax.nn.sigmoid(SWIGLU_ALPHA * hg)
        _pack_rows(_dot(a.astype(BF16), wd_s[...]) + bd_ref[0], o_ref)

    @pl.when(nv == 0)
    def _():
        o_ref[...] = jnp.zeros_like(o_ref)


def _moe_ffn(blk_e, nvalid, xs, wgu, wd, p_mat, bg, bl, bd):
    _, n_slots, half = xs.shape
    wspec = lambda a: pl.BlockSpec((1,) + a.shape[1:], lambda i, be, nv: (be[i], 0, 0))
    bspec = pl.BlockSpec((1, 1, D_MODEL), lambda i, be, nv: (be[i], 0, 0))
    rows = pl.BlockSpec((2, MOE_CHUNK, half), lambda i, be, nv: (0, i, 0))
    return pl.pallas_call(
        _moe_ffn_kernel,
        grid_spec=pltpu.PrefetchScalarGridSpec(
            num_scalar_prefetch=2,
            grid=(n_slots // MOE_CHUNK,),
            in_specs=[rows, wspec(wgu), wspec(wd), pl.BlockSpec(p_mat.shape, lambda i, be, nv: (0, 0)),
                      bspec, bspec, bspec],
            out_specs=rows,
            scratch_shapes=_EXPERT_WEIGHT_SCRATCH),
        out_shape=jax.ShapeDtypeStruct((2, n_slots, half), jnp.uint32),
        compiler_params=_cparams("arbitrary"),
        name="moe_ffn_sorted",
    )(blk_e, nvalid, xs, wgu, wd, p_mat, bg, bl, bd)


def _ln2_combine_kernel(x_ref, r_ref, gate_ref, g_ref, b_ref, o_ref):
    gate = gate_ref[...]
    f = gate[:, 0:1] * _unpack_rows(r_ref[0, 0], r_ref[0, 1])
    for k in range(1, TOP_K):
        f = f + gate[:, k:k + 1] * _unpack_rows(r_ref[k, 0], r_ref[k, 1])
    o_ref[...] = _layer_norm(DEEPNORM_ALPHA * x_ref[...] + f, g_ref[...], b_ref[...])


def _ln2_combine(x1, returned, gate_tok, g, b, tm=256):
    n = x1.shape[0]
    row = lambda i: (i, 0)
    fix = lambda i: (0, 0)
    return pl.pallas_call(
        _ln2_combine_kernel,
        grid=(n // tm,),
        in_specs=[pl.BlockSpec((tm, D_MODEL), row),
                  pl.BlockSpec((TOP_K, 2, tm, D_MODEL // 4), lambda i: (0, 0, i, 0)),
                  pl.BlockSpec((tm, 8), row), pl.BlockSpec((1, D_MODEL), fix), pl.BlockSpec((1, D_MODEL), fix)],
        out_specs=pl.BlockSpec((tm, D_MODEL), row),
        out_shape=jax.ShapeDtypeStruct((n, D_MODEL), F32),
        compiler_params=_cparams("parallel"),
        name="combine_ln2",
    )(x1, returned, gate_tok, g, b)


def _post_mixer_sorted(o_nsa, o_diff, x2d, p):
    n = x2d.shape[0]
    x1, _, x1p, sel_t, _, tope, gate8, cnt = _out_proj_ln_router(o_nsa, o_diff, x2d, p["wo_a"], p["wo_b"], p["ln1_g"],
                                                                 p["ln1_b"], p["wr_t"], p["br"])
    counts = jnp.sum(cnt[:, :, 0], axis=0).astype(jnp.int32)
    padded = (counts + (MOE_CHUNK - 1)) // MOE_CHUNK * MOE_CHUNK
    pend = jnp.cumsum(padded)
    pstart = pend - padded
    n_blocks = n * TOP_K // MOE_CHUNK + N_EXPERTS
    n_slots = n_blocks * MOE_CHUNK
    blk0 = jnp.arange(n_blocks, dtype=jnp.int32)[:, None] * MOE_CHUNK
    blk_e = jnp.minimum(jnp.sum((blk0 >= pend[None, :]).astype(jnp.int32), axis=1), N_EXPERTS - 1)
    in_region = (blk0 >= pstart[None, :]) & (blk0 < pend[None, :])
    nvalid = jnp.sum(jnp.where(in_region, jnp.clip(pstart + counts - blk0, 0, MOE_CHUNK), 0), axis=1)
    slot = _moe_slots(sel_t, tope, pstart.astype(F32)[:, None])[:TOP_K]
    half = D_MODEL // 4
    slot2 = jnp.concatenate([slot, slot + n_slots], axis=1)
    xs = _sc_scatter_rows(x1p.reshape(2 * n, half), slot2, 2 * n_slots)
    outs = _moe_ffn(blk_e, nvalid, xs.reshape(2, n_slots, half), p["wgu"], p["wd"], p["p_mat"], p["bg"], p["bl"],
                    p["bd"])
    returned = _sc_gather_rows(outs.reshape(2 * n_slots, half), slot2.reshape(1, TOP_K * 2 * n))
    return _ln2_combine(x1, returned.reshape(TOP_K, 2, n, half), gate8.T, p["ln2_g"], p["ln2_b"])


def _sample_kernel(pt_ref, qm_ref, qd_ref, gt_ref, nrow_ref, wrow_ref, wcol_ref, drow_ref, win_ref,
                   w1_ref, pe_ref, w2_ref, m_ref, e_ref, r_ref, sn_ref, sd_ref, lam_ref, subg_ref, *rest,
                   n_pages, past_len, lam_init):
    nsa_pages = rest[:n_pages]
    diff_pages = rest[n_pages:2 * n_pages]
    onsa_ref, odiff_ref, nwin_ref, x_ref = rest[2 * n_pages:]
    n16 = past_len // CMP_STRIDE
    qpos = past_len
    kpos = lax.broadcasted_iota(jnp.int32, (1, past_len), 1)
    dist_k = (qpos - kpos).astype(F32)

    for j, pg in enumerate(nsa_pages):
        for kind in range(2):
            x_ref[kind, j * PAGE_SIZE:(j + 1) * PAGE_SIZE, :] = pg[kind * LANES:(kind + 1) * LANES, :].T

    kv_cmp = [_compress_chunks(lambda l, kind=kind: x_ref[kind, pl.ds(l, n16, stride=CMP_STRIDE), :],
                               w1_ref[kind], pe_ref[kind], w2_ref[kind], n16) for kind in range(2)]

    qm = qm_ref[0]
    qmb = qm.astype(BF16)
    slope_n = sn_ref[:, 0:1]
    nrow = nrow_ref[0]
    wrow = wrow_ref[0]
    gt = gt_ref[0]

    def new_key_score(qrows, krow):
        return jnp.sum(qrows * krow, axis=1, keepdims=True) * ATTN_SCALE

    cmp_end = lax.broadcasted_iota(jnp.int32, (1, n16), 1) * CMP_STRIDE + (CMP_LEN - 1)
    dist_c = qpos - cmp_end
    s = _dot_nt(qmb, kv_cmp[0].astype(BF16)) * ATTN_SCALE - slope_n * dist_c.astype(F32)
    e, d = _softmax_parts(s, dist_c >= 0)
    p_cmp = e / d
    o_cmp = _dot(p_cmp.astype(BF16), kv_cmp[1].astype(BF16))
    pgrp = jnp.dot(r_ref[...], p_cmp, precision=HIGHEST, preferred_element_type=F32)
    imp = jnp.dot(pgrp, m_ref[...], precision=HIGHEST, preferred_element_type=F32)
    nbl = m_ref.shape[1]
    blk = lax.broadcasted_iota(jnp.int32, (NSA_HEADS, nbl), 1)
    sel = _select_blocks(imp, blk, jnp.full((NSA_HEADS, 1), qpos, jnp.int32), past_len // SEL_BLOCK + 1)
    mask_s = _dot(sel.astype(BF16), e_ref[...]) > 0.5

    _sample_diff_branch(qd_ref, drow_ref, sd_ref, lam_ref, subg_ref, diff_pages, odiff_ref, dist_k, lam_init)

    s = jnp.concatenate([_dot(qmb, pg[2 * LANES:3 * LANES, :].astype(BF16)) for pg in nsa_pages], axis=1)
    s = jnp.where(mask_s, s * ATTN_SCALE - slope_n * dist_k, -jnp.inf)
    s_new = new_key_score(qm, nrow[:, 2 * LANES:3 * LANES])
    m = jnp.maximum(jnp.max(s, axis=1, keepdims=True), s_new)
    e = jnp.exp(s - m)
    e_new = jnp.exp(s_new - m)
    d = jnp.sum(e, axis=1, keepdims=True) + e_new
    eb = e.astype(BF16)
    acc = e_new * nrow[:, 3 * LANES:4 * LANES]
    for j, pg in enumerate(nsa_pages):
        acc = acc + _dot_nt(eb[:, j * PAGE_SIZE:(j + 1) * PAGE_SIZE], pg[3 * LANES:4 * LANES, :].astype(BF16))
    o_sel = acc / d

    nwin = win_ref.shape[2]
    wpos = past_len - nwin + lax.broadcasted_iota(jnp.int32, (1, nwin), 1)
    dist_w = qpos - wpos
    mask_w = (dist_w >= 0) & (dist_w <= WINDOW)
    win = win_ref[0]
    s = _dot(qmb, win[0:LANES, :].astype(BF16)) * ATTN_SCALE - slope_n * dist_w.astype(F32)
    s = jnp.where(mask_w, s, -jnp.inf)
    s_new = new_key_score(qm, wrow[:, 0:LANES])
    m = jnp.maximum(jnp.max(s, axis=1, keepdims=True), s_new)
    e = jnp.exp(s - m)
    e_new = jnp.exp(s_new - m)
    d = jnp.sum(e, axis=1, keepdims=True) + e_new
    o_win = (_dot_nt(e.astype(BF16), win[LANES:2 * LANES, :].astype(BF16)) + e_new * wrow[:, LANES:2 * LANES]) / d

    onsa_ref[0] = gt[:, 0:1] * o_cmp + gt[:, 1:2] * o_sel + gt[:, 2:3] * o_win

    lane = lax.broadcasted_iota(jnp.int32, win.shape, 1)
    nwin_ref[0] = jnp.where(lane == nwin - 1, wcol_ref[0], pltpu.roll(win, nwin - 1, 1))


def _sample_diff_branch(qd_ref, drow_ref, sd_ref, lam_ref, subg_ref, diff_pages, odiff_ref, dist_k, lam_init):
    drow = drow_ref[0]

    def diff_rows(pg, j):
        return pg[pl.ds(j, PAGE_SIZE, stride=DIFF_ROWS), :]

    qds = [qd_ref[0, h] for h in range(DIFF_HEADS)]
    qdb = [qh.astype(BF16) for qh in qds]
    s_pages = []
    for pg in diff_pages:
        sp = _dot_nt(qdb[0], diff_rows(pg, 0).astype(BF16))
        for h in range(1, DIFF_HEADS):
            sp = sp + _dot_nt(qdb[h], diff_rows(pg, h).astype(BF16))
        s_pages.append(sp)
    s = jnp.concatenate(s_pages, axis=1) * ATTN_SCALE - sd_ref[:, 0:1] * dist_k
    prod = qds[0] * drow[0:1]
    for h in range(1, DIFF_HEADS):
        prod = prod + qds[h] * drow[h:h + 1]
    s_new = jnp.sum(prod, axis=1, keepdims=True) * ATTN_SCALE
    m = jnp.maximum(jnp.max(s, axis=1, keepdims=True), s_new)
    e = jnp.exp(s - m)
    e_new = jnp.exp(s_new - m)
    d = jnp.sum(e, axis=1, keepdims=True) + e_new
    lam = lam_ref[0:1, 0:1]
    p = e / d
    p_new = e_new / d
    a = (p - lam * pltpu.roll(p, DIFF_HEADS, 0)).astype(BF16)
    a_new = p_new - lam * pltpu.roll(p_new, DIFF_HEADS, 0)
    accs = []
    for h in range(DIFF_HEADS):
        acc = a_new * drow[DIFF_HEADS + h:DIFF_HEADS + h + 1]
        for j, pg in enumerate(diff_pages):
            acc = acc + _dot(a[:, j * PAGE_SIZE:(j + 1) * PAGE_SIZE], diff_rows(pg, DIFF_HEADS + h).astype(BF16))
        accs.append(acc)
    acc = jnp.concatenate(accs, axis=1)
    lane_head = lax.broadcasted_iota(jnp.int32, acc.shape, 1) // DIFF_VDIM
    row = lax.broadcasted_iota(jnp.int32, acc.shape, 0)
    own = lane_head == row
    ms = jnp.sum(jnp.where(own, jnp.square(acc), 0.0), axis=1, keepdims=True) / DIFF_VDIM
    odiff_ref[0] = acc * lax.rsqrt(ms + RMS_EPS) * subg_ref[...] * (1.0 - lam_init)


def _sample_attention(page_table, qm, qd4, gt8, nrow, wrow, wcol, drow, win_t, pool_nsa, pool_diff,
                      w1ab, pe2, w2bd, m_mat, e_mat, r_mat, sn, sd, lam_tile, subg_tile, past_len, lam_init):
    B = qm.shape[0]
    n_pages = page_table.shape[1]
    nwin = win_t.shape[2]

    def per_b(shape):
        nd = len(shape)
        return pl.BlockSpec((1,) + shape, lambda b, pt: (b,) + (0,) * nd)

    def fixed(shape):
        nd = len(shape)
        return pl.BlockSpec(shape, lambda b, pt: (0,) * nd)

    def page_spec(pool, j):
        return pl.BlockSpec((None,) + pool.shape[1:], lambda b, pt: (pt[b, j], 0, 0))

    in_specs = [per_b((NSA_HEADS, LANES)), per_b((DIFF_HEADS, 8, LANES)), per_b((NSA_HEADS, LANES)),
                per_b((1, NSA_W)), per_b((1, WIN_W)), per_b((WIN_W, 1)), per_b((DIFF_ROWS, LANES)),
                per_b((WIN_W, nwin)),
                fixed(w1ab.shape), fixed(pe2.shape), fixed(w2bd.shape), fixed(m_mat.shape), fixed(e_mat.shape),
                fixed(r_mat.shape), fixed(sn.shape), fixed(sd.shape), fixed(lam_tile.shape), fixed(subg_tile.shape)]
    in_specs += [page_spec(pool_nsa, j) for j in range(n_pages)]
    in_specs += [page_spec(pool_diff, j) for j in range(n_pages)]
    return pl.pallas_call(
        functools.partial(_sample_kernel, n_pages=n_pages, past_len=past_len, lam_init=lam_init),
        grid_spec=pltpu.PrefetchScalarGridSpec(
            num_scalar_prefetch=1,
            grid=(B,),
            in_specs=in_specs,
            out_specs=(per_b((NSA_HEADS, LANES)), per_b((8, DIFF_HEADS * DIFF_VDIM)), per_b((WIN_W, nwin))),
            scratch_shapes=[pltpu.VMEM((2, past_len, LANES), F32)]),
        out_shape=(jax.ShapeDtypeStruct((B, NSA_HEADS, LANES), F32),
                   jax.ShapeDtypeStruct((B, 8, DIFF_HEADS * DIFF_VDIM), F32),
                   jax.ShapeDtypeStruct((B, WIN_W, nwin), F32)),
        compiler_params=_cparams("arbitrary"),
        name="sample_attention",
    )(page_table, qm, qd4, gt8, nrow, wrow, wcol, drow, win_t, w1ab, pe2, w2bd, m_mat, e_mat, r_mat, sn, sd,
      lam_tile, subg_tile, *([pool_nsa] * n_pages), *([pool_diff] * n_pages))


def _cmp_to_sel(n_cmp, n_sel):
    c0 = np.arange(n_cmp)[:, None] * CMP_STRIDE
    s0 = np.arange(n_sel)[None, :] * SEL_BLOCK
    ov = np.clip(np.minimum(c0 + CMP_LEN, s0 + SEL_BLOCK) - np.maximum(c0, s0), 0, None)
    return (ov / CMP_LEN).astype(np.float32)


def _lambda_init(layer):
    return 0.8 - 0.6 * math.exp(-0.3 * layer)


def _prep_params(l, w_in, w_out, diff_subln_g, ln1_g, ln1_b, ln2_g, ln2_b, w_router, b_router,
                 w_gate_up, b_gate_up, w_down, b_down):
    gate_end = _GT0 + 3 * NSA_HEADS
    wt = w_in[l].T
    w_t = jnp.concatenate([wt[:gate_end], jnp.zeros((_GATE_PAD, D_MODEL), F32), wt[gate_end:]], axis=0).astype(BF16)
    bgu = b_gate_up[l]
    return {
        "w_t": w_t,
        "wo_a": w_out[l][:512].astype(BF16), "wo_b": w_out[l][512:].astype(BF16),
        "ln1_g": ln1_g[l][None], "ln1_b": ln1_b[l][None], "ln2_g": ln2_g[l][None], "ln2_b": ln2_b[l][None],
        "wr_t": w_router[l].T, "br": b_router[l][:, None],
        "wgu": w_gate_up[l], "wd": w_down[l], "p_mat": _deinterleave_matrix(),
        "bg": bgu[:, None, 0::2], "bl": bgu[:, None, 1::2], "bd": b_down[l][:, None, :],
        "subln_g": diff_subln_g[l][None],
    }


def _cmp_params(cmp_pe, cmp_w1, cmp_w2):
    eye = jnp.eye(NSA_KV_HEADS, dtype=F32)
    halves = []
    for half in range(2):
        w = cmp_w1[:, half * CMP_STRIDE:(half + 1) * CMP_STRIDE]
        wb = jnp.einsum('klde,gh->klgdhe', w, eye)
        halves.append(wb.reshape(2, CMP_STRIDE * LANES, LANES))
    w1ab = jnp.concatenate(halves, axis=-1).astype(BF16)
    pe = cmp_pe.reshape(2, 2, CMP_STRIDE, 1, HEAD_DIM)
    pe2 = jnp.broadcast_to(pe, (2, 2, CMP_STRIDE, NSA_KV_HEADS, HEAD_DIM)).reshape(2, 2, 1, CMP_STRIDE * LANES)
    pe2 = jnp.broadcast_to(pe2, (2, 2, 8, CMP_STRIDE * LANES))
    w2bd = jnp.einsum('kde,gh->kgdhe', cmp_w2, eye).reshape(2, LANES, LANES).astype(BF16)
    return w1ab, pe2, w2bd


def _feature_major_to_rows(a_t, kinds):
    B, _, T = a_t.shape
    return a_t.reshape(B, kinds, NSA_KV_HEADS, HEAD_DIM, T).transpose(0, 4, 1, 2, 3)


def _prompt_group(x, p, cmp_pe, cmp_w1, cmp_w2, lam_tile, lam_init):
    B, T, _ = x.shape
    x2d = x.reshape(B * T, D_MODEL)
    q_nsa, q_tb, nsa_rows, nsa_t, nsa_tb, _, win_t, win_tb, gates, dq, diff8, diff_b = _in_proj(x2d, p["w_t"], B)
    r3 = lambda a: a.reshape(B, T, a.shape[-1])
    q_nsa, nsa_rows3, gates, dq = map(r3, (q_nsa, nsa_rows, gates, dq))
    kvc = _compress_prompt(nsa_rows3, *_cmp_params(cmp_pe, cmp_w1, cmp_w2))
    n16 = T // CMP_STRIDE
    n_sel = -(-T // SEL_BLOCK)
    m_np = np.zeros((n16, n_sel), np.float32)
    m_np[:n16 - 1] = _cmp_to_sel(n16 - 1, n_sel)
    o_cmp, sel = _cmp_attn_prompt(q_nsa, q_tb, kvc, jnp.asarray(m_np.T))
    e_np = np.zeros((NSA_KV_HEADS, LANES, T), np.float32)
    for g in range(NSA_KV_HEADS):
        e_np[g, g * n_sel + np.arange(T) // SEL_BLOCK, np.arange(T)] = 1.0
    o_nsa = _sel_win_prompt(q_nsa, nsa_tb, win_tb, sel, jnp.asarray(e_np, BF16), o_cmp, gates)
    o_diff = _diff_prompt(lam_tile, dq, diff_b.reshape(B, T, DIFF_W), p["subln_g"], lam_init)
    y = _post_mixer_sorted(o_nsa.reshape(B * T, 512), o_diff.reshape(B * T, 512), x2d, p)
    nwin = min(WINDOW, T)
    return (y.reshape(B, T, D_MODEL),
            _feature_major_to_rows(nsa_t, 4),
            diff8.reshape(B, T, 2, DIFF_HEADS, DIFF_VDIM),
            _feature_major_to_rows(win_t[:, :, T - nwin:], 2))


def _sample_group(x, pool_nsa, pool_diff, win_buf, page_table, p, cmp_pe, cmp_w1, cmp_w2, lam_tile, lam_init):
    B, T, _ = x.shape
    past_len = page_table.shape[1] * PAGE_SIZE
    x2d = x.reshape(B, D_MODEL)
    q_nsa, _, nsa_rows, nsa_t, _, win_rows, _, _, gates, dq, diff8, _ = _in_proj(x2d, p["w_t"], 1)
    qh = q_nsa.reshape(B, NSA_HEADS, HEAD_DIM)
    grp = (np.arange(NSA_HEADS) // NSA_GROUP)[None, :, None, None] == np.arange(NSA_KV_HEADS)[None, None, :, None]
    qm = (qh[:, :, None, :] * jnp.asarray(grp, F32)).reshape(B, NSA_HEADS, LANES)
    dq4 = dq.reshape(B, DIFF_HEADS, 1, 2 * HEAD_DIM)
    rr = np.arange(8)
    rowmask = (rr[None, :, None] % DIFF_HEADS == np.arange(DIFF_HEADS)[:, None, None]) & (
        rr[None, :, None] // DIFF_HEADS == (np.arange(LANES) // HEAD_DIM)[None, None, :])
    qd4 = dq4 * jnp.asarray(rowmask, F32)[None]
    gt8 = jnp.pad(gates[:, :3 * NSA_HEADS].reshape(B, 3, NSA_HEADS).transpose(0, 2, 1), ((0, 0), (0, 0), (0, LANES - 3)))
    nwin = win_buf.shape[1]
    win_t = win_buf.transpose(0, 2, 3, 4, 1).reshape(B, WIN_W, nwin)
    n_pool = pool_nsa.shape[0]
    pool_nsa_t = pool_nsa.transpose(0, 2, 3, 4, 1).reshape(n_pool, NSA_W, PAGE_SIZE)
    pool_diff_r = pool_diff.reshape(n_pool, PAGE_SIZE * DIFF_ROWS, LANES)
    w1ab, pe2, w2bd = _cmp_params(cmp_pe, cmp_w1, cmp_w2)
    n16 = past_len // CMP_STRIDE
    n_sel = past_len // SEL_BLOCK + 1
    m_np = np.zeros((n16, 64), np.float32)
    m_np[:n16 - 1, :n_sel] = _cmp_to_sel(n16 - 1, n_sel)
    e_np = np.zeros((64, past_len), np.float32)
    e_np[np.arange(past_len) // SEL_BLOCK, np.arange(past_len)] = 1.0
    r_np = (np.arange(8)[:, None] // NSA_GROUP == np.arange(8)[None, :] // NSA_GROUP).astype(np.float32)
    sn = jnp.asarray(np.tile(np.asarray(NSA_SLOPES, np.float32)[:, None], (1, LANES)))
    sd = jnp.asarray(np.tile(np.asarray(DIFF_SLOPES, np.float32)[np.arange(8) % DIFF_HEADS, None], (1, LANES)))
    subg_tile = jnp.tile(p["subln_g"], (1, DIFF_HEADS))
    o_nsa8, o_diff8, new_win_t = _sample_attention(
        page_table, qm, qd4, gt8, nsa_rows[:, None, :], win_rows[:, None, :], win_rows[:, :, None],
        diff8.reshape(B, DIFF_ROWS, LANES), win_t, pool_nsa_t, pool_diff_r,
        w1ab, pe2, w2bd, jnp.asarray(m_np), jnp.asarray(e_np, BF16), jnp.asarray(r_np), sn, sd, lam_tile, subg_tile,
        past_len, lam_init)
    o8 = o_nsa8.reshape(B, NSA_KV_HEADS, NSA_GROUP, NSA_KV_HEADS, HEAD_DIM)
    o_nsa = jnp.stack([o8[:, g, :, g] for g in range(NSA_KV_HEADS)], axis=1).reshape(B, 512)
    d8 = o_diff8[:, :DIFF_HEADS].reshape(B, DIFF_HEADS, DIFF_HEADS, DIFF_VDIM)
    o_diff = jnp.stack([d8[:, h, h] for h in range(DIFF_HEADS)], axis=1).reshape(B, 512)
    y = _post_mixer(o_nsa, o_diff, x2d, p, s_tile=B, chunk=B)
    return (y.reshape(B, T, D_MODEL),
            _feature_major_to_rows(nsa_t, 4).reshape(B, T, 4, NSA_KV_HEADS, HEAD_DIM),
            diff8.reshape(B, T, 2, DIFF_HEADS, DIFF_VDIM),
            _feature_major_to_rows(new_win_t, 2))


def kernel(x_prompt, x_sample, cache_nsa_kv, cache_diff_kv, state_nsa_win, page_table, w_in, w_out, cmp_pe, cmp_w1,
           cmp_w2, diff_lambda, diff_subln_g, ln1_g, ln1_b, ln2_g, ln2_b, w_router, b_router, w_gate_up, b_gate_up,
           w_down, b_down):
    depth = w_in.shape[0]
    xp, xs = x_prompt, x_sample
    outs = [[] for _ in range(6)]
    for l in range(depth):
        lam0 = _lambda_init(l)
        lv = diff_lambda[l].astype(F32)
        lam = jnp.exp(jnp.sum(lv[0] * lv[1])) - jnp.exp(jnp.sum(lv[2] * lv[3])) + lam0
        lam_tile = jnp.full((8, LANES), lam, F32)
        p = _prep_params(l, w_in, w_out, diff_subln_g, ln1_g, ln1_b, ln2_g, ln2_b, w_router, b_router,
                         w_gate_up, b_gate_up, w_down, b_down)
        xp, r_nsa, r_diff, r_win = _prompt_group(xp, p, cmp_pe[l], cmp_w1[l], cmp_w2[l], lam_tile, lam0)
        xs, s_nsa, s_diff, s_win = _sample_group(xs, cache_nsa_kv[l], cache_diff_kv[l], state_nsa_win[l], page_table,
                                                 p, cmp_pe[l], cmp_w1[l], cmp_w2[l], lam_tile, lam0)
        for lst, v in zip(outs, (r_nsa, r_diff, r_win, s_nsa, s_diff, s_win)):
            lst.append(v)
    return (xp, xs) + tuple(jnp.stack(o) for o in outs)
```

```python
import functools
import math

import numpy as np
import jax
import jax.numpy as jnp
from jax import lax
from jax.experimental import pallas as pl
from jax.experimental.pallas import tpu as pltpu
from jax.experimental.pallas import tpu_sc as plsc

F32 = jnp.float32
BF16 = jnp.bfloat16
HIGHEST = lax.Precision.HIGHEST

D_MODEL = 1024
HEAD_DIM = 64
NSA_HEADS = 8
NSA_KV_HEADS = 2
NSA_GROUP = NSA_HEADS // NSA_KV_HEADS
CMP_LEN = 32
CMP_STRIDE = 16
SEL_BLOCK = 64
SEL_TOPN = 16
WINDOW = 512
DIFF_HEADS = 4
DIFF_VDIM = 2 * HEAD_DIM
N_EXPERTS = 32
TOP_K = 4
SWIGLU_ALPHA = 1.702
SWIGLU_LIMIT = 7.0
LN_EPS = 1e-5
RMS_EPS = 1e-5
ATTN_SCALE = HEAD_DIM ** -0.5
LOG2E = math.log2(math.e)
DEPTH = 1
DEEPNORM_ALPHA = (2 * DEPTH) ** 0.25
PAGE_SIZE = 128

NSA_SLOPES = tuple(2.0 ** (-8.0 * (i + 1) / NSA_HEADS) for i in range(NSA_HEADS))
DIFF_SLOPES = tuple(2.0 ** (-8.0 * (i + 1) / DIFF_HEADS) for i in range(DIFF_HEADS))

VMEM_LIMIT_BYTES = 56 * 1024 * 1024
LANES = 128
MXU_DIM = 256

MOE_CHUNK = 2 * MXU_DIM
MOE_SUPERTILE = 13 * LANES
SC_CORES = 2
SC_SUBCORES = 16
SC_WINDOW = 128

NSA_W = 4 * NSA_KV_HEADS * HEAD_DIM
WIN_W = 2 * NSA_KV_HEADS * HEAD_DIM
DIFF_W = 2 * DIFF_HEADS * DIFF_VDIM
DIFF_ROWS = DIFF_W // LANES
_GATE_PAD = LANES - 3 * NSA_HEADS
_Q0, _NSA0, _WIN0, _GT0, _DQ0, _DIFF0, _PROJ_ROWS = 0, 512, 1024, 1280, 1408, 1920, 2944


def _cparams(*sem):
    return pltpu.CompilerParams(dimension_semantics=sem, vmem_limit_bytes=VMEM_LIMIT_BYTES)


def _softmax_parts(s, mask):
    s = jnp.where(mask, s, -jnp.inf)
    m = jnp.max(s, axis=-1, keepdims=True)
    m = jnp.where(jnp.isfinite(m), m, 0.0)
    e = jnp.exp(s - m)
    d = jnp.sum(e, axis=-1, keepdims=True)
    return e, jnp.where(d > 0, d, 1.0)


def _softmax2_parts(s, mask):
    s = jnp.where(mask, s, -jnp.inf)
    m = jnp.max(s, axis=-1, keepdims=True)
    m = jnp.where(jnp.isfinite(m), m, 0.0)
    e = jnp.exp2(s - m)
    d = jnp.sum(e, axis=-1, keepdims=True)
    return e, jnp.where(d > 0, d, 1.0)


def _dot_nt(a, b):
    return lax.dot_general(a, b, (((1,), (1,)), ((), ())), preferred_element_type=F32)


def _dot(a, b):
    return jnp.dot(a, b, preferred_element_type=F32)


def _pad_head(qh, g):
    z = jnp.zeros_like(qh)
    return jnp.concatenate([qh, z] if g == 0 else [z, qh], axis=1)


def _select_blocks(imp, blk, qpos, n_blk_lanes):
    cur = jnp.right_shift(qpos, int(math.log2(SEL_BLOCK)))
    valid = blk * SEL_BLOCK <= qpos
    forced = (blk == 0) | (blk == cur) | (blk == cur - 1)
    val = jnp.where(forced, jnp.inf, jnp.where(valid, imp, -jnp.inf))
    rank = jnp.zeros(val.shape, F32)
    for i in range(n_blk_lanes):
        ci = val[:, i:i + 1]
        beats = (ci > val) | ((ci == val) & (blk > i))
        rank = rank + jnp.where(beats, 1.0, 0.0)
    return jnp.where(rank < SEL_TOPN, 1.0, 0.0)


def _in_proj_kernel(x_ref, w_ref, q_ref, qtb_ref, nsa_ref, nsat_ref, nsatb_ref, win_ref, wint_ref, wintb_ref, gt_ref,
                    dq_ref, diff_ref, diffb_ref):
    tm = x_ref.shape[0]
    xb = x_ref[...].astype(BF16)
    q_ref[...] = _dot_nt(xb, w_ref[_Q0:_NSA0, :])
    qtb_ref[0] = _dot_nt(w_ref[_Q0:_NSA0, :], xb).astype(BF16)
    nsa_ref[...] = _dot_nt(xb, w_ref[_NSA0:_NSA0 + nsa_ref.shape[1], :])
    r = _dot_nt(w_ref[_NSA0:_WIN0, :], xb)
    nsat_ref[0] = r
    nsatb_ref[0] = r.astype(BF16)
    win_ref[...] = _dot_nt(xb, w_ref[_WIN0:_WIN0 + win_ref.shape[1], :])
    r = _dot_nt(w_ref[_WIN0:_GT0, :], xb)
    wint_ref[0] = r
    wintb_ref[0] = r.astype(BF16)
    gt_ref[...] = jax.nn.sigmoid(_dot_nt(xb, w_ref[_GT0:_DQ0, :]))
    dq_ref[...] = _dot_nt(xb, w_ref[_DQ0:_DIFF0, :])
    r = _dot_nt(xb, w_ref[_DIFF0:_PROJ_ROWS, :])
    diffb_ref[...] = r.astype(BF16)
    for j in range(DIFF_ROWS):
        diff_ref[pl.ds(j, tm, stride=DIFF_ROWS), :] = r[:, j * LANES:(j + 1) * LANES]


def _in_proj(x2d, w_t, batch, nsa_tok_w=NSA_W, win_tok_w=WIN_W):
    n = x2d.shape[0]
    t = n // batch
    tm = min(512, t)
    nt = t // tm
    row = lambda b, i: (b * nt + i, 0)
    tr = lambda b, i: (b, 0, i)
    tok = lambda w: pl.BlockSpec((tm, w), row)
    return pl.pallas_call(
        _in_proj_kernel,
        grid=(batch, nt),
        in_specs=[pl.BlockSpec((tm, D_MODEL), row),
                  pl.BlockSpec((_PROJ_ROWS, D_MODEL), lambda b, i: (0, 0))],
        out_specs=(tok(512), pl.BlockSpec((1, 512, tm), tr),
                   tok(nsa_tok_w), pl.BlockSpec((1, NSA_W, tm), tr), pl.BlockSpec((1, NSA_W, tm), tr),
                   tok(win_tok_w), pl.BlockSpec((1, WIN_W, tm), tr), pl.BlockSpec((1, WIN_W, tm), tr),
                   tok(LANES), tok(512), pl.BlockSpec((tm * DIFF_ROWS, LANES), row), tok(DIFF_W)),
        out_shape=(jax.ShapeDtypeStruct((n, 512), F32), jax.ShapeDtypeStruct((batch, 512, t), BF16),
                   jax.ShapeDtypeStruct((n, nsa_tok_w), F32),
                   jax.ShapeDtypeStruct((batch, NSA_W, t), F32), jax.ShapeDtypeStruct((batch, NSA_W, t), BF16),
                   jax.ShapeDtypeStruct((n, win_tok_w), F32),
                   jax.ShapeDtypeStruct((batch, WIN_W, t), F32), jax.ShapeDtypeStruct((batch, WIN_W, t), BF16),
                   jax.ShapeDtypeStruct((n, LANES), F32), jax.ShapeDtypeStruct((n, 512), F32),
                   jax.ShapeDtypeStruct((n * DIFF_ROWS, LANES), F32), jax.ShapeDtypeStruct((n, DIFF_W), BF16)),
        compiler_params=_cparams("parallel", "parallel"),
        name="in_proj",
    )(x2d, w_t)


def _compress_chunks(load_rows, w1, pe, w2, n16):
    cst = _dot(pe[0].astype(BF16), w1[:, 0:LANES]) + _dot(pe[1].astype(BF16), w1[:, LANES:])
    chunks = jnp.concatenate([load_rows(l) for l in range(CMP_STRIDE)], axis=1)
    ab = _dot(chunks.astype(BF16), w1)
    nxt = pltpu.roll(ab[:, LANES:], n16 - 1, 0)
    h = jax.nn.gelu(ab[:, 0:LANES] + nxt + cst[0:1])
    return _dot(h.astype(BF16), w2)


def _compress_kernel(rows_ref, w1_ref, pe_ref, w2_ref, o_ref):
    n16 = rows_ref.shape[1] // CMP_STRIDE
    o_ref[0, 0] = _compress_chunks(lambda l: rows_ref[0, pl.ds(l, n16, stride=CMP_STRIDE), :],
                                   w1_ref[0], pe_ref[0], w2_ref[0], n16)


def _compress_prompt(nsa_rows, w1ab, pe2, w2bd):
    B, T, _ = nsa_rows.shape
    n16 = T // CMP_STRIDE
    return pl.pallas_call(
        _compress_kernel,
        grid=(B, 2),
        in_specs=[pl.BlockSpec((1, T, LANES), lambda b, k: (b, 0, k)),
                  pl.BlockSpec((1,) + w1ab.shape[1:], lambda b, k: (k, 0, 0)),
                  pl.BlockSpec((1,) + pe2.shape[1:], lambda b, k: (k, 0, 0, 0)),
                  pl.BlockSpec((1,) + w2bd.shape[1:], lambda b, k: (k, 0, 0))],
        out_specs=pl.BlockSpec((1, 1, n16, LANES), lambda b, k: (b, k, 0, 0)),
        out_shape=jax.ShapeDtypeStruct((B, 2, n16, LANES), F32),
        compiler_params=_cparams("parallel", "parallel"),
        name="compress_prompt",
    )(nsa_rows, w1ab, pe2, w2bd)


def _select_blocks_t(imp, qpos):
    n_blk = imp.shape[0]
    blk = lax.broadcasted_iota(jnp.int32, imp.shape, 0)
    cur = jnp.right_shift(qpos, int(math.log2(SEL_BLOCK)))
    valid = blk * SEL_BLOCK <= qpos
    forced = (blk == 0) | (blk == cur) | (blk == cur - 1)
    val = jnp.where(forced, jnp.inf, jnp.where(valid, imp, -jnp.inf))
    rank = jnp.zeros(val.shape, F32)
    for i in range(n_blk):
        ci = val[i:i + 1, :]
        beats = (ci > val) | ((ci == val) & (blk > i))
        rank = rank + jnp.where(beats, 1.0, 0.0)
    return jnp.where(rank < SEL_TOPN, 1.0, 0.0)


def _cmp_attn_kernel(q_ref, qt_ref, kc_ref, vc_ref, mt_ref, o_ref, selt_ref, *, tq):
    i = pl.program_id(1)
    ncmp = kc_ref.shape[2]
    kcb = kc_ref[0, 0].astype(BF16)
    vcb = vc_ref[0, 0].astype(BF16)
    qpos_c = i * tq + lax.broadcasted_iota(jnp.int32, (tq, 1), 0)
    end_r = lax.broadcasted_iota(jnp.int32, (1, ncmp), 1) * CMP_STRIDE + (CMP_LEN - 1)
    mask = end_r <= qpos_c
    end_rf = end_r.astype(F32)
    q = q_ref[0] * (ATTN_SCALE * LOG2E)
    for h in range(NSA_HEADS):
        g = h // NSA_GROUP
        qp = _pad_head(q[:, h * HEAD_DIM:(h + 1) * HEAD_DIM], g).astype(BF16)
        e, d = _softmax2_parts(_dot_nt(qp, kcb) + (NSA_SLOPES[h] * LOG2E) * end_rf, mask)
        oh = _dot(e.astype(BF16), vcb)[:, g * HEAD_DIM:(g + 1) * HEAD_DIM] / d
        o_ref[0, :, h * HEAD_DIM:(h + 1) * HEAD_DIM] = oh
    qpos_r = i * tq + lax.broadcasted_iota(jnp.int32, (1, tq), 1)
    end_c = lax.broadcasted_iota(jnp.int32, (ncmp, 1), 0) * CMP_STRIDE + (CMP_LEN - 1)
    mask_t = end_c <= qpos_r
    end_cf = end_c.astype(F32)
    qt = qt_ref[0]
    zero = jnp.zeros((HEAD_DIM, tq), BF16)
    psum = [None, None]
    for h in range(NSA_HEADS):
        g = h // NSA_GROUP
        qh = qt[h * HEAD_DIM:(h + 1) * HEAD_DIM, :]
        qpt = jnp.concatenate([qh, zero] if g == 0 else [zero, qh], axis=0)
        s = _dot(kcb, qpt) * (ATTN_SCALE * LOG2E) + (NSA_SLOPES[h] * LOG2E) * end_cf
        s = jnp.where(mask_t, s, -jnp.inf)
        m = jnp.max(s, axis=0, keepdims=True)
        m = jnp.where(jnp.isfinite(m), m, 0.0)
        e = jnp.exp2(s - m)
        d = jnp.sum(e, axis=0, keepdims=True)
        p = e / jnp.where(d > 0, d, 1.0)
        psum[g] = p if psum[g] is None else psum[g] + p
    sels = []
    for g in range(NSA_KV_HEADS):
        imp = jnp.dot(mt_ref[...], psum[g], precision=HIGHEST, preferred_element_type=F32)
        sels.append(_select_blocks_t(imp, qpos_r))
    sels.append(jnp.zeros((LANES - NSA_KV_HEADS * mt_ref.shape[0], tq), F32))
    selt_ref[0] = jnp.concatenate(sels, axis=0)


def _cmp_attn_prompt(q_nsa, q_tb, kvc, mt_mat, tq=512):
    B, T, _ = q_nsa.shape
    ncmp = kvc.shape[2]
    n_sel = mt_mat.shape[0]
    return pl.pallas_call(
        functools.partial(_cmp_attn_kernel, tq=tq),
        grid=(B, T // tq),
        in_specs=[pl.BlockSpec((1, tq, 512), lambda b, i: (b, i, 0)),
                  pl.BlockSpec((1, 512, tq), lambda b, i: (b, 0, i)),
                  pl.BlockSpec((1, 1, ncmp, LANES), lambda b, i: (b, 0, 0, 0)),
                  pl.BlockSpec((1, 1, ncmp, LANES), lambda b, i: (b, 1, 0, 0)),
                  pl.BlockSpec((n_sel, ncmp), lambda b, i: (0, 0))],
        out_specs=(pl.BlockSpec((1, tq, 512), lambda b, i: (b, i, 0)),
                   pl.BlockSpec((1, LANES, tq), lambda b, i: (b, 0, i))),
        out_shape=(jax.ShapeDtypeStruct((B, T, 512), F32), jax.ShapeDtypeStruct((B, LANES, T), F32)),
        compiler_params=_cparams("parallel", "parallel"),
        name="cmp_attn_prompt",
    )(q_nsa, q_tb, kvc, kvc, mt_mat)


def _sel_win_kernel(q_ref, ks_ref, vs_ref, kw_ref, vw_ref, sel_ref, e_ref, ocmp_ref, gt_ref, o_ref, *, tq, span,
                    kv_step):
    i = pl.program_id(1)
    T = ks_ref.shape[2]
    qpos = i * tq + lax.broadcasted_iota(jnp.int32, (tq, 1), 0)
    q = q_ref[0] * (ATTN_SCALE * LOG2E)
    gt = gt_ref[0]
    ocmp = ocmp_ref[0]
    selb = sel_ref[0].astype(BF16)
    heads = [(h, h // NSA_GROUP, h * HEAD_DIM, (h + 1) * HEAD_DIM) for h in range(NSA_HEADS)]
    qps = [_pad_head(q[:, lo:hi], g).astype(BF16) for _, g, lo, hi in heads]
    start = pl.multiple_of(jnp.maximum(i * tq - WINDOW, 0), LANES)
    wpos = start + lax.broadcasted_iota(jnp.int32, (1, span), 1)
    dist_w = qpos - wpos
    mask_w = (dist_w >= 0) & (dist_w <= WINDOW)
    wposf = wpos.astype(F32)
    kwb = kw_ref[0, :, pl.ds(start, span)]
    vwb = vw_ref[0, :, pl.ds(start, span)]
    partial = []
    for h, g, lo, hi in heads:
        s = _dot(qps[h], kwb) + (NSA_SLOPES[h] * LOG2E) * wposf
        e, d = _softmax2_parts(s, mask_w)
        o_win = _dot_nt(e.astype(BF16), vwb)[:, g * HEAD_DIM:(g + 1) * HEAD_DIM] / d
        partial.append(gt[:, h:h + 1] * ocmp[:, lo:hi] + gt[:, 2 * NSA_HEADS + h:2 * NSA_HEADS + h + 1] * o_win)
    tiles_per_step = kv_step // tq
    for br in range(T // kv_step):
        @pl.when((i >= br * tiles_per_step) & (i < (br + 1) * tiles_per_step))
        def _(br=br):
            kv = (br + 1) * kv_step
            kpos = lax.broadcasted_iota(jnp.int32, (1, kv), 1)
            causal = kpos <= qpos
            kposf = kpos.astype(F32)
            ksb = ks_ref[0, :, 0:kv]
            vsb = vs_ref[0, :, 0:kv]
            for g in range(NSA_KV_HEADS):
                expand = lax.dot_general(selb, e_ref[g, :, 0:kv], (((0,), (0,)), ((), ())), preferred_element_type=F32)
                mask_s = (expand > 0.5) & causal
                for h, _, lo, hi in heads[g * NSA_GROUP:(g + 1) * NSA_GROUP]:
                    s = _dot(qps[h], ksb) + (NSA_SLOPES[h] * LOG2E) * kposf
                    e, d = _softmax2_parts(s, mask_s)
                    o_sel = _dot_nt(e.astype(BF16), vsb)[:, g * HEAD_DIM:(g + 1) * HEAD_DIM] / d
                    o_ref[0, :, lo:hi] = partial[h] + gt[:, NSA_HEADS + h:NSA_HEADS + h + 1] * o_sel


def _sel_win_prompt(q_nsa, nsa_t, win_t, sel_t, e_mat, o_cmp, gates, tq=256):
    B, T, _ = q_nsa.shape
    span = WINDOW + tq
    kv = lambda j: pl.BlockSpec((1, LANES, T), lambda b, i: (b, j, 0))
    return pl.pallas_call(
        functools.partial(_sel_win_kernel, tq=tq, span=span, kv_step=min(2 * tq, T)),
        grid=(B, T // tq),
        in_specs=[pl.BlockSpec((1, tq, 512), lambda b, i: (b, i, 0)),
                  kv(2), kv(3), kv(0), kv(1),
                  pl.BlockSpec((1, LANES, tq), lambda b, i: (b, 0, i)),
                  pl.BlockSpec((NSA_KV_HEADS, LANES, T), lambda b, i: (0, 0, 0)),
                  pl.BlockSpec((1, tq, 512), lambda b, i: (b, i, 0)),
                  pl.BlockSpec((1, tq, LANES), lambda b, i: (b, i, 0))],
        out_specs=pl.BlockSpec((1, tq, 512), lambda b, i: (b, i, 0)),
        out_shape=jax.ShapeDtypeStruct((B, T, 512), F32),
        compiler_params=_cparams("parallel", "parallel"),
        name="sel_win_prompt",
    )(q_nsa, nsa_t, nsa_t, win_t, win_t, sel_t, e_mat, o_cmp, gates)


def _diff_kernel(lam_ref, sl_ref, q_ref, k_ref, v_ref, g_ref, o_ref, *, tq, lam_init):
    i = pl.program_id(2)
    T = k_ref.shape[1]
    lam = lam_ref[0:1, 0:1]
    q = q_ref[0] * (ATTN_SCALE * LOG2E)
    qps = [_pad_head(q[:, c * HEAD_DIM:(c + 1) * HEAD_DIM], c).astype(BF16) for c in range(2)]
    tri = lax.broadcasted_iota(jnp.int32, (tq, tq), 0) >= lax.broadcasted_iota(jnp.int32, (tq, tq), 1)
    for br in range(T // tq):
        @pl.when(i == br)
        def _(br=br):
            off = br * tq
            kv = off + tq
            kb = k_ref[0, 0:kv, :]
            vb = v_ref[0, 0:kv, :]
            col = (sl_ref[0, :, 0:1] * LOG2E) * lax.broadcasted_iota(jnp.int32, (1, kv), 1).astype(F32)
            outs = []
            for c in range(2):
                s = _dot_nt(qps[c], kb) + col
                s_d = jnp.where(tri, s[:, off:], -jnp.inf)
                m = jnp.max(s_d, axis=-1, keepdims=True)
                if off:
                    m = jnp.maximum(m, jnp.max(s[:, :off], axis=-1, keepdims=True))
                e_d = jnp.exp2(s_d - m)
                d = jnp.sum(e_d, axis=-1, keepdims=True)
                o = _dot(e_d.astype(BF16), vb[off:, :])
                if off:
                    e_o = jnp.exp2(s[:, :off] - m)
                    d = d + jnp.sum(e_o, axis=-1, keepdims=True)
                    o = o + _dot(e_o.astype(BF16), vb[:off, :])
                outs.append(o / d)
            od = outs[0] - lam * outs[1]
            od = od * lax.rsqrt(jnp.mean(jnp.square(od), axis=-1, keepdims=True) + RMS_EPS)
            o_ref[0] = od * g_ref[...] * (1.0 - lam_init)


def _diff_prompt(lam_tile, dq, diff_b, subln_g, lam_init, tq=256):
    B, T, _ = dq.shape
    slopes = jnp.asarray(np.tile(np.asarray(DIFF_SLOPES, np.float32)[:, None, None], (1, 1, LANES)))
    return pl.pallas_call(
        functools.partial(_diff_kernel, tq=tq, lam_init=lam_init),
        grid=(B, DIFF_HEADS, T // tq),
        in_specs=[pl.BlockSpec((8, LANES), lambda b, h, i: (0, 0)),
                  pl.BlockSpec((1, 1, LANES), lambda b, h, i: (h, 0, 0)),
                  pl.BlockSpec((1, tq, LANES), lambda b, h, i: (b, i, h)),
                  pl.BlockSpec((1, T, LANES), lambda b, h, i: (b, 0, h)),
                  pl.BlockSpec((1, T, LANES), lambda b, h, i: (b, 0, DIFF_HEADS + h)),
                  pl.BlockSpec((1, DIFF_VDIM), lambda b, h, i: (0, 0))],
        out_specs=pl.BlockSpec((1, tq, LANES), lambda b, h, i: (b, i, h)),
        out_shape=jax.ShapeDtypeStruct((B, T, DIFF_HEADS * DIFF_VDIM), F32),
        compiler_params=_cparams("parallel", "parallel", "parallel"),
        name="diff_prompt",
    )(lam_tile, slopes, dq, diff_b, diff_b, subln_g)


def _layer_norm(z, g, b):
    mu = jnp.mean(z, axis=-1, keepdims=True)
    zc = z - mu
    var = jnp.mean(jnp.square(zc), axis=-1, keepdims=True)
    return zc * lax.rsqrt(var + LN_EPS) * g + b


def _pack_rows(x, ref):
    w = x.shape[1] // 2
    hi = lax.bitcast_convert_type(x[:, :w].astype(BF16).astype(F32), jnp.uint32)
    lo = lax.bitcast_convert_type(x[:, w:].astype(BF16).astype(F32), jnp.uint32)
    packed = hi | (lo >> 16)
    ref[0] = packed[:, :w // 2]
    ref[1] = packed[:, w // 2:]


def _unpack_rows(u0, u1):
    hi = [lax.bitcast_convert_type(u & jnp.uint32(0xFFFF0000), F32) for u in (u0, u1)]
    lo = [lax.bitcast_convert_type(u << 16, F32) for u in (u0, u1)]
    return jnp.concatenate(hi + lo, axis=1)


def _out_proj_kernel(on_ref, od_ref, x_ref, wa_ref, wb_ref, g_ref, b_ref, wr_ref, br_ref,
                     x1_ref, x1b_ref, x1p_ref, sel_ref, gate_ref, tope_ref, gate8_ref, cnt_ref):
    y = _dot(on_ref[...].astype(BF16), wa_ref[...]) + _dot(od_ref[...].astype(BF16), wb_ref[...])
    x1 = _layer_norm(DEEPNORM_ALPHA * x_ref[...] + y, g_ref[...], b_ref[...])
    x1_ref[...] = x1
    x1b_ref[...] = x1.astype(BF16)
    _pack_rows(x1, x1p_ref)
    logits = lax.dot_general(wr_ref[...], x1, (((1,), (1,)), ((), ())), precision=HIGHEST,
                             preferred_element_type=F32) + br_ref[...]
    eidx = lax.broadcasted_iota(jnp.int32, logits.shape, 0).astype(F32)
    sel = jnp.zeros(logits.shape, F32)
    picked, vals = [], []
    for k in range(TOP_K):
        cur = jnp.where(sel > 0.5, -jnp.inf, logits)
        m = jnp.max(cur, axis=0, keepdims=True)
        first = jnp.min(jnp.where(cur == m, eidx, float(N_EXPERTS)), axis=0, keepdims=True)
        sel = jnp.where(eidx == first, 1.0, sel)
        picked.append(first)
        vals.append(m)
    ex = jnp.where(sel > 0.5, jnp.exp(logits - vals[0]), 0.0)
    denom = jnp.sum(ex, axis=0, keepdims=True)
    sel_ref[...] = sel
    gate_ref[...] = ex / denom
    pad = [jnp.zeros_like(denom)] * (8 - TOP_K)
    tope_ref[...] = jnp.concatenate(picked + pad, axis=0)
    gate8_ref[...] = jnp.concatenate([jnp.exp(v - vals[0]) / denom for v in vals] + pad, axis=0)
    cnt_ref[0] = jnp.sum(sel, axis=1, keepdims=True)


def _out_proj_ln_router(o_nsa, o_diff, x2d, wa, wb, ln_g, ln_b, wr_t, br):
    n = x2d.shape[0]
    tm = min(512, n)
    row = lambda i: (i, 0)
    fix = lambda i: (0, 0)
    col = lambda i: (0, i)
    return pl.pallas_call(
        _out_proj_kernel,
        grid=(n // tm,),
        in_specs=[pl.BlockSpec((tm, 512), row), pl.BlockSpec((tm, 512), row), pl.BlockSpec((tm, D_MODEL), row),
                  pl.BlockSpec((512, D_MODEL), fix), pl.BlockSpec((512, D_MODEL), fix),
                  pl.BlockSpec((1, D_MODEL), fix), pl.BlockSpec((1, D_MODEL), fix),
                  pl.BlockSpec((N_EXPERTS, D_MODEL), fix), pl.BlockSpec((N_EXPERTS, 1), fix)],
        out_specs=(pl.BlockSpec((tm, D_MODEL), row), pl.BlockSpec((tm, D_MODEL), row),
                   pl.BlockSpec((2, tm, D_MODEL // 4), lambda i: (0, i, 0)),
                   pl.BlockSpec((N_EXPERTS, tm), col), pl.BlockSpec((N_EXPERTS, tm), col),
                   pl.BlockSpec((8, tm), col), pl.BlockSpec((8, tm), col),
                   pl.BlockSpec((1, N_EXPERTS, 1), lambda i: (i, 0, 0))),
        out_shape=(jax.ShapeDtypeStruct((n, D_MODEL), F32), jax.ShapeDtypeStruct((n, D_MODEL), BF16),
                   jax.ShapeDtypeStruct((2, n, D_MODEL // 4), jnp.uint32),
                   jax.ShapeDtypeStruct((N_EXPERTS, n), F32), jax.ShapeDtypeStruct((N_EXPERTS, n), F32),
                   jax.ShapeDtypeStruct((8, n), F32), jax.ShapeDtypeStruct((8, n), F32),
                   jax.ShapeDtypeStruct((n // tm, N_EXPERTS, 1), F32)),
        compiler_params=_cparams("parallel"),
        name="out_proj_ln_router",
    )(o_nsa, o_diff, x2d, wa, wb, ln_g, ln_b, wr_t, br)


def _load_expert_weights(wgu_ref, wd_ref, p_ref, wg_s, wl_s, wd_s):
    half = MXU_DIM // 2
    for m in range(wgu_ref.shape[2] // MXU_DIM):
        y = _dot(wgu_ref[0, :, m * MXU_DIM:(m + 1) * MXU_DIM].astype(BF16), p_ref[...])
        wg_s[:, m * half:(m + 1) * half] = y[:, :half].astype(BF16)
        wl_s[:, m * half:(m + 1) * half] = y[:, half:].astype(BF16)
    wd_s[...] = wd_ref[0].astype(BF16)


def _deinterleave_matrix():
    half = MXU_DIM // 2
    p_np = np.zeros((MXU_DIM, MXU_DIM), np.float32)
    p_np[2 * np.arange(half), np.arange(half)] = 1.0
    p_np[2 * np.arange(half) + 1, half + np.arange(half)] = 1.0
    return jnp.asarray(p_np, BF16)


_EXPERT_WEIGHT_SCRATCH = [pltpu.VMEM((D_MODEL, D_MODEL), BF16)] * 3


def _moe_pos_kernel(sel_ref, u_ref, pos_ref, cnt_ref, *, n_valid):
    s_tile = sel_ref.shape[1]
    tok = pl.program_id(0) * s_tile + lax.broadcasted_iota(jnp.int32, (1, s_tile), 1)
    sel = jnp.where(tok < n_valid, sel_ref[...], 0.0)
    rank = _dot(sel.astype(BF16), u_ref[...])
    pos_ref[...] = jnp.where(sel > 0.5, rank, -1.0)
    cnt_ref[0] = jnp.sum(sel, axis=1, keepdims=True)


def _moe_positions(sel_t, s_tile):
    n = sel_t.shape[1]
    ns = pl.cdiv(n, s_tile)
    upper = jnp.asarray(np.triu(np.ones((s_tile, s_tile), np.float32), 1), BF16)
    return pl.pallas_call(
        functools.partial(_moe_pos_kernel, n_valid=n),
        grid=(ns,),
        in_specs=[pl.BlockSpec((N_EXPERTS, s_tile), lambda s: (0, s)),
                  pl.BlockSpec((s_tile, s_tile), lambda s: (0, 0))],
        out_specs=(pl.BlockSpec((N_EXPERTS, s_tile), lambda s: (0, s)),
                   pl.BlockSpec((1, N_EXPERTS, 1), lambda s: (s, 0, 0))),
        out_shape=(jax.ShapeDtypeStruct((N_EXPERTS, ns * s_tile), F32),
                   jax.ShapeDtypeStruct((ns, N_EXPERTS, 1), F32)),
        compiler_params=_cparams("parallel"),
        name="moe_positions",
    )(sel_t, upper)


def _moe_kernel(nch_ref, x_ref, pos_ref, gate_ref, wgu_ref, wd_ref, p_ref, bg_ref, bl_ref, bd_ref, y_ref,
                wg_s, wl_s, wd_s, *, chunk, n_valid):
    s = pl.program_id(0)
    e = pl.program_id(1)
    s_tile = x_ref.shape[0]
    n_chunks = nch_ref[s * N_EXPERTS + e]

    @pl.when(e == 0)
    def _():
        y_ref[...] = jnp.zeros_like(y_ref)

    @pl.when(n_chunks > 0)
    def _():
        _load_expert_weights(wgu_ref, wd_ref, p_ref, wg_s, wl_s, wd_s)

    pos = pos_ref[0, 0].astype(jnp.int32)
    gate = gate_ref[0, 0]
    slot0 = lax.broadcasted_iota(jnp.int32, (chunk, 1), 0)

    def body(j, carry):
        hit = pos == slot0 + j * chunk
        onehot = jnp.where(hit, 1.0, 0.0).astype(BF16)
        x = x_ref[...]
        if n_valid % s_tile:
            row = s * s_tile + lax.broadcasted_iota(jnp.int32, (s_tile, 1), 0)
            x = jnp.where(row < n_valid, x, jnp.zeros_like(x))
        xs = _dot(onehot, x).astype(BF16)
        hg = jnp.minimum(_dot(xs, wg_s[...]) + bg_ref[0], SWIGLU_LIMIT)
        hl = jnp.clip(_dot(xs, wl_s[...]) + bl_ref[0], -SWIGLU_LIMIT, SWIGLU_LIMIT)
        a = (hl + 1.0) * hg * jax.nn.sigmoid(SWIGLU_ALPHA * hg)
        out = _dot(a.astype(BF16), wd_s[...]) + bd_ref[0]
        gslot = jnp.sum(jnp.where(hit, gate, 0.0), axis=1, keepdims=True)
        outg = (out * gslot).astype(BF16)
        y_ref[...] += lax.dot_general(onehot, outg, (((0,), (0,)), ((), ())), preferred_element_type=F32)
        return carry

    lax.fori_loop(0, n_chunks, body, 0)


def _moe(x1b, pos_t, gate_t, counts, wgu, wd, p_mat, bg, bl, bd, s_tile, chunk):
    n = x1b.shape[0]
    ns = pl.cdiv(n, s_tile)
    nch = ((counts.reshape(ns * N_EXPERTS) + (chunk - 1)) // chunk).astype(jnp.int32)
    pos4 = pos_t.reshape(N_EXPERTS, ns, 1, s_tile)
    gate4 = jnp.pad(gate_t, ((0, 0), (0, ns * s_tile - n))).reshape(N_EXPERTS, ns, 1, s_tile)
    wspec = lambda a: pl.BlockSpec((1,) + a.shape[1:], lambda s, e, nch: (e, 0, 0))
    bspec = pl.BlockSpec((1, 1, D_MODEL), lambda s, e, nch: (e, 0, 0))
    rspec = pl.BlockSpec((1, 1, 1, s_tile), lambda s, e, nch: (e, s, 0, 0))
    return pl.pallas_call(
        functools.partial(_moe_kernel, chunk=chunk, n_valid=n),
        grid_spec=pltpu.PrefetchScalarGridSpec(
            num_scalar_prefetch=1,
            grid=(ns, N_EXPERTS),
            in_specs=[pl.BlockSpec((s_tile, D_MODEL), lambda s, e, nch: (s, 0)),
                      rspec, rspec, wspec(wgu), wspec(wd), pl.BlockSpec(p_mat.shape, lambda s, e, nch: (0, 0)),
                      bspec, bspec, bspec],
            out_specs=pl.BlockSpec((s_tile, D_MODEL), lambda s, e, nch: (s, 0)),
            scratch_shapes=_EXPERT_WEIGHT_SCRATCH),
        out_shape=jax.ShapeDtypeStruct((n, D_MODEL), F32),
        compiler_params=_cparams("parallel", "arbitrary"),
        name="moe_experts",
    )(nch, x1b, pos4, gate4, wgu, wd, p_mat, bg, bl, bd)


def _ln2_kernel(x_ref, f_ref, g_ref, b_ref, o_ref):
    o_ref[...] = _layer_norm(DEEPNORM_ALPHA * x_ref[...] + f_ref[...], g_ref[...], b_ref[...])


def _residual_ln(x1, f, g, b):
    n = x1.shape[0]
    tm = min(512, n)
    row = lambda i: (i, 0)
    fix = lambda i: (0, 0)
    return pl.pallas_call(
        _ln2_kernel,
        grid=(n // tm,),
        in_specs=[pl.BlockSpec((tm, D_MODEL), row), pl.BlockSpec((tm, D_MODEL), row),
                  pl.BlockSpec((1, D_MODEL), fix), pl.BlockSpec((1, D_MODEL), fix)],
        out_specs=pl.BlockSpec((tm, D_MODEL), row),
        out_shape=jax.ShapeDtypeStruct((n, D_MODEL), F32),
        compiler_params=_cparams("parallel"),
        name="residual_ln2",
    )(x1, f, g, b)


def _post_mixer(o_nsa, o_diff, x2d, p, s_tile, chunk):
    x1, x1b, _, sel_t, gate_t, _, _, _ = _out_proj_ln_router(o_nsa, o_diff, x2d, p["wo_a"], p["wo_b"], p["ln1_g"],
                                                             p["ln1_b"], p["wr_t"], p["br"])
    pos_t, counts = _moe_positions(sel_t, s_tile)
    f = _moe(x1b, pos_t, gate_t, counts, p["wgu"], p["wd"], p["p_mat"], p["bg"], p["bl"], p["bd"], s_tile, chunk)
    return _residual_ln(x1, f, p["ln2_g"], p["ln2_b"])


def _moe_slots_kernel(sel_ref, tope_ref, pstart_ref, u_ref, slot_ref, carry_ref):
    @pl.when(pl.program_id(0) == 0)
    def _():
        carry_ref[...] = jnp.zeros_like(carry_ref)

    sel = sel_ref[...]
    slot_all = pstart_ref[...] + carry_ref[...] + _dot(sel.astype(BF16), u_ref[...])
    eidx = lax.broadcasted_iota(jnp.int32, sel.shape, 0).astype(F32)
    tope = tope_ref[...]
    rows = [jnp.sum(jnp.where(eidx == tope[k:k + 1], slot_all, 0.0), axis=0, keepdims=True) for k in range(TOP_K)]
    rows += [jnp.zeros_like(rows[0])] * (8 - TOP_K)
    slot_ref[...] = jnp.concatenate(rows, axis=0).astype(jnp.int32)
    carry_ref[...] += jnp.sum(sel, axis=1, keepdims=True)


def _moe_slots(sel_t, tope, pstart, tm=2048):
    n = sel_t.shape[1]
    upper = jnp.asarray(np.triu(np.ones((tm, tm), np.float32), 1), BF16)
    return pl.pallas_call(
        _moe_slots_kernel,
        grid=(n // tm,),
        in_specs=[pl.BlockSpec((N_EXPERTS, tm), lambda i: (0, i)), pl.BlockSpec((8, tm), lambda i: (0, i)),
                  pl.BlockSpec((N_EXPERTS, 1), lambda i: (0, 0)), pl.BlockSpec((tm, tm), lambda i: (0, 0))],
        out_specs=pl.BlockSpec((8, tm), lambda i: (0, i)),
        out_shape=jax.ShapeDtypeStruct((8, n), jnp.int32),
        scratch_shapes=[pltpu.VMEM((N_EXPERTS, 1), F32)],
        compiler_params=_cparams("arbitrary"),
        name="moe_slots",
    )(sel_t, tope, pstart, upper)


def _sc_mesh():
    return plsc.VectorSubcoreMesh(core_axis_name="c", subcore_axis_name="s", num_cores=SC_CORES,
                                  num_subcores=SC_SUBCORES)


def _sc_scatter_rows(src, idx, n_out):
    n_rows, width = src.shape
    n_idx = idx.shape[0]
    per_core = n_rows // SC_WINDOW // SC_CORES

    @functools.partial(pl.kernel, out_type=jax.ShapeDtypeStruct((n_out, width), src.dtype), mesh=_sc_mesh(),
                       scratch_types=[], name="moe_dispatch_rows")
    def scatter(src_hbm, idx_hbm, out_hbm):
        def body(src_vmem, idx_vmem):
            for k in range(n_idx):
                pltpu.sync_copy(src_vmem, out_hbm.at[idx_vmem.at[k]])

        pltpu.emit_pipeline(
            body,
            grid=(SC_CORES, per_core),
            in_specs=[pl.BlockSpec((SC_WINDOW, width), lambda c, j: (c * per_core + j, 0)),
                      pl.BlockSpec((n_idx, SC_WINDOW), lambda c, j: (0, c * per_core + j))],
            out_specs=[],
            core_axis_name=("c", "s"),
            dimension_semantics=(pltpu.PARALLEL, pltpu.PARALLEL),
        )(src_hbm, idx_hbm)

    return scatter(src, idx)


def _sc_gather_rows(table, idx):
    n_rows = idx.shape[1]
    width = table.shape[1]
    per_core = n_rows // SC_WINDOW // SC_CORES

    @functools.partial(pl.kernel, out_type=jax.ShapeDtypeStruct((n_rows, width), table.dtype), mesh=_sc_mesh(),
                       scratch_types=[], name="moe_return_rows")
    def gather(table_hbm, idx_hbm, out_hbm):
        def body(idx_vmem, out_vmem):
            pltpu.sync_copy(table_hbm.at[idx_vmem.at[0]], out_vmem)

        pltpu.emit_pipeline(
            body,
            grid=(SC_CORES, per_core),
            in_specs=[pl.BlockSpec((1, SC_WINDOW), lambda c, j: (0, c * per_core + j))],
            out_specs=[pl.BlockSpec((SC_WINDOW, width), lambda c, j: (c * per_core + j, 0))],
            core_axis_name=("c", "s"),
            dimension_semantics=(pltpu.PARALLEL, pltpu.PARALLEL),
        )(idx_hbm, out_hbm)

    return gather(table, idx)


def _moe_ffn_kernel(blk_e_ref, nvalid_ref, xs_ref, wgu_ref, wd_ref, p_ref, bg_ref, bl_ref, bd_ref, o_ref,
                    wg_s, wl_s, wd_s):
    i = pl.program_id(0)
    nv = nvalid_ref[i]
    new_expert = (i == 0) | (blk_e_ref[i] != blk_e_ref[jnp.maximum(i - 1, 0)])

    @pl.when(new_expert & (nv > 0))
    def _():
        _load_expert_weights(wgu_ref, wd_ref, p_ref, wg_s, wl_s, wd_s)

    @pl.when(nv > 0)
    def _():
        live = lax.broadcasted_iota(jnp.int32, (o_ref.shape[1], 1), 0) < nv
        xs = jnp.where(live, _unpack_rows(xs_ref[0], xs_ref[1]), 0.0).astype(BF16)
        hg = jnp.minimum(_dot(xs, wg_s[...]) + bg_ref[0], SWIGLU_LIMIT)
        hl = jnp.clip(_dot(xs, wl_s[...]) + bl_ref[0], -SWIGLU_LIMIT, SWIGLU_LIMIT)
        a = (hl + 1.0) * hg * jax.nn.sigmoid(SWIGLU_ALPHA * hg)
        _pack_rows(_dot(a.astype(BF16), wd_s[...]) + bd_ref[0], o_ref)

    @pl.when(nv == 0)
    def _():
        o_ref[...] = jnp.zeros_like(o_ref)


def _moe_ffn(blk_e, nvalid, xs, wgu, wd, p_mat, bg, bl, bd):
    _, n_slots, half = xs.shape
    wspec = lambda a: pl.BlockSpec((1,) + a.shape[1:], lambda i, be, nv: (be[i], 0, 0))
    bspec = pl.BlockSpec((1, 1, D_MODEL), lambda i, be, nv: (be[i], 0, 0))
    rows = pl.BlockSpec((2, MOE_CHUNK, half), lambda i, be, nv: (0, i, 0))
    return pl.pallas_call(
        _moe_ffn_kernel,
        grid_spec=pltpu.PrefetchScalarGridSpec(
            num_scalar_prefetch=2,
            grid=(n_slots // MOE_CHUNK,),
            in_specs=[rows, wspec(wgu), wspec(wd), pl.BlockSpec(p_mat.shape, lambda i, be, nv: (0, 0)),
                      bspec, bspec, bspec],
            out_specs=rows,
            scratch_shapes=_EXPERT_WEIGHT_SCRATCH),
        out_shape=jax.ShapeDtypeStruct((2, n_slots, half), jnp.uint32),
        compiler_params=_cparams("arbitrary"),
        name="moe_ffn_sorted",
    )(blk_e, nvalid, xs, wgu, wd, p_mat, bg, bl, bd)


def _ln2_combine_kernel(x_ref, r_ref, gate_ref, g_ref, b_ref, o_ref):
    gate = gate_ref[...]
    f = gate[:, 0:1] * _unpack_rows(r_ref[0, 0], r_ref[0, 1])
    for k in range(1, TOP_K):
        f = f + gate[:, k:k + 1] * _unpack_rows(r_ref[k, 0], r_ref[k, 1])
    o_ref[...] = _layer_norm(DEEPNORM_ALPHA * x_ref[...] + f, g_ref[...], b_ref[...])


def _ln2_combine(x1, returned, gate_tok, g, b, tm=256):
    n = x1.shape[0]
    row = lambda i: (i, 0)
    fix = lambda i: (0, 0)
    return pl.pallas_call(
        _ln2_combine_kernel,
        grid=(n // tm,),
        in_specs=[pl.BlockSpec((tm, D_MODEL), row),
                  pl.BlockSpec((TOP_K, 2, tm, D_MODEL // 4), lambda i: (0, 0, i, 0)),
                  pl.BlockSpec((tm, 8), row), pl.BlockSpec((1, D_MODEL), fix), pl.BlockSpec((1, D_MODEL), fix)],
        out_specs=pl.BlockSpec((tm, D_MODEL), row),
        out_shape=jax.ShapeDtypeStruct((n, D_MODEL), F32),
        compiler_params=_cparams("parallel"),
        name="combine_ln2",
    )(x1, returned, gate_tok, g, b)


def _post_mixer_sorted(o_nsa, o_diff, x2d, p):
    n = x2d.shape[0]
    x1, _, x1p, sel_t, _, tope, gate8, cnt = _out_proj_ln_router(o_nsa, o_diff, x2d, p["wo_a"], p["wo_b"], p["ln1_g"],
                                                                 p["ln1_b"], p["wr_t"], p["br"])
    counts = jnp.sum(cnt[:, :, 0], axis=0).astype(jnp.int32)
    padded = (counts + (MOE_CHUNK - 1)) // MOE_CHUNK * MOE_CHUNK
    pend = jnp.cumsum(padded)
    pstart = pend - padded
    n_blocks = n * TOP_K // MOE_CHUNK + N_EXPERTS
    n_slots = n_blocks * MOE_CHUNK
    blk0 = jnp.arange(n_blocks, dtype=jnp.int32)[:, None] * MOE_CHUNK
    blk_e = jnp.minimum(jnp.sum((blk0 >= pend[None, :]).astype(jnp.int32), axis=1), N_EXPERTS - 1)
    in_region = (blk0 >= pstart[None, :]) & (blk0 < pend[None, :])
    nvalid = jnp.sum(jnp.where(in_region, jnp.clip(pstart + counts - blk0, 0, MOE_CHUNK), 0), axis=1)
    slot = _moe_slots(sel_t, tope, pstart.astype(F32)[:, None])[:TOP_K]
    half = D_MODEL // 4
    slot2 = jnp.concatenate([slot, slot + n_slots], axis=1)
    xs = _sc_scatter_rows(x1p.reshape(2 * n, half), slot2, 2 * n_slots)
    outs = _moe_ffn(blk_e, nvalid, xs.reshape(2, n_slots, half), p["wgu"], p["wd"], p["p_mat"], p["bg"], p["bl"],
                    p["bd"])
    returned = _sc_gather_rows(outs.reshape(2 * n_slots, half), slot2.reshape(1, TOP_K * 2 * n))
    return _ln2_combine(x1, returned.reshape(TOP_K, 2, n, half), gate8.T, p["ln2_g"], p["ln2_b"])


def _sample_kernel(pt_ref, qm_ref, qd_ref, gt_ref, nrow_ref, wrow_ref, wcol_ref, drow_ref, win_ref,
                   w1_ref, pe_ref, w2_ref, m_ref, e_ref, r_ref, sn_ref, sd_ref, lam_ref, subg_ref, *rest,
                   n_pages, past_len, lam_init):
    nsa_pages = rest[:n_pages]
    diff_pages = rest[n_pages:2 * n_pages]
    onsa_ref, odiff_ref, nwin_ref, x_ref = rest[2 * n_pages:]
    n16 = past_len // CMP_STRIDE
    qpos = past_len
    kpos = lax.broadcasted_iota(jnp.int32, (1, past_len), 1)
    dist_k = (qpos - kpos).astype(F32)

    for j, pg in enumerate(nsa_pages):
        for kind in range(2):
            x_ref[kind, j * PAGE_SIZE:(j + 1) * PAGE_SIZE, :] = pg[kind * LANES:(kind + 1) * LANES, :].T

    kv_cmp = [_compress_chunks(lambda l, kind=kind: x_ref[kind, pl.ds(l, n16, stride=CMP_STRIDE), :],
                               w1_ref[kind], pe_ref[kind], w2_ref[kind], n16) for kind in range(2)]

    qm = qm_ref[0]
    qmb = qm.astype(BF16)
    slope_n = sn_ref[:, 0:1]
    nrow = nrow_ref[0]
    wrow = wrow_ref[0]
    gt = gt_ref[0]

    def new_key_score(qrows, krow):
        return jnp.sum(qrows * krow, axis=1, keepdims=True) * ATTN_SCALE

    cmp_end = lax.broadcasted_iota(jnp.int32, (1, n16), 1) * CMP_STRIDE + (CMP_LEN - 1)
    dist_c = qpos - cmp_end
    s = _dot_nt(qmb, kv_cmp[0].astype(BF16)) * ATTN_SCALE - slope_n * dist_c.astype(F32)
    e, d = _softmax_parts(s, dist_c >= 0)
    p_cmp = e / d
    o_cmp = _dot(p_cmp.astype(BF16), kv_cmp[1].astype(BF16))
    pgrp = jnp.dot(r_ref[...], p_cmp, precision=HIGHEST, preferred_element_type=F32)
    imp = jnp.dot(pgrp, m_ref[...], precision=HIGHEST, preferred_element_type=F32)
    nbl = m_ref.shape[1]
    blk = lax.broadcasted_iota(jnp.int32, (NSA_HEADS, nbl), 1)
    sel = _select_blocks(imp, blk, jnp.full((NSA_HEADS, 1), qpos, jnp.int32), past_len // SEL_BLOCK + 1)
    mask_s = _dot(sel.astype(BF16), e_ref[...]) > 0.5

    _sample_diff_branch(qd_ref, drow_ref, sd_ref, lam_ref, subg_ref, diff_pages, odiff_ref, dist_k, lam_init)

    s = jnp.concatenate([_dot(qmb, pg[2 * LANES:3 * LANES, :].astype(BF16)) for pg in nsa_pages], axis=1)
    s = jnp.where(mask_s, s * ATTN_SCALE - slope_n * dist_k, -jnp.inf)
    s_new = new_key_score(qm, nrow[:, 2 * LANES:3 * LANES])
    m = jnp.maximum(jnp.max(s, axis=1, keepdims=True), s_new)
    e = jnp.exp(s - m)
    e_new = jnp.exp(s_new - m)
    d = jnp.sum(e, axis=1, keepdims=True) + e_new
    eb = e.astype(BF16)
    acc = e_new * nrow[:, 3 * LANES:4 * LANES]
    for j, pg in enumerate(nsa_pages):
        acc = acc + _dot_nt(eb[:, j * PAGE_SIZE:(j + 1) * PAGE_SIZE], pg[3 * LANES:4 * LANES, :].astype(BF16))
    o_sel = acc / d

    nwin = win_ref.shape[2]
    wpos = past_len - nwin + lax.broadcasted_iota(jnp.int32, (1, nwin), 1)
    dist_w = qpos - wpos
    mask_w = (dist_w >= 0) & (dist_w <= WINDOW)
    win = win_ref[0]
    s = _dot(qmb, win[0:LANES, :].astype(BF16)) * ATTN_SCALE - slope_n * dist_w.astype(F32)
    s = jnp.where(mask_w, s, -jnp.inf)
    s_new = new_key_score(qm, wrow[:, 0:LANES])
    m = jnp.maximum(jnp.max(s, axis=1, keepdims=True), s_new)
    e = jnp.exp(s - m)
    e_new = jnp.exp(s_new - m)
    d = jnp.sum(e, axis=1, keepdims=True) + e_new
    o_win = (_dot_nt(e.astype(BF16), win[LANES:2 * LANES, :].astype(BF16)) + e_new * wrow[:, LANES:2 * LANES]) / d

    onsa_ref[0] = gt[:, 0:1] * o_cmp + gt[:, 1:2] * o_sel + gt[:, 2:3] * o_win

    lane = lax.broadcasted_iota(jnp.int32, win.shape, 1)
    nwin_ref[0] = jnp.where(lane == nwin - 1, wcol_ref[0], pltpu.roll(win, nwin - 1, 1))


def _sample_diff_branch(qd_ref, drow_ref, sd_ref, lam_ref, subg_ref, diff_pages, odiff_ref, dist_k, lam_init):
    drow = drow_ref[0]

    def diff_rows(pg, j):
        return pg[pl.ds(j, PAGE_SIZE, stride=DIFF_ROWS), :]

    qds = [qd_ref[0, h] for h in range(DIFF_HEADS)]
    qdb = [qh.astype(BF16) for qh in qds]
    s_pages = []
    for pg in diff_pages:
        sp = _dot_nt(qdb[0], diff_rows(pg, 0).astype(BF16))
        for h in range(1, DIFF_HEADS):
            sp = sp + _dot_nt(qdb[h], diff_rows(pg, h).astype(BF16))
        s_pages.append(sp)
    s = jnp.concatenate(s_pages, axis=1) * ATTN_SCALE - sd_ref[:, 0:1] * dist_k
    prod = qds[0] * drow[0:1]
    for h in range(1, DIFF_HEADS):
        prod = prod + qds[h] * drow[h:h + 1]
    s_new = jnp.sum(prod, axis=1, keepdims=True) * ATTN_SCALE
    m = jnp.maximum(jnp.max(s, axis=1, keepdims=True), s_new)
    e = jnp.exp(s - m)
    e_new = jnp.exp(s_new - m)
    d = jnp.sum(e, axis=1, keepdims=True) + e_new
    lam = lam_ref[0:1, 0:1]
    p = e / d
    p_new = e_new / d
    a = (p - lam * pltpu.roll(p, DIFF_HEADS, 0)).astype(BF16)
    a_new = p_new - lam * pltpu.roll(p_new, DIFF_HEADS, 0)
    accs = []
    for h in range(DIFF_HEADS):
        acc = a_new * drow[DIFF_HEADS + h:DIFF_HEADS + h + 1]
        for j, pg in enumerate(diff_pages):
            acc = acc + _dot(a[:, j * PAGE_SIZE:(j + 1) * PAGE_SIZE], diff_rows(pg, DIFF_HEADS + h).astype(BF16))
        accs.append(acc)
    acc = jnp.concatenate(accs, axis=1)
    lane_head = lax.broadcasted_iota(jnp.int32, acc.shape, 1) // DIFF_VDIM
    row = lax.broadcasted_iota(jnp.int32, acc.shape, 0)
    own = lane_head == row
    ms = jnp.sum(jnp.where(own, jnp.square(acc), 0.0), axis=1, keepdims=True) / DIFF_VDIM
    odiff_ref[0] = acc * lax.rsqrt(ms + RMS_EPS) * subg_ref[...] * (1.0 - lam_init)


def _sample_attention(page_table, qm, qd4, gt8, nrow, wrow, wcol, drow, win_t, pool_nsa, pool_diff,
                      w1ab, pe2, w2bd, m_mat, e_mat, r_mat, sn, sd, lam_tile, subg_tile, past_len, lam_init):
    B = qm.shape[0]
    n_pages = page_table.shape[1]
    nwin = win_t.shape[2]

    def per_b(shape):
        nd = len(shape)
        return pl.BlockSpec((1,) + shape, lambda b, pt: (b,) + (0,) * nd)

    def fixed(shape):
        nd = len(shape)
        return pl.BlockSpec(shape, lambda b, pt: (0,) * nd)

    def page_spec(pool, j):
        return pl.BlockSpec((None,) + pool.shape[1:], lambda b, pt: (pt[b, j], 0, 0))

    in_specs = [per_b((NSA_HEADS, LANES)), per_b((DIFF_HEADS, 8, LANES)), per_b((NSA_HEADS, LANES)),
                per_b((1, NSA_W)), per_b((1, WIN_W)), per_b((WIN_W, 1)), per_b((DIFF_ROWS, LANES)),
                per_b((WIN_W, nwin)),
                fixed(w1ab.shape), fixed(pe2.shape), fixed(w2bd.shape), fixed(m_mat.shape), fixed(e_mat.shape),
                fixed(r_mat.shape), fixed(sn.shape), fixed(sd.shape), fixed(lam_tile.shape), fixed(subg_tile.shape)]
    in_specs += [page_spec(pool_nsa, j) for j in range(n_pages)]
    in_specs += [page_spec(pool_diff, j) for j in range(n_pages)]
    return pl.pallas_call(
        functools.partial(_sample_kernel, n_pages=n_pages, past_len=past_len, lam_init=lam_init),
        grid_spec=pltpu.PrefetchScalarGridSpec(
            num_scalar_prefetch=1,
            grid=(B,),
            in_specs=in_specs,
            out_specs=(per_b((NSA_HEADS, LANES)), per_b((8, DIFF_HEADS * DIFF_VDIM)), per_b((WIN_W, nwin))),
            scratch_shapes=[pltpu.VMEM((2, past_len, LANES), F32)]),
        out_shape=(jax.ShapeDtypeStruct((B, NSA_HEADS, LANES), F32),
                   jax.ShapeDtypeStruct((B, 8, DIFF_HEADS * DIFF_VDIM), F32),
                   jax.ShapeDtypeStruct((B, WIN_W, nwin), F32)),
        compiler_params=_cparams("arbitrary"),
        name="sample_attention",
    )(page_table, qm, qd4, gt8, nrow, wrow, wcol, drow, win_t, w1ab, pe2, w2bd, m_mat, e_mat, r_mat, sn, sd,
      lam_tile, subg_tile, *([pool_nsa] * n_pages), *([pool_diff] * n_pages))


def _cmp_to_sel(n_cmp, n_sel):
    c0 = np.arange(n_cmp)[:, None] * CMP_STRIDE
    s0 = np.arange(n_sel)[None, :] * SEL_BLOCK
    ov = np.clip(np.minimum(c0 + CMP_LEN, s0 + SEL_BLOCK) - np.maximum(c0, s0), 0, None)
    return (ov / CMP_LEN).astype(np.float32)


def _lambda_init(layer):
    return 0.8 - 0.6 * math.exp(-0.3 * layer)


def _prep_params(l, w_in, w_out, diff_subln_g, ln1_g, ln1_b, ln2_g, ln2_b, w_router, b_router,
                 w_gate_up, b_gate_up, w_down, b_down):
    gate_end = _GT0 + 3 * NSA_HEADS
    wt = w_in[l].T
    w_t = jnp.concatenate([wt[:gate_end], jnp.zeros((_GATE_PAD, D_MODEL), F32), wt[gate_end:]], axis=0).astype(BF16)
    bgu = b_gate_up[l]
    return {
        "w_t": w_t,
        "wo_a": w_out[l][:512].astype(BF16), "wo_b": w_out[l][512:].astype(BF16),
        "ln1_g": ln1_g[l][None], "ln1_b": ln1_b[l][None], "ln2_g": ln2_g[l][None], "ln2_b": ln2_b[l][None],
        "wr_t": w_router[l].T, "br": b_router[l][:, None],
        "wgu": w_gate_up[l], "wd": w_down[l], "p_mat": _deinterleave_matrix(),
        "bg": bgu[:, None, 0::2], "bl": bgu[:, None, 1::2], "bd": b_down[l][:, None, :],
        "subln_g": diff_subln_g[l][None],
    }


def _cmp_params(cmp_pe, cmp_w1, cmp_w2):
    eye = jnp.eye(NSA_KV_HEADS, dtype=F32)
    halves = []
    for half in range(2):
        w = cmp_w1[:, half * CMP_STRIDE:(half + 1) * CMP_STRIDE]
        wb = jnp.einsum('klde,gh->klgdhe', w, eye)
        halves.append(wb.reshape(2, CMP_STRIDE * LANES, LANES))
    w1ab = jnp.concatenate(halves, axis=-1).astype(BF16)
    pe = cmp_pe.reshape(2, 2, CMP_STRIDE, 1, HEAD_DIM)
    pe2 = jnp.broadcast_to(pe, (2, 2, CMP_STRIDE, NSA_KV_HEADS, HEAD_DIM)).reshape(2, 2, 1, CMP_STRIDE * LANES)
    pe2 = jnp.broadcast_to(pe2, (2, 2, 8, CMP_STRIDE * LANES))
    w2bd = jnp.einsum('kde,gh->kgdhe', cmp_w2, eye).reshape(2, LANES, LANES).astype(BF16)
    return w1ab, pe2, w2bd


def _feature_major_to_rows(a_t, kinds):
    B, _, T = a_t.shape
    return a_t.reshape(B, kinds, NSA_KV_HEADS, HEAD_DIM, T).transpose(0, 4, 1, 2, 3)


def _prompt_group(x, p, cmp_pe, cmp_w1, cmp_w2, lam_tile, lam_init):
    B, T, _ = x.shape
    x2d = x.reshape(B * T, D_MODEL)
    q_nsa, q_tb, nsa_rows, nsa_t, nsa_tb, _, win_t, win_tb, gates, dq, diff8, diff_b = _in_proj(
        x2d, p["w_t"], B, nsa_tok_w=NSA_W // 2, win_tok_w=LANES)
    r3 = lambda a: a.reshape(B, T, a.shape[-1])
    q_nsa, nsa_rows3, gates, dq = map(r3, (q_nsa, nsa_rows, gates, dq))
    kvc = _compress_prompt(nsa_rows3, *_cmp_params(cmp_pe, cmp_w1, cmp_w2))
    n16 = T // CMP_STRIDE
    n_sel = -(-T // SEL_BLOCK)
    m_np = np.zeros((n16, n_sel), np.float32)
    m_np[:n16 - 1] = _cmp_to_sel(n16 - 1, n_sel)
    o_cmp, sel = _cmp_attn_prompt(q_nsa, q_tb, kvc, jnp.asarray(m_np.T))
    e_np = np.zeros((NSA_KV_HEADS, LANES, T), np.float32)
    for g in range(NSA_KV_HEADS):
        e_np[g, g * n_sel + np.arange(T) // SEL_BLOCK, np.arange(T)] = 1.0
    o_nsa = _sel_win_prompt(q_nsa, nsa_tb, win_tb, sel, jnp.asarray(e_np, BF16), o_cmp, gates)
    o_diff = _diff_prompt(lam_tile, dq, diff_b.reshape(B, T, DIFF_W), p["subln_g"], lam_init)
    y = _post_mixer_sorted(o_nsa.reshape(B * T, 512), o_diff.reshape(B * T, 512), x2d, p)
    nwin = min(WINDOW, T)
    return (y.reshape(B, T, D_MODEL),
            _feature_major_to_rows(nsa_t, 4),
            diff8.reshape(B, T, 2, DIFF_HEADS, DIFF_VDIM),
            _feature_major_to_rows(win_t[:, :, T - nwin:], 2))


def _sample_group(x, pool_nsa, pool_diff, win_buf, page_table, p, cmp_pe, cmp_w1, cmp_w2, lam_tile, lam_init):
    B, T, _ = x.shape
    past_len = page_table.shape[1] * PAGE_SIZE
    x2d = x.reshape(B, D_MODEL)
    q_nsa, _, nsa_rows, nsa_t, _, win_rows, _, _, gates, dq, diff8, _ = _in_proj(x2d, p["w_t"], 1)
    qh = q_nsa.reshape(B, NSA_HEADS, HEAD_DIM)
    grp = (np.arange(NSA_HEADS) // NSA_GROUP)[None, :, None, None] == np.arange(NSA_KV_HEADS)[None, None, :, None]
    qm = (qh[:, :, None, :] * jnp.asarray(grp, F32)).reshape(B, NSA_HEADS, LANES)
    dq4 = dq.reshape(B, DIFF_HEADS, 1, 2 * HEAD_DIM)
    rr = np.arange(8)
    rowmask = (rr[None, :, None] % DIFF_HEADS == np.arange(DIFF_HEADS)[:, None, None]) & (
        rr[None, :, None] // DIFF_HEADS == (np.arange(LANES) // HEAD_DIM)[None, None, :])
    qd4 = dq4 * jnp.asarray(rowmask, F32)[None]
    gt8 = jnp.pad(gates[:, :3 * NSA_HEADS].reshape(B, 3, NSA_HEADS).transpose(0, 2, 1), ((0, 0), (0, 0), (0, LANES - 3)))
    nwin = win_buf.shape[1]
    win_t = win_buf.transpose(0, 2, 3, 4, 1).reshape(B, WIN_W, nwin)
    n_pool = pool_nsa.shape[0]
    pool_nsa_t = pool_nsa.transpose(0, 2, 3, 4, 1).reshape(n_pool, NSA_W, PAGE_SIZE)
    pool_diff_r = pool_diff.reshape(n_pool, PAGE_SIZE * DIFF_ROWS, LANES)
    w1ab, pe2, w2bd = _cmp_params(cmp_pe, cmp_w1, cmp_w2)
    n16 = past_len // CMP_STRIDE
    n_sel = past_len // SEL_BLOCK + 1
    m_np = np.zeros((n16, 64), np.float32)
    m_np[:n16 - 1, :n_sel] = _cmp_to_sel(n16 - 1, n_sel)
    e_np = np.zeros((64, past_len), np.float32)
    e_np[np.arange(past_len) // SEL_BLOCK, np.arange(past_len)] = 1.0
    r_np = (np.arange(8)[:, None] // NSA_GROUP == np.arange(8)[None, :] // NSA_GROUP).astype(np.float32)
    sn = jnp.asarray(np.tile(np.asarray(NSA_SLOPES, np.float32)[:, None], (1, LANES)))
    sd = jnp.asarray(np.tile(np.asarray(DIFF_SLOPES, np.float32)[np.arange(8) % DIFF_HEADS, None], (1, LANES)))
    subg_tile = jnp.tile(p["subln_g"], (1, DIFF_HEADS))
    o_nsa8, o_diff8, new_win_t = _sample_attention(
        page_table, qm, qd4, gt8, nsa_rows[:, None, :], win_rows[:, None, :], win_rows[:, :, None],
        diff8.reshape(B, DIFF_ROWS, LANES), win_t, pool_nsa_t, pool_diff_r,
        w1ab, pe2, w2bd, jnp.asarray(m_np), jnp.asarray(e_np, BF16), jnp.asarray(r_np), sn, sd, lam_tile, subg_tile,
        past_len, lam_init)
    o8 = o_nsa8.reshape(B, NSA_KV_HEADS, NSA_GROUP, NSA_KV_HEADS, HEAD_DIM)
    o_nsa = jnp.stack([o8[:, g, :, g] for g in range(NSA_KV_HEADS)], axis=1).reshape(B, 512)
    d8 = o_diff8[:, :DIFF_HEADS].reshape(B, DIFF_HEADS, DIFF_HEADS, DIFF_VDIM)
    o_diff = jnp.stack([d8[:, h, h] for h in range(DIFF_HEADS)], axis=1).reshape(B, 512)
    y = _post_mixer(o_nsa, o_diff, x2d, p, s_tile=B, chunk=B)
    return (y.reshape(B, T, D_MODEL),
            _feature_major_to_rows(nsa_t, 4).reshape(B, T, 4, NSA_KV_HEADS, HEAD_DIM),
            diff8.reshape(B, T, 2, DIFF_HEADS, DIFF_VDIM),
            _feature_major_to_rows(new_win_t, 2))


def kernel(x_prompt, x_sample, cache_nsa_kv, cache_diff_kv, state_nsa_win, page_table, w_in, w_out, cmp_pe, cmp_w1,
           cmp_w2, diff_lambda, diff_subln_g, ln1_g, ln1_b, ln2_g, ln2_b, w_router, b_router, w_gate_up, b_gate_up,
           w_down, b_down):
    depth = w_in.shape[0]
    xp, xs = x_prompt, x_sample
    outs = [[] for _ in range(6)]
    for l in range(depth):
        lam0 = _lambda_init(l)
        lv = diff_lambda[l].astype(F32)
        lam = jnp.exp(jnp.sum(lv[0] * lv[1])) - jnp.exp(jnp.sum(lv[2] * lv[3])) + lam0
        lam_tile = jnp.full((8, LANES), lam, F32)
        p = _prep_params(l, w_in, w_out, diff_subln_g, ln1_g, ln1_b, ln2_g, ln2_b, w_router, b_router,
                         w_gate_up, b_gate_up, w_down, b_down)
        xp, r_nsa, r_diff, r_win = _prompt_group(xp, p, cmp_pe[l], cmp_w1[l], cmp_w2[l], lam_tile, lam0)
        xs, s_nsa, s_diff, s_win = _sample_group(xs, cache_nsa_kv[l], cache_diff_kv[l], state_nsa_win[l], page_table,
                                                 p, cmp_pe[l], cmp_w1[l], cmp_w2[l], lam_tile, lam0)
        for lst, v in zip(outs, (r_nsa, r_diff, r_win, s_nsa, s_diff, s_win)):
            lst.append(v)
    return (xp, xs) + tuple(jnp.stack(o) for o in outs)
```

```python
import functools
import math

import numpy as np
import jax
import jax.numpy as jnp
from jax import lax
from jax.experimental import pallas as pl
from jax.experimental.pallas import tpu as pltpu
from jax.experimental.pallas import tpu_sc as plsc

F32 = jnp.float32
BF16 = jnp.bfloat16
HIGHEST = lax.Precision.HIGHEST

D_MODEL = 1024
HEAD_DIM = 64
NSA_HEADS = 8
NSA_KV_HEADS = 2
NSA_GROUP = NSA_HEADS // NSA_KV_HEADS
CMP_LEN = 32
CMP_STRIDE = 16
SEL_BLOCK = 64
SEL_TOPN = 16
WINDOW = 512
DIFF_HEADS = 4
DIFF_VDIM = 2 * HEAD_DIM
N_EXPERTS = 32
TOP_K = 4
SWIGLU_ALPHA = 1.702
SWIGLU_LIMIT = 7.0
LN_EPS = 1e-5
RMS_EPS = 1e-5
ATTN_SCALE = HEAD_DIM ** -0.5
LOG2E = math.log2(math.e)
DEPTH = 1
DEEPNORM_ALPHA = (2 * DEPTH) ** 0.25
PAGE_SIZE = 128

NSA_SLOPES = tuple(2.0 ** (-8.0 * (i + 1) / NSA_HEADS) for i in range(NSA_HEADS))
DIFF_SLOPES = tuple(2.0 ** (-8.0 * (i + 1) / DIFF_HEADS) for i in range(DIFF_HEADS))

VMEM_LIMIT_BYTES = 56 * 1024 * 1024
LANES = 128
MXU_DIM = 256

MOE_CHUNK = 2 * MXU_DIM
SC_CORES = 2
SC_SUBCORES = 16
SC_WINDOW = 128

NSA_W = 4 * NSA_KV_HEADS * HEAD_DIM
WIN_W = 2 * NSA_KV_HEADS * HEAD_DIM
DIFF_W = 2 * DIFF_HEADS * DIFF_VDIM
DIFF_ROWS = DIFF_W // LANES
_GATE_PAD = LANES - 3 * NSA_HEADS
_Q0, _NSA0, _WIN0, _GT0, _DQ0, _DIFF0, _PROJ_ROWS = 0, 512, 1024, 1280, 1408, 1920, 2944


def _cparams(*sem):
    return pltpu.CompilerParams(dimension_semantics=sem, vmem_limit_bytes=VMEM_LIMIT_BYTES)


def _softmax_parts(s, mask):
    s = jnp.where(mask, s, -jnp.inf)
    m = jnp.max(s, axis=-1, keepdims=True)
    m = jnp.where(jnp.isfinite(m), m, 0.0)
    e = jnp.exp(s - m)
    d = jnp.sum(e, axis=-1, keepdims=True)
    return e, jnp.where(d > 0, d, 1.0)


def _softmax2_parts(s, mask):
    s = jnp.where(mask, s, -jnp.inf)
    m = jnp.max(s, axis=-1, keepdims=True)
    m = jnp.where(jnp.isfinite(m), m, 0.0)
    e = jnp.exp2(s - m)
    d = jnp.sum(e, axis=-1, keepdims=True)
    return e, jnp.where(d > 0, d, 1.0)


def _dot_nt(a, b):
    return lax.dot_general(a, b, (((1,), (1,)), ((), ())), preferred_element_type=F32)


def _dot(a, b):
    return jnp.dot(a, b, preferred_element_type=F32)


def _pad_head(qh, g):
    z = jnp.zeros_like(qh)
    return jnp.concatenate([qh, z] if g == 0 else [z, qh], axis=1)


def _select_blocks(imp, blk, qpos, n_blk_lanes):
    cur = jnp.right_shift(qpos, int(math.log2(SEL_BLOCK)))
    valid = blk * SEL_BLOCK <= qpos
    forced = (blk == 0) | (blk == cur) | (blk == cur - 1)
    val = jnp.where(forced, jnp.inf, jnp.where(valid, imp, -jnp.inf))
    rank = jnp.zeros(val.shape, F32)
    for i in range(n_blk_lanes):
        ci = val[:, i:i + 1]
        beats = (ci > val) | ((ci == val) & (blk > i))
        rank = rank + jnp.where(beats, 1.0, 0.0)
    return jnp.where(rank < SEL_TOPN, 1.0, 0.0)


def _in_proj_kernel(x_ref, w_ref, q_ref, qtb_ref, nsa_ref, nsat_ref, nsatb_ref, win_ref, wint_ref, wintb_ref, gt_ref,
                    dq_ref, diff_ref, diffb_ref):
    tm = x_ref.shape[0]
    xb = x_ref[...].astype(BF16)
    q_ref[...] = _dot_nt(xb, w_ref[_Q0:_NSA0, :])
    qtb_ref[0] = _dot_nt(w_ref[_Q0:_NSA0, :], xb).astype(BF16)
    nsa_ref[...] = _dot_nt(xb, w_ref[_NSA0:_NSA0 + nsa_ref.shape[1], :])
    r = _dot_nt(w_ref[_NSA0:_WIN0, :], xb)
    nsat_ref[0] = r
    nsatb_ref[0] = r.astype(BF16)
    win_ref[...] = _dot_nt(xb, w_ref[_WIN0:_WIN0 + win_ref.shape[1], :])
    r = _dot_nt(w_ref[_WIN0:_GT0, :], xb)
    wint_ref[0] = r
    wintb_ref[0] = r.astype(BF16)
    gt_ref[...] = jax.nn.sigmoid(_dot_nt(xb, w_ref[_GT0:_DQ0, :]))
    dq_ref[...] = _dot_nt(xb, w_ref[_DQ0:_DIFF0, :])
    r = _dot_nt(xb, w_ref[_DIFF0:_PROJ_ROWS, :])
    diffb_ref[...] = r.astype(BF16)
    for j in range(DIFF_ROWS):
        diff_ref[pl.ds(j, tm, stride=DIFF_ROWS), :] = r[:, j * LANES:(j + 1) * LANES]


def _in_proj(x2d, w_t, batch, nsa_tok_w=NSA_W, win_tok_w=WIN_W):
    n = x2d.shape[0]
    t = n // batch
    tm = min(512, t)
    nt = t // tm
    row = lambda b, i: (b * nt + i, 0)
    tr = lambda b, i: (b, 0, i)
    tok = lambda w: pl.BlockSpec((tm, w), row)
    return pl.pallas_call(
        _in_proj_kernel,
        grid=(batch, nt),
        in_specs=[pl.BlockSpec((tm, D_MODEL), row),
                  pl.BlockSpec((_PROJ_ROWS, D_MODEL), lambda b, i: (0, 0))],
        out_specs=(tok(512), pl.BlockSpec((1, 512, tm), tr),
                   tok(nsa_tok_w), pl.BlockSpec((1, NSA_W, tm), tr), pl.BlockSpec((1, NSA_W, tm), tr),
                   tok(win_tok_w), pl.BlockSpec((1, WIN_W, tm), tr), pl.BlockSpec((1, WIN_W, tm), tr),
                   tok(LANES), tok(512), pl.BlockSpec((tm * DIFF_ROWS, LANES), row), tok(DIFF_W)),
        out_shape=(jax.ShapeDtypeStruct((n, 512), F32), jax.ShapeDtypeStruct((batch, 512, t), BF16),
                   jax.ShapeDtypeStruct((n, nsa_tok_w), F32),
                   jax.ShapeDtypeStruct((batch, NSA_W, t), F32), jax.ShapeDtypeStruct((batch, NSA_W, t), BF16),
                   jax.ShapeDtypeStruct((n, win_tok_w), F32),
                   jax.ShapeDtypeStruct((batch, WIN_W, t), F32), jax.ShapeDtypeStruct((batch, WIN_W, t), BF16),
                   jax.ShapeDtypeStruct((n, LANES), F32), jax.ShapeDtypeStruct((n, 512), F32),
                   jax.ShapeDtypeStruct((n * DIFF_ROWS, LANES), F32), jax.ShapeDtypeStruct((n, DIFF_W), BF16)),
        compiler_params=_cparams("parallel", "parallel"),
        name="in_proj",
    )(x2d, w_t)


def _compress_chunks(load_rows, w1, pe, w2, n16):
    cst = _dot(pe[0].astype(BF16), w1[:, 0:LANES]) + _dot(pe[1].astype(BF16), w1[:, LANES:])
    chunks = jnp.concatenate([load_rows(l) for l in range(CMP_STRIDE)], axis=1)
    ab = _dot(chunks.astype(BF16), w1)
    nxt = pltpu.roll(ab[:, LANES:], n16 - 1, 0)
    h = jax.nn.gelu(ab[:, 0:LANES] + nxt + cst[0:1])
    return _dot(h.astype(BF16), w2)


def _compress_kernel(rows_ref, w1_ref, pe_ref, w2_ref, o_ref):
    n16 = rows_ref.shape[1] // CMP_STRIDE
    o_ref[0, 0] = _compress_chunks(lambda l: rows_ref[0, pl.ds(l, n16, stride=CMP_STRIDE), :],
                                   w1_ref[0], pe_ref[0], w2_ref[0], n16)


def _compress_prompt(nsa_rows, w1ab, pe2, w2bd):
    B, T, _ = nsa_rows.shape
    n16 = T // CMP_STRIDE
    return pl.pallas_call(
        _compress_kernel,
        grid=(B, 2),
        in_specs=[pl.BlockSpec((1, T, LANES), lambda b, k: (b, 0, k)),
                  pl.BlockSpec((1,) + w1ab.shape[1:], lambda b, k: (k, 0, 0)),
                  pl.BlockSpec((1,) + pe2.shape[1:], lambda b, k: (k, 0, 0, 0)),
                  pl.BlockSpec((1,) + w2bd.shape[1:], lambda b, k: (k, 0, 0))],
        out_specs=pl.BlockSpec((1, 1, n16, LANES), lambda b, k: (b, k, 0, 0)),
        out_shape=jax.ShapeDtypeStruct((B, 2, n16, LANES), F32),
        compiler_params=_cparams("parallel", "parallel"),
        name="compress_prompt",
    )(nsa_rows, w1ab, pe2, w2bd)


def _select_blocks_t(imp, qpos):
    n_blk = imp.shape[0]
    blk = lax.broadcasted_iota(jnp.int32, imp.shape, 0)
    cur = jnp.right_shift(qpos, int(math.log2(SEL_BLOCK)))
    valid = blk * SEL_BLOCK <= qpos
    forced = (blk == 0) | (blk == cur) | (blk == cur - 1)
    val = jnp.where(forced, jnp.inf, jnp.where(valid, imp, -jnp.inf))
    rank = jnp.zeros(val.shape, F32)
    for i in range(n_blk):
        ci = val[i:i + 1, :]
        beats = (ci > val) | ((ci == val) & (blk > i))
        rank = rank + jnp.where(beats, 1.0, 0.0)
    return jnp.where(rank < SEL_TOPN, 1.0, 0.0)


def _cmp_attn_kernel(q_ref, qt_ref, kc_ref, vc_ref, mt_ref, o_ref, selt_ref, *, tq):
    i = pl.program_id(1)
    ncmp = kc_ref.shape[2]
    kcb = kc_ref[0, 0].astype(BF16)
    vcb = vc_ref[0, 0].astype(BF16)
    qpos_c = i * tq + lax.broadcasted_iota(jnp.int32, (tq, 1), 0)
    end_r = lax.broadcasted_iota(jnp.int32, (1, ncmp), 1) * CMP_STRIDE + (CMP_LEN - 1)
    mask = end_r <= qpos_c
    end_rf = end_r.astype(F32)
    q = q_ref[0] * (ATTN_SCALE * LOG2E)
    for h in range(NSA_HEADS):
        g = h // NSA_GROUP
        qp = _pad_head(q[:, h * HEAD_DIM:(h + 1) * HEAD_DIM], g).astype(BF16)
        e, d = _softmax2_parts(_dot_nt(qp, kcb) + (NSA_SLOPES[h] * LOG2E) * end_rf, mask)
        oh = _dot(e.astype(BF16), vcb)[:, g * HEAD_DIM:(g + 1) * HEAD_DIM] / d
        o_ref[0, :, h * HEAD_DIM:(h + 1) * HEAD_DIM] = oh
    qpos_r = i * tq + lax.broadcasted_iota(jnp.int32, (1, tq), 1)
    end_c = lax.broadcasted_iota(jnp.int32, (ncmp, 1), 0) * CMP_STRIDE + (CMP_LEN - 1)
    mask_t = end_c <= qpos_r
    end_cf = end_c.astype(F32)
    qt = qt_ref[0]
    zero = jnp.zeros((HEAD_DIM, tq), BF16)
    psum = [None, None]
    for h in range(NSA_HEADS):
        g = h // NSA_GROUP
        qh = qt[h * HEAD_DIM:(h + 1) * HEAD_DIM, :]
        qpt = jnp.concatenate([qh, zero] if g == 0 else [zero, qh], axis=0)
        s = _dot(kcb, qpt) * (ATTN_SCALE * LOG2E) + (NSA_SLOPES[h] * LOG2E) * end_cf
        s = jnp.where(mask_t, s, -jnp.inf)
        m = jnp.max(s, axis=0, keepdims=True)
        m = jnp.where(jnp.isfinite(m), m, 0.0)
        e = jnp.exp2(s - m)
        d = jnp.sum(e, axis=0, keepdims=True)
        p = e / jnp.where(d > 0, d, 1.0)
        psum[g] = p if psum[g] is None else psum[g] + p
    sels = []
    for g in range(NSA_KV_HEADS):
        imp = jnp.dot(mt_ref[...], psum[g], precision=HIGHEST, preferred_element_type=F32)
        sels.append(_select_blocks_t(imp, qpos_r))
    sels.append(jnp.zeros((LANES - NSA_KV_HEADS * mt_ref.shape[0], tq), F32))
    selt_ref[0] = jnp.concatenate(sels, axis=0)


def _cmp_attn_prompt(q_nsa, q_tb, kvc, mt_mat, tq=512):
    B, T, _ = q_nsa.shape
    ncmp = kvc.shape[2]
    n_sel = mt_mat.shape[0]
    return pl.pallas_call(
        functools.partial(_cmp_attn_kernel, tq=tq),
        grid=(B, T // tq),
        in_specs=[pl.BlockSpec((1, tq, 512), lambda b, i: (b, i, 0)),
                  pl.BlockSpec((1, 512, tq), lambda b, i: (b, 0, i)),
                  pl.BlockSpec((1, 1, ncmp, LANES), lambda b, i: (b, 0, 0, 0)),
                  pl.BlockSpec((1, 1, ncmp, LANES), lambda b, i: (b, 1, 0, 0)),
                  pl.BlockSpec((n_sel, ncmp), lambda b, i: (0, 0))],
        out_specs=(pl.BlockSpec((1, tq, 512), lambda b, i: (b, i, 0)),
                   pl.BlockSpec((1, LANES, tq), lambda b, i: (b, 0, i))),
        out_shape=(jax.ShapeDtypeStruct((B, T, 512), F32), jax.ShapeDtypeStruct((B, LANES, T), F32)),
        compiler_params=_cparams("parallel", "parallel"),
        name="cmp_attn_prompt",
    )(q_nsa, q_tb, kvc, kvc, mt_mat)


def _sel_win_kernel(q_ref, ks_ref, vs_ref, kw_ref, vw_ref, sel_ref, e_ref, ocmp_ref, gt_ref, o_ref, *, tq, span,
                    kv_step):
    i = pl.program_id(1)
    T = ks_ref.shape[2]
    qpos = i * tq + lax.broadcasted_iota(jnp.int32, (tq, 1), 0)
    q = q_ref[0] * (ATTN_SCALE * LOG2E)
    gt = gt_ref[0]
    ocmp = ocmp_ref[0]
    selb = sel_ref[0].astype(BF16)
    heads = [(h, h // NSA_GROUP, h * HEAD_DIM, (h + 1) * HEAD_DIM) for h in range(NSA_HEADS)]
    qps = [_pad_head(q[:, lo:hi], g).astype(BF16) for _, g, lo, hi in heads]
    start = pl.multiple_of(jnp.maximum(i * tq - WINDOW, 0), LANES)
    wpos = start + lax.broadcasted_iota(jnp.int32, (1, span), 1)
    dist_w = qpos - wpos
    mask_w = (dist_w >= 0) & (dist_w <= WINDOW)
    wposf = wpos.astype(F32)
    kwb = kw_ref[0, :, pl.ds(start, span)]
    vwb = vw_ref[0, :, pl.ds(start, span)]
    partial = []
    for h, g, lo, hi in heads:
        s = _dot(qps[h], kwb) + (NSA_SLOPES[h] * LOG2E) * wposf
        e, d = _softmax2_parts(s, mask_w)
        o_win = _dot_nt(e.astype(BF16), vwb)[:, g * HEAD_DIM:(g + 1) * HEAD_DIM] / d
        partial.append(gt[:, h:h + 1] * ocmp[:, lo:hi] + gt[:, 2 * NSA_HEADS + h:2 * NSA_HEADS + h + 1] * o_win)
    tiles_per_step = kv_step // tq
    for br in range(T // kv_step):
        @pl.when((i >= br * tiles_per_step) & (i < (br + 1) * tiles_per_step))
        def _(br=br):
            kv = (br + 1) * kv_step
            kpos = lax.broadcasted_iota(jnp.int32, (1, kv), 1)
            causal = kpos <= qpos
            kposf = kpos.astype(F32)
            ksb = ks_ref[0, :, 0:kv]
            vsb = vs_ref[0, :, 0:kv]
            for g in range(NSA_KV_HEADS):
                expand = lax.dot_general(selb, e_ref[g, :, 0:kv], (((0,), (0,)), ((), ())), preferred_element_type=F32)
                mask_s = (expand > 0.5) & causal
                for h, _, lo, hi in heads[g * NSA_GROUP:(g + 1) * NSA_GROUP]:
                    s = _dot(qps[h], ksb) + (NSA_SLOPES[h] * LOG2E) * kposf
                    e, d = _softmax2_parts(s, mask_s)
                    o_sel = _dot_nt(e.astype(BF16), vsb)[:, g * HEAD_DIM:(g + 1) * HEAD_DIM] / d
                    o_ref[0, :, lo:hi] = partial[h] + gt[:, NSA_HEADS + h:NSA_HEADS + h + 1] * o_sel


def _sel_win_prompt(q_nsa, nsa_t, win_t, sel_t, e_mat, o_cmp, gates, tq=256):
    B, T, _ = q_nsa.shape
    span = WINDOW + tq
    kv = lambda j: pl.BlockSpec((1, LANES, T), lambda b, i: (b, j, 0))
    return pl.pallas_call(
        functools.partial(_sel_win_kernel, tq=tq, span=span, kv_step=min(2 * tq, T)),
        grid=(B, T // tq),
        in_specs=[pl.BlockSpec((1, tq, 512), lambda b, i: (b, i, 0)),
                  kv(2), kv(3), kv(0), kv(1),
                  pl.BlockSpec((1, LANES, tq), lambda b, i: (b, 0, i)),
                  pl.BlockSpec((NSA_KV_HEADS, LANES, T), lambda b, i: (0, 0, 0)),
                  pl.BlockSpec((1, tq, 512), lambda b, i: (b, i, 0)),
                  pl.BlockSpec((1, tq, LANES), lambda b, i: (b, i, 0))],
        out_specs=pl.BlockSpec((1, tq, 512), lambda b, i: (b, i, 0)),
        out_shape=jax.ShapeDtypeStruct((B, T, 512), F32),
        compiler_params=_cparams("parallel", "parallel"),
        name="sel_win_prompt",
    )(q_nsa, nsa_t, nsa_t, win_t, win_t, sel_t, e_mat, o_cmp, gates)


def _diff_kernel(lam_ref, sl_ref, q_ref, k_ref, v_ref, g_ref, o_ref, *, tq, lam_init):
    i = pl.program_id(2)
    T = k_ref.shape[1]
    lam = lam_ref[0:1, 0:1]
    q = q_ref[0] * (ATTN_SCALE * LOG2E)
    qps = [_pad_head(q[:, c * HEAD_DIM:(c + 1) * HEAD_DIM], c).astype(BF16) for c in range(2)]
    tri = lax.broadcasted_iota(jnp.int32, (tq, tq), 0) >= lax.broadcasted_iota(jnp.int32, (tq, tq), 1)
    for br in range(T // tq):
        @pl.when(i == br)
        def _(br=br):
            off = br * tq
            kv = off + tq
            kb = k_ref[0, 0:kv, :]
            vb = v_ref[0, 0:kv, :]
            col = (sl_ref[0, :, 0:1] * LOG2E) * lax.broadcasted_iota(jnp.int32, (1, kv), 1).astype(F32)
            outs = []
            for c in range(2):
                s = _dot_nt(qps[c], kb) + col
                s_d = jnp.where(tri, s[:, off:], -jnp.inf)
                m = jnp.max(s_d, axis=-1, keepdims=True)
                if off:
                    m = jnp.maximum(m, jnp.max(s[:, :off], axis=-1, keepdims=True))
                e_d = jnp.exp2(s_d - m)
                d = jnp.sum(e_d, axis=-1, keepdims=True)
                o = _dot(e_d.astype(BF16), vb[off:, :])
                if off:
                    e_o = jnp.exp2(s[:, :off] - m)
                    d = d + jnp.sum(e_o, axis=-1, keepdims=True)
                    o = o + _dot(e_o.astype(BF16), vb[:off, :])
                outs.append(o / d)
            od = outs[0] - lam * outs[1]
            od = od * lax.rsqrt(jnp.mean(jnp.square(od), axis=-1, keepdims=True) + RMS_EPS)
            o_ref[0] = od * g_ref[...] * (1.0 - lam_init)


def _diff_prompt(lam_tile, dq, diff_b, subln_g, lam_init, tq=256):
    B, T, _ = dq.shape
    slopes = jnp.asarray(np.tile(np.asarray(DIFF_SLOPES, np.float32)[:, None, None], (1, 1, LANES)))
    return pl.pallas_call(
        functools.partial(_diff_kernel, tq=tq, lam_init=lam_init),
        grid=(B, DIFF_HEADS, T // tq),
        in_specs=[pl.BlockSpec((8, LANES), lambda b, h, i: (0, 0)),
                  pl.BlockSpec((1, 1, LANES), lambda b, h, i: (h, 0, 0)),
                  pl.BlockSpec((1, tq, LANES), lambda b, h, i: (b, i, h)),
                  pl.BlockSpec((1, T, LANES), lambda b, h, i: (b, 0, h)),
                  pl.BlockSpec((1, T, LANES), lambda b, h, i: (b, 0, DIFF_HEADS + h)),
                  pl.BlockSpec((1, DIFF_VDIM), lambda b, h, i: (0, 0))],
        out_specs=pl.BlockSpec((1, tq, LANES), lambda b, h, i: (b, i, h)),
        out_shape=jax.ShapeDtypeStruct((B, T, DIFF_HEADS * DIFF_VDIM), F32),
        compiler_params=_cparams("parallel", "parallel", "parallel"),
        name="diff_prompt",
    )(lam_tile, slopes, dq, diff_b, diff_b, subln_g)


def _layer_norm(z, g, b):
    mu = jnp.mean(z, axis=-1, keepdims=True)
    zc = z - mu
    var = jnp.mean(jnp.square(zc), axis=-1, keepdims=True)
    return zc * lax.rsqrt(var + LN_EPS) * g + b


def _pack_rows(x, ref):
    w = x.shape[1] // 2
    hi = lax.bitcast_convert_type(x[:, :w].astype(BF16).astype(F32), jnp.uint32)
    lo = lax.bitcast_convert_type(x[:, w:].astype(BF16).astype(F32), jnp.uint32)
    packed = hi | (lo >> 16)
    ref[0] = packed[:, :w // 2]
    ref[1] = packed[:, w // 2:]


def _unpack_rows(u0, u1):
    hi = [lax.bitcast_convert_type(u & jnp.uint32(0xFFFF0000), F32) for u in (u0, u1)]
    lo = [lax.bitcast_convert_type(u << 16, F32) for u in (u0, u1)]
    return jnp.concatenate(hi + lo, axis=1)


def _out_proj_kernel(on_ref, od_ref, x_ref, wa_ref, wb_ref, g_ref, b_ref, wr_ref, br_ref,
                     x1_ref, x1b_ref, x1p_ref, sel_ref, gate_ref, tope_ref, gate8_ref, cnt_ref):
    y = _dot(on_ref[...].astype(BF16), wa_ref[...]) + _dot(od_ref[...].astype(BF16), wb_ref[...])
    x1 = _layer_norm(DEEPNORM_ALPHA * x_ref[...] + y, g_ref[...], b_ref[...])
    x1_ref[...] = x1
    x1b_ref[...] = x1.astype(BF16)
    _pack_rows(x1, x1p_ref)
    logits = lax.dot_general(wr_ref[...], x1, (((1,), (1,)), ((), ())), precision=HIGHEST,
                             preferred_element_type=F32) + br_ref[...]
    eidx = lax.broadcasted_iota(jnp.int32, logits.shape, 0).astype(F32)
    sel = jnp.zeros(logits.shape, F32)
    picked, vals = [], []
    for k in range(TOP_K):
        cur = jnp.where(sel > 0.5, -jnp.inf, logits)
        m = jnp.max(cur, axis=0, keepdims=True)
        first = jnp.min(jnp.where(cur == m, eidx, float(N_EXPERTS)), axis=0, keepdims=True)
        sel = jnp.where(eidx == first, 1.0, sel)
        picked.append(first)
        vals.append(m)
    ex = jnp.where(sel > 0.5, jnp.exp(logits - vals[0]), 0.0)
    denom = jnp.sum(ex, axis=0, keepdims=True)
    sel_ref[...] = sel
    gate_ref[...] = ex / denom
    pad = [jnp.zeros_like(denom)] * (8 - TOP_K)
    tope_ref[...] = jnp.concatenate(picked + pad, axis=0)
    gate8_ref[...] = jnp.concatenate([jnp.exp(v - vals[0]) / denom for v in vals] + pad, axis=0)
    cnt_ref[0] = jnp.sum(sel, axis=1, keepdims=True)


def _out_proj_ln_router(o_nsa, o_diff, x2d, wa, wb, ln_g, ln_b, wr_t, br):
    n = x2d.shape[0]
    tm = min(512, n)
    row = lambda i: (i, 0)
    fix = lambda i: (0, 0)
    col = lambda i: (0, i)
    return pl.pallas_call(
        _out_proj_kernel,
        grid=(n // tm,),
        in_specs=[pl.BlockSpec((tm, 512), row), pl.BlockSpec((tm, 512), row), pl.BlockSpec((tm, D_MODEL), row),
                  pl.BlockSpec((512, D_MODEL), fix), pl.BlockSpec((512, D_MODEL), fix),
                  pl.BlockSpec((1, D_MODEL), fix), pl.BlockSpec((1, D_MODEL), fix),
                  pl.BlockSpec((N_EXPERTS, D_MODEL), fix), pl.BlockSpec((N_EXPERTS, 1), fix)],
        out_specs=(pl.BlockSpec((tm, D_MODEL), row), pl.BlockSpec((tm, D_MODEL), row),
                   pl.BlockSpec((2, tm, D_MODEL // 4), lambda i: (0, i, 0)),
                   pl.BlockSpec((N_EXPERTS, tm), col), pl.BlockSpec((N_EXPERTS, tm), col),
                   pl.BlockSpec((8, tm), col), pl.BlockSpec((8, tm), col),
                   pl.BlockSpec((1, N_EXPERTS, 1), lambda i: (i, 0, 0))),
        out_shape=(jax.ShapeDtypeStruct((n, D_MODEL), F32), jax.ShapeDtypeStruct((n, D_MODEL), BF16),
                   jax.ShapeDtypeStruct((2, n, D_MODEL // 4), jnp.uint32),
                   jax.ShapeDtypeStruct((N_EXPERTS, n), F32), jax.ShapeDtypeStruct((N_EXPERTS, n), F32),
                   jax.ShapeDtypeStruct((8, n), F32), jax.ShapeDtypeStruct((8, n), F32),
                   jax.ShapeDtypeStruct((n // tm, N_EXPERTS, 1), F32)),
        compiler_params=_cparams("parallel"),
        name="out_proj_ln_router",
    )(o_nsa, o_diff, x2d, wa, wb, ln_g, ln_b, wr_t, br)


def _load_expert_weights(wgu_ref, wd_ref, p_ref, wg_s, wl_s, wd_s):
    half = MXU_DIM // 2
    for m in range(wgu_ref.shape[2] // MXU_DIM):
        y = _dot(wgu_ref[0, :, m * MXU_DIM:(m + 1) * MXU_DIM].astype(BF16), p_ref[...])
        wg_s[:, m * half:(m + 1) * half] = y[:, :half].astype(BF16)
        wl_s[:, m * half:(m + 1) * half] = y[:, half:].astype(BF16)
    wd_s[...] = wd_ref[0].astype(BF16)


def _deinterleave_matrix():
    half = MXU_DIM // 2
    p_np = np.zeros((MXU_DIM, MXU_DIM), np.float32)
    p_np[2 * np.arange(half), np.arange(half)] = 1.0
    p_np[2 * np.arange(half) + 1, half + np.arange(half)] = 1.0
    return jnp.asarray(p_np, BF16)


_EXPERT_WEIGHT_SCRATCH = [pltpu.VMEM((D_MODEL, D_MODEL), BF16)] * 3


def _moe_pos_kernel(sel_ref, u_ref, pos_ref, cnt_ref, *, n_valid):
    s_tile = sel_ref.shape[1]
    tok = pl.program_id(0) * s_tile + lax.broadcasted_iota(jnp.int32, (1, s_tile), 1)
    sel = jnp.where(tok < n_valid, sel_ref[...], 0.0)
    rank = _dot(sel.astype(BF16), u_ref[...])
    pos_ref[...] = jnp.where(sel > 0.5, rank, -1.0)
    cnt_ref[0] = jnp.sum(sel, axis=1, keepdims=True)


def _moe_positions(sel_t, s_tile):
    n = sel_t.shape[1]
    ns = pl.cdiv(n, s_tile)
    upper = jnp.asarray(np.triu(np.ones((s_tile, s_tile), np.float32), 1), BF16)
    return pl.pallas_call(
        functools.partial(_moe_pos_kernel, n_valid=n),
        grid=(ns,),
        in_specs=[pl.BlockSpec((N_EXPERTS, s_tile), lambda s: (0, s)),
                  pl.BlockSpec((s_tile, s_tile), lambda s: (0, 0))],
        out_specs=(pl.BlockSpec((N_EXPERTS, s_tile), lambda s: (0, s)),
                   pl.BlockSpec((1, N_EXPERTS, 1), lambda s: (s, 0, 0))),
        out_shape=(jax.ShapeDtypeStruct((N_EXPERTS, ns * s_tile), F32),
                   jax.ShapeDtypeStruct((ns, N_EXPERTS, 1), F32)),
        compiler_params=_cparams("parallel"),
        name="moe_positions",
    )(sel_t, upper)


def _moe_kernel(nch_ref, x_ref, pos_ref, gate_ref, wgu_ref, wd_ref, p_ref, bg_ref, bl_ref, bd_ref, y_ref,
                wg_s, wl_s, wd_s, *, chunk, n_valid):
    s = pl.program_id(0)
    e = pl.program_id(1)
    s_tile = x_ref.shape[0]
    n_chunks = nch_ref[s * N_EXPERTS + e]

    @pl.when(e == 0)
    def _():
        y_ref[...] = jnp.zeros_like(y_ref)

    @pl.when(n_chunks > 0)
    def _():
        _load_expert_weights(wgu_ref, wd_ref, p_ref, wg_s, wl_s, wd_s)

    pos = pos_ref[0, 0].astype(jnp.int32)
    gate = gate_ref[0, 0]
    slot0 = lax.broadcasted_iota(jnp.int32, (chunk, 1), 0)

    def body(j, carry):
        hit = pos == slot0 + j * chunk
        onehot = jnp.where(hit, 1.0, 0.0).astype(BF16)
        x = x_ref[...]
        if n_valid % s_tile:
            row = s * s_tile + lax.broadcasted_iota(jnp.int32, (s_tile, 1), 0)
            x = jnp.where(row < n_valid, x, jnp.zeros_like(x))
        xs = _dot(onehot, x).astype(BF16)
        hg = jnp.minimum(_dot(xs, wg_s[...]) + bg_ref[0], SWIGLU_LIMIT)
        hl = jnp.clip(_dot(xs, wl_s[...]) + bl_ref[0], -SWIGLU_LIMIT, SWIGLU_LIMIT)
        a = (hl + 1.0) * hg * jax.nn.sigmoid(SWIGLU_ALPHA * hg)
        out = _dot(a.astype(BF16), wd_s[...]) + bd_ref[0]
        gslot = jnp.sum(jnp.where(hit, gate, 0.0), axis=1, keepdims=True)
        outg = (out * gslot).astype(BF16)
        y_ref[...] += lax.dot_general(onehot, outg, (((0,), (0,)), ((), ())), preferred_element_type=F32)
        return carry

    lax.fori_loop(0, n_chunks, body, 0)


def _moe(x1b, pos_t, gate_t, counts, wgu, wd, p_mat, bg, bl, bd, s_tile, chunk):
    n = x1b.shape[0]
    ns = pl.cdiv(n, s_tile)
    nch = ((counts.reshape(ns * N_EXPERTS) + (chunk - 1)) // chunk).astype(jnp.int32)
    pos4 = pos_t.reshape(N_EXPERTS, ns, 1, s_tile)
    gate4 = jnp.pad(gate_t, ((0, 0), (0, ns * s_tile - n))).reshape(N_EXPERTS, ns, 1, s_tile)
    wspec = lambda a: pl.BlockSpec((1,) + a.shape[1:], lambda s, e, nch: (e, 0, 0))
    bspec = pl.BlockSpec((1, 1, D_MODEL), lambda s, e, nch: (e, 0, 0))
    rspec = pl.BlockSpec((1, 1, 1, s_tile), lambda s, e, nch: (e, s, 0, 0))
    return pl.pallas_call(
        functools.partial(_moe_kernel, chunk=chunk, n_valid=n),
        grid_spec=pltpu.PrefetchScalarGridSpec(
            num_scalar_prefetch=1,
            grid=(ns, N_EXPERTS),
            in_specs=[pl.BlockSpec((s_tile, D_MODEL), lambda s, e, nch: (s, 0)),
                      rspec, rspec, wspec(wgu), wspec(wd), pl.BlockSpec(p_mat.shape, lambda s, e, nch: (0, 0)),
                      bspec, bspec, bspec],
            out_specs=pl.BlockSpec((s_tile, D_MODEL), lambda s, e, nch: (s, 0)),
            scratch_shapes=_EXPERT_WEIGHT_SCRATCH),
        out_shape=jax.ShapeDtypeStruct((n, D_MODEL), F32),
        compiler_params=_cparams("parallel", "arbitrary"),
        name="moe_experts",
    )(nch, x1b, pos4, gate4, wgu, wd, p_mat, bg, bl, bd)


def _ln2_kernel(x_ref, f_ref, g_ref, b_ref, o_ref):
    o_ref[...] = _layer_norm(DEEPNORM_ALPHA * x_ref[...] + f_ref[...], g_ref[...], b_ref[...])


def _residual_ln(x1, f, g, b):
    n = x1.shape[0]
    tm = min(512, n)
    row = lambda i: (i, 0)
    fix = lambda i: (0, 0)
    return pl.pallas_call(
        _ln2_kernel,
        grid=(n // tm,),
        in_specs=[pl.BlockSpec((tm, D_MODEL), row), pl.BlockSpec((tm, D_MODEL), row),
                  pl.BlockSpec((1, D_MODEL), fix), pl.BlockSpec((1, D_MODEL), fix)],
        out_specs=pl.BlockSpec((tm, D_MODEL), row),
        out_shape=jax.ShapeDtypeStruct((n, D_MODEL), F32),
        compiler_params=_cparams("parallel"),
        name="residual_ln2",
    )(x1, f, g, b)


def _post_mixer(o_nsa, o_diff, x2d, p, s_tile, chunk):
    x1, x1b, _, sel_t, gate_t, _, _, _ = _out_proj_ln_router(o_nsa, o_diff, x2d, p["wo_a"], p["wo_b"], p["ln1_g"],
                                                             p["ln1_b"], p["wr_t"], p["br"])
    pos_t, counts = _moe_positions(sel_t, s_tile)
    f = _moe(x1b, pos_t, gate_t, counts, p["wgu"], p["wd"], p["p_mat"], p["bg"], p["bl"], p["bd"], s_tile, chunk)
    return _residual_ln(x1, f, p["ln2_g"], p["ln2_b"])


def _moe_slots_kernel(sel_ref, tope_ref, pstart_ref, u_ref, slot_ref, carry_ref):
    @pl.when(pl.program_id(0) == 0)
    def _():
        carry_ref[...] = jnp.zeros_like(carry_ref)

    sel = sel_ref[...]
    slot_all = pstart_ref[...] + carry_ref[...] + _dot(sel.astype(BF16), u_ref[...])
    eidx = lax.broadcasted_iota(jnp.int32, sel.shape, 0).astype(F32)
    tope = tope_ref[...]
    rows = [jnp.sum(jnp.where(eidx == tope[k:k + 1], slot_all, 0.0), axis=0, keepdims=True) for k in range(TOP_K)]
    rows += [jnp.zeros_like(rows[0])] * (8 - TOP_K)
    slot_ref[...] = jnp.concatenate(rows, axis=0).astype(jnp.int32)
    carry_ref[...] += jnp.sum(sel, axis=1, keepdims=True)


def _moe_slots(sel_t, tope, pstart, tm=2048):
    n = sel_t.shape[1]
    upper = jnp.asarray(np.triu(np.ones((tm, tm), np.float32), 1), BF16)
    return pl.pallas_call(
        _moe_slots_kernel,
        grid=(n // tm,),
        in_specs=[pl.BlockSpec((N_EXPERTS, tm), lambda i: (0, i)), pl.BlockSpec((8, tm), lambda i: (0, i)),
                  pl.BlockSpec((N_EXPERTS, 1), lambda i: (0, 0)), pl.BlockSpec((tm, tm), lambda i: (0, 0))],
        out_specs=pl.BlockSpec((8, tm), lambda i: (0, i)),
        out_shape=jax.ShapeDtypeStruct((8, n), jnp.int32),
        scratch_shapes=[pltpu.VMEM((N_EXPERTS, 1), F32)],
        compiler_params=_cparams("arbitrary"),
        name="moe_slots",
    )(sel_t, tope, pstart, upper)


def _sc_mesh():
    return plsc.VectorSubcoreMesh(core_axis_name="c", subcore_axis_name="s", num_cores=SC_CORES,
                                  num_subcores=SC_SUBCORES)


def _sc_scatter_rows(src, idx, n_out):
    n_rows, width = src.shape
    n_idx = idx.shape[0]
    per_core = n_rows // SC_WINDOW // SC_CORES

    @functools.partial(pl.kernel, out_type=jax.ShapeDtypeStruct((n_out, width), src.dtype), mesh=_sc_mesh(),
                       scratch_types=[], name="moe_dispatch_rows")
    def scatter(src_hbm, idx_hbm, out_hbm):
        def body(src_vmem, idx_vmem):
            for k in range(n_idx):
                pltpu.sync_copy(src_vmem, out_hbm.at[idx_vmem.at[k]])

        pltpu.emit_pipeline(
            body,
            grid=(SC_CORES, per_core),
            in_specs=[pl.BlockSpec((SC_WINDOW, width), lambda c, j: (c * per_core + j, 0)),
                      pl.BlockSpec((n_idx, SC_WINDOW), lambda c, j: (0, c * per_core + j))],
            out_specs=[],
            core_axis_name=("c", "s"),
            dimension_semantics=(pltpu.PARALLEL, pltpu.PARALLEL),
        )(src_hbm, idx_hbm)

    return scatter(src, idx)


def _sc_gather_rows(table, idx):
    n_rows = idx.shape[1]
    width = table.shape[1]
    per_core = n_rows // SC_WINDOW // SC_CORES

    @functools.partial(pl.kernel, out_type=jax.ShapeDtypeStruct((n_rows, width), table.dtype), mesh=_sc_mesh(),
                       scratch_types=[], name="moe_return_rows")
    def gather(table_hbm, idx_hbm, out_hbm):
        def body(idx_vmem, out_vmem):
            pltpu.sync_copy(table_hbm.at[idx_vmem.at[0]], out_vmem)

        pltpu.emit_pipeline(
            body,
            grid=(SC_CORES, per_core),
            in_specs=[pl.BlockSpec((1, SC_WINDOW), lambda c, j: (0, c * per_core + j))],
            out_specs=[pl.BlockSpec((SC_WINDOW, width), lambda c, j: (c * per_core + j, 0))],
            core_axis_name=("c", "s"),
            dimension_semantics=(pltpu.PARALLEL, pltpu.PARALLEL),
        )(idx_hbm, out_hbm)

    return gather(table, idx)


def _moe_ffn_kernel(blk_e_ref, nvalid_ref, xs_ref, wgu_ref, wd_ref, p_ref, bg_ref, bl_ref, bd_ref, o_ref,
                    wg_s, wl_s, wd_s):
    i = pl.program_id(0)
    nv = nvalid_ref[i]
    new_expert = (i == 0) | (blk_e_ref[i] != blk_e_ref[jnp.maximum(i - 1, 0)])

    @pl.when(new_expert & (nv > 0))
    def _():
        _load_expert_weights(wgu_ref, wd_ref, p_ref, wg_s, wl_s, wd_s)

    @pl.when(nv > 0)
    def _():
        live = lax.broadcasted_iota(jnp.int32, (o_ref.shape[1], 1), 0) < nv
        xs = jnp.where(live, _unpack_rows(xs_ref[0], xs_ref[1]), 0.0).astype(BF16)
        hg = jnp.minimum(_dot(xs, wg_s[...]) + bg_ref[0], SWIGLU_LIMIT)
        hl = jnp.clip(_dot(xs, wl_s[...]) + bl_ref[0], -SWIGLU_LIMIT, SWIGLU_LIMIT)
        a = (hl + 1.0) * hg * jax.nn.sigmoid(SWIGLU_ALPHA * hg)
        _pack_rows(_dot(a.astype(BF16), wd_s[...]) + bd_ref[0], o_ref)

    @pl.when(nv == 0)
    def _():
        o_ref[...] = jnp.zeros_like(o_ref)


def _moe_ffn(blk_e, nvalid, xs, wgu, wd, p_mat, bg, bl, bd):
    _, n_slots, half = xs.shape
    wspec = lambda a: pl.BlockSpec((1,) + a.shape[1:], lambda i, be, nv: (be[i], 0, 0))
    bspec = pl.BlockSpec((1, 1, D_MODEL), lambda i, be, nv: (be[i], 0, 0))
    rows = pl.BlockSpec((2, MOE_CHUNK, half), lambda i, be, nv: (0, i, 0))
    return pl.pallas_call(
        _moe_ffn_kernel,
        grid_spec=pltpu.PrefetchScalarGridSpec(
            num_scalar_prefetch=2,
            grid=(n_slots // MOE_CHUNK,),
            in_specs=[rows, wspec(wgu), wspec(wd), pl.BlockSpec(p_mat.shape, lambda i, be, nv: (0, 0)),
                      bspec, bspec, bspec],
            out_specs=rows,
            scratch_shapes=_EXPERT_WEIGHT_SCRATCH),
        out_shape=jax.ShapeDtypeStruct((2, n_slots, half), jnp.uint32),
        compiler_params=_cparams("arbitrary"),
        name="moe_ffn_sorted",
    )(blk_e, nvalid, xs, wgu, wd, p_mat, bg, bl, bd)


def _ln2_combine_kernel(x_ref, r_ref, gate_ref, g_ref, b_ref, o_ref):
    gate = gate_ref[...]
    f = gate[:, 0:1] * _unpack_rows(r_ref[0, 0], r_ref[0, 1])
    for k in range(1, TOP_K):
        f = f + gate[:, k:k + 1] * _unpack_rows(r_ref[k, 0], r_ref[k, 1])
    o_ref[...] = _layer_norm(DEEPNORM_ALPHA * x_ref[...] + f, g_ref[...], b_ref[...])


def _ln2_combine(x1, returned, gate_tok, g, b, tm=256):
    n = x1.shape[0]
    row = lambda i: (i, 0)
    fix = lambda i: (0, 0)
    return pl.pallas_call(
        _ln2_combine_kernel,
        grid=(n // tm,),
        in_specs=[pl.BlockSpec((tm, D_MODEL), row),
                  pl.BlockSpec((TOP_K, 2, tm, D_MODEL // 4), lambda i: (0, 0, i, 0)),
                  pl.BlockSpec((tm, 8), row), pl.BlockSpec((1, D_MODEL), fix), pl.BlockSpec((1, D_MODEL), fix)],
        out_specs=pl.BlockSpec((tm, D_MODEL), row),
        out_shape=jax.ShapeDtypeStruct((n, D_MODEL), F32),
        compiler_params=_cparams("parallel"),
        name="combine_ln2",
    )(x1, returned, gate_tok, g, b)


def _post_mixer_sorted(o_nsa, o_diff, x2d, p):
    n = x2d.shape[0]
    x1, _, x1p, sel_t, _, tope, gate8, cnt = _out_proj_ln_router(o_nsa, o_diff, x2d, p["wo_a"], p["wo_b"], p["ln1_g"],
                                                                 p["ln1_b"], p["wr_t"], p["br"])
    counts = jnp.sum(cnt[:, :, 0], axis=0).astype(jnp.int32)
    padded = (counts + (MOE_CHUNK - 1)) // MOE_CHUNK * MOE_CHUNK
    pend = jnp.cumsum(padded)
    pstart = pend - padded
    n_blocks = n * TOP_K // MOE_CHUNK + N_EXPERTS
    n_slots = n_blocks * MOE_CHUNK
    blk0 = jnp.arange(n_blocks, dtype=jnp.int32)[:, None] * MOE_CHUNK
    blk_e = jnp.minimum(jnp.sum((blk0 >= pend[None, :]).astype(jnp.int32), axis=1), N_EXPERTS - 1)
    in_region = (blk0 >= pstart[None, :]) & (blk0 < pend[None, :])
    nvalid = jnp.sum(jnp.where(in_region, jnp.clip(pstart + counts - blk0, 0, MOE_CHUNK), 0), axis=1)
    slot = _moe_slots(sel_t, tope, pstart.astype(F32)[:, None])[:TOP_K]
    half = D_MODEL // 4
    slot2 = jnp.concatenate([slot, slot + n_slots], axis=1)
    xs = _sc_scatter_rows(x1p.reshape(2 * n, half), slot2, 2 * n_slots)
    outs = _moe_ffn(blk_e, nvalid, xs.reshape(2, n_slots, half), p["wgu"], p["wd"], p["p_mat"], p["bg"], p["bl"],
                    p["bd"])
    returned = _sc_gather_rows(outs.reshape(2 * n_slots, half), slot2.reshape(1, TOP_K * 2 * n))
    return _ln2_combine(x1, returned.reshape(TOP_K, 2, n, half), gate8.T, p["ln2_g"], p["ln2_b"])


def _sample_kernel(pt_ref, qm_ref, qd_ref, gt_ref, nrow_ref, wrow_ref, wcol_ref, drow_ref, win_ref,
                   w1_ref, pe_ref, w2_ref, m_ref, e_ref, r_ref, sn_ref, sd_ref, lam_ref, subg_ref, *rest,
                   n_pages, past_len, lam_init):
    nsa_pages = rest[:n_pages]
    diff_pages = rest[n_pages:2 * n_pages]
    onsa_ref, odiff_ref, nwin_ref, x_ref = rest[2 * n_pages:]
    n16 = past_len // CMP_STRIDE
    qpos = past_len
    kpos = lax.broadcasted_iota(jnp.int32, (1, past_len), 1)
    dist_k = (qpos - kpos).astype(F32)

    for j, pg in enumerate(nsa_pages):
        for kind in range(2):
            x_ref[kind, j * PAGE_SIZE:(j + 1) * PAGE_SIZE, :] = pg[kind * LANES:(kind + 1) * LANES, :].T

    kv_cmp = [_compress_chunks(lambda l, kind=kind: x_ref[kind, pl.ds(l, n16, stride=CMP_STRIDE), :],
                               w1_ref[kind], pe_ref[kind], w2_ref[kind], n16) for kind in range(2)]

    qm = qm_ref[0]
    qmb = qm.astype(BF16)
    slope_n = sn_ref[:, 0:1]
    nrow = nrow_ref[0]
    wrow = wrow_ref[0]
    gt = gt_ref[0]

    def new_key_score(qrows, krow):
        return jnp.sum(qrows * krow, axis=1, keepdims=True) * ATTN_SCALE

    cmp_end = lax.broadcasted_iota(jnp.int32, (1, n16), 1) * CMP_STRIDE + (CMP_LEN - 1)
    dist_c = qpos - cmp_end
    s = _dot_nt(qmb, kv_cmp[0].astype(BF16)) * ATTN_SCALE - slope_n * dist_c.astype(F32)
    e, d = _softmax_parts(s, dist_c >= 0)
    p_cmp = e / d
    o_cmp = _dot(p_cmp.astype(BF16), kv_cmp[1].astype(BF16))
    pgrp = jnp.dot(r_ref[...], p_cmp, precision=HIGHEST, preferred_element_type=F32)
    imp = jnp.dot(pgrp, m_ref[...], precision=HIGHEST, preferred_element_type=F32)
    nbl = m_ref.shape[1]
    blk = lax.broadcasted_iota(jnp.int32, (NSA_HEADS, nbl), 1)
    sel = _select_blocks(imp, blk, jnp.full((NSA_HEADS, 1), qpos, jnp.int32), past_len // SEL_BLOCK + 1)
    mask_s = _dot(sel.astype(BF16), e_ref[...]) > 0.5

    _sample_diff_branch(qd_ref, drow_ref, sd_ref, lam_ref, subg_ref, diff_pages, odiff_ref, dist_k, lam_init)

    s = jnp.concatenate([_dot(qmb, pg[2 * LANES:3 * LANES, :].astype(BF16)) for pg in nsa_pages], axis=1)
    s = jnp.where(mask_s, s * ATTN_SCALE - slope_n * dist_k, -jnp.inf)
    s_new = new_key_score(qm, nrow[:, 2 * LANES:3 * LANES])
    m = jnp.maximum(jnp.max(s, axis=1, keepdims=True), s_new)
    e = jnp.exp(s - m)
    e_new = jnp.exp(s_new - m)
    d = jnp.sum(e, axis=1, keepdims=True) + e_new
    eb = e.astype(BF16)
    acc = e_new * nrow[:, 3 * LANES:4 * LANES]
    for j, pg in enumerate(nsa_pages):
        acc = acc + _dot_nt(eb[:, j * PAGE_SIZE:(j + 1) * PAGE_SIZE], pg[3 * LANES:4 * LANES, :].astype(BF16))
    o_sel = acc / d

    nwin = win_ref.shape[2]
    wpos = past_len - nwin + lax.broadcasted_iota(jnp.int32, (1, nwin), 1)
    dist_w = qpos - wpos
    mask_w = (dist_w >= 0) & (dist_w <= WINDOW)
    win = win_ref[0]
    s = _dot(qmb, win[0:LANES, :].astype(BF16)) * ATTN_SCALE - slope_n * dist_w.astype(F32)
    s = jnp.where(mask_w, s, -jnp.inf)
    s_new = new_key_score(qm, wrow[:, 0:LANES])
    m = jnp.maximum(jnp.max(s, axis=1, keepdims=True), s_new)
    e = jnp.exp(s - m)
    e_new = jnp.exp(s_new - m)
    d = jnp.sum(e, axis=1, keepdims=True) + e_new
    o_win = (_dot_nt(e.astype(BF16), win[LANES:2 * LANES, :].astype(BF16)) + e_new * wrow[:, LANES:2 * LANES]) / d

    onsa_ref[0] = gt[:, 0:1] * o_cmp + gt[:, 1:2] * o_sel + gt[:, 2:3] * o_win

    lane = lax.broadcasted_iota(jnp.int32, win.shape, 1)
    nwin_ref[0] = jnp.where(lane == nwin - 1, wcol_ref[0], pltpu.roll(win, nwin - 1, 1))


def _sample_diff_branch(qd_ref, drow_ref, sd_ref, lam_ref, subg_ref, diff_pages, odiff_ref, dist_k, lam_init):
    drow = drow_ref[0]

    def diff_rows(pg, j):
        return pg[pl.ds(j, PAGE_SIZE, stride=DIFF_ROWS), :]

    qds = [qd_ref[0, h] for h in range(DIFF_HEADS)]
    qdb = [qh.astype(BF16) for qh in qds]
    s_pages = []
    for pg in diff_pages:
        sp = _dot_nt(qdb[0], diff_rows(pg, 0).astype(BF16))
        for h in range(1, DIFF_HEADS):
            sp = sp + _dot_nt(qdb[h], diff_rows(pg, h).astype(BF16))
        s_pages.append(sp)
    s = jnp.concatenate(s_pages, axis=1) * ATTN_SCALE - sd_ref[:, 0:1] * dist_k
    prod = qds[0] * drow[0:1]
    for h in range(1, DIFF_HEADS):
        prod = prod + qds[h] * drow[h:h + 1]
    s_new = jnp.sum(prod, axis=1, keepdims=True) * ATTN_SCALE
    m = jnp.maximum(jnp.max(s, axis=1, keepdims=True), s_new)
    e = jnp.exp(s - m)
    e_new = jnp.exp(s_new - m)
    d = jnp.sum(e, axis=1, keepdims=True) + e_new
    lam = lam_ref[0:1, 0:1]
    p = e / d
    p_new = e_new / d
    a = (p - lam * pltpu.roll(p, DIFF_HEADS, 0)).astype(BF16)
    a_new = p_new - lam * pltpu.roll(p_new, DIFF_HEADS, 0)
    accs = []
    for h in range(DIFF_HEADS):
        acc = a_new * drow[DIFF_HEADS + h:DIFF_HEADS + h + 1]
        for j, pg in enumerate(diff_pages):
            acc = acc + _dot(a[:, j * PAGE_SIZE:(j + 1) * PAGE_SIZE], diff_rows(pg, DIFF_HEADS + h).astype(BF16))
        accs.append(acc)
    acc = jnp.concatenate(accs, axis=1)
    lane_head = lax.broadcasted_iota(jnp.int32, acc.shape, 1) // DIFF_VDIM
    row = lax.broadcasted_iota(jnp.int32, acc.shape, 0)
    own = lane_head == row
    ms = jnp.sum(jnp.where(own, jnp.square(acc), 0.0), axis=1, keepdims=True) / DIFF_VDIM
    odiff_ref[0] = acc * lax.rsqrt(ms + RMS_EPS) * subg_ref[...] * (1.0 - lam_init)


def _sample_attention(page_table, qm, qd4, gt8, nrow, wrow, wcol, drow, win_t, pool_nsa, pool_diff,
                      w1ab, pe2, w2bd, m_mat, e_mat, r_mat, sn, sd, lam_tile, subg_tile, past_len, lam_init):
    B = qm.shape[0]
    n_pages = page_table.shape[1]
    nwin = win_t.shape[2]

    def per_b(shape):
        nd = len(shape)
        return pl.BlockSpec((1,) + shape, lambda b, pt: (b,) + (0,) * nd)

    def fixed(shape):
        nd = len(shape)
        return pl.BlockSpec(shape, lambda b, pt: (0,) * nd)

    def page_spec(pool, j):
        return pl.BlockSpec((None,) + pool.shape[1:], lambda b, pt: (pt[b, j], 0, 0))

    in_specs = [per_b((NSA_HEADS, LANES)), per_b((DIFF_HEADS, 8, LANES)), per_b((NSA_HEADS, LANES)),
                per_b((1, NSA_W)), per_b((1, WIN_W)), per_b((WIN_W, 1)), per_b((DIFF_ROWS, LANES)),
                per_b((WIN_W, nwin)),
                fixed(w1ab.shape), fixed(pe2.shape), fixed(w2bd.shape), fixed(m_mat.shape), fixed(e_mat.shape),
                fixed(r_mat.shape), fixed(sn.shape), fixed(sd.shape), fixed(lam_tile.shape), fixed(subg_tile.shape)]
    in_specs += [page_spec(pool_nsa, j) for j in range(n_pages)]
    in_specs += [page_spec(pool_diff, j) for j in range(n_pages)]
    return pl.pallas_call(
        functools.partial(_sample_kernel, n_pages=n_pages, past_len=past_len, lam_init=lam_init),
        grid_spec=pltpu.PrefetchScalarGridSpec(
            num_scalar_prefetch=1,
            grid=(B,),
            in_specs=in_specs,
            out_specs=(per_b((NSA_HEADS, LANES)), per_b((8, DIFF_HEADS * DIFF_VDIM)), per_b((WIN_W, nwin))),
            scratch_shapes=[pltpu.VMEM((2, past_len, LANES), F32)]),
        out_shape=(jax.ShapeDtypeStruct((B, NSA_HEADS, LANES), F32),
                   jax.ShapeDtypeStruct((B, 8, DIFF_HEADS * DIFF_VDIM), F32),
                   jax.ShapeDtypeStruct((B, WIN_W, nwin), F32)),
        compiler_params=_cparams("arbitrary"),
        name="sample_attention",
    )(page_table, qm, qd4, gt8, nrow, wrow, wcol, drow, win_t, w1ab, pe2, w2bd, m_mat, e_mat, r_mat, sn, sd,
      lam_tile, subg_tile, *([pool_nsa] * n_pages), *([pool_diff] * n_pages))


def _cmp_to_sel(n_cmp, n_sel):
    c0 = np.arange(n_cmp)[:, None] * CMP_STRIDE
    s0 = np.arange(n_sel)[None, :] * SEL_BLOCK
    ov = np.clip(np.minimum(c0 + CMP_LEN, s0 + SEL_BLOCK) - np.maximum(c0, s0), 0, None)
    return (ov / CMP_LEN).astype(np.float32)


def _lambda_init(layer):
    return 0.8 - 0.6 * math.exp(-0.3 * layer)


def _prep_params(l, w_in, w_out, diff_subln_g, ln1_g, ln1_b, ln2_g, ln2_b, w_router, b_router,
                 w_gate_up, b_gate_up, w_down, b_down):
    gate_end = _GT0 + 3 * NSA_HEADS
    wt = w_in[l].T
    w_t = jnp.concatenate([wt[:gate_end], jnp.zeros((_GATE_PAD, D_MODEL), F32), wt[gate_end:]], axis=0).astype(BF16)
    bgu = b_gate_up[l]
    return {
        "w_t": w_t,
        "wo_a": w_out[l][:512].astype(BF16), "wo_b": w_out[l][512:].astype(BF16),
        "ln1_g": ln1_g[l][None], "ln1_b": ln1_b[l][None], "ln2_g": ln2_g[l][None], "ln2_b": ln2_b[l][None],
        "wr_t": w_router[l].T, "br": b_router[l][:, None],
        "wgu": w_gate_up[l], "wd": w_down[l], "p_mat": _deinterleave_matrix(),
        "bg": bgu[:, None, 0::2], "bl": bgu[:, None, 1::2], "bd": b_down[l][:, None, :],
        "subln_g": diff_subln_g[l][None],
    }


def _cmp_params(cmp_pe, cmp_w1, cmp_w2):
    eye = jnp.eye(NSA_KV_HEADS, dtype=F32)
    halves = []
    for half in range(2):
        w = cmp_w1[:, half * CMP_STRIDE:(half + 1) * CMP_STRIDE]
        wb = jnp.einsum('klde,gh->klgdhe', w, eye)
        halves.append(wb.reshape(2, CMP_STRIDE * LANES, LANES))
    w1ab = jnp.concatenate(halves, axis=-1).astype(BF16)
    pe = cmp_pe.reshape(2, 2, CMP_STRIDE, 1, HEAD_DIM)
    pe2 = jnp.broadcast_to(pe, (2, 2, CMP_STRIDE, NSA_KV_HEADS, HEAD_DIM)).reshape(2, 2, 1, CMP_STRIDE * LANES)
    pe2 = jnp.broadcast_to(pe2, (2, 2, 8, CMP_STRIDE * LANES))
    w2bd = jnp.einsum('kde,gh->kgdhe', cmp_w2, eye).reshape(2, LANES, LANES).astype(BF16)
    return w1ab, pe2, w2bd


def _feature_major_to_rows(a_t, kinds):
    B, _, T = a_t.shape
    return a_t.reshape(B, kinds, NSA_KV_HEADS, HEAD_DIM, T).transpose(0, 4, 1, 2, 3)


def _prompt_group(x, p, cmp_pe, cmp_w1, cmp_w2, lam_tile, lam_init):
    B, T, _ = x.shape
    x2d = x.reshape(B * T, D_MODEL)
    q_nsa, q_tb, nsa_rows, nsa_t, nsa_tb, _, win_t, win_tb, gates, dq, diff8, diff_b = _in_proj(
        x2d, p["w_t"], B, nsa_tok_w=NSA_W // 2, win_tok_w=LANES)
    r3 = lambda a: a.reshape(B, T, a.shape[-1])
    q_nsa, nsa_rows3, gates, dq = map(r3, (q_nsa, nsa_rows, gates, dq))
    kvc = _compress_prompt(nsa_rows3, *_cmp_params(cmp_pe, cmp_w1, cmp_w2))
    n16 = T // CMP_STRIDE
    n_sel = -(-T // SEL_BLOCK)
    m_np = np.zeros((n16, n_sel), np.float32)
    m_np[:n16 - 1] = _cmp_to_sel(n16 - 1, n_sel)
    o_cmp, sel = _cmp_attn_prompt(q_nsa, q_tb, kvc, jnp.asarray(m_np.T))
    e_np = np.zeros((NSA_KV_HEADS, LANES, T), np.float32)
    for g in range(NSA_KV_HEADS):
        e_np[g, g * n_sel + np.arange(T) // SEL_BLOCK, np.arange(T)] = 1.0
    o_nsa = _sel_win_prompt(q_nsa, nsa_tb, win_tb, sel, jnp.asarray(e_np, BF16), o_cmp, gates)
    o_diff = _diff_prompt(lam_tile, dq, diff_b.reshape(B, T, DIFF_W), p["subln_g"], lam_init)
    y = _post_mixer_sorted(o_nsa.reshape(B * T, 512), o_diff.reshape(B * T, 512), x2d, p)
    nwin = min(WINDOW, T)
    return (y.reshape(B, T, D_MODEL),
            _feature_major_to_rows(nsa_t, 4),
            diff8.reshape(B, T, 2, DIFF_HEADS, DIFF_VDIM),
            _feature_major_to_rows(win_t[:, :, T - nwin:], 2))


def _sample_group(x, pool_nsa, pool_diff, win_buf, page_table, p, cmp_pe, cmp_w1, cmp_w2, lam_tile, lam_init):
    B, T, _ = x.shape
    past_len = page_table.shape[1] * PAGE_SIZE
    x2d = x.reshape(B, D_MODEL)
    q_nsa, _, nsa_rows, nsa_t, _, win_rows, _, _, gates, dq, diff8, _ = _in_proj(x2d, p["w_t"], 1)
    qh = q_nsa.reshape(B, NSA_HEADS, HEAD_DIM)
    grp = (np.arange(NSA_HEADS) // NSA_GROUP)[None, :, None, None] == np.arange(NSA_KV_HEADS)[None, None, :, None]
    qm = (qh[:, :, None, :] * jnp.asarray(grp, F32)).reshape(B, NSA_HEADS, LANES)
    dq4 = dq.reshape(B, DIFF_HEADS, 1, 2 * HEAD_DIM)
    rr = np.arange(8)
    rowmask = (rr[None, :, None] % DIFF_HEADS == np.arange(DIFF_HEADS)[:, None, None]) & (
        rr[None, :, None] // DIFF_HEADS == (np.arange(LANES) // HEAD_DIM)[None, None, :])
    qd4 = dq4 * jnp.asarray(rowmask, F32)[None]
    gt8 = jnp.pad(gates[:, :3 * NSA_HEADS].reshape(B, 3, NSA_HEADS).transpose(0, 2, 1), ((0, 0), (0, 0), (0, LANES - 3)))
    nwin = win_buf.shape[1]
    win_t = win_buf.transpose(0, 2, 3, 4, 1).reshape(B, WIN_W, nwin)
    n_pool = pool_nsa.shape[0]
    pool_nsa_t = pool_nsa.transpose(0, 2, 3, 4, 1).reshape(n_pool, NSA_W, PAGE_SIZE)
    pool_diff_r = pool_diff.reshape(n_pool, PAGE_SIZE * DIFF_ROWS, LANES)
    w1ab, pe2, w2bd = _cmp_params(cmp_pe, cmp_w1, cmp_w2)
    n16 = past_len // CMP_STRIDE
    n_sel = past_len // SEL_BLOCK + 1
    m_np = np.zeros((n16, 64), np.float32)
    m_np[:n16 - 1, :n_sel] = _cmp_to_sel(n16 - 1, n_sel)
    e_np = np.zeros((64, past_len), np.float32)
    e_np[np.arange(past_len) // SEL_BLOCK, np.arange(past_len)] = 1.0
    r_np = (np.arange(8)[:, None] // NSA_GROUP == np.arange(8)[None, :] // NSA_GROUP).astype(np.float32)
    sn = jnp.asarray(np.tile(np.asarray(NSA_SLOPES, np.float32)[:, None], (1, LANES)))
    sd = jnp.asarray(np.tile(np.asarray(DIFF_SLOPES, np.float32)[np.arange(8) % DIFF_HEADS, None], (1, LANES)))
    subg_tile = jnp.tile(p["subln_g"], (1, DIFF_HEADS))
    o_nsa8, o_diff8, new_win_t = _sample_attention(
        page_table, qm, qd4, gt8, nsa_rows[:, None, :], win_rows[:, None, :], win_rows[:, :, None],
        diff8.reshape(B, DIFF_ROWS, LANES), win_t, pool_nsa_t, pool_diff_r,
        w1ab, pe2, w2bd, jnp.asarray(m_np), jnp.asarray(e_np, BF16), jnp.asarray(r_np), sn, sd, lam_tile, subg_tile,
        past_len, lam_init)
    o8 = o_nsa8.reshape(B, NSA_KV_HEADS, NSA_GROUP, NSA_KV_HEADS, HEAD_DIM)
    o_nsa = jnp.stack([o8[:, g, :, g] for g in range(NSA_KV_HEADS)], axis=1).reshape(B, 512)
    d8 = o_diff8[:, :DIFF_HEADS].reshape(B, DIFF_HEADS, DIFF_HEADS, DIFF_VDIM)
    o_diff = jnp.stack([d8[:, h, h] for h in range(DIFF_HEADS)], axis=1).reshape(B, 512)
    y = _post_mixer(o_nsa, o_diff, x2d, p, s_tile=B, chunk=B)
    return (y.reshape(B, T, D_MODEL),
            _feature_major_to_rows(nsa_t, 4).reshape(B, T, 4, NSA_KV_HEADS, HEAD_DIM),
            diff8.reshape(B, T, 2, DIFF_HEADS, DIFF_VDIM),
            _feature_major_to_rows(new_win_t, 2))


def kernel(x_prompt, x_sample, cache_nsa_kv, cache_diff_kv, state_nsa_win, page_table, w_in, w_out, cmp_pe, cmp_w1,
           cmp_w2, diff_lambda, diff_subln_g, ln1_g, ln1_b, ln2_g, ln2_b, w_router, b_router, w_gate_up, b_gate_up,
           w_down, b_down):
    depth = w_in.shape[0]
    xp, xs = x_prompt, x_sample
    outs = [[] for _ in range(6)]
    for l in range(depth):
        lam0 = _lambda_init(l)
        lv = diff_lambda[l].astype(F32)
        lam = jnp.exp(jnp.sum(lv[0] * lv[1])) - jnp.exp(jnp.sum(lv[2] * lv[3])) + lam0
        lam_tile = jnp.full((8, LANES), lam, F32)
        p = _prep_params(l, w_in, w_out, diff_subln_g, ln1_g, ln1_b, ln2_g, ln2_b, w_router, b_router,
                         w_gate_up, b_gate_up, w_down, b_down)
        xp, r_nsa, r_diff, r_win = _prompt_group(xp, p, cmp_pe[l], cmp_w1[l], cmp_w2[l], lam_tile, lam0)
        xs, s_nsa, s_diff, s_win = _sample_group(xs, cache_nsa_kv[l], cache_diff_kv[l], state_nsa_win[l], page_table,
                                                 p, cmp_pe[l], cmp_w1[l], cmp_w2[l], lam_tile, lam0)
        for lst, v in zip(outs, (r_nsa, r_diff, r_win, s_nsa, s_diff, s_win)):
            lst.append(v)
    return (xp, xs) + tuple(jnp.stack(o) for o in outs)
```

```python
import functools
import math

import numpy as np
import jax
import jax.numpy as jnp
from jax import lax
from jax.experimental import pallas as pl
from jax.experimental.pallas import tpu as pltpu
from jax.experimental.pallas import tpu_sc as plsc

F32 = jnp.float32
BF16 = jnp.bfloat16
HIGHEST = lax.Precision.HIGHEST

D_MODEL = 1024
HEAD_DIM = 64
NSA_HEADS = 8
NSA_KV_HEADS = 2
NSA_GROUP = NSA_HEADS // NSA_KV_HEADS
CMP_LEN = 32
CMP_STRIDE = 16
SEL_BLOCK = 64
SEL_TOPN = 16
WINDOW = 512
DIFF_HEADS = 4
DIFF_VDIM = 2 * HEAD_DIM
N_EXPERTS = 32
TOP_K = 4
SWIGLU_ALPHA = 1.702
SWIGLU_LIMIT = 7.0
LN_EPS = 1e-5
RMS_EPS = 1e-5
ATTN_SCALE = HEAD_DIM ** -0.5
LOG2E = math.log2(math.e)
DEPTH = 1
DEEPNORM_ALPHA = (2 * DEPTH) ** 0.25
PAGE_SIZE = 128

NSA_SLOPES = tuple(2.0 ** (-8.0 * (i + 1) / NSA_HEADS) for i in range(NSA_HEADS))
DIFF_SLOPES = tuple(2.0 ** (-8.0 * (i + 1) / DIFF_HEADS) for i in range(DIFF_HEADS))

VMEM_LIMIT_BYTES = 56 * 1024 * 1024
LANES = 128
MXU_DIM = 256

MOE_CHUNK = 2 * MXU_DIM
SC_CORES = 2
SC_SUBCORES = 16
SC_WINDOW = 128

NSA_W = 4 * NSA_KV_HEADS * HEAD_DIM
WIN_W = 2 * NSA_KV_HEADS * HEAD_DIM
DIFF_W = 2 * DIFF_HEADS * DIFF_VDIM
DIFF_ROWS = DIFF_W // LANES
_GATE_PAD = LANES - 3 * NSA_HEADS
_Q0, _NSA0, _WIN0, _GT0, _DQ0, _DIFF0, _PROJ_ROWS = 0, 512, 1024, 1280, 1408, 1920, 2944


def _cparams(*sem):
    return pltpu.CompilerParams(dimension_semantics=sem, vmem_limit_bytes=VMEM_LIMIT_BYTES)


def _softmax_parts(s, mask):
    s = jnp.where(mask, s, -jnp.inf)
    m = jnp.max(s, axis=-1, keepdims=True)
    m = jnp.where(jnp.isfinite(m), m, 0.0)
    e = jnp.exp(s - m)
    d = jnp.sum(e, axis=-1, keepdims=True)
    return e, jnp.where(d > 0, d, 1.0)


def _softmax2_parts(s, mask):
    s = jnp.where(mask, s, -jnp.inf)
    m = jnp.max(s, axis=-1, keepdims=True)
    m = jnp.where(jnp.isfinite(m), m, 0.0)
    e = jnp.exp2(s - m)
    d = jnp.sum(e, axis=-1, keepdims=True)
    return e, jnp.where(d > 0, d, 1.0)


def _dot_nt(a, b):
    return lax.dot_general(a, b, (((1,), (1,)), ((), ())), preferred_element_type=F32)


def _dot(a, b):
    return jnp.dot(a, b, preferred_element_type=F32)


def _pad_head(qh, g):
    z = jnp.zeros_like(qh)
    return jnp.concatenate([qh, z] if g == 0 else [z, qh], axis=1)


def _select_blocks(imp, blk, qpos, n_blk_lanes):
    cur = jnp.right_shift(qpos, int(math.log2(SEL_BLOCK)))
    valid = blk * SEL_BLOCK <= qpos
    forced = (blk == 0) | (blk == cur) | (blk == cur - 1)
    val = jnp.where(forced, jnp.inf, jnp.where(valid, imp, -jnp.inf))
    rank = jnp.zeros(val.shape, F32)
    for i in range(n_blk_lanes):
        ci = val[:, i:i + 1]
        beats = (ci > val) | ((ci == val) & (blk > i))
        rank = rank + jnp.where(beats, 1.0, 0.0)
    return jnp.where(rank < SEL_TOPN, 1.0, 0.0)


def _in_proj_kernel(x_ref, w_ref, q_ref, qtb_ref, nsa_ref, nsat_ref, nsatb_ref, win_ref, wint_ref, wintb_ref, gt_ref,
                    dq_ref, diff_ref, diffb_ref):
    tm = x_ref.shape[0]
    xb = x_ref[...].astype(BF16)
    q_ref[...] = _dot_nt(xb, w_ref[_Q0:_NSA0, :])
    qtb_ref[0] = _dot_nt(w_ref[_Q0:_NSA0, :], xb).astype(BF16)
    nsa_ref[...] = _dot_nt(xb, w_ref[_NSA0:_NSA0 + nsa_ref.shape[1], :])
    r = _dot_nt(w_ref[_NSA0:_WIN0, :], xb)
    nsat_ref[0] = r
    nsatb_ref[0] = r.astype(BF16)
    win_ref[...] = _dot_nt(xb, w_ref[_WIN0:_WIN0 + win_ref.shape[1], :])
    r = _dot_nt(w_ref[_WIN0:_GT0, :], xb)
    wint_ref[0] = r
    wintb_ref[0] = r.astype(BF16)
    gt_ref[...] = jax.nn.sigmoid(_dot_nt(xb, w_ref[_GT0:_DQ0, :]))
    dq_ref[...] = _dot_nt(xb, w_ref[_DQ0:_DIFF0, :])
    r = _dot_nt(xb, w_ref[_DIFF0:_PROJ_ROWS, :])
    diffb_ref[...] = r.astype(BF16)
    for j in range(DIFF_ROWS):
        diff_ref[pl.ds(j, tm, stride=DIFF_ROWS), :] = r[:, j * LANES:(j + 1) * LANES]


def _in_proj(x2d, w_t, batch, nsa_tok_w=NSA_W, win_tok_w=WIN_W):
    n = x2d.shape[0]
    t = n // batch
    tm = min(512, t)
    nt = t // tm
    row = lambda b, i: (b * nt + i, 0)
    tr = lambda b, i: (b, 0, i)
    tok = lambda w: pl.BlockSpec((tm, w), row)
    return pl.pallas_call(
        _in_proj_kernel,
        grid=(batch, nt),
        in_specs=[pl.BlockSpec((tm, D_MODEL), row),
                  pl.BlockSpec((_PROJ_ROWS, D_MODEL), lambda b, i: (0, 0))],
        out_specs=(tok(512), pl.BlockSpec((1, 512, tm), tr),
                   tok(nsa_tok_w), pl.BlockSpec((1, NSA_W, tm), tr), pl.BlockSpec((1, NSA_W, tm), tr),
                   tok(win_tok_w), pl.BlockSpec((1, WIN_W, tm), tr), pl.BlockSpec((1, WIN_W, tm), tr),
                   tok(LANES), tok(512), pl.BlockSpec((tm * DIFF_ROWS, LANES), row), tok(DIFF_W)),
        out_shape=(jax.ShapeDtypeStruct((n, 512), F32), jax.ShapeDtypeStruct((batch, 512, t), BF16),
                   jax.ShapeDtypeStruct((n, nsa_tok_w), F32),
                   jax.ShapeDtypeStruct((batch, NSA_W, t), F32), jax.ShapeDtypeStruct((batch, NSA_W, t), BF16),
                   jax.ShapeDtypeStruct((n, win_tok_w), F32),
                   jax.ShapeDtypeStruct((batch, WIN_W, t), F32), jax.ShapeDtypeStruct((batch, WIN_W, t), BF16),
                   jax.ShapeDtypeStruct((n, LANES), F32), jax.ShapeDtypeStruct((n, 512), F32),
                   jax.ShapeDtypeStruct((n * DIFF_ROWS, LANES), F32), jax.ShapeDtypeStruct((n, DIFF_W), BF16)),
        compiler_params=_cparams("parallel", "parallel"),
        name="in_proj",
    )(x2d, w_t)


def _compress_chunks(load_rows, w1, pe, w2, n16):
    cst = _dot(pe[0].astype(BF16), w1[:, 0:LANES]) + _dot(pe[1].astype(BF16), w1[:, LANES:])
    chunks = jnp.concatenate([load_rows(l) for l in range(CMP_STRIDE)], axis=1)
    ab = _dot(chunks.astype(BF16), w1)
    nxt = pltpu.roll(ab[:, LANES:], n16 - 1, 0)
    h = jax.nn.gelu(ab[:, 0:LANES] + nxt + cst[0:1])
    return _dot(h.astype(BF16), w2)


def _compress_kernel(rows_ref, w1_ref, pe_ref, w2_ref, o_ref):
    n16 = rows_ref.shape[1] // CMP_STRIDE
    o_ref[0, 0] = _compress_chunks(lambda l: rows_ref[0, pl.ds(l, n16, stride=CMP_STRIDE), :],
                                   w1_ref[0], pe_ref[0], w2_ref[0], n16)


def _compress_prompt(nsa_rows, w1ab, pe2, w2bd):
    B, T, _ = nsa_rows.shape
    n16 = T // CMP_STRIDE
    return pl.pallas_call(
        _compress_kernel,
        grid=(B, 2),
        in_specs=[pl.BlockSpec((1, T, LANES), lambda b, k: (b, 0, k)),
                  pl.BlockSpec((1,) + w1ab.shape[1:], lambda b, k: (k, 0, 0)),
                  pl.BlockSpec((1,) + pe2.shape[1:], lambda b, k: (k, 0, 0, 0)),
                  pl.BlockSpec((1,) + w2bd.shape[1:], lambda b, k: (k, 0, 0))],
        out_specs=pl.BlockSpec((1, 1, n16, LANES), lambda b, k: (b, k, 0, 0)),
        out_shape=jax.ShapeDtypeStruct((B, 2, n16, LANES), F32),
        compiler_params=_cparams("parallel", "parallel"),
        name="compress_prompt",
    )(nsa_rows, w1ab, pe2, w2bd)


def _select_blocks_t(imp, qpos):
    n_blk = imp.shape[0]
    blk = lax.broadcasted_iota(jnp.int32, imp.shape, 0)
    cur = jnp.right_shift(qpos, int(math.log2(SEL_BLOCK)))
    valid = blk * SEL_BLOCK <= qpos
    forced = (blk == 0) | (blk == cur) | (blk == cur - 1)
    val = jnp.where(forced, jnp.inf, jnp.where(valid, imp, -jnp.inf))
    rank = jnp.zeros(val.shape, F32)
    for i in range(n_blk):
        ci = val[i:i + 1, :]
        beats = (ci > val) | ((ci == val) & (blk > i))
        rank = rank + jnp.where(beats, 1.0, 0.0)
    return jnp.where(rank < SEL_TOPN, 1.0, 0.0)


def _cmp_attn_kernel(q_ref, qt_ref, kc_ref, vc_ref, mt_ref, o_ref, selt_ref, *, tq):
    i = pl.program_id(1)
    ncmp = kc_ref.shape[2]
    kcb = kc_ref[0, 0].astype(BF16)
    vcb = vc_ref[0, 0].astype(BF16)
    qpos_c = i * tq + lax.broadcasted_iota(jnp.int32, (tq, 1), 0)
    end_r = lax.broadcasted_iota(jnp.int32, (1, ncmp), 1) * CMP_STRIDE + (CMP_LEN - 1)
    mask = end_r <= qpos_c
    end_rf = end_r.astype(F32)
    q = q_ref[0] * (ATTN_SCALE * LOG2E)
    for h in range(NSA_HEADS):
        g = h // NSA_GROUP
        qp = _pad_head(q[:, h * HEAD_DIM:(h + 1) * HEAD_DIM], g).astype(BF16)
        e, d = _softmax2_parts(_dot_nt(qp, kcb) + (NSA_SLOPES[h] * LOG2E) * end_rf, mask)
        oh = _dot(e.astype(BF16), vcb)[:, g * HEAD_DIM:(g + 1) * HEAD_DIM] / d
        o_ref[0, :, h * HEAD_DIM:(h + 1) * HEAD_DIM] = oh
    qpos_r = i * tq + lax.broadcasted_iota(jnp.int32, (1, tq), 1)
    end_c = lax.broadcasted_iota(jnp.int32, (ncmp, 1), 0) * CMP_STRIDE + (CMP_LEN - 1)
    mask_t = end_c <= qpos_r
    end_cf = end_c.astype(F32)
    qt = qt_ref[0]
    zero = jnp.zeros((HEAD_DIM, tq), BF16)
    psum = [None, None]
    for h in range(NSA_HEADS):
        g = h // NSA_GROUP
        qh = qt[h * HEAD_DIM:(h + 1) * HEAD_DIM, :]
        qpt = jnp.concatenate([qh, zero] if g == 0 else [zero, qh], axis=0)
        s = _dot(kcb, qpt) * (ATTN_SCALE * LOG2E) + (NSA_SLOPES[h] * LOG2E) * end_cf
        s = jnp.where(mask_t, s, -jnp.inf)
        m = jnp.max(s, axis=0, keepdims=True)
        m = jnp.where(jnp.isfinite(m), m, 0.0)
        e = jnp.exp2(s - m)
        d = jnp.sum(e, axis=0, keepdims=True)
        p = e / jnp.where(d > 0, d, 1.0)
        psum[g] = p if psum[g] is None else psum[g] + p
    sels = []
    for g in range(NSA_KV_HEADS):
        imp = jnp.dot(mt_ref[...], psum[g], precision=HIGHEST, preferred_element_type=F32)
        sels.append(_select_blocks_t(imp, qpos_r))
    sels.append(jnp.zeros((LANES - NSA_KV_HEADS * mt_ref.shape[0], tq), F32))
    selt_ref[0] = jnp.concatenate(sels, axis=0)


def _cmp_attn_prompt(q_nsa, q_tb, kvc, mt_mat, tq=512):
    B, T, _ = q_nsa.shape
    ncmp = kvc.shape[2]
    n_sel = mt_mat.shape[0]
    return pl.pallas_call(
        functools.partial(_cmp_attn_kernel, tq=tq),
        grid=(B, T // tq),
        in_specs=[pl.BlockSpec((1, tq, 512), lambda b, i: (b, i, 0)),
                  pl.BlockSpec((1, 512, tq), lambda b, i: (b, 0, i)),
                  pl.BlockSpec((1, 1, ncmp, LANES), lambda b, i: (b, 0, 0, 0)),
                  pl.BlockSpec((1, 1, ncmp, LANES), lambda b, i: (b, 1, 0, 0)),
                  pl.BlockSpec((n_sel, ncmp), lambda b, i: (0, 0))],
        out_specs=(pl.BlockSpec((1, tq, 512), lambda b, i: (b, i, 0)),
                   pl.BlockSpec((1, LANES, tq), lambda b, i: (b, 0, i))),
        out_shape=(jax.ShapeDtypeStruct((B, T, 512), F32), jax.ShapeDtypeStruct((B, LANES, T), F32)),
        compiler_params=_cparams("parallel", "parallel"),
        name="cmp_attn_prompt",
    )(q_nsa, q_tb, kvc, kvc, mt_mat)


def _sel_win_kernel(q_ref, ks_ref, vs_ref, kw_ref, vw_ref, sel_ref, e_ref, ocmp_ref, gt_ref, o_ref, *, tq, span,
                    kv_step):
    i = pl.program_id(1)
    T = ks_ref.shape[2]
    qpos = i * tq + lax.broadcasted_iota(jnp.int32, (tq, 1), 0)
    q = q_ref[0] * (ATTN_SCALE * LOG2E)
    gt = gt_ref[0]
    ocmp = ocmp_ref[0]
    selb = sel_ref[0].astype(BF16)
    heads = [(h, h // NSA_GROUP, h * HEAD_DIM, (h + 1) * HEAD_DIM) for h in range(NSA_HEADS)]
    qps = [_pad_head(q[:, lo:hi], g).astype(BF16) for _, g, lo, hi in heads]
    start = pl.multiple_of(jnp.maximum(i * tq - WINDOW, 0), LANES)
    wpos = start + lax.broadcasted_iota(jnp.int32, (1, span), 1)
    dist_w = qpos - wpos
    mask_w = (dist_w >= 0) & (dist_w <= WINDOW)
    wposf = wpos.astype(F32)
    kwb = kw_ref[0, :, pl.ds(start, span)]
    vwb = vw_ref[0, :, pl.ds(start, span)]
    partial = []
    for h, g, lo, hi in heads:
        s = _dot(qps[h], kwb) + (NSA_SLOPES[h] * LOG2E) * wposf
        e, d = _softmax2_parts(s, mask_w)
        o_win = _dot_nt(e.astype(BF16), vwb)[:, g * HEAD_DIM:(g + 1) * HEAD_DIM] / d
        partial.append(gt[:, h:h + 1] * ocmp[:, lo:hi] + gt[:, 2 * NSA_HEADS + h:2 * NSA_HEADS + h + 1] * o_win)
    tiles_per_step = kv_step // tq
    for br in range(T // kv_step):
        @pl.when((i >= br * tiles_per_step) & (i < (br + 1) * tiles_per_step))
        def _(br=br):
            kv = (br + 1) * kv_step
            kpos = lax.broadcasted_iota(jnp.int32, (1, kv), 1)
            causal = kpos <= qpos
            kposf = kpos.astype(F32)
            ksb = ks_ref[0, :, 0:kv]
            vsb = vs_ref[0, :, 0:kv]
            for g in range(NSA_KV_HEADS):
                expand = lax.dot_general(selb, e_ref[g, :, 0:kv], (((0,), (0,)), ((), ())), preferred_element_type=F32)
                mask_s = (expand > 0.5) & causal
                for h, _, lo, hi in heads[g * NSA_GROUP:(g + 1) * NSA_GROUP]:
                    s = _dot(qps[h], ksb) + (NSA_SLOPES[h] * LOG2E) * kposf
                    e, d = _softmax2_parts(s, mask_s)
                    o_sel = _dot_nt(e.astype(BF16), vsb)[:, g * HEAD_DIM:(g + 1) * HEAD_DIM] / d
                    o_ref[0, :, lo:hi] = partial[h] + gt[:, NSA_HEADS + h:NSA_HEADS + h + 1] * o_sel


def _sel_win_prompt(q_nsa, nsa_t, win_t, sel_t, e_mat, o_cmp, gates, tq=256):
    B, T, _ = q_nsa.shape
    span = WINDOW + tq
    kv = lambda j: pl.BlockSpec((1, LANES, T), lambda b, i: (b, j, 0))
    return pl.pallas_call(
        functools.partial(_sel_win_kernel, tq=tq, span=span, kv_step=min(2 * tq, T)),
        grid=(B, T // tq),
        in_specs=[pl.BlockSpec((1, tq, 512), lambda b, i: (b, i, 0)),
                  kv(2), kv(3), kv(0), kv(1),
                  pl.BlockSpec((1, LANES, tq), lambda b, i: (b, 0, i)),
                  pl.BlockSpec((NSA_KV_HEADS, LANES, T), lambda b, i: (0, 0, 0)),
                  pl.BlockSpec((1, tq, 512), lambda b, i: (b, i, 0)),
                  pl.BlockSpec((1, tq, LANES), lambda b, i: (b, i, 0))],
        out_specs=pl.BlockSpec((1, tq, 512), lambda b, i: (b, i, 0)),
        out_shape=jax.ShapeDtypeStruct((B, T, 512), F32),
        compiler_params=_cparams("parallel", "parallel"),
        name="sel_win_prompt",
    )(q_nsa, nsa_t, nsa_t, win_t, win_t, sel_t, e_mat, o_cmp, gates)


def _diff_kernel(lam_ref, sl_ref, q_ref, k_ref, v_ref, g_ref, o_ref, *, tq, lam_init):
    i = pl.program_id(2)
    T = k_ref.shape[1]
    lam = lam_ref[0:1, 0:1]
    q = q_ref[0] * (ATTN_SCALE * LOG2E)
    qps = [_pad_head(q[:, c * HEAD_DIM:(c + 1) * HEAD_DIM], c).astype(BF16) for c in range(2)]
    tri = lax.broadcasted_iota(jnp.int32, (tq, tq), 0) >= lax.broadcasted_iota(jnp.int32, (tq, tq), 1)
    for br in range(T // tq):
        @pl.when(i == br)
        def _(br=br):
            off = br * tq
            kv = off + tq
            kb = k_ref[0, 0:kv, :]
            vb = v_ref[0, 0:kv, :]
            col = (sl_ref[0, :, 0:1] * LOG2E) * lax.broadcasted_iota(jnp.int32, (1, kv), 1).astype(F32)
            es_d, es_o, ds = [], [], []
            for c in range(2):
                s = _dot_nt(qps[c], kb) + col
                s_d = jnp.where(tri, s[:, off:], -jnp.inf)
                m = jnp.max(s_d, axis=-1, keepdims=True)
                if off:
                    m = jnp.maximum(m, jnp.max(s[:, :off], axis=-1, keepdims=True))
                e_d = jnp.exp2(s_d - m)
                d = jnp.sum(e_d, axis=-1, keepdims=True)
                es_d.append(e_d.astype(BF16))
                if off:
                    e_o = jnp.exp2(s[:, :off] - m)
                    d = d + jnp.sum(e_o, axis=-1, keepdims=True)
                    es_o.append(e_o.astype(BF16))
                ds.append(d)
            o = _dot(jnp.concatenate(es_d, axis=0), vb[off:, :])
            if off:
                o = o + _dot(jnp.concatenate(es_o, axis=0), vb[:off, :])
            outs = [o[:tq] / ds[0], o[tq:] / ds[1]]
            od = outs[0] - lam * outs[1]
            od = od * lax.rsqrt(jnp.mean(jnp.square(od), axis=-1, keepdims=True) + RMS_EPS)
            o_ref[0] = od * g_ref[...] * (1.0 - lam_init)


def _diff_prompt(lam_tile, dq, diff_b, subln_g, lam_init, tq=256):
    B, T, _ = dq.shape
    slopes = jnp.asarray(np.tile(np.asarray(DIFF_SLOPES, np.float32)[:, None, None], (1, 1, LANES)))
    return pl.pallas_call(
        functools.partial(_diff_kernel, tq=tq, lam_init=lam_init),
        grid=(B, DIFF_HEADS, T // tq),
        in_specs=[pl.BlockSpec((8, LANES), lambda b, h, i: (0, 0)),
                  pl.BlockSpec((1, 1, LANES), lambda b, h, i: (h, 0, 0)),
                  pl.BlockSpec((1, tq, LANES), lambda b, h, i: (b, i, h)),
                  pl.BlockSpec((1, T, LANES), lambda b, h, i: (b, 0, h)),
                  pl.BlockSpec((1, T, LANES), lambda b, h, i: (b, 0, DIFF_HEADS + h)),
                  pl.BlockSpec((1, DIFF_VDIM), lambda b, h, i: (0, 0))],
        out_specs=pl.BlockSpec((1, tq, LANES), lambda b, h, i: (b, i, h)),
        out_shape=jax.ShapeDtypeStruct((B, T, DIFF_HEADS * DIFF_VDIM), F32),
        compiler_params=_cparams("parallel", "parallel", "parallel"),
        name="diff_prompt",
    )(lam_tile, slopes, dq, diff_b, diff_b, subln_g)


def _layer_norm(z, g, b):
    mu = jnp.mean(z, axis=-1, keepdims=True)
    zc = z - mu
    var = jnp.mean(jnp.square(zc), axis=-1, keepdims=True)
    return zc * lax.rsqrt(var + LN_EPS) * g + b


def _pack_rows(x, ref):
    w = x.shape[1] // 2
    hi = lax.bitcast_convert_type(x[:, :w].astype(BF16).astype(F32), jnp.uint32)
    lo = lax.bitcast_convert_type(x[:, w:].astype(BF16).astype(F32), jnp.uint32)
    packed = hi | (lo >> 16)
    ref[0] = packed[:, :w // 2]
    ref[1] = packed[:, w // 2:]


def _unpack_rows(u0, u1):
    hi = [lax.bitcast_convert_type(u & jnp.uint32(0xFFFF0000), F32) for u in (u0, u1)]
    lo = [lax.bitcast_convert_type(u << 16, F32) for u in (u0, u1)]
    return jnp.concatenate(hi + lo, axis=1)


def _out_proj_kernel(on_ref, od_ref, x_ref, wa_ref, wb_ref, g_ref, b_ref, wr_ref, br_ref,
                     x1_ref, x1b_ref, x1p_ref, sel_ref, gate_ref, tope_ref, gate8_ref, cnt_ref):
    y = _dot(on_ref[...].astype(BF16), wa_ref[...]) + _dot(od_ref[...].astype(BF16), wb_ref[...])
    x1 = _layer_norm(DEEPNORM_ALPHA * x_ref[...] + y, g_ref[...], b_ref[...])
    x1_ref[...] = x1
    x1b_ref[...] = x1.astype(BF16)
    _pack_rows(x1, x1p_ref)
    logits = lax.dot_general(wr_ref[...], x1, (((1,), (1,)), ((), ())), precision=HIGHEST,
                             preferred_element_type=F32) + br_ref[...]
    eidx = lax.broadcasted_iota(jnp.int32, logits.shape, 0).astype(F32)
    sel = jnp.zeros(logits.shape, F32)
    picked, vals = [], []
    for k in range(TOP_K):
        cur = jnp.where(sel > 0.5, -jnp.inf, logits)
        m = jnp.max(cur, axis=0, keepdims=True)
        first = jnp.min(jnp.where(cur == m, eidx, float(N_EXPERTS)), axis=0, keepdims=True)
        sel = jnp.where(eidx == first, 1.0, sel)
        picked.append(first)
        vals.append(m)
    ex = jnp.where(sel > 0.5, jnp.exp(logits - vals[0]), 0.0)
    denom = jnp.sum(ex, axis=0, keepdims=True)
    sel_ref[...] = sel
    gate_ref[...] = ex / denom
    pad = [jnp.zeros_like(denom)] * (8 - TOP_K)
    tope_ref[...] = jnp.concatenate(picked + pad, axis=0)
    gate8_ref[...] = jnp.concatenate([jnp.exp(v - vals[0]) / denom for v in vals] + pad, axis=0)
    cnt_ref[0] = jnp.sum(sel, axis=1, keepdims=True)


def _out_proj_ln_router(o_nsa, o_diff, x2d, wa, wb, ln_g, ln_b, wr_t, br):
    n = x2d.shape[0]
    tm = min(512, n)
    row = lambda i: (i, 0)
    fix = lambda i: (0, 0)
    col = lambda i: (0, i)
    return pl.pallas_call(
        _out_proj_kernel,
        grid=(n // tm,),
        in_specs=[pl.BlockSpec((tm, 512), row), pl.BlockSpec((tm, 512), row), pl.BlockSpec((tm, D_MODEL), row),
                  pl.BlockSpec((512, D_MODEL), fix), pl.BlockSpec((512, D_MODEL), fix),
                  pl.BlockSpec((1, D_MODEL), fix), pl.BlockSpec((1, D_MODEL), fix),
                  pl.BlockSpec((N_EXPERTS, D_MODEL), fix), pl.BlockSpec((N_EXPERTS, 1), fix)],
        out_specs=(pl.BlockSpec((tm, D_MODEL), row), pl.BlockSpec((tm, D_MODEL), row),
                   pl.BlockSpec((2, tm, D_MODEL // 4), lambda i: (0, i, 0)),
                   pl.BlockSpec((N_EXPERTS, tm), col), pl.BlockSpec((N_EXPERTS, tm), col),
                   pl.BlockSpec((8, tm), col), pl.BlockSpec((8, tm), col),
                   pl.BlockSpec((1, N_EXPERTS, 1), lambda i: (i, 0, 0))),
        out_shape=(jax.ShapeDtypeStruct((n, D_MODEL), F32), jax.ShapeDtypeStruct((n, D_MODEL), BF16),
                   jax.ShapeDtypeStruct((2, n, D_MODEL // 4), jnp.uint32),
                   jax.ShapeDtypeStruct((N_EXPERTS, n), F32), jax.ShapeDtypeStruct((N_EXPERTS, n), F32),
                   jax.ShapeDtypeStruct((8, n), F32), jax.ShapeDtypeStruct((8, n), F32),
                   jax.ShapeDtypeStruct((n // tm, N_EXPERTS, 1), F32)),
        compiler_params=_cparams("parallel"),
        name="out_proj_ln_router",
    )(o_nsa, o_diff, x2d, wa, wb, ln_g, ln_b, wr_t, br)


def _load_expert_weights(wgu_ref, wd_ref, p_ref, wg_s, wl_s, wd_s):
    half = MXU_DIM // 2
    for m in range(wgu_ref.shape[2] // MXU_DIM):
        y = _dot(wgu_ref[0, :, m * MXU_DIM:(m + 1) * MXU_DIM].astype(BF16), p_ref[...])
        wg_s[:, m * half:(m + 1) * half] = y[:, :half].astype(BF16)
        wl_s[:, m * half:(m + 1) * half] = y[:, half:].astype(BF16)
    wd_s[...] = wd_ref[0].astype(BF16)


def _deinterleave_matrix():
    half = MXU_DIM // 2
    p_np = np.zeros((MXU_DIM, MXU_DIM), np.float32)
    p_np[2 * np.arange(half), np.arange(half)] = 1.0
    p_np[2 * np.arange(half) + 1, half + np.arange(half)] = 1.0
    return jnp.asarray(p_np, BF16)


_EXPERT_WEIGHT_SCRATCH = [pltpu.VMEM((D_MODEL, D_MODEL), BF16)] * 3


def _moe_pos_kernel(sel_ref, u_ref, pos_ref, cnt_ref, *, n_valid):
    s_tile = sel_ref.shape[1]
    tok = pl.program_id(0) * s_tile + lax.broadcasted_iota(jnp.int32, (1, s_tile), 1)
    sel = jnp.where(tok < n_valid, sel_ref[...], 0.0)
    rank = _dot(sel.astype(BF16), u_ref[...])
    pos_ref[...] = jnp.where(sel > 0.5, rank, -1.0)
    cnt_ref[0] = jnp.sum(sel, axis=1, keepdims=True)


def _moe_positions(sel_t, s_tile):
    n = sel_t.shape[1]
    ns = pl.cdiv(n, s_tile)
    upper = jnp.asarray(np.triu(np.ones((s_tile, s_tile), np.float32), 1), BF16)
    return pl.pallas_call(
        functools.partial(_moe_pos_kernel, n_valid=n),
        grid=(ns,),
        in_specs=[pl.BlockSpec((N_EXPERTS, s_tile), lambda s: (0, s)),
                  pl.BlockSpec((s_tile, s_tile), lambda s: (0, 0))],
        out_specs=(pl.BlockSpec((N_EXPERTS, s_tile), lambda s: (0, s)),
                   pl.BlockSpec((1, N_EXPERTS, 1), lambda s: (s, 0, 0))),
        out_shape=(jax.ShapeDtypeStruct((N_EXPERTS, ns * s_tile), F32),
                   jax.ShapeDtypeStruct((ns, N_EXPERTS, 1), F32)),
        compiler_params=_cparams("parallel"),
        name="moe_positions",
    )(sel_t, upper)


def _moe_kernel(nch_ref, x_ref, pos_ref, gate_ref, wgu_ref, wd_ref, p_ref, bg_ref, bl_ref, bd_ref, y_ref,
                wg_s, wl_s, wd_s, *, chunk, n_valid):
    s = pl.program_id(0)
    e = pl.program_id(1)
    s_tile = x_ref.shape[0]
    n_chunks = nch_ref[s * N_EXPERTS + e]

    @pl.when(e == 0)
    def _():
        y_ref[...] = jnp.zeros_like(y_ref)

    @pl.when(n_chunks > 0)
    def _():
        _load_expert_weights(wgu_ref, wd_ref, p_ref, wg_s, wl_s, wd_s)

    pos = pos_ref[0, 0].astype(jnp.int32)
    gate = gate_ref[0, 0]
    slot0 = lax.broadcasted_iota(jnp.int32, (chunk, 1), 0)

    def body(j, carry):
        hit = pos == slot0 + j * chunk
        onehot = jnp.where(hit, 1.0, 0.0).astype(BF16)
        x = x_ref[...]
        if n_valid % s_tile:
            row = s * s_tile + lax.broadcasted_iota(jnp.int32, (s_tile, 1), 0)
            x = jnp.where(row < n_valid, x, jnp.zeros_like(x))
        xs = _dot(onehot, x).astype(BF16)
        hg = jnp.minimum(_dot(xs, wg_s[...]) + bg_ref[0], SWIGLU_LIMIT)
        hl = jnp.clip(_dot(xs, wl_s[...]) + bl_ref[0], -SWIGLU_LIMIT, SWIGLU_LIMIT)
        a = (hl + 1.0) * hg * jax.nn.sigmoid(SWIGLU_ALPHA * hg)
        out = _dot(a.astype(BF16), wd_s[...]) + bd_ref[0]
        gslot = jnp.sum(jnp.where(hit, gate, 0.0), axis=1, keepdims=True)
        outg = (out * gslot).astype(BF16)
        y_ref[...] += lax.dot_general(onehot, outg, (((0,), (0,)), ((), ())), preferred_element_type=F32)
        return carry

    lax.fori_loop(0, n_chunks, body, 0)


def _moe(x1b, pos_t, gate_t, counts, wgu, wd, p_mat, bg, bl, bd, s_tile, chunk):
    n = x1b.shape[0]
    ns = pl.cdiv(n, s_tile)
    nch = ((counts.reshape(ns * N_EXPERTS) + (chunk - 1)) // chunk).astype(jnp.int32)
    pos4 = pos_t.reshape(N_EXPERTS, ns, 1, s_tile)
    gate4 = jnp.pad(gate_t, ((0, 0), (0, ns * s_tile - n))).reshape(N_EXPERTS, ns, 1, s_tile)
    wspec = lambda a: pl.BlockSpec((1,) + a.shape[1:], lambda s, e, nch: (e, 0, 0))
    bspec = pl.BlockSpec((1, 1, D_MODEL), lambda s, e, nch: (e, 0, 0))
    rspec = pl.BlockSpec((1, 1, 1, s_tile), lambda s, e, nch: (e, s, 0, 0))
    return pl.pallas_call(
        functools.partial(_moe_kernel, chunk=chunk, n_valid=n),
        grid_spec=pltpu.PrefetchScalarGridSpec(
            num_scalar_prefetch=1,
            grid=(ns, N_EXPERTS),
            in_specs=[pl.BlockSpec((s_tile, D_MODEL), lambda s, e, nch: (s, 0)),
                      rspec, rspec, wspec(wgu), wspec(wd), pl.BlockSpec(p_mat.shape, lambda s, e, nch: (0, 0)),
                      bspec, bspec, bspec],
            out_specs=pl.BlockSpec((s_tile, D_MODEL), lambda s, e, nch: (s, 0)),
            scratch_shapes=_EXPERT_WEIGHT_SCRATCH),
        out_shape=jax.ShapeDtypeStruct((n, D_MODEL), F32),
        compiler_params=_cparams("parallel", "arbitrary"),
        name="moe_experts",
    )(nch, x1b, pos4, gate4, wgu, wd, p_mat, bg, bl, bd)


def _ln2_kernel(x_ref, f_ref, g_ref, b_ref, o_ref):
    o_ref[...] = _layer_norm(DEEPNORM_ALPHA * x_ref[...] + f_ref[...], g_ref[...], b_ref[...])


def _residual_ln(x1, f, g, b):
    n = x1.shape[0]
    tm = min(512, n)
    row = lambda i: (i, 0)
    fix = lambda i: (0, 0)
    return pl.pallas_call(
        _ln2_kernel,
        grid=(n // tm,),
        in_specs=[pl.BlockSpec((tm, D_MODEL), row), pl.BlockSpec((tm, D_MODEL), row),
                  pl.BlockSpec((1, D_MODEL), fix), pl.BlockSpec((1, D_MODEL), fix)],
        out_specs=pl.BlockSpec((tm, D_MODEL), row),
        out_shape=jax.ShapeDtypeStruct((n, D_MODEL), F32),
        compiler_params=_cparams("parallel"),
        name="residual_ln2",
    )(x1, f, g, b)


def _post_mixer(o_nsa, o_diff, x2d, p, s_tile, chunk):
    x1, x1b, _, sel_t, gate_t, _, _, _ = _out_proj_ln_router(o_nsa, o_diff, x2d, p["wo_a"], p["wo_b"], p["ln1_g"],
                                                             p["ln1_b"], p["wr_t"], p["br"])
    pos_t, counts = _moe_positions(sel_t, s_tile)
    f = _moe(x1b, pos_t, gate_t, counts, p["wgu"], p["wd"], p["p_mat"], p["bg"], p["bl"], p["bd"], s_tile, chunk)
    return _residual_ln(x1, f, p["ln2_g"], p["ln2_b"])


def _moe_slots_kernel(sel_ref, tope_ref, pstart_ref, u_ref, slot_ref, carry_ref):
    @pl.when(pl.program_id(0) == 0)
    def _():
        carry_ref[...] = jnp.zeros_like(carry_ref)

    sel = sel_ref[...]
    slot_all = pstart_ref[...] + carry_ref[...] + _dot(sel.astype(BF16), u_ref[...])
    eidx = lax.broadcasted_iota(jnp.int32, sel.shape, 0).astype(F32)
    tope = tope_ref[...]
    rows = [jnp.sum(jnp.where(eidx == tope[k:k + 1], slot_all, 0.0), axis=0, keepdims=True) for k in range(TOP_K)]
    rows += [jnp.zeros_like(rows[0])] * (8 - TOP_K)
    slot_ref[...] = jnp.concatenate(rows, axis=0).astype(jnp.int32)
    carry_ref[...] += jnp.sum(sel, axis=1, keepdims=True)


def _moe_slots(sel_t, tope, pstart, tm=2048):
    n = sel_t.shape[1]
    upper = jnp.asarray(np.triu(np.ones((tm, tm), np.float32), 1), BF16)
    return pl.pallas_call(
        _moe_slots_kernel,
        grid=(n // tm,),
        in_specs=[pl.BlockSpec((N_EXPERTS, tm), lambda i: (0, i)), pl.BlockSpec((8, tm), lambda i: (0, i)),
                  pl.BlockSpec((N_EXPERTS, 1), lambda i: (0, 0)), pl.BlockSpec((tm, tm), lambda i: (0, 0))],
        out_specs=pl.BlockSpec((8, tm), lambda i: (0, i)),
        out_shape=jax.ShapeDtypeStruct((8, n), jnp.int32),
        scratch_shapes=[pltpu.VMEM((N_EXPERTS, 1), F32)],
        compiler_params=_cparams("arbitrary"),
        name="moe_slots",
    )(sel_t, tope, pstart, upper)


def _sc_mesh():
    return plsc.VectorSubcoreMesh(core_axis_name="c", subcore_axis_name="s", num_cores=SC_CORES,
                                  num_subcores=SC_SUBCORES)


def _sc_scatter_rows(src, idx, n_out):
    n_rows, width = src.shape
    n_idx = idx.shape[0]
    per_core = n_rows // SC_WINDOW // SC_CORES

    @functools.partial(pl.kernel, out_type=jax.ShapeDtypeStruct((n_out, width), src.dtype), mesh=_sc_mesh(),
                       scratch_types=[], name="moe_dispatch_rows")
    def scatter(src_hbm, idx_hbm, out_hbm):
        def body(src_vmem, idx_vmem):
            for k in range(n_idx):
                pltpu.sync_copy(src_vmem, out_hbm.at[idx_vmem.at[k]])

        pltpu.emit_pipeline(
            body,
            grid=(SC_CORES, per_core),
            in_specs=[pl.BlockSpec((SC_WINDOW, width), lambda c, j: (c * per_core + j, 0)),
                      pl.BlockSpec((n_idx, SC_WINDOW), lambda c, j: (0, c * per_core + j))],
            out_specs=[],
            core_axis_name=("c", "s"),
            dimension_semantics=(pltpu.PARALLEL, pltpu.PARALLEL),
        )(src_hbm, idx_hbm)

    return scatter(src, idx)


def _sc_gather_rows(table, idx):
    n_rows = idx.shape[1]
    width = table.shape[1]
    per_core = n_rows // SC_WINDOW // SC_CORES

    @functools.partial(pl.kernel, out_type=jax.ShapeDtypeStruct((n_rows, width), table.dtype), mesh=_sc_mesh(),
                       scratch_types=[], name="moe_return_rows")
    def gather(table_hbm, idx_hbm, out_hbm):
        def body(idx_vmem, out_vmem):
            pltpu.sync_copy(table_hbm.at[idx_vmem.at[0]], out_vmem)

        pltpu.emit_pipeline(
            body,
            grid=(SC_CORES, per_core),
            in_specs=[pl.BlockSpec((1, SC_WINDOW), lambda c, j: (0, c * per_core + j))],
            out_specs=[pl.BlockSpec((SC_WINDOW, width), lambda c, j: (c * per_core + j, 0))],
            core_axis_name=("c", "s"),
            dimension_semantics=(pltpu.PARALLEL, pltpu.PARALLEL),
        )(idx_hbm, out_hbm)

    return gather(table, idx)


def _moe_ffn_kernel(blk_e_ref, nvalid_ref, xs_ref, wgu_ref, wd_ref, p_ref, bg_ref, bl_ref, bd_ref, o_ref,
                    wg_s, wl_s, wd_s):
    i = pl.program_id(0)
    nv = nvalid_ref[i]
    new_expert = (i == 0) | (blk_e_ref[i] != blk_e_ref[jnp.maximum(i - 1, 0)])

    @pl.when(new_expert & (nv > 0))
    def _():
        _load_expert_weights(wgu_ref, wd_ref, p_ref, wg_s, wl_s, wd_s)

    @pl.when(nv > 0)
    def _():
        live = lax.broadcasted_iota(jnp.int32, (o_ref.shape[1], 1), 0) < nv
        xs = jnp.where(live, _unpack_rows(xs_ref[0], xs_ref[1]), 0.0).astype(BF16)
        hg = jnp.minimum(_dot(xs, wg_s[...]) + bg_ref[0], SWIGLU_LIMIT)
        hl = jnp.clip(_dot(xs, wl_s[...]) + bl_ref[0], -SWIGLU_LIMIT, SWIGLU_LIMIT)
        a = (hl + 1.0) * hg * jax.nn.sigmoid(SWIGLU_ALPHA * hg)
        _pack_rows(_dot(a.astype(BF16), wd_s[...]) + bd_ref[0], o_ref)

    @pl.when(nv == 0)
    def _():
        o_ref[...] = jnp.zeros_like(o_ref)


def _moe_ffn(blk_e, nvalid, xs, wgu, wd, p_mat, bg, bl, bd):
    _, n_slots, half = xs.shape
    wspec = lambda a: pl.BlockSpec((1,) + a.shape[1:], lambda i, be, nv: (be[i], 0, 0))
    bspec = pl.BlockSpec((1, 1, D_MODEL), lambda i, be, nv: (be[i], 0, 0))
    rows = pl.BlockSpec((2, MOE_CHUNK, half), lambda i, be, nv: (0, i, 0))
    return pl.pallas_call(
        _moe_ffn_kernel,
        grid_spec=pltpu.PrefetchScalarGridSpec(
            num_scalar_prefetch=2,
            grid=(n_slots // MOE_CHUNK,),
            in_specs=[rows, wspec(wgu), wspec(wd), pl.BlockSpec(p_mat.shape, lambda i, be, nv: (0, 0)),
                      bspec, bspec, bspec],
            out_specs=rows,
            scratch_shapes=_EXPERT_WEIGHT_SCRATCH),
        out_shape=jax.ShapeDtypeStruct((2, n_slots, half), jnp.uint32),
        compiler_params=_cparams("arbitrary"),
        name="moe_ffn_sorted",
    )(blk_e, nvalid, xs, wgu, wd, p_mat, bg, bl, bd)


def _ln2_combine_kernel(x_ref, r_ref, gate_ref, g_ref, b_ref, o_ref):
    gate = gate_ref[...]
    f = gate[:, 0:1] * _unpack_rows(r_ref[0, 0], r_ref[0, 1])
    for k in range(1, TOP_K):
        f = f + gate[:, k:k + 1] * _unpack_rows(r_ref[k, 0], r_ref[k, 1])
    o_ref[...] = _layer_norm(DEEPNORM_ALPHA * x_ref[...] + f, g_ref[...], b_ref[...])


def _ln2_combine(x1, returned, gate_tok, g, b, tm=256):
    n = x1.shape[0]
    row = lambda i: (i, 0)
    fix = lambda i: (0, 0)
    return pl.pallas_call(
        _ln2_combine_kernel,
        grid=(n // tm,),
        in_specs=[pl.BlockSpec((tm, D_MODEL), row),
                  pl.BlockSpec((TOP_K, 2, tm, D_MODEL // 4), lambda i: (0, 0, i, 0)),
                  pl.BlockSpec((tm, 8), row), pl.BlockSpec((1, D_MODEL), fix), pl.BlockSpec((1, D_MODEL), fix)],
        out_specs=pl.BlockSpec((tm, D_MODEL), row),
        out_shape=jax.ShapeDtypeStruct((n, D_MODEL), F32),
        compiler_params=_cparams("parallel"),
        name="combine_ln2",
    )(x1, returned, gate_tok, g, b)


def _post_mixer_sorted(o_nsa, o_diff, x2d, p):
    n = x2d.shape[0]
    x1, _, x1p, sel_t, _, tope, gate8, cnt = _out_proj_ln_router(o_nsa, o_diff, x2d, p["wo_a"], p["wo_b"], p["ln1_g"],
                                                                 p["ln1_b"], p["wr_t"], p["br"])
    counts = jnp.sum(cnt[:, :, 0], axis=0).astype(jnp.int32)
    padded = (counts + (MOE_CHUNK - 1)) // MOE_CHUNK * MOE_CHUNK
    pend = jnp.cumsum(padded)
    pstart = pend - padded
    n_blocks = n * TOP_K // MOE_CHUNK + N_EXPERTS
    n_slots = n_blocks * MOE_CHUNK
    blk0 = jnp.arange(n_blocks, dtype=jnp.int32)[:, None] * MOE_CHUNK
    blk_e = jnp.minimum(jnp.sum((blk0 >= pend[None, :]).astype(jnp.int32), axis=1), N_EXPERTS - 1)
    in_region = (blk0 >= pstart[None, :]) & (blk0 < pend[None, :])
    nvalid = jnp.sum(jnp.where(in_region, jnp.clip(pstart + counts - blk0, 0, MOE_CHUNK), 0), axis=1)
    slot = _moe_slots(sel_t, tope, pstart.astype(F32)[:, None])[:TOP_K]
    half = D_MODEL // 4
    slot2 = jnp.concatenate([slot, slot + n_slots], axis=1)
    xs = _sc_scatter_rows(x1p.reshape(2 * n, half), slot2, 2 * n_slots)
    outs = _moe_ffn(blk_e, nvalid, xs.reshape(2, n_slots, half), p["wgu"], p["wd"], p["p_mat"], p["bg"], p["bl"],
                    p["bd"])
    returned = _sc_gather_rows(outs.reshape(2 * n_slots, half), slot2.reshape(1, TOP_K * 2 * n))
    return _ln2_combine(x1, returned.reshape(TOP_K, 2, n, half), gate8.T, p["ln2_g"], p["ln2_b"])


def _sample_kernel(pt_ref, qm_ref, qd_ref, gt_ref, nrow_ref, wrow_ref, wcol_ref, drow_ref, win_ref,
                   w1_ref, pe_ref, w2_ref, m_ref, e_ref, r_ref, sn_ref, sd_ref, lam_ref, subg_ref, *rest,
                   n_pages, past_len, lam_init):
    nsa_pages = rest[:n_pages]
    diff_pages = rest[n_pages:2 * n_pages]
    onsa_ref, odiff_ref, nwin_ref, x_ref = rest[2 * n_pages:]
    n16 = past_len // CMP_STRIDE
    qpos = past_len
    kpos = lax.broadcasted_iota(jnp.int32, (1, past_len), 1)
    dist_k = (qpos - kpos).astype(F32)

    for j, pg in enumerate(nsa_pages):
        for kind in range(2):
            x_ref[kind, j * PAGE_SIZE:(j + 1) * PAGE_SIZE, :] = pg[kind * LANES:(kind + 1) * LANES, :].T

    kv_cmp = [_compress_chunks(lambda l, kind=kind: x_ref[kind, pl.ds(l, n16, stride=CMP_STRIDE), :],
                               w1_ref[kind], pe_ref[kind], w2_ref[kind], n16) for kind in range(2)]

    qm = qm_ref[0]
    qmb = qm.astype(BF16)
    slope_n = sn_ref[:, 0:1]
    nrow = nrow_ref[0]
    wrow = wrow_ref[0]
    gt = gt_ref[0]

    def new_key_score(qrows, krow):
        return jnp.sum(qrows * krow, axis=1, keepdims=True) * ATTN_SCALE

    cmp_end = lax.broadcasted_iota(jnp.int32, (1, n16), 1) * CMP_STRIDE + (CMP_LEN - 1)
    dist_c = qpos - cmp_end
    s = _dot_nt(qmb, kv_cmp[0].astype(BF16)) * ATTN_SCALE - slope_n * dist_c.astype(F32)
    e, d = _softmax_parts(s, dist_c >= 0)
    p_cmp = e / d
    o_cmp = _dot(p_cmp.astype(BF16), kv_cmp[1].astype(BF16))
    pgrp = jnp.dot(r_ref[...], p_cmp, precision=HIGHEST, preferred_element_type=F32)
    imp = jnp.dot(pgrp, m_ref[...], precision=HIGHEST, preferred_element_type=F32)
    nbl = m_ref.shape[1]
    blk = lax.broadcasted_iota(jnp.int32, (NSA_HEADS, nbl), 1)
    sel = _select_blocks(imp, blk, jnp.full((NSA_HEADS, 1), qpos, jnp.int32), past_len // SEL_BLOCK + 1)
    mask_s = _dot(sel.astype(BF16), e_ref[...]) > 0.5

    _sample_diff_branch(qd_ref, drow_ref, sd_ref, lam_ref, subg_ref, diff_pages, odiff_ref, dist_k, lam_init)

    s = jnp.concatenate([_dot(qmb, pg[2 * LANES:3 * LANES, :].astype(BF16)) for pg in nsa_pages], axis=1)
    s = jnp.where(mask_s, s * ATTN_SCALE - slope_n * dist_k, -jnp.inf)
    s_new = new_key_score(qm, nrow[:, 2 * LANES:3 * LANES])
    m = jnp.maximum(jnp.max(s, axis=1, keepdims=True), s_new)
    e = jnp.exp(s - m)
    e_new = jnp.exp(s_new - m)
    d = jnp.sum(e, axis=1, keepdims=True) + e_new
    eb = e.astype(BF16)
    acc = e_new * nrow[:, 3 * LANES:4 * LANES]
    for j, pg in enumerate(nsa_pages):
        acc = acc + _dot_nt(eb[:, j * PAGE_SIZE:(j + 1) * PAGE_SIZE], pg[3 * LANES:4 * LANES, :].astype(BF16))
    o_sel = acc / d

    nwin = win_ref.shape[2]
    wpos = past_len - nwin + lax.broadcasted_iota(jnp.int32, (1, nwin), 1)
    dist_w = qpos - wpos
    mask_w = (dist_w >= 0) & (dist_w <= WINDOW)
    win = win_ref[0]
    s = _dot(qmb, win[0:LANES, :].astype(BF16)) * ATTN_SCALE - slope_n * dist_w.astype(F32)
    s = jnp.where(mask_w, s, -jnp.inf)
    s_new = new_key_score(qm, wrow[:, 0:LANES])
    m = jnp.maximum(jnp.max(s, axis=1, keepdims=True), s_new)
    e = jnp.exp(s - m)
    e_new = jnp.exp(s_new - m)
    d = jnp.sum(e, axis=1, keepdims=True) + e_new
    o_win = (_dot_nt(e.astype(BF16), win[LANES:2 * LANES, :].astype(BF16)) + e_new * wrow[:, LANES:2 * LANES]) / d

    onsa_ref[0] = gt[:, 0:1] * o_cmp + gt[:, 1:2] * o_sel + gt[:, 2:3] * o_win

    lane = lax.broadcasted_iota(jnp.int32, win.shape, 1)
    nwin_ref[0] = jnp.where(lane == nwin - 1, wcol_ref[0], pltpu.roll(win, nwin - 1, 1))


def _sample_diff_branch(qd_ref, drow_ref, sd_ref, lam_ref, subg_ref, diff_pages, odiff_ref, dist_k, lam_init):
    drow = drow_ref[0]

    def diff_rows(pg, j):
        return pg[pl.ds(j, PAGE_SIZE, stride=DIFF_ROWS), :]

    qds = [qd_ref[0, h] for h in range(DIFF_HEADS)]
    qdb = [qh.astype(BF16) for qh in qds]
    s_pages = []
    for pg in diff_pages:
        sp = _dot_nt(qdb[0], diff_rows(pg, 0).astype(BF16))
        for h in range(1, DIFF_HEADS):
            sp = sp + _dot_nt(qdb[h], diff_rows(pg, h).astype(BF16))
        s_pages.append(sp)
    s = jnp.concatenate(s_pages, axis=1) * ATTN_SCALE - sd_ref[:, 0:1] * dist_k
    prod = qds[0] * drow[0:1]
    for h in range(1, DIFF_HEADS):
        prod = prod + qds[h] * drow[h:h + 1]
    s_new = jnp.sum(prod, axis=1, keepdims=True) * ATTN_SCALE
    m = jnp.maximum(jnp.max(s, axis=1, keepdims=True), s_new)
    e = jnp.exp(s - m)
    e_new = jnp.exp(s_new - m)
    d = jnp.sum(e, axis=1, keepdims=True) + e_new
    lam = lam_ref[0:1, 0:1]
    p = e / d
    p_new = e_new / d
    a = (p - lam * pltpu.roll(p, DIFF_HEADS, 0)).astype(BF16)
    a_new = p_new - lam * pltpu.roll(p_new, DIFF_HEADS, 0)
    accs = []
    for h in range(DIFF_HEADS):
        acc = a_new * drow[DIFF_HEADS + h:DIFF_HEADS + h + 1]
        for j, pg in enumerate(diff_pages):
            acc = acc + _dot(a[:, j * PAGE_SIZE:(j + 1) * PAGE_SIZE], diff_rows(pg, DIFF_HEADS + h).astype(BF16))
        accs.append(acc)
    acc = jnp.concatenate(accs, axis=1)
    lane_head = lax.broadcasted_iota(jnp.int32, acc.shape, 1) // DIFF_VDIM
    row = lax.broadcasted_iota(jnp.int32, acc.shape, 0)
    own = lane_head == row
    ms = jnp.sum(jnp.where(own, jnp.square(acc), 0.0), axis=1, keepdims=True) / DIFF_VDIM
    odiff_ref[0] = acc * lax.rsqrt(ms + RMS_EPS) * subg_ref[...] * (1.0 - lam_init)


def _sample_attention(page_table, qm, qd4, gt8, nrow, wrow, wcol, drow, win_t, pool_nsa, pool_diff,
                      w1ab, pe2, w2bd, m_mat, e_mat, r_mat, sn, sd, lam_tile, subg_tile, past_len, lam_init):
    B = qm.shape[0]
    n_pages = page_table.shape[1]
    nwin = win_t.shape[2]

    def per_b(shape):
        nd = len(shape)
        return pl.BlockSpec((1,) + shape, lambda b, pt: (b,) + (0,) * nd)

    def fixed(shape):
        nd = len(shape)
        return pl.BlockSpec(shape, lambda b, pt: (0,) * nd)

    def page_spec(pool, j):
        return pl.BlockSpec((None,) + pool.shape[1:], lambda b, pt: (pt[b, j], 0, 0))

    in_specs = [per_b((NSA_HEADS, LANES)), per_b((DIFF_HEADS, 8, LANES)), per_b((NSA_HEADS, LANES)),
                per_b((1, NSA_W)), per_b((1, WIN_W)), per_b((WIN_W, 1)), per_b((DIFF_ROWS, LANES)),
                per_b((WIN_W, nwin)),
                fixed(w1ab.shape), fixed(pe2.shape), fixed(w2bd.shape), fixed(m_mat.shape), fixed(e_mat.shape),
                fixed(r_mat.shape), fixed(sn.shape), fixed(sd.shape), fixed(lam_tile.shape), fixed(subg_tile.shape)]
    in_specs += [page_spec(pool_nsa, j) for j in range(n_pages)]
    in_specs += [page_spec(pool_diff, j) for j in range(n_pages)]
    return pl.pallas_call(
        functools.partial(_sample_kernel, n_pages=n_pages, past_len=past_len, lam_init=lam_init),
        grid_spec=pltpu.PrefetchScalarGridSpec(
            num_scalar_prefetch=1,
            grid=(B,),
            in_specs=in_specs,
            out_specs=(per_b((NSA_HEADS, LANES)), per_b((8, DIFF_HEADS * DIFF_VDIM)), per_b((WIN_W, nwin))),
            scratch_shapes=[pltpu.VMEM((2, past_len, LANES), F32)]),
        out_shape=(jax.ShapeDtypeStruct((B, NSA_HEADS, LANES), F32),
                   jax.ShapeDtypeStruct((B, 8, DIFF_HEADS * DIFF_VDIM), F32),
                   jax.ShapeDtypeStruct((B, WIN_W, nwin), F32)),
        compiler_params=_cparams("arbitrary"),
        name="sample_attention",
    )(page_table, qm, qd4, gt8, nrow, wrow, wcol, drow, win_t, w1ab, pe2, w2bd, m_mat, e_mat, r_mat, sn, sd,
      lam_tile, subg_tile, *([pool_nsa] * n_pages), *([pool_diff] * n_pages))


def _cmp_to_sel(n_cmp, n_sel):
    c0 = np.arange(n_cmp)[:, None] * CMP_STRIDE
    s0 = np.arange(n_sel)[None, :] * SEL_BLOCK
    ov = np.clip(np.minimum(c0 + CMP_LEN, s0 + SEL_BLOCK) - np.maximum(c0, s0), 0, None)
    return (ov / CMP_LEN).astype(np.float32)


def _lambda_init(layer):
    return 0.8 - 0.6 * math.exp(-0.3 * layer)


def _prep_params(l, w_in, w_out, diff_subln_g, ln1_g, ln1_b, ln2_g, ln2_b, w_router, b_router,
                 w_gate_up, b_gate_up, w_down, b_down):
    gate_end = _GT0 + 3 * NSA_HEADS
    wt = w_in[l].T
    w_t = jnp.concatenate([wt[:gate_end], jnp.zeros((_GATE_PAD, D_MODEL), F32), wt[gate_end:]], axis=0).astype(BF16)
    bgu = b_gate_up[l]
    return {
        "w_t": w_t,
        "wo_a": w_out[l][:512].astype(BF16), "wo_b": w_out[l][512:].astype(BF16),
        "ln1_g": ln1_g[l][None], "ln1_b": ln1_b[l][None], "ln2_g": ln2_g[l][None], "ln2_b": ln2_b[l][None],
        "wr_t": w_router[l].T, "br": b_router[l][:, None],
        "wgu": w_gate_up[l], "wd": w_down[l], "p_mat": _deinterleave_matrix(),
        "bg": bgu[:, None, 0::2], "bl": bgu[:, None, 1::2], "bd": b_down[l][:, None, :],
        "subln_g": diff_subln_g[l][None],
    }


def _cmp_params(cmp_pe, cmp_w1, cmp_w2):
    eye = jnp.eye(NSA_KV_HEADS, dtype=F32)
    halves = []
    for half in range(2):
        w = cmp_w1[:, half * CMP_STRIDE:(half + 1) * CMP_STRIDE]
        wb = jnp.einsum('klde,gh->klgdhe', w, eye)
        halves.append(wb.reshape(2, CMP_STRIDE * LANES, LANES))
    w1ab = jnp.concatenate(halves, axis=-1).astype(BF16)
    pe = cmp_pe.reshape(2, 2, CMP_STRIDE, 1, HEAD_DIM)
    pe2 = jnp.broadcast_to(pe, (2, 2, CMP_STRIDE, NSA_KV_HEADS, HEAD_DIM)).reshape(2, 2, 1, CMP_STRIDE * LANES)
    pe2 = jnp.broadcast_to(pe2, (2, 2, 8, CMP_STRIDE * LANES))
    w2bd = jnp.einsum('kde,gh->kgdhe', cmp_w2, eye).reshape(2, LANES, LANES).astype(BF16)
    return w1ab, pe2, w2bd


def _feature_major_to_rows(a_t, kinds):
    B, _, T = a_t.shape
    return a_t.reshape(B, kinds, NSA_KV_HEADS, HEAD_DIM, T).transpose(0, 4, 1, 2, 3)


def _prompt_group(x, p, cmp_pe, cmp_w1, cmp_w2, lam_tile, lam_init):
    B, T, _ = x.shape
    x2d = x.reshape(B * T, D_MODEL)
    q_nsa, q_tb, nsa_rows, nsa_t, nsa_tb, _, win_t, win_tb, gates, dq, diff8, diff_b = _in_proj(
        x2d, p["w_t"], B, nsa_tok_w=NSA_W // 2, win_tok_w=LANES)
    r3 = lambda a: a.reshape(B, T, a.shape[-1])
    q_nsa, nsa_rows3, gates, dq = map(r3, (q_nsa, nsa_rows, gates, dq))
    kvc = _compress_prompt(nsa_rows3, *_cmp_params(cmp_pe, cmp_w1, cmp_w2))
    n16 = T // CMP_STRIDE
    n_sel = -(-T // SEL_BLOCK)
    m_np = np.zeros((n16, n_sel), np.float32)
    m_np[:n16 - 1] = _cmp_to_sel(n16 - 1, n_sel)
    o_cmp, sel = _cmp_attn_prompt(q_nsa, q_tb, kvc, jnp.asarray(m_np.T))
    e_np = np.zeros((NSA_KV_HEADS, LANES, T), np.float32)
    for g in range(NSA_KV_HEADS):
        e_np[g, g * n_sel + np.arange(T) // SEL_BLOCK, np.arange(T)] = 1.0
    o_nsa = _sel_win_prompt(q_nsa, nsa_tb, win_tb, sel, jnp.asarray(e_np, BF16), o_cmp, gates)
    o_diff = _diff_prompt(lam_tile, dq, diff_b.reshape(B, T, DIFF_W), p["subln_g"], lam_init)
    y = _post_mixer_sorted(o_nsa.reshape(B * T, 512), o_diff.reshape(B * T, 512), x2d, p)
    nwin = min(WINDOW, T)
    return (y.reshape(B, T, D_MODEL),
            _feature_major_to_rows(nsa_t, 4),
            diff8.reshape(B, T, 2, DIFF_HEADS, DIFF_VDIM),
            _feature_major_to_rows(win_t[:, :, T - nwin:], 2))


def _sample_group(x, pool_nsa, pool_diff, win_buf, page_table, p, cmp_pe, cmp_w1, cmp_w2, lam_tile, lam_init):
    B, T, _ = x.shape
    past_len = page_table.shape[1] * PAGE_SIZE
    x2d = x.reshape(B, D_MODEL)
    q_nsa, _, nsa_rows, nsa_t, _, win_rows, _, _, gates, dq, diff8, _ = _in_proj(x2d, p["w_t"], 1)
    qh = q_nsa.reshape(B, NSA_HEADS, HEAD_DIM)
    grp = (np.arange(NSA_HEADS) // NSA_GROUP)[None, :, None, None] == np.arange(NSA_KV_HEADS)[None, None, :, None]
    qm = (qh[:, :, None, :] * jnp.asarray(grp, F32)).reshape(B, NSA_HEADS, LANES)
    dq4 = dq.reshape(B, DIFF_HEADS, 1, 2 * HEAD_DIM)
    rr = np.arange(8)
    rowmask = (rr[None, :, None] % DIFF_HEADS == np.arange(DIFF_HEADS)[:, None, None]) & (
        rr[None, :, None] // DIFF_HEADS == (np.arange(LANES) // HEAD_DIM)[None, None, :])
    qd4 = dq4 * jnp.asarray(rowmask, F32)[None]
    gt8 = jnp.pad(gates[:, :3 * NSA_HEADS].reshape(B, 3, NSA_HEADS).transpose(0, 2, 1), ((0, 0), (0, 0), (0, LANES - 3)))
    nwin = win_buf.shape[1]
    win_t = win_buf.transpose(0, 2, 3, 4, 1).reshape(B, WIN_W, nwin)
    n_pool = pool_nsa.shape[0]
    pool_nsa_t = pool_nsa.transpose(0, 2, 3, 4, 1).reshape(n_pool, NSA_W, PAGE_SIZE)
    pool_diff_r = pool_diff.reshape(n_pool, PAGE_SIZE * DIFF_ROWS, LANES)
    w1ab, pe2, w2bd = _cmp_params(cmp_pe, cmp_w1, cmp_w2)
    n16 = past_len // CMP_STRIDE
    n_sel = past_len // SEL_BLOCK + 1
    m_np = np.zeros((n16, 64), np.float32)
    m_np[:n16 - 1, :n_sel] = _cmp_to_sel(n16 - 1, n_sel)
    e_np = np.zeros((64, past_len), np.float32)
    e_np[np.arange(past_len) // SEL_BLOCK, np.arange(past_len)] = 1.0
    r_np = (np.arange(8)[:, None] // NSA_GROUP == np.arange(8)[None, :] // NSA_GROUP).astype(np.float32)
    sn = jnp.asarray(np.tile(np.asarray(NSA_SLOPES, np.float32)[:, None], (1, LANES)))
    sd = jnp.asarray(np.tile(np.asarray(DIFF_SLOPES, np.float32)[np.arange(8) % DIFF_HEADS, None], (1, LANES)))
    subg_tile = jnp.tile(p["subln_g"], (1, DIFF_HEADS))
    o_nsa8, o_diff8, new_win_t = _sample_attention(
        page_table, qm, qd4, gt8, nsa_rows[:, None, :], win_rows[:, None, :], win_rows[:, :, None],
        diff8.reshape(B, DIFF_ROWS, LANES), win_t, pool_nsa_t, pool_diff_r,
        w1ab, pe2, w2bd, jnp.asarray(m_np), jnp.asarray(e_np, BF16), jnp.asarray(r_np), sn, sd, lam_tile, subg_tile,
        past_len, lam_init)
    o8 = o_nsa8.reshape(B, NSA_KV_HEADS, NSA_GROUP, NSA_KV_HEADS, HEAD_DIM)
    o_nsa = jnp.stack([o8[:, g, :, g] for g in range(NSA_KV_HEADS)], axis=1).reshape(B, 512)
    d8 = o_diff8[:, :DIFF_HEADS].reshape(B, DIFF_HEADS, DIFF_HEADS, DIFF_VDIM)
    o_diff = jnp.stack([d8[:, h, h] for h in range(DIFF_HEADS)], axis=1).reshape(B, 512)
    y = _post_mixer(o_nsa, o_diff, x2d, p, s_tile=B, chunk=B)
    return (y.reshape(B, T, D_MODEL),
            _feature_major_to_rows(nsa_t, 4).reshape(B, T, 4, NSA_KV_HEADS, HEAD_DIM),
            diff8.reshape(B, T, 2, DIFF_HEADS, DIFF_VDIM),
            _feature_major_to_rows(new_win_t, 2))


def kernel(x_prompt, x_sample, cache_nsa_kv, cache_diff_kv, state_nsa_win, page_table, w_in, w_out, cmp_pe, cmp_w1,
           cmp_w2, diff_lambda, diff_subln_g, ln1_g, ln1_b, ln2_g, ln2_b, w_router, b_router, w_gate_up, b_gate_up,
           w_down, b_down):
    depth = w_in.shape[0]
    xp, xs = x_prompt, x_sample
    outs = [[] for _ in range(6)]
    for l in range(depth):
        lam0 = _lambda_init(l)
        lv = diff_lambda[l].astype(F32)
        lam = jnp.exp(jnp.sum(lv[0] * lv[1])) - jnp.exp(jnp.sum(lv[2] * lv[3])) + lam0
        lam_tile = jnp.full((8, LANES), lam, F32)
        p = _prep_params(l, w_in, w_out, diff_subln_g, ln1_g, ln1_b, ln2_g, ln2_b, w_router, b_router,
                         w_gate_up, b_gate_up, w_down, b_down)
        xp, r_nsa, r_diff, r_win = _prompt_group(xp, p, cmp_pe[l], cmp_w1[l], cmp_w2[l], lam_tile, lam0)
        xs, s_nsa, s_diff, s_win = _sample_group(xs, cache_nsa_kv[l], cache_diff_kv[l], state_nsa_win[l], page_table,
                                                 p, cmp_pe[l], cmp_w1[l], cmp_w2[l], lam_tile, lam0)
        for lst, v in zip(outs, (r_nsa, r_diff, r_win, s_nsa, s_diff, s_win)):
            lst.append(v)
    return (xp, xs) + tuple(jnp.stack(o) for o in outs)
```
